```python
import jax
import jax.numpy as jnp
from jax import lax
import numpy as np


D_MODEL = 1024
BATCH = 32
SEQ = 2048
DEPTH = 1

CHUNK = 64
Q_BLOCK = 128
A_HEADS = 8
A_HEAD_DIM = 64
A_KV_RANK = 128
IDX_HEADS = 8
IDX_DIM = 64
IDX_TOPK = 256
IDX_SCALE = (IDX_HEADS * IDX_DIM) ** -0.5
B_HEADS = 8
B_HEAD_DIM = 64
B_LEFT_CHUNKS = 8
B_BAND = (B_LEFT_CHUNKS + 1) * CHUNK
REL_CLIP = 128
N_EXPERTS = 32
TOP_K = 4
D_FF = 1024
SWIGLU_LIMIT = 7.0
SWIGLU_ALPHA = 1.702
EXPERT_BLOCK = 256
DN_ALPHA = (2.0 * DEPTH) ** 0.25
DN_BETA = (8.0 * DEPTH) ** -0.25
LN_EPS = 1e-5

A_WIDTH = A_HEADS * A_HEAD_DIM
B_WIDTH = B_HEADS * B_HEAD_DIM
SPLITS = [A_WIDTH, A_KV_RANK, IDX_HEADS * IDX_DIM, IDX_DIM, IDX_HEADS,
          B_WIDTH, B_WIDTH, B_WIDTH, D_MODEL, D_MODEL]
IN_WIDTH = sum(SPLITS)

kernel_name = 'streaming_hybrid_dsa_chunkband_moe_block'


def layer_norm(x, g, b):
    xf = x.astype(jnp.float32)
    mu = jnp.mean(xf, axis=-1, keepdims=True)
    var = jnp.mean(jnp.square(xf - mu), axis=-1, keepdims=True)
    return ((xf - mu) * lax.rsqrt(var + LN_EPS)).astype(x.dtype) * g + b


def rms_norm(x, g):
    xf = x.astype(jnp.float32)
    ms = jnp.mean(jnp.square(xf), axis=-1, keepdims=True)
    return (xf * lax.rsqrt(ms + LN_EPS)).astype(x.dtype) * g


def alibi_slopes(n_heads):
    return jnp.array([2.0 ** (-8.0 * (h + 1) / n_heads) for h in range(n_heads)], jnp.float32)


def sparse_indexed_attention(q, c_kv, q_idx, k_idx, w_idx, w_uk, w_uv):
    bsz, seq = q.shape[0], q.shape[1]
    topk = min(IDX_TOPK, seq // 4)
    nb = seq // Q_BLOCK
    scale = A_HEAD_DIM ** -0.5
    q_lat = jnp.einsum('bshd,hdr->bshr', q, w_uk)
    key_chunk = jnp.arange(seq) // CHUNK
    slopes = alibi_slopes(A_HEADS)

    def to_blocks(a):
        return a.reshape((bsz, nb, Q_BLOCK) + a.shape[2:]).swapaxes(0, 1)

    def block_fn(args):
        blk, qb, qib, wb = args
        qpos = blk * Q_BLOCK + jnp.arange(Q_BLOCK)
        qchunk = qpos // CHUNK
        admissible = key_chunk[None, :] <= qchunk[:, None]
        logits = jax.nn.relu(jnp.einsum('bthd,bsd->bths', qib, k_idx).astype(jnp.float32))
        iscore = jnp.einsum('bth,bths->bts', wb.astype(jnp.float32) * IDX_SCALE, logits)
        iscore = jnp.where(admissible[None], iscore, -jnp.inf)
        _, sel = lax.top_k(iscore, topk)
        kv_sel = jax.vmap(lambda c, i: c[i])(c_kv, sel)
        scores = jnp.einsum('bthr,btkr->bthk', qb, kv_sel).astype(jnp.float32) * scale
        dist = jnp.abs(qpos[None, :, None] - sel)
        scores = scores - slopes[None, None, :, None] * dist[:, :, None, :]
        valid = (sel // CHUNK) <= qchunk[None, :, None]
        scores = jnp.where(valid[:, :, None, :], scores, -jnp.inf)
        p = jax.nn.softmax(scores, axis=-1).astype(c_kv.dtype)
        return jnp.einsum('bthk,btkr->bthr', p, kv_sel)

    o_lat = lax.map(block_fn, (jnp.arange(nb), to_blocks(q_lat), to_blocks(q_idx), to_blocks(w_idx)))
    o_lat = o_lat.swapaxes(0, 1).reshape(bsz, seq, A_HEADS, A_KV_RANK)
    o = jnp.einsum('bshr,hrd->bshd', o_lat, w_uv)
    return o.reshape(bsz, seq, A_WIDTH)


def chunk_band_attention(q, k, v, rel_bias):
    bsz, seq = q.shape[0], q.shape[1]
    nc = seq // CHUNK
    pad = B_LEFT_CHUNKS * CHUNK
    scale = B_HEAD_DIM ** -0.5
    k_pad = jnp.pad(k, ((0, 0), (pad, 0), (0, 0), (0, 0)))
    v_pad = jnp.pad(v, ((0, 0), (pad, 0), (0, 0), (0, 0)))
    rel = jnp.arange(CHUNK)[:, None] - jnp.arange(B_BAND)[None, :] + pad
    bias = rel_bias[:, jnp.clip(rel, -REL_CLIP, REL_CLIP) + REL_CLIP].astype(jnp.float32)
    q_chunks = q.reshape(bsz, nc, CHUNK, B_HEADS, B_HEAD_DIM).swapaxes(0, 1)

    def chunk_fn(args):
        c, qc = args
        kb = lax.dynamic_slice_in_dim(k_pad, c * CHUNK, B_BAND, axis=1)
        vb = lax.dynamic_slice_in_dim(v_pad, c * CHUNK, B_BAND, axis=1)
        s = jnp.einsum('bqhd,bkhd->bhqk', qc, kb).astype(jnp.float32) * scale + bias[None]
        key_valid = (c * CHUNK - pad + jnp.arange(B_BAND)) >= 0
        s = jnp.where(key_valid[None, None, None, :], s, -jnp.inf)
        p = jax.nn.softmax(s, axis=-1).astype(vb.dtype)
        return jnp.einsum('bhqk,bkhd->bqhd', p, vb)

    o = lax.map(chunk_fn, (jnp.arange(nc), q_chunks))
    return o.swapaxes(0, 1).reshape(bsz, seq, B_WIDTH)


def moe_ffn(x, w_router, b_router, w_gate, b_gate, w_up, b_up, w_down, b_down):
    bsz, seq, d = x.shape
    n = bsz * seq
    xf = x.reshape(n, d)
    logits = (xf @ w_router + b_router).astype(jnp.float32)
    top_val, top_idx = lax.top_k(logits, TOP_K)
    gates = jax.nn.softmax(top_val, axis=-1)
    n_assign = n * TOP_K
    expert_flat = top_idx.reshape(-1)
    token_flat = jnp.repeat(jnp.arange(n, dtype=jnp.int32), TOP_K)
    gate_flat = gates.reshape(-1)
    order = jnp.argsort(expert_flat)
    e_s = expert_flat[order]
    t_s = token_flat[order]
    g_s = gate_flat[order]
    counts = jnp.bincount(expert_flat, length=N_EXPERTS).astype(jnp.int32)
    offsets = jnp.cumsum(counts) - counts
    padded = (counts + EXPERT_BLOCK - 1) // EXPERT_BLOCK * EXPERT_BLOCK
    padded_ends = jnp.cumsum(padded)
    padded_starts = padded_ends - padded
    dest = padded_starts[e_s] + jnp.arange(n_assign, dtype=jnp.int32) - offsets[e_s]
    nblk = -(-n_assign // EXPERT_BLOCK) + N_EXPERTS
    cap = nblk * EXPERT_BLOCK
    tok_buf = jnp.zeros((cap,), jnp.int32).at[dest].set(t_s)
    gate_buf = jnp.zeros((cap,), jnp.float32).at[dest].set(g_s)
    blk_start = jnp.arange(nblk, dtype=jnp.int32) * EXPERT_BLOCK
    blk_expert = jnp.minimum(jnp.searchsorted(padded_ends, blk_start, side='right'), N_EXPERTS - 1)

    def block_fn(args):
        e, tok, g = args
        xb = xf[tok]
        hg = jnp.minimum(xb @ w_gate[e] + b_gate[e], SWIGLU_LIMIT)
        hu = jnp.clip(xb @ w_up[e] + b_up[e], -SWIGLU_LIMIT, SWIGLU_LIMIT)
        h = (hu + 1.0) * (hg * jax.nn.sigmoid(SWIGLU_ALPHA * hg))
        y = h @ w_down[e] + b_down[e]
        return y * g[:, None].astype(y.dtype)

    y_buf = lax.map(block_fn, (blk_expert, tok_buf.reshape(nblk, EXPERT_BLOCK),
                               gate_buf.reshape(nblk, EXPERT_BLOCK)))
    y = jax.ops.segment_sum(y_buf.reshape(cap, d), tok_buf, num_segments=n)
    return y.reshape(bsz, seq, d)


def setup_inputs(seed: int = 0) -> dict:
    key = jax.random.key(seed)
    ks = jax.random.split(key, 24)
    f32 = jnp.float32
    nrm = lambda k, shape, s: jax.random.normal(k, shape, f32) * s
    bounds = np.cumsum([0] + SPLITS)
    col_scale = np.ones((IN_WIDTH,), np.float32)
    col_scale[bounds[7]:bounds[8]] = DN_BETA
    w_in = nrm(ks[1], (DEPTH, D_MODEL, IN_WIDTH), D_MODEL ** -0.5) * jnp.asarray(col_scale)
    return {
        'x': jax.random.normal(ks[0], (BATCH, SEQ, D_MODEL), f32),
        'w_in': w_in,
        'kv_norm_g': 1.0 + nrm(ks[2], (DEPTH, A_KV_RANK), 0.02),
        'idx_k_norm_g': 1.0 + nrm(ks[3], (DEPTH, IDX_DIM), 0.02),
        'idx_k_norm_b': nrm(ks[4], (DEPTH, IDX_DIM), 0.02),
        'w_uk': nrm(ks[5], (DEPTH, A_HEADS, A_HEAD_DIM, A_KV_RANK), A_KV_RANK ** -0.5),
        'w_uv': nrm(ks[6], (DEPTH, A_HEADS, A_KV_RANK, A_HEAD_DIM), DN_BETA * A_KV_RANK ** -0.5),
        'rel_bias': nrm(ks[7], (DEPTH, B_HEADS, 2 * REL_CLIP + 1), 0.1),
        'w_branch_a': nrm(ks[8], (DEPTH, A_WIDTH, D_MODEL), DN_BETA * A_WIDTH ** -0.5),
        'w_branch_b': nrm(ks[9], (DEPTH, B_WIDTH, D_MODEL), DN_BETA * B_WIDTH ** -0.5),
        'w_out': nrm(ks[10], (DEPTH, D_MODEL, D_MODEL), DN_BETA * D_MODEL ** -0.5),
        'ln1_g': 1.0 + nrm(ks[11], (DEPTH, D_MODEL), 0.02),
        'ln1_b': nrm(ks[12], (DEPTH, D_MODEL), 0.02),
        'w_router': nrm(ks[13], (DEPTH, D_MODEL, N_EXPERTS), D_MODEL ** -0.5),
        'b_router': nrm(ks[14], (DEPTH, N_EXPERTS), 0.01),
        'w_gate': nrm(ks[15], (DEPTH, N_EXPERTS, D_MODEL, D_FF), DN_BETA * D_MODEL ** -0.5),
        'b_gate': nrm(ks[16], (DEPTH, N_EXPERTS, D_FF), 0.01),
        'w_up': nrm(ks[17], (DEPTH, N_EXPERTS, D_MODEL, D_FF), DN_BETA * D_MODEL ** -0.5),
        'b_up': nrm(ks[18], (DEPTH, N_EXPERTS, D_FF), 0.01),
        'w_down': nrm(ks[19], (DEPTH, N_EXPERTS, D_FF, D_MODEL), DN_BETA * D_FF ** -0.5),
        'b_down': nrm(ks[20], (DEPTH, N_EXPERTS, D_MODEL), 0.01),
        'ln2_g': 1.0 + nrm(ks[21], (DEPTH, D_MODEL), 0.02),
        'ln2_b': nrm(ks[22], (DEPTH, D_MODEL), 0.02),
    }


def reference(x, w_in, kv_norm_g, idx_k_norm_g, idx_k_norm_b, w_uk, w_uv, rel_bias,
              w_branch_a, w_branch_b, w_out, ln1_g, ln1_b, w_router, b_router,
              w_gate, b_gate, w_up, b_up, w_down, b_down, ln2_g, ln2_b):
    bsz, seq, _ = x.shape
    split_at = np.cumsum(SPLITS)[:-1].tolist()
    for l in range(DEPTH):
        proj = jnp.einsum('bsd,de->bse', x, w_in[l])
        (qa, ckv, qi, ki, wi, qb, kb, vb, ga, gb) = jnp.split(proj, split_at, axis=-1)
        qa = qa.reshape(bsz, seq, A_HEADS, A_HEAD_DIM)
        ckv = rms_norm(ckv, kv_norm_g[l])
        qi = qi.reshape(bsz, seq, IDX_HEADS, IDX_DIM)
        ki = layer_norm(ki, idx_k_norm_g[l], idx_k_norm_b[l])
        y_a = sparse_indexed_attention(qa, ckv, qi, ki, wi, w_uk[l], w_uv[l])
        y_b = chunk_band_attention(qb.reshape(bsz, seq, B_HEADS, B_HEAD_DIM),
                                   kb.reshape(bsz, seq, B_HEADS, B_HEAD_DIM),
                                   vb.reshape(bsz, seq, B_HEADS, B_HEAD_DIM), rel_bias[l])
        merged = (jax.nn.sigmoid(ga) * (y_a @ w_branch_a[l])
                  + jax.nn.sigmoid(gb) * (y_b @ w_branch_b[l]))
        x = layer_norm(DN_ALPHA * x + merged @ w_out[l], ln1_g[l], ln1_b[l])
        ffn = moe_ffn(x, w_router[l], b_router[l], w_gate[l], b_gate[l], w_up[l], b_up[l],
                      w_down[l], b_down[l])
        x = layer_norm(DN_ALPHA * x + ffn, ln2_g[l], ln2_b[l])
    return x
```

```python
import functools

import jax
import jax.numpy as jnp
import numpy as np
from jax import lax
from jax.experimental import pallas as pl
from jax.experimental.pallas import tpu as pltpu

F32 = jnp.float32
BF16 = jnp.bfloat16
I32 = jnp.int32

D_MODEL = 1024
CHUNK = 64
Q_BLOCK = 128
A_HEADS = 8
A_HEAD_DIM = 64
A_KV_RANK = 128
IDX_HEADS = 8
IDX_DIM = 64
IDX_TOPK = 256
IDX_SCALE = (IDX_HEADS * IDX_DIM) ** -0.5
B_HEADS = 8
B_HEAD_DIM = 64
B_LEFT_CHUNKS = 8
REL_CLIP = 128
N_EXPERTS = 32
TOP_K = 4
D_FF = 1024
SWIGLU_LIMIT = 7.0
SWIGLU_ALPHA = 1.702
DEPTH = 1
DN_ALPHA = (2.0 * DEPTH) ** 0.25
LN_EPS = 1e-5

A_WIDTH = A_HEADS * A_HEAD_DIM
B_WIDTH = B_HEADS * B_HEAD_DIM
SPLITS = [A_WIDTH, A_KV_RANK, IDX_HEADS * IDX_DIM, IDX_DIM, IDX_HEADS,
          B_WIDTH, B_WIDTH, B_WIDTH, D_MODEL, D_MODEL]

LANES = 128
KEY_TILE = 256
B_QTILE = 256
B_PAD = B_LEFT_CHUNKS * CHUNK
B_WIN = B_PAD + B_QTILE
MOE_BLOCK = 512
INT_MIN = -2 ** 31
VMEM_LIMIT = 56 * 1024 * 1024

A_SLOPES = [2.0 ** (-8.0 * (h + 1) / A_HEADS) for h in range(A_HEADS)]

_C_QA = 0
_C_QI = _C_QA + A_WIDTH
_C_CKV = _C_QI + IDX_HEADS * IDX_DIM
_C_KI = _C_CKV + A_KV_RANK
_C_WI = _C_KI + LANES
_C_QKVB = _C_WI + LANES
_C_END = _C_QKVB + 3 * B_WIDTH


def _params(sem, vmem=VMEM_LIMIT):
    return pltpu.CompilerParams(dimension_semantics=sem, vmem_limit_bytes=vmem)


def _proj_kernel(x_ref, w_ref, wuk_ref, kvg_ref, kig_ref, kib_ref,
                 qlat_ref, qi_ref, ckv_ref, ki_ref, wi_ref, qb_ref, kb_ref, vb_ref):
    xb = x_ref[...].astype(BF16)

    def mm(c0, c1):
        return jnp.dot(xb, w_ref[:, c0:c1], preferred_element_type=F32)

    qa = mm(_C_QA, _C_QI)
    qlat = jnp.dot(qa.astype(BF16), wuk_ref[...], preferred_element_type=F32)
    qlat_ref[...] = (qlat * (A_HEAD_DIM ** -0.5)).astype(BF16)
    qi_ref[...] = mm(_C_QI, _C_CKV).astype(BF16)

    ckv = mm(_C_CKV, _C_KI)
    ms = jnp.mean(ckv * ckv, axis=-1, keepdims=True)
    ckv_ref[...] = ((ckv * lax.rsqrt(ms + LN_EPS)) * kvg_ref[...]).astype(BF16)

    ki = mm(_C_KI, _C_WI)[:, :IDX_DIM]
    mu = jnp.mean(ki, axis=-1, keepdims=True)
    var = jnp.mean(jnp.square(ki - mu), axis=-1, keepdims=True)
    ki_ref[...] = (((ki - mu) * lax.rsqrt(var + LN_EPS)) * kig_ref[...] + kib_ref[...]).astype(BF16)

    wi_ref[...] = mm(_C_WI, _C_QKVB)[:, :IDX_HEADS]
    qb_ref[...] = mm(_C_QKVB, _C_QKVB + B_WIDTH).astype(BF16)
    kb_ref[...] = mm(_C_QKVB + B_WIDTH, _C_QKVB + 2 * B_WIDTH).astype(BF16)
    vb_ref[...] = mm(_C_QKVB + 2 * B_WIDTH, _C_END).astype(BF16)


def _projection(xf, w_mix, wuk_bd, kvg, kig, kib, tm):
    n = xf.shape[0]
    row = lambda w: pl.BlockSpec((tm, w), lambda i: (i, 0))
    full = lambda a: pl.BlockSpec(a.shape, lambda i: (0,) * a.ndim)
    outs = [(A_HEADS * A_KV_RANK, BF16), (IDX_HEADS * IDX_DIM, BF16), (A_KV_RANK, BF16),
            (IDX_DIM, BF16), (IDX_HEADS, F32), (B_WIDTH, BF16), (B_WIDTH, BF16), (B_WIDTH, BF16)]
    return pl.pallas_call(
        _proj_kernel,
        out_shape=[jax.ShapeDtypeStruct((n, w), dt) for w, dt in outs],
        grid=(n // tm,),
        in_specs=[row(D_MODEL), full(w_mix), full(wuk_bd), full(kvg), full(kig), full(kib)],
        out_specs=[row(w) for w, _ in outs],
        compiler_params=_params(("parallel",)),
        name="proj",
    )(xf, w_mix, wuk_bd, kvg, kig, kib)


def _mixer_a_kernel(kidx_ref, ckv_ref, ckvt_ref, qi_ref, wit_ref, qlat_ref, wuv_ref,
                    ya_ref, keys_ref, s_ref, acc_ref, *, seq, topk):
    i = pl.program_id(1)
    nkt = i // 2 + 1
    nsub = KEY_TILE // 8
    lane = lax.broadcasted_iota(I32, (1, LANES), 1)
    qpos = i * Q_BLOCK + lane
    qchunk = qpos >> 6
    row_iota = lax.broadcasted_iota(I32, (KEY_TILE, LANES), 0)

    qi_blk = qi_ref[0, 0]
    w_all = wit_ref[0] * IDX_SCALE

    def idx_body(kt, carry):
        lg = pl.dot(kidx_ref[0, kt], qi_blk, trans_b=True)
        isc = jnp.zeros((KEY_TILE, LANES), F32)
        for h in range(IDX_HEADS):
            isc = isc + jnp.maximum(lg[:, h * LANES:(h + 1) * LANES], 0.0) * w_all[h:h + 1, :]
        bits = lax.bitcast_convert_type(isc + 0.0, I32)
        key = bits ^ ((bits >> 31) & 0x7FFFFFFF)
        spos = kt * KEY_TILE + row_iota
        key = jnp.where((spos >> 6) <= qchunk, key, INT_MIN)
        keys_ref[kt] = key
        return carry

    lax.fori_loop(0, nkt, idx_body, 0)

    def count(pred):
        def body(kt, acc):
            spos = kt * KEY_TILE + row_iota
            m = jnp.where(pred(keys_ref[kt], spos), 1, 0).astype(I32)
            return acc + jnp.sum(m.reshape(nsub, 8, LANES), axis=0)
        acc = lax.fori_loop(0, nkt, body, jnp.zeros((8, LANES), I32))
        return jnp.sum(acc, axis=0, keepdims=True)

    def bit_body(it, carry):
        thr, cge = carry
        cand = thr ^ lax.shift_left(jnp.int32(1), 31 - it)
        cnt = count(lambda kv, spos: kv >= cand)
        ok = cnt >= topk
        return jnp.where(ok, cand, thr), jnp.where(ok, cnt, cge)

    thr0 = jnp.full((1, LANES), INT_MIN, I32)
    thr, cge = lax.fori_loop(0, 32, bit_body, (thr0, jnp.zeros((1, LANES), I32)))

    nbits = int(np.log2(seq)) + 1
    has_thr = thr > INT_MIN
    need_tie = jnp.max(jnp.where(has_thr, cge, 0)) > topk

    def tie_fn():
        room = topk - count(lambda kv, spos: kv > thr)

        def jb(it, cut):
            cand = cut | lax.shift_left(jnp.int32(1), nbits - 1 - it)
            f = count(lambda kv, spos: (kv == thr) & (spos < cand))
            return jnp.where(f <= room, cand, cut)
        return lax.fori_loop(0, nbits, jb, jnp.zeros((1, LANES), I32))

    cut = lax.cond(need_tie, tie_fn, lambda: jnp.full((1, LANES), 2 * seq, I32))
    cut = jnp.where(has_thr, cut, 0)

    qlat = qlat_ref[0]
    qr = jnp.concatenate([qlat[:, h * LANES:(h + 1) * LANES] for h in range(A_HEADS)], axis=0)

    def sc_body(kt, m):
        st = pl.dot(ckv_ref[0, kt], qr, trans_b=True)
        kv = keys_ref[kt]
        spos = kt * KEY_TILE + row_iota
        sel = (kv > thr) | ((kv == thr) & (spos < cut))
        dist = jnp.abs(qpos - spos).astype(F32)
        out = []
        for h in range(A_HEADS):
            sh = st[:, h * LANES:(h + 1) * LANES] - A_SLOPES[h] * dist
            sh = jnp.where(sel, sh, -jnp.inf)
            s_ref[kt, :, h * LANES:(h + 1) * LANES] = sh
            out.append(jnp.maximum(m[h], jnp.max(sh.reshape(nsub, 8, LANES), axis=0)))
        return tuple(out)

    m8 = lax.fori_loop(0, nkt, sc_body,
                       tuple(jnp.full((8, LANES), -jnp.inf, F32) for _ in range(A_HEADS)))
    mx = [jnp.max(m, axis=0, keepdims=True) for m in m8]

    acc_ref[...] = jnp.zeros_like(acc_ref)

    def pv_body(kt, l):
        out, ps = [], []
        for h in range(A_HEADS):
            p = jnp.exp(s_ref[kt, :, h * LANES:(h + 1) * LANES] - mx[h])
            out.append(l[h] + jnp.sum(p.reshape(nsub, 8, LANES), axis=0))
            ps.append(p.astype(BF16))
        acc_ref[...] += jnp.dot(ckvt_ref[0, kt], jnp.concatenate(ps, axis=1),
                                preferred_element_type=F32)
        return tuple(out)

    l8 = lax.fori_loop(0, nkt, pv_body, tuple(jnp.zeros((8, LANES), F32) for _ in range(A_HEADS)))
    inv = [1.0 / jnp.sum(l, axis=0, keepdims=True) for l in l8]

    for j in range(A_HEADS // 2):
        y = None
        for h in (2 * j, 2 * j + 1):
            o = (acc_ref[:, h * LANES:(h + 1) * LANES] * inv[h]).astype(BF16)
            t = pl.dot(o, wuv_ref[h], trans_a=True)
            y = t if y is None else y + t
        ya_ref[0, :, j * LANES:(j + 1) * LANES] = y.astype(BF16)


def _mixer_a(kidx, ckv, ckvt, qi_r, wit, qlat, wuv_pad, bsz, seq):
    nqb = seq // Q_BLOCK
    nkt = seq // KEY_TILE
    topk = min(IDX_TOPK, seq // 4)
    kern = functools.partial(_mixer_a_kernel, seq=seq, topk=topk)
    return pl.pallas_call(
        kern,
        out_shape=jax.ShapeDtypeStruct((bsz, seq, A_WIDTH), BF16),
        grid=(bsz, nqb),
        in_specs=[
            pl.BlockSpec((1, nkt, KEY_TILE, IDX_DIM), lambda b, i: (b, 0, 0, 0)),
            pl.BlockSpec((1, nkt, KEY_TILE, A_KV_RANK), lambda b, i: (b, 0, 0, 0)),
            pl.BlockSpec((1, nkt, A_KV_RANK, KEY_TILE), lambda b, i: (b, 0, 0, 0)),
            pl.BlockSpec((1, 1, IDX_HEADS * Q_BLOCK, IDX_DIM), lambda b, i: (b, i, 0, 0)),
            pl.BlockSpec((1, IDX_HEADS, Q_BLOCK), lambda b, i: (b, 0, i)),
            pl.BlockSpec((1, Q_BLOCK, A_HEADS * A_KV_RANK), lambda b, i: (b, i, 0)),
            pl.BlockSpec(wuv_pad.shape, lambda b, i: (0, 0, 0)),
        ],
        out_specs=pl.BlockSpec((1, Q_BLOCK, A_WIDTH), lambda b, i: (b, i, 0)),
        scratch_shapes=[
            pltpu.VMEM((nkt, KEY_TILE, LANES), I32),
            pltpu.VMEM((nkt, KEY_TILE, A_HEADS * LANES), F32),
            pltpu.VMEM((A_KV_RANK, A_HEADS * LANES), F32),
        ],
        compiler_params=_params(("parallel", "arbitrary")),
        name="mixer_a",
    )(kidx, ckv, ckvt, qi_r, wit, qlat, wuv_pad)


def _mixer_b_kernel(q_ref, k_ref, v_ref, bias_ref, yb_ref):
    t0 = pl.multiple_of(pl.program_id(1) * B_QTILE, B_QTILE)
    kw = k_ref[0, pl.ds(t0, B_WIN), :]
    vw = v_ref[0, pl.ds(t0, B_WIN), :]
    q = q_ref[0]
    kvalid = (t0 - B_PAD + lax.broadcasted_iota(I32, (1, B_WIN), 1)) >= 0
    lane = lax.broadcasted_iota(I32, (1, LANES), 1)
    scale = B_HEAD_DIM ** -0.5
    for j in range(B_HEADS // 2):
        sl = slice(j * LANES, (j + 1) * LANES)
        qs, ks, vs = q[:, sl], kw[:, sl], vw[:, sl]
        outs = []
        for hh in range(2):
            mine = (lane >= B_HEAD_DIM) if hh else (lane < B_HEAD_DIM)
            qm = jnp.where(mine, qs, jnp.zeros_like(qs))
            s = pl.dot(qm, ks, trans_b=True) * scale + bias_ref[2 * j + hh]
            s = jnp.where(kvalid, s, -jnp.inf)
            m = jnp.max(s, axis=-1, keepdims=True)
            p = jnp.exp(s - m)
            l = jnp.sum(p, axis=-1, keepdims=True)
            outs.append(jnp.dot(p.astype(BF16), vs, preferred_element_type=F32) / l)
        yb_ref[0, :, sl] = jnp.where(lane < B_HEAD_DIM, outs[0], outs[1]).astype(BF16)


def _band_bias(rel_bias):
    tq = np.arange(B_QTILE)[:, None]
    j = np.arange(B_WIN)[None, :]
    rel = np.clip(tq + B_PAD - j, -REL_CLIP, REL_CLIP) + REL_CLIP
    band = (j // CHUNK >= tq // CHUNK) & (j // CHUNK <= tq // CHUNK + B_LEFT_CHUNKS)
    return jnp.where(jnp.asarray(band)[None], rel_bias[:, rel].astype(F32), -jnp.inf)


def _mixer_b(qb, kb_pad, vb_pad, bias, bsz, seq):
    return pl.pallas_call(
        _mixer_b_kernel,
        out_shape=jax.ShapeDtypeStruct((bsz, seq, B_WIDTH), BF16),
        grid=(bsz, seq // B_QTILE),
        in_specs=[
            pl.BlockSpec((1, B_QTILE, B_WIDTH), lambda b, i: (b, i, 0)),
            pl.BlockSpec((1, seq + B_PAD, B_WIDTH), lambda b, i: (b, 0, 0)),
            pl.BlockSpec((1, seq + B_PAD, B_WIDTH), lambda b, i: (b, 0, 0)),
            pl.BlockSpec(bias.shape, lambda b, i: (0, 0, 0)),
        ],
        out_specs=pl.BlockSpec((1, B_QTILE, B_WIDTH), lambda b, i: (b, i, 0)),
        compiler_params=_params(("parallel", "arbitrary")),
        name="mixer_b",
    )(qb, kb_pad, vb_pad, bias)


def _layer_norm(z, g, b):
    mu = jnp.mean(z, axis=-1, keepdims=True)
    var = jnp.mean(jnp.square(z - mu), axis=-1, keepdims=True)
    return ((z - mu) * lax.rsqrt(var + LN_EPS)) * g + b


def _merge_kernel(x_ref, ya_ref, yb_ref, wg_ref, wba_ref, wbb_ref, wo_ref, g_ref, b_ref,
                  wr_ref, br_ref, x1_ref, topi_ref, gate_ref):
    x = x_ref[...]
    xb = x.astype(BF16)
    ga = jnp.dot(xb, wg_ref[:, :D_MODEL], preferred_element_type=F32)
    a = jnp.dot(ya_ref[...], wba_ref[...], preferred_element_type=F32)
    merged = jax.nn.sigmoid(ga) * a
    gb = jnp.dot(xb, wg_ref[:, D_MODEL:], preferred_element_type=F32)
    b = jnp.dot(yb_ref[...], wbb_ref[...], preferred_element_type=F32)
    merged = merged + jax.nn.sigmoid(gb) * b
    out = jnp.dot(merged.astype(BF16), wo_ref[...], preferred_element_type=F32)
    x1 = _layer_norm(DN_ALPHA * x + out, g_ref[...], b_ref[...])
    x1_ref[...] = x1

    lg = pl.dot(wr_ref[...], x1.astype(BF16), trans_b=True) + br_ref[...]
    eidx = lax.broadcasted_iota(I32, lg.shape, 0)
    vals, idxs = [], []
    for _ in range(TOP_K):
        m = jnp.max(lg, axis=0, keepdims=True)
        sel = jnp.min(jnp.where(lg == m, eidx, N_EXPERTS), axis=0, keepdims=True)
        vals.append(m)
        idxs.append(sel)
        lg = jnp.where(eidx == sel, -jnp.inf, lg)
    es = [jnp.exp(v - vals[0]) for v in vals]
    tot = es[0] + es[1] + es[2] + es[3]
    topi_ref[...] = jnp.concatenate(idxs, axis=0)
    gate_ref[...] = jnp.concatenate([e / tot for e in es], axis=0)


def _merge(xf, ya, yb, wg, wba, wbb, wo, g1, b1, wr_t, br, tm):
    n = xf.shape[0]
    row = lambda w: pl.BlockSpec((tm, w), lambda i: (i, 0))
    full = lambda a: pl.BlockSpec(a.shape, lambda i: (0,) * a.ndim)
    col = pl.BlockSpec((TOP_K, tm), lambda i: (0, i))
    return pl.pallas_call(
        _merge_kernel,
        out_shape=[jax.ShapeDtypeStruct((n, D_MODEL), F32),
                   jax.ShapeDtypeStruct((TOP_K, n), I32),
                   jax.ShapeDtypeStruct((TOP_K, n), F32)],
        grid=(n // tm,),
        in_specs=[row(D_MODEL), row(A_WIDTH), row(B_WIDTH), full(wg), full(wba), full(wbb),
                  full(wo), full(g1), full(b1), full(wr_t), full(br)],
        out_specs=[row(D_MODEL), col, col],
        compiler_params=_params(("parallel",)),
        name="merge",
    )(xf, ya, yb, wg, wba, wbb, wo, g1, b1, wr_t, br)


def _route(topi, gates, n):
    blk = MOE_BLOCK
    n_assign = n * TOP_K
    cap = n_assign + N_EXPERTS * blk
    nblk = cap // blk
    e_flat = topi.T.reshape(-1)
    g_flat = gates.T.reshape(-1)
    order = jnp.argsort(e_flat).astype(I32)
    counts = jnp.sum(e_flat[:, None] == jnp.arange(N_EXPERTS, dtype=I32)[None, :], axis=0, dtype=I32)
    offsets = jnp.cumsum(counts) - counts
    padded = (counts + blk - 1) // blk * blk
    pends = jnp.cumsum(padded)
    pstarts = pends - padded
    blk_expert = jnp.minimum(
        jnp.searchsorted(pends, jnp.arange(nblk, dtype=I32) * blk, side="right"), N_EXPERTS - 1).astype(I32)
    p = jnp.arange(cap, dtype=I32)
    e_p = blk_expert[p // blk]
    j = p - pstarts[e_p]
    valid = j < counts[e_p]
    a = order[jnp.clip(offsets[e_p] + j, 0, n_assign - 1)]
    tok = jnp.where(valid, a // TOP_K, 0)
    dst = jnp.where(valid, a, n_assign + p - (offsets[e_p] + counts[e_p]))
    gate = jnp.where(valid, g_flat[a], 0.0)
    idx = jnp.stack([tok.reshape(nblk, blk), dst.reshape(nblk, blk)], axis=1).astype(I32)
    return blk_expert, idx, gate.reshape(cap, 1)


def _moe_kernel(be_ref, idx_hbm, gate_ref, x_hbm, wg_ref, bg_ref, wu_ref, bu_ref, wd_ref, bd_ref,
                y_hbm, idx_smem, xbuf, ybuf, sem_idx, sem_g, sem_s):
    del be_ref
    i = pl.program_id(0)
    n = pl.num_programs(0)
    blk = MOE_BLOCK

    def idx_copy(b, slot):
        return pltpu.make_async_copy(idx_hbm.at[b], idx_smem.at[slot], sem_idx.at[slot])

    def gather_rows(islot, xslot):
        def body(r, c):
            tok = idx_smem[islot, 0, r]
            pltpu.make_async_copy(x_hbm.at[pl.ds(tok, 1)], xbuf.at[xslot, pl.ds(r, 1)],
                                  sem_g.at[xslot]).start()
            return c
        lax.fori_loop(0, blk, body, 0, unroll=8)

    def gather_wait(xslot):
        pltpu.make_async_copy(x_hbm.at[pl.ds(0, blk)], xbuf.at[xslot], sem_g.at[xslot]).wait()

    def scatter_rows(islot, yslot):
        def body(r, c):
            dst = idx_smem[islot, 1, r]
            pltpu.make_async_copy(ybuf.at[yslot, pl.ds(r, 1)], y_hbm.at[pl.ds(dst, 1)],
                                  sem_s.at[yslot]).start()
            return c
        lax.fori_loop(0, blk, body, 0, unroll=8)

    def scatter_wait(yslot):
        pltpu.make_async_copy(ybuf.at[yslot], y_hbm.at[pl.ds(0, blk)], sem_s.at[yslot]).wait()

    @pl.when(i == 0)
    def _():
        c = idx_copy(0, 0)
        c.start()
        c.wait()
        gather_rows(0, 0)

        @pl.when(n > 1)
        def _():
            idx_copy(1, 1).start()

    @pl.when(i + 1 < n)
    def _():
        idx_copy(i + 1, (i + 1) % 3).wait()
        gather_rows((i + 1) % 3, (i + 1) % 2)

    @pl.when(i + 2 < n)
    def _():
        idx_copy(i + 2, (i + 2) % 3).start()

    slot = i % 2
    gather_wait(slot)
    xb = xbuf[slot].astype(BF16)
    hg = jnp.minimum(jnp.dot(xb, wg_ref[0], preferred_element_type=F32) + bg_ref[0], SWIGLU_LIMIT)
    hu = jnp.clip(jnp.dot(xb, wu_ref[0], preferred_element_type=F32) + bu_ref[0],
                  -SWIGLU_LIMIT, SWIGLU_LIMIT)
    h = (hu + 1.0) * (hg * jax.nn.sigmoid(SWIGLU_ALPHA * hg))
    y = (jnp.dot(h.astype(BF16), wd_ref[0], preferred_element_type=F32) + bd_ref[0]) * gate_ref[...]

    @pl.when(i >= 2)
    def _():
        scatter_wait(slot)

    ybuf[slot] = y
    scatter_rows(i % 3, slot)

    @pl.when(i == n - 1)
    def _():
        scatter_wait(slot)

        @pl.when(n > 1)
        def _():
            scatter_wait(1 - slot)


def _moe(blk_expert, idx, gate, x1, wg, bg, wu, bu, wd, bd):
    nblk = idx.shape[0]
    blk = MOE_BLOCK
    cap = nblk * blk
    wspec = lambda a: pl.BlockSpec((1,) + a.shape[1:], lambda i, be: (be[i], 0, 0))
    any_spec = pl.BlockSpec(memory_space=pl.ANY)
    return pl.pallas_call(
        _moe_kernel,
        out_shape=jax.ShapeDtypeStruct((cap, D_MODEL), F32),
        grid_spec=pltpu.PrefetchScalarGridSpec(
            num_scalar_prefetch=1,
            grid=(nblk,),
            in_specs=[any_spec, pl.BlockSpec((blk, 1), lambda i, be: (i, 0)), any_spec,
                      wspec(wg), wspec(bg), wspec(wu), wspec(bu), wspec(wd), wspec(bd)],
            out_specs=any_spec,
            scratch_shapes=[
                pltpu.SMEM((3, 2, blk), I32),
                pltpu.VMEM((2, blk, D_MODEL), F32),
                pltpu.VMEM((2, blk, D_MODEL), F32),
                pltpu.SemaphoreType.DMA((3,)),
                pltpu.SemaphoreType.DMA((2,)),
                pltpu.SemaphoreType.DMA((2,)),
            ],
        ),
        compiler_params=_params(("arbitrary",)),
        name="moe",
    )(blk_expert, idx, gate, x1, wg, bg, wu, bu, wd, bd)


def _final_kernel(x1_ref, y_ref, g_ref, b_ref, o_ref):
    y = y_ref[...]
    ffn = (y[:, :D_MODEL] + y[:, D_MODEL:2 * D_MODEL]) + (y[:, 2 * D_MODEL:3 * D_MODEL] + y[:, 3 * D_MODEL:])
    o_ref[...] = _layer_norm(DN_ALPHA * x1_ref[...] + ffn, g_ref[...], b_ref[...])


def _final(x1, y4, g2, b2, tm):
    n = x1.shape[0]
    full = lambda a: pl.BlockSpec(a.shape, lambda i: (0,) * a.ndim)
    return pl.pallas_call(
        _final_kernel,
        out_shape=jax.ShapeDtypeStruct((n, D_MODEL), F32),
        grid=(n // tm,),
        in_specs=[pl.BlockSpec((tm, D_MODEL), lambda i: (i, 0)),
                  pl.BlockSpec((tm, TOP_K * D_MODEL), lambda i: (i, 0)), full(g2), full(b2)],
        out_specs=pl.BlockSpec((tm, D_MODEL), lambda i: (i, 0)),
        compiler_params=_params(("parallel",)),
        name="final",
    )(x1, y4, g2, b2)


def _mix_weight(w_in):
    qa, ckv, qi, ki, wi, qb, kb, vb, _, _ = jnp.split(w_in, np.cumsum(SPLITS)[:-1].tolist(), axis=-1)
    pad = lambda a: jnp.pad(a, ((0, 0), (0, LANES - a.shape[1])))
    return jnp.concatenate([qa, qi, ckv, pad(ki), pad(wi), qb, kb, vb], axis=1).astype(BF16)


def _layer(x, w_in, kv_norm_g, idx_k_norm_g, idx_k_norm_b, w_uk, w_uv, rel_bias, w_branch_a,
           w_branch_b, w_out, ln1_g, ln1_b, w_router, b_router, w_gate, b_gate, w_up, b_up,
           w_down, b_down, ln2_g, ln2_b):
    bsz, seq, _ = x.shape
    n = bsz * seq
    assert seq % (2 * KEY_TILE) == 0 and n % 512 == 0
    xf = x.reshape(n, D_MODEL)
    row = lambda v: v.reshape(1, -1).astype(F32)

    w_mix = _mix_weight(w_in)
    w_gates = w_in[:, sum(SPLITS[:8]):].astype(BF16)
    wuk_bd = jnp.zeros((A_WIDTH, A_HEADS * A_KV_RANK), F32)
    wuv_pad = jnp.zeros((A_HEADS, A_KV_RANK, LANES), F32)
    for h in range(A_HEADS):
        wuk_bd = wuk_bd.at[h * A_HEAD_DIM:(h + 1) * A_HEAD_DIM, h * A_KV_RANK:(h + 1) * A_KV_RANK].set(w_uk[h])
        c0 = (h % 2) * A_HEAD_DIM
        wuv_pad = wuv_pad.at[h, :, c0:c0 + A_HEAD_DIM].set(w_uv[h])
    wuk_bd, wuv_pad = wuk_bd.astype(BF16), wuv_pad.astype(BF16)

    qlat, qi, ckv, ki, wi, qb, kb, vb = _projection(
        xf, w_mix, wuk_bd, row(kv_norm_g), row(idx_k_norm_g), row(idx_k_norm_b), tm=512)

    nqb, nkt = seq // Q_BLOCK, seq // KEY_TILE
    kidx = ki.reshape(bsz, nkt, KEY_TILE, IDX_DIM)
    ckv4 = ckv.reshape(bsz, nkt, KEY_TILE, A_KV_RANK)
    ckvt = ckv4.transpose(0, 1, 3, 2)
    qi_r = qi.reshape(bsz, nqb, Q_BLOCK, IDX_HEADS, IDX_DIM).transpose(0, 1, 3, 2, 4).reshape(
        bsz, nqb, IDX_HEADS * Q_BLOCK, IDX_DIM)
    wit = wi.reshape(bsz, seq, IDX_HEADS).transpose(0, 2, 1)
    ya = _mixer_a(kidx, ckv4, ckvt, qi_r, wit, qlat.reshape(bsz, seq, -1), wuv_pad, bsz, seq)

    front = lambda a: jnp.pad(a.reshape(bsz, seq, B_WIDTH), ((0, 0), (B_PAD, 0), (0, 0)))
    yb = _mixer_b(qb.reshape(bsz, seq, B_WIDTH), front(kb), front(vb), _band_bias(rel_bias), bsz, seq)

    x1, topi, gates = _merge(
        xf, ya.reshape(n, A_WIDTH), yb.reshape(n, B_WIDTH), w_gates, w_branch_a.astype(BF16),
        w_branch_b.astype(BF16), w_out.astype(BF16), row(ln1_g), row(ln1_b),
        w_router.T.astype(BF16), b_router.reshape(-1, 1).astype(F32), tm=512)

    blk_expert, idx, gate = _route(topi, gates, n)
    y = _moe(blk_expert, idx, gate, x1, w_gate.astype(BF16), b_gate[:, None, :], w_up.astype(BF16),
             b_up[:, None, :], w_down.astype(BF16), b_down[:, None, :])
    y4 = y.reshape(-1, TOP_K * D_MODEL)
    out = _final(x1, y4, row(ln2_g), row(ln2_b), tm=256)
    return out.reshape(bsz, seq, D_MODEL)


def kernel(x, w_in, kv_norm_g, idx_k_norm_g, idx_k_norm_b, w_uk, w_uv, rel_bias, w_branch_a,
           w_branch_b, w_out, ln1_g, ln1_b, w_router, b_router, w_gate, b_gate, w_up, b_up,
           w_down, b_down, ln2_g, ln2_b):
    for l in range(DEPTH):
        x = _layer(x, w_in[l], kv_norm_g[l], idx_k_norm_g[l], idx_k_norm_b[l], w_uk[l], w_uv[l],
                   rel_bias[l], w_branch_a[l], w_branch_b[l], w_out[l], ln1_g[l], ln1_b[l],
                   w_router[l], b_router[l], w_gate[l], b_gate[l], w_up[l], b_up[l], w_down[l],
                   b_down[l], ln2_g[l], ln2_b[l])
    return x
```

```python
import functools

import jax
import jax.numpy as jnp
import numpy as np
from jax import lax
from jax.experimental import pallas as pl
from jax.experimental.pallas import tpu as pltpu

F32 = jnp.float32
BF16 = jnp.bfloat16
I32 = jnp.int32

D_MODEL = 1024
CHUNK = 64
Q_BLOCK = 128
A_HEADS = 8
A_HEAD_DIM = 64
A_KV_RANK = 128
IDX_HEADS = 8
IDX_DIM = 64
IDX_TOPK = 256
IDX_SCALE = (IDX_HEADS * IDX_DIM) ** -0.5
B_HEADS = 8
B_HEAD_DIM = 64
B_LEFT_CHUNKS = 8
REL_CLIP = 128
N_EXPERTS = 32
TOP_K = 4
D_FF = 1024
SWIGLU_LIMIT = 7.0
SWIGLU_ALPHA = 1.702
DEPTH = 1
DN_ALPHA = (2.0 * DEPTH) ** 0.25
LN_EPS = 1e-5

A_WIDTH = A_HEADS * A_HEAD_DIM
B_WIDTH = B_HEADS * B_HEAD_DIM
SPLITS = [A_WIDTH, A_KV_RANK, IDX_HEADS * IDX_DIM, IDX_DIM, IDX_HEADS,
          B_WIDTH, B_WIDTH, B_WIDTH, D_MODEL, D_MODEL]

LANES = 128
SUBLANES = 8
KEY_TILE = 256
B_QTILE = 256
B_PAD = B_LEFT_CHUNKS * CHUNK
B_WIN = B_PAD + B_QTILE
TOKEN_TILE = 512
MOE_BLOCK = 512
FINAL_TILE = 256
ROW_TILES = D_MODEL // LANES
DMA_GROUP = 4
INT_MIN = -2 ** 31
VMEM_LIMIT = 56 * 1024 * 1024

A_SLOPES = [2.0 ** (-8.0 * (h + 1) / A_HEADS) for h in range(A_HEADS)]

_C_QA = 0
_C_QI = _C_QA + A_WIDTH
_C_CKV = _C_QI + IDX_HEADS * IDX_DIM
_C_KI = _C_CKV + A_KV_RANK
_C_WI = _C_KI + LANES
_C_QKVB = _C_WI + LANES
_C_END = _C_QKVB + 3 * B_WIDTH


def _params(sem, vmem=VMEM_LIMIT):
    return pltpu.CompilerParams(dimension_semantics=sem, vmem_limit_bytes=vmem)


def _full(a):
    return pl.BlockSpec(a.shape, lambda *_: (0,) * a.ndim)


def _proj_kernel(x_ref, w_ref, wuk_ref, kvg_ref, kig_ref, kib_ref,
                 qlat_ref, qi_ref, ckv_ref, ki_ref, wi_ref, qb_ref, kb_ref, vb_ref):
    xb = x_ref[...].astype(BF16)

    def mm(c0, c1):
        return jnp.dot(xb, w_ref[:, c0:c1], preferred_element_type=F32)

    qa = mm(_C_QA, _C_QI)
    qlat = jnp.dot(qa.astype(BF16), wuk_ref[...], preferred_element_type=F32)
    qlat_ref[...] = (qlat * (A_HEAD_DIM ** -0.5)).astype(BF16)
    qi_ref[...] = mm(_C_QI, _C_CKV).astype(BF16)

    ckv = mm(_C_CKV, _C_KI)
    ms = jnp.mean(ckv * ckv, axis=-1, keepdims=True)
    ckv_ref[...] = ((ckv * lax.rsqrt(ms + LN_EPS)) * kvg_ref[...]).astype(BF16)

    ki = mm(_C_KI, _C_WI)[:, :IDX_DIM]
    mu = jnp.mean(ki, axis=-1, keepdims=True)
    var = jnp.mean(jnp.square(ki - mu), axis=-1, keepdims=True)
    ki_ref[...] = (((ki - mu) * lax.rsqrt(var + LN_EPS)) * kig_ref[...] + kib_ref[...]).astype(BF16)

    wi_ref[...] = mm(_C_WI, _C_QKVB)[:, :IDX_HEADS]
    qb_ref[...] = mm(_C_QKVB, _C_QKVB + B_WIDTH).astype(BF16)
    kb_ref[...] = mm(_C_QKVB + B_WIDTH, _C_QKVB + 2 * B_WIDTH).astype(BF16)
    vb_ref[...] = mm(_C_QKVB + 2 * B_WIDTH, _C_END).astype(BF16)


def _projection(xf, w_mix, wuk_bd, kvg, kig, kib):
    n, tm = xf.shape[0], TOKEN_TILE
    row = lambda w: pl.BlockSpec((tm, w), lambda i: (i, 0))
    outs = [(A_HEADS * A_KV_RANK, BF16), (IDX_HEADS * IDX_DIM, BF16), (A_KV_RANK, BF16),
            (IDX_DIM, BF16), (IDX_HEADS, F32), (B_WIDTH, BF16), (B_WIDTH, BF16), (B_WIDTH, BF16)]
    return pl.pallas_call(
        _proj_kernel,
        out_shape=[jax.ShapeDtypeStruct((n, w), dt) for w, dt in outs],
        grid=(n // tm,),
        in_specs=[row(D_MODEL), _full(w_mix), _full(wuk_bd), _full(kvg), _full(kig), _full(kib)],
        out_specs=[row(w) for w, _ in outs],
        compiler_params=_params(("parallel",)),
        name="proj",
    )(xf, w_mix, wuk_bd, kvg, kig, kib)


def _mixer_a_kernel(kidx_ref, ckv_ref, ckvt_ref, qi_ref, wit_ref, qlat_ref, wuv_ref,
                    ya_ref, keys_ref, s_ref, acc_ref, *, seq, topk):
    i = pl.program_id(1)
    nkt = i // 2 + 1
    nsub = KEY_TILE // SUBLANES
    lane = lax.broadcasted_iota(I32, (1, LANES), 1)
    qpos = i * Q_BLOCK + lane
    qchunk = qpos >> 6
    row_iota = lax.broadcasted_iota(I32, (KEY_TILE, LANES), 0)

    qi_blk = qi_ref[0, 0]
    w_all = wit_ref[0] * IDX_SCALE

    def idx_body(kt, carry):
        lg = pl.dot(kidx_ref[0, kt], qi_blk, trans_b=True)
        isc = jnp.zeros((KEY_TILE, LANES), F32)
        for h in range(IDX_HEADS):
            isc = isc + jnp.maximum(lg[:, h * LANES:(h + 1) * LANES], 0.0) * w_all[h:h + 1, :]
        bits = lax.bitcast_convert_type(isc + 0.0, I32)
        key = bits ^ ((bits >> 31) & 0x7FFFFFFF)
        spos = kt * KEY_TILE + row_iota
        key = jnp.where((spos >> 6) <= qchunk, key, INT_MIN)
        keys_ref[kt] = key
        return carry

    lax.fori_loop(0, nkt, idx_body, 0)

    def count(pred):
        def body(kt, acc):
            spos = kt * KEY_TILE + row_iota
            m = jnp.where(pred(keys_ref[kt], spos), 1, 0).astype(I32)
            return acc + jnp.sum(m.reshape(nsub, SUBLANES, LANES), axis=0)
        acc = lax.fori_loop(0, nkt, body, jnp.zeros((SUBLANES, LANES), I32))
        return jnp.sum(acc, axis=0, keepdims=True)

    def bit_body(it, carry):
        thr, cge = carry
        cand = thr ^ lax.shift_left(jnp.int32(1), 31 - it)
        cnt = count(lambda kv, spos: kv >= cand)
        ok = cnt >= topk
        return jnp.where(ok, cand, thr), jnp.where(ok, cnt, cge)

    thr0 = jnp.full((1, LANES), INT_MIN, I32)
    thr, cge = lax.fori_loop(0, 32, bit_body, (thr0, jnp.zeros((1, LANES), I32)))

    nbits = int(np.log2(seq)) + 1
    has_thr = thr > INT_MIN
    need_tie = jnp.max(jnp.where(has_thr, cge, 0)) > topk

    def tie_fn():
        room = topk - count(lambda kv, spos: kv > thr)

        def jb(it, cut):
            cand = cut | lax.shift_left(jnp.int32(1), nbits - 1 - it)
            f = count(lambda kv, spos: (kv == thr) & (spos < cand))
            return jnp.where(f <= room, cand, cut)
        return lax.fori_loop(0, nbits, jb, jnp.zeros((1, LANES), I32))

    cut = lax.cond(need_tie, tie_fn, lambda: jnp.full((1, LANES), 2 * seq, I32))
    cut = jnp.where(has_thr, cut, 0)

    qlat = qlat_ref[0]
    qr = jnp.concatenate([qlat[:, h * LANES:(h + 1) * LANES] for h in range(A_HEADS)], axis=0)

    def sc_body(kt, m):
        st = pl.dot(ckv_ref[0, kt], qr, trans_b=True)
        kv = keys_ref[kt]
        spos = kt * KEY_TILE + row_iota
        sel = (kv > thr) | ((kv == thr) & (spos < cut))
        dist = jnp.abs(qpos - spos).astype(F32)
        out = []
        for h in range(A_HEADS):
            sh = st[:, h * LANES:(h + 1) * LANES] - A_SLOPES[h] * dist
            sh = jnp.where(sel, sh, -jnp.inf)
            s_ref[kt, :, h * LANES:(h + 1) * LANES] = sh
            out.append(jnp.maximum(m[h], jnp.max(sh.reshape(nsub, SUBLANES, LANES), axis=0)))
        return tuple(out)

    m8 = lax.fori_loop(0, nkt, sc_body,
                       tuple(jnp.full((SUBLANES, LANES), -jnp.inf, F32) for _ in range(A_HEADS)))
    mx = [jnp.max(m, axis=0, keepdims=True) for m in m8]

    acc_ref[...] = jnp.zeros_like(acc_ref)

    def pv_body(kt, l):
        out, ps = [], []
        for h in range(A_HEADS):
            p = jnp.exp(s_ref[kt, :, h * LANES:(h + 1) * LANES] - mx[h])
            out.append(l[h] + jnp.sum(p.reshape(nsub, SUBLANES, LANES), axis=0))
            ps.append(p.astype(BF16))
        acc_ref[...] += jnp.dot(ckvt_ref[0, kt], jnp.concatenate(ps, axis=1),
                                preferred_element_type=F32)
        return tuple(out)

    l8 = lax.fori_loop(0, nkt, pv_body,
                       tuple(jnp.zeros((SUBLANES, LANES), F32) for _ in range(A_HEADS)))
    inv = [1.0 / jnp.sum(l, axis=0, keepdims=True) for l in l8]

    for j in range(A_HEADS // 2):
        y = None
        for h in (2 * j, 2 * j + 1):
            o = (acc_ref[:, h * LANES:(h + 1) * LANES] * inv[h]).astype(BF16)
            t = pl.dot(o, wuv_ref[h], trans_a=True)
            y = t if y is None else y + t
        ya_ref[0, :, j * LANES:(j + 1) * LANES] = y.astype(BF16)


def _mixer_a(kidx, ckv, ckvt, qi_r, wit, qlat, wuv_pad, bsz, seq):
    nqb = seq // Q_BLOCK
    nkt = seq // KEY_TILE
    topk = min(IDX_TOPK, seq // 4)
    kern = functools.partial(_mixer_a_kernel, seq=seq, topk=topk)
    return pl.pallas_call(
        kern,
        out_shape=jax.ShapeDtypeStruct((bsz, seq, A_WIDTH), BF16),
        grid=(bsz, nqb),
        in_specs=[
            pl.BlockSpec((1, nkt, KEY_TILE, IDX_DIM), lambda b, i: (b, 0, 0, 0)),
            pl.BlockSpec((1, nkt, KEY_TILE, A_KV_RANK), lambda b, i: (b, 0, 0, 0)),
            pl.BlockSpec((1, nkt, A_KV_RANK, KEY_TILE), lambda b, i: (b, 0, 0, 0)),
            pl.BlockSpec((1, 1, IDX_HEADS * Q_BLOCK, IDX_DIM), lambda b, i: (b, i, 0, 0)),
            pl.BlockSpec((1, IDX_HEADS, Q_BLOCK), lambda b, i: (b, 0, i)),
            pl.BlockSpec((1, Q_BLOCK, A_HEADS * A_KV_RANK), lambda b, i: (b, i, 0)),
            pl.BlockSpec(wuv_pad.shape, lambda b, i: (0, 0, 0)),
        ],
        out_specs=pl.BlockSpec((1, Q_BLOCK, A_WIDTH), lambda b, i: (b, i, 0)),
        scratch_shapes=[
            pltpu.VMEM((nkt, KEY_TILE, LANES), I32),
            pltpu.VMEM((nkt, KEY_TILE, A_HEADS * LANES), F32),
            pltpu.VMEM((A_KV_RANK, A_HEADS * LANES), F32),
        ],
        compiler_params=_params(("parallel", "arbitrary")),
        name="mixer_a",
    )(kidx, ckv, ckvt, qi_r, wit, qlat, wuv_pad)


def _mixer_b_kernel(q_ref, k_ref, v_ref, bias_ref, yb_ref):
    t0 = pl.multiple_of(pl.program_id(1) * B_QTILE, B_QTILE)
    kw = k_ref[0, pl.ds(t0, B_WIN), :]
    vw = v_ref[0, pl.ds(t0, B_WIN), :]
    q = q_ref[0]
    kvalid = (t0 - B_PAD + lax.broadcasted_iota(I32, (1, B_WIN), 1)) >= 0
    lane = lax.broadcasted_iota(I32, (1, LANES), 1)
    scale = B_HEAD_DIM ** -0.5
    for j in range(B_HEADS // 2):
        sl = slice(j * LANES, (j + 1) * LANES)
        qs, ks, vs = q[:, sl], kw[:, sl], vw[:, sl]
        outs = []
        for hh in range(2):
            mine = (lane >= B_HEAD_DIM) if hh else (lane < B_HEAD_DIM)
            qm = jnp.where(mine, qs, jnp.zeros_like(qs))
            s = pl.dot(qm, ks, trans_b=True) * scale + bias_ref[2 * j + hh]
            s = jnp.where(kvalid, s, -jnp.inf)
            m = jnp.max(s, axis=-1, keepdims=True)
            p = jnp.exp(s - m)
            l = jnp.sum(p, axis=-1, keepdims=True)
            outs.append(jnp.dot(p.astype(BF16), vs, preferred_element_type=F32) / l)
        yb_ref[0, :, sl] = jnp.where(lane < B_HEAD_DIM, outs[0], outs[1]).astype(BF16)


def _band_bias(rel_bias):
    tq = np.arange(B_QTILE)[:, None]
    j = np.arange(B_WIN)[None, :]
    rel = np.clip(tq + B_PAD - j, -REL_CLIP, REL_CLIP) + REL_CLIP
    band = (j // CHUNK >= tq // CHUNK) & (j // CHUNK <= tq // CHUNK + B_LEFT_CHUNKS)
    return jnp.where(jnp.asarray(band)[None], rel_bias[:, rel].astype(F32), -jnp.inf)


def _mixer_b(qb, kb_pad, vb_pad, bias, bsz, seq):
    return pl.pallas_call(
        _mixer_b_kernel,
        out_shape=jax.ShapeDtypeStruct((bsz, seq, B_WIDTH), BF16),
        grid=(bsz, seq // B_QTILE),
        in_specs=[
            pl.BlockSpec((1, B_QTILE, B_WIDTH), lambda b, i: (b, i, 0)),
            pl.BlockSpec((1, seq + B_PAD, B_WIDTH), lambda b, i: (b, 0, 0)),
            pl.BlockSpec((1, seq + B_PAD, B_WIDTH), lambda b, i: (b, 0, 0)),
            pl.BlockSpec(bias.shape, lambda b, i: (0, 0, 0)),
        ],
        out_specs=pl.BlockSpec((1, B_QTILE, B_WIDTH), lambda b, i: (b, i, 0)),
        compiler_params=_params(("parallel", "arbitrary")),
        name="mixer_b",
    )(qb, kb_pad, vb_pad, bias)


def _layer_norm(z, g, b):
    mu = jnp.mean(z, axis=-1, keepdims=True)
    var = jnp.mean(jnp.square(z - mu), axis=-1, keepdims=True)
    return ((z - mu) * lax.rsqrt(var + LN_EPS)) * g + b


def _merge_kernel(x_ref, ya_ref, yb_ref, wg_ref, wba_ref, wbb_ref, wo_ref, g_ref, b_ref,
                  wr_ref, br_ref, x1t_ref, topi_ref, gate_ref):
    tm = x_ref.shape[0]
    x = x_ref[...]
    xb = x.astype(BF16)
    ga = jnp.dot(xb, wg_ref[:, :D_MODEL], preferred_element_type=F32)
    a = jnp.dot(ya_ref[...], wba_ref[...], preferred_element_type=F32)
    merged = jax.nn.sigmoid(ga) * a
    gb = jnp.dot(xb, wg_ref[:, D_MODEL:], preferred_element_type=F32)
    b = jnp.dot(yb_ref[...], wbb_ref[...], preferred_element_type=F32)
    merged = merged + jax.nn.sigmoid(gb) * b
    out = jnp.dot(merged.astype(BF16), wo_ref[...], preferred_element_type=F32)
    x1 = _layer_norm(DN_ALPHA * x + out, g_ref[...], b_ref[...])
    for c in range(ROW_TILES):
        x1t_ref[pl.ds(c, tm, stride=ROW_TILES), :] = x1[:, c * LANES:(c + 1) * LANES]

    lg = pl.dot(wr_ref[...], x1.astype(BF16), trans_b=True) + br_ref[...]
    eidx = lax.broadcasted_iota(I32, lg.shape, 0)
    vals, idxs = [], []
    for _ in range(TOP_K):
        m = jnp.max(lg, axis=0, keepdims=True)
        sel = jnp.min(jnp.where(lg == m, eidx, N_EXPERTS), axis=0, keepdims=True)
        vals.append(m)
        idxs.append(sel)
        lg = jnp.where(eidx == sel, -jnp.inf, lg)
    es = [jnp.exp(v - vals[0]) for v in vals]
    tot = es[0] + es[1] + es[2] + es[3]
    topi_ref[...] = jnp.concatenate(idxs, axis=0)
    gate_ref[...] = jnp.concatenate([e / tot for e in es], axis=0)


def _merge(xf, ya, yb, wg, wba, wbb, wo, g1, b1, wr_t, br):
    n, tm = xf.shape[0], TOKEN_TILE
    row = lambda w: pl.BlockSpec((tm, w), lambda i: (i, 0))
    col = pl.BlockSpec((TOP_K, tm), lambda i: (0, i))
    return pl.pallas_call(
        _merge_kernel,
        out_shape=[jax.ShapeDtypeStruct((n * ROW_TILES, LANES), F32),
                   jax.ShapeDtypeStruct((TOP_K, n), I32),
                   jax.ShapeDtypeStruct((TOP_K, n), F32)],
        grid=(n // tm,),
        in_specs=[row(D_MODEL), row(A_WIDTH), row(B_WIDTH), _full(wg), _full(wba), _full(wbb),
                  _full(wo), _full(g1), _full(b1), _full(wr_t), _full(br)],
        out_specs=[pl.BlockSpec((tm * ROW_TILES, LANES), lambda i: (i, 0)), col, col],
        compiler_params=_params(("parallel",)),
        name="merge",
    )(xf, ya, yb, wg, wba, wbb, wo, g1, b1, wr_t, br)


def _one_hot_rows(topi, k, width):
    eidx = lax.broadcasted_iota(I32, (N_EXPERTS, width), 0)
    return eidx == topi[k:k + 1, :]


def _route_kernel(topi_ref, tri_ref, rank_ref, cnt_ref, run_ref):
    @pl.when(pl.program_id(0) == 0)
    def _():
        run_ref[...] = jnp.zeros_like(run_ref)

    topi = topi_ref[...]
    tr = topi.shape[1]
    hot = [_one_hot_rows(topi, k, tr) for k in range(TOP_K)]
    oh = jnp.concatenate([jnp.where(h, 1.0, 0.0).astype(BF16) for h in hot], axis=0)
    prefix = jnp.dot(oh, tri_ref[...], preferred_element_type=F32)
    base = run_ref[...]
    ranks = []
    for k in range(TOP_K):
        tbl = prefix[k * N_EXPERTS:(k + 1) * N_EXPERTS, :] + base
        ranks.append(jnp.sum(jnp.where(hot[k], tbl, 0.0), axis=0, keepdims=True))
        base = base + jnp.sum(jnp.where(hot[k], 1.0, 0.0), axis=1, keepdims=True)
    rank_ref[...] = jnp.concatenate(ranks, axis=0).astype(I32)
    run_ref[...] = base
    cnt_ref[...] = base.astype(I32)


def _route(topi):
    n, tr = topi.shape[1], TOKEN_TILE
    tri = (np.arange(tr)[:, None] < np.arange(tr)[None, :]).astype(np.float32)
    tri = jnp.asarray(tri, BF16)
    col = pl.BlockSpec((TOP_K, tr), lambda i: (0, i))
    return pl.pallas_call(
        _route_kernel,
        out_shape=[jax.ShapeDtypeStruct((TOP_K, n), I32), jax.ShapeDtypeStruct((N_EXPERTS, 1), I32)],
        grid=(n // tr,),
        in_specs=[col, _full(tri)],
        out_specs=[col, pl.BlockSpec((N_EXPERTS, 1), lambda i: (0, 0))],
        scratch_shapes=[pltpu.VMEM((N_EXPERTS, 1), F32)],
        compiler_params=_params(("arbitrary",)),
        name="route",
    )(topi, tri)


def _positions_kernel(topi_ref, rank_ref, pstv_ref, pos_ref):
    topi = topi_ref[...]
    rows = []
    for k in range(TOP_K):
        start = jnp.sum(jnp.where(_one_hot_rows(topi, k, topi.shape[1]), pstv_ref[...], 0),
                        axis=0, keepdims=True)
        rows.append(rank_ref[k:k + 1, :] + start)
    pos_ref[...] = jnp.concatenate(rows, axis=0)


def _positions(topi, rank, pstarts):
    n, td = topi.shape[1], TOKEN_TILE
    col = pl.BlockSpec((TOP_K, td), lambda i: (0, i))
    pstv = pstarts.reshape(N_EXPERTS, 1)
    return pl.pallas_call(
        _positions_kernel,
        out_shape=jax.ShapeDtypeStruct((TOP_K, n), I32),
        grid=(n // td,),
        in_specs=[col, col, _full(pstv)],
        out_specs=col,
        compiler_params=_params(("parallel",)),
        name="positions",
    )(topi, rank, pstv)


def _dispatch_kernel(cnt_ref, pst_ref, pos_hbm, x_hbm, xs_hbm, pos_smem, zbuf, sem_p, sem_d, sem_z):
    i = pl.program_id(0)
    n = pl.num_programs(0)
    ng = pos_smem.shape[1]
    td = ng * DMA_GROUP

    def pos_copy(step, slot):
        return pltpu.make_async_copy(pos_hbm.at[pl.ds(step * ng, ng)], pos_smem.at[slot], sem_p.at[slot])

    @pl.when(i == 0)
    def _():
        pos_copy(0, 0).start()

    @pl.when(i + 1 < n)
    def _():
        pos_copy(i + 1, (i + 1) % 2).start()

    slot = i % 2
    pos_copy(i, slot).wait()

    def step_bytes():
        rows_ = td * TOP_K * ROW_TILES
        return pltpu.make_async_copy(x_hbm.at[pl.ds(0, rows_)], xs_hbm.at[pl.ds(0, rows_)], sem_d)

    def body(g, c):
        dst = [pos_smem[slot, g, j] * ROW_TILES for j in range(DMA_GROUP * TOP_K)]
        for u in range(DMA_GROUP):
            src = x_hbm.at[pl.ds((i * td + g * DMA_GROUP + u) * ROW_TILES, ROW_TILES)]
            for k in range(TOP_K):
                pltpu.make_async_copy(src, xs_hbm.at[pl.ds(dst[u * TOP_K + k], ROW_TILES)], sem_d).start()
        return c
    lax.fori_loop(0, ng, body, 0)

    @pl.when(i > 0)
    def _():
        step_bytes().wait()

    @pl.when(i == n - 1)
    def _():
        step_bytes().wait()
        zbuf[...] = jnp.zeros_like(zbuf)

        def zero_copy(r):
            return pltpu.make_async_copy(zbuf, xs_hbm.at[pl.ds(r * ROW_TILES, ROW_TILES)], sem_z)

        def ebody(e, c):
            lo = pst_ref[e] + cnt_ref[e]
            hi = pst_ref[e] + (cnt_ref[e] + MOE_BLOCK - 1) // MOE_BLOCK * MOE_BLOCK

            def start(r, c2):
                zero_copy(r).start()
                return c2

            def wait(r, c2):
                zero_copy(r).wait()
                return c2
            lax.fori_loop(lo, hi, start, 0)
            lax.fori_loop(lo, hi, wait, 0)
            return c
        lax.fori_loop(0, N_EXPERTS, ebody, 0)


def _dispatch(counts, pstarts, pos_g, x1t, cap):
    n, td = pos_g.shape[0] * DMA_GROUP, TOKEN_TILE
    any_spec = pl.BlockSpec(memory_space=pl.ANY)
    return pl.pallas_call(
        _dispatch_kernel,
        out_shape=jax.ShapeDtypeStruct((cap * ROW_TILES, LANES), F32),
        grid_spec=pltpu.PrefetchScalarGridSpec(
            num_scalar_prefetch=2,
            grid=(n // td,),
            in_specs=[any_spec, any_spec],
            out_specs=any_spec,
            scratch_shapes=[
                pltpu.SMEM((2, td // DMA_GROUP, DMA_GROUP * TOP_K), I32),
                pltpu.VMEM((ROW_TILES, LANES), F32),
                pltpu.SemaphoreType.DMA((2,)),
                pltpu.SemaphoreType.DMA,
                pltpu.SemaphoreType.DMA,
            ],
        ),
        compiler_params=_params(("arbitrary",)),
        name="dispatch",
    )(counts, pstarts, pos_g, x1t)


def _moe_kernel(be_ref, nu_ref, xs_ref, wg_ref, bg_ref, wu_ref, bu_ref, wd_ref, bd_ref, ys_ref):
    del be_ref
    blk = MOE_BLOCK
    used = pl.program_id(0) < nu_ref[0]

    @pl.when(used)
    def _():
        xb = jnp.concatenate([xs_ref[pl.ds(c, blk, stride=ROW_TILES), :].astype(BF16)
                              for c in range(ROW_TILES)], axis=1)
        hg = jnp.minimum(jnp.dot(xb, wg_ref[0], preferred_element_type=F32) + bg_ref[0], SWIGLU_LIMIT)
        hu = jnp.clip(jnp.dot(xb, wu_ref[0], preferred_element_type=F32) + bu_ref[0],
                      -SWIGLU_LIMIT, SWIGLU_LIMIT)
        h = (hu + 1.0) * (hg * jax.nn.sigmoid(SWIGLU_ALPHA * hg))
        y = jnp.dot(h.astype(BF16), wd_ref[0], preferred_element_type=F32) + bd_ref[0]
        for c in range(ROW_TILES):
            ys_ref[pl.ds(c, blk, stride=ROW_TILES), :] = y[:, c * LANES:(c + 1) * LANES]

    @pl.when(jnp.logical_not(used))
    def _():
        ys_ref[...] = jnp.zeros_like(ys_ref)


def _moe(blk_expert, nused, xs, wg, bg, wu, bu, wd, bd):
    blk = MOE_BLOCK
    nblk = xs.shape[0] // (blk * ROW_TILES)
    wspec = lambda a: pl.BlockSpec((1,) + a.shape[1:], lambda i, be, nu: (be[i], 0, 0))
    xspec = pl.BlockSpec((blk * ROW_TILES, LANES), lambda i, be, nu: (jnp.minimum(i, nu[0] - 1), 0))
    return pl.pallas_call(
        _moe_kernel,
        out_shape=jax.ShapeDtypeStruct(xs.shape, F32),
        grid_spec=pltpu.PrefetchScalarGridSpec(
            num_scalar_prefetch=2,
            grid=(nblk,),
            in_specs=[xspec, wspec(wg), wspec(bg), wspec(wu), wspec(bu), wspec(wd), wspec(bd)],
            out_specs=pl.BlockSpec((blk * ROW_TILES, LANES), lambda i, be, nu: (i, 0)),
        ),
        compiler_params=_params(("arbitrary",)),
        name="moe",
    )(blk_expert, nused, xs, wg, bg, wu, bu, wd, bd)


def _final_kernel(pos_hbm, gt_ref, x1t_ref, ys_hbm, g_ref, b_ref, o_ref,
                  pos_smem, ybuf, sem_p, sem_g):
    i = pl.program_id(0)
    n = pl.num_programs(0)
    ft = FINAL_TILE
    slot_rows = TOP_K * ft * ROW_TILES

    ng = ft // DMA_GROUP

    def pos_copy(tile, slot):
        return pltpu.make_async_copy(pos_hbm.at[pl.ds(tile * ng, ng)], pos_smem.at[slot], sem_p.at[slot])

    def gather_rows(pslot, yslot):
        def body(g, c):
            src = [pos_smem[pslot, g, j] * ROW_TILES for j in range(DMA_GROUP * TOP_K)]
            base = yslot * slot_rows + g * (DMA_GROUP * ROW_TILES)
            for u in range(DMA_GROUP):
                for k in range(TOP_K):
                    dst = base + (k * ft + u) * ROW_TILES
                    pltpu.make_async_copy(ys_hbm.at[pl.ds(src[u * TOP_K + k], ROW_TILES)],
                                          ybuf.at[pl.ds(dst, ROW_TILES)], sem_g.at[yslot]).start()
            return c
        lax.fori_loop(0, ng, body, 0)

    def gather_wait(yslot):
        pltpu.make_async_copy(ys_hbm.at[pl.ds(0, slot_rows)],
                              ybuf.at[pl.ds(yslot * slot_rows, slot_rows)], sem_g.at[yslot]).wait()

    @pl.when(i == 0)
    def _():
        c = pos_copy(0, 0)
        c.start()
        c.wait()
        gather_rows(0, 0)

        @pl.when(n > 1)
        def _():
            pos_copy(1, 1).start()

    @pl.when(i + 1 < n)
    def _():
        pos_copy(i + 1, (i + 1) % 3).wait()
        gather_rows((i + 1) % 3, (i + 1) % 2)

    @pl.when(i + 2 < n)
    def _():
        pos_copy(i + 2, (i + 2) % 3).start()

    slot = i % 2
    gather_wait(slot)
    g = gt_ref[...]
    zs = []
    for c in range(ROW_TILES):
        f = None
        for k in range(TOP_K):
            yk = ybuf[pl.ds(slot * slot_rows + k * ft * ROW_TILES + c, ft, stride=ROW_TILES), :]
            term = g[:, k:k + 1] * yk
            f = term if f is None else f + term
        zs.append(DN_ALPHA * x1t_ref[pl.ds(c, ft, stride=ROW_TILES), :] + f)
    o_ref[...] = _layer_norm(jnp.concatenate(zs, axis=1), g_ref[...], b_ref[...])


def _final(pos, gates_t, x1t, ys, g2, b2):
    ft = FINAL_TILE
    n = gates_t.shape[0]
    any_spec = pl.BlockSpec(memory_space=pl.ANY)
    return pl.pallas_call(
        _final_kernel,
        out_shape=jax.ShapeDtypeStruct((n, D_MODEL), F32),
        grid=(n // ft,),
        in_specs=[any_spec, pl.BlockSpec((ft, TOP_K), lambda i: (i, 0)),
                  pl.BlockSpec((ft * ROW_TILES, LANES), lambda i: (i, 0)), any_spec,
                  _full(g2), _full(b2)],
        out_specs=pl.BlockSpec((ft, D_MODEL), lambda i: (i, 0)),
        scratch_shapes=[
            pltpu.SMEM((3, ft // DMA_GROUP, DMA_GROUP * TOP_K), I32),
            pltpu.VMEM((2 * TOP_K * ft * ROW_TILES, LANES), F32),
            pltpu.SemaphoreType.DMA((3,)),
            pltpu.SemaphoreType.DMA((2,)),
        ],
        compiler_params=_params(("arbitrary",)),
        name="final",
    )(pos, gates_t, x1t, ys, g2, b2)


def _mix_weight(w_in):
    qa, ckv, qi, ki, wi, qb, kb, vb, _, _ = jnp.split(w_in, np.cumsum(SPLITS)[:-1].tolist(), axis=-1)
    pad = lambda a: jnp.pad(a, ((0, 0), (0, LANES - a.shape[1])))
    return jnp.concatenate([qa, qi, ckv, pad(ki), pad(wi), qb, kb, vb], axis=1).astype(BF16)


def _layer(x, w_in, kv_norm_g, idx_k_norm_g, idx_k_norm_b, w_uk, w_uv, rel_bias, w_branch_a,
           w_branch_b, w_out, ln1_g, ln1_b, w_router, b_router, w_gate, b_gate, w_up, b_up,
           w_down, b_down, ln2_g, ln2_b):
    bsz, seq, _ = x.shape
    n = bsz * seq
    assert seq % KEY_TILE == 0 and n % TOKEN_TILE == 0
    xf = x.reshape(n, D_MODEL)
    row = lambda v: v.reshape(1, -1).astype(F32)

    w_mix = _mix_weight(w_in)
    w_gates = w_in[:, sum(SPLITS[:8]):].astype(BF16)
    wuk_bd = jnp.zeros((A_WIDTH, A_HEADS * A_KV_RANK), F32)
    wuv_pad = jnp.zeros((A_HEADS, A_KV_RANK, LANES), F32)
    for h in range(A_HEADS):
        wuk_bd = wuk_bd.at[h * A_HEAD_DIM:(h + 1) * A_HEAD_DIM, h * A_KV_RANK:(h + 1) * A_KV_RANK].set(w_uk[h])
        c0 = (h % 2) * A_HEAD_DIM
        wuv_pad = wuv_pad.at[h, :, c0:c0 + A_HEAD_DIM].set(w_uv[h])
    wuk_bd, wuv_pad = wuk_bd.astype(BF16), wuv_pad.astype(BF16)

    qlat, qi, ckv, ki, wi, qb, kb, vb = _projection(
        xf, w_mix, wuk_bd, row(kv_norm_g), row(idx_k_norm_g), row(idx_k_norm_b))

    nqb, nkt = seq // Q_BLOCK, seq // KEY_TILE
    kidx = ki.reshape(bsz, nkt, KEY_TILE, IDX_DIM)
    ckv4 = ckv.reshape(bsz, nkt, KEY_TILE, A_KV_RANK)
    ckvt = ckv4.transpose(0, 1, 3, 2)
    qi_r = qi.reshape(bsz, nqb, Q_BLOCK, IDX_HEADS, IDX_DIM).transpose(0, 1, 3, 2, 4).reshape(
        bsz, nqb, IDX_HEADS * Q_BLOCK, IDX_DIM)
    wit = wi.reshape(bsz, seq, IDX_HEADS).transpose(0, 2, 1)
    ya = _mixer_a(kidx, ckv4, ckvt, qi_r, wit, qlat.reshape(bsz, seq, -1), wuv_pad, bsz, seq)

    front = lambda a: jnp.pad(a.reshape(bsz, seq, B_WIDTH), ((0, 0), (B_PAD, 0), (0, 0)))
    yb = _mixer_b(qb.reshape(bsz, seq, B_WIDTH), front(kb), front(vb), _band_bias(rel_bias), bsz, seq)

    x1t, topi, gates = _merge(
        xf, ya.reshape(n, A_WIDTH), yb.reshape(n, B_WIDTH), w_gates, w_branch_a.astype(BF16),
        w_branch_b.astype(BF16), w_out.astype(BF16), row(ln1_g), row(ln1_b),
        w_router.T.astype(BF16), b_router.reshape(-1, 1).astype(F32))

    blk = MOE_BLOCK
    cap = n * TOP_K + N_EXPERTS * blk
    nblk = cap // blk
    rank, cnt = _route(topi)
    counts = cnt[:, 0]
    padded = (counts + blk - 1) // blk * blk
    pends = jnp.cumsum(padded).astype(I32)
    pstarts = pends - padded
    nused = (pends[-1:] // blk).astype(I32)
    blk_start = jnp.arange(nblk, dtype=I32) * blk
    blk_expert = jnp.minimum(jnp.sum(blk_start[:, None] >= pends[None, :], axis=1), N_EXPERTS - 1).astype(I32)

    pos = _positions(topi, rank, pstarts)
    pos_g = pos.T.reshape(n // DMA_GROUP, DMA_GROUP * TOP_K)
    xs = _dispatch(counts, pstarts, pos_g, x1t, cap)
    ys = _moe(blk_expert, nused, xs, w_gate.astype(BF16), b_gate[:, None, :], w_up.astype(BF16),
              b_up[:, None, :], w_down.astype(BF16), b_down[:, None, :])
    out = _final(pos_g, gates.T, x1t, ys, row(ln2_g), row(ln2_b))
    return out.reshape(bsz, seq, D_MODEL)


def kernel(x, w_in, kv_norm_g, idx_k_norm_g, idx_k_norm_b, w_uk, w_uv, rel_bias, w_branch_a,
           w_branch_b, w_out, ln1_g, ln1_b, w_router, b_router, w_gate, b_gate, w_up, b_up,
           w_down, b_down, ln2_g, ln2_b):
    for l in range(DEPTH):
        x = _layer(x, w_in[l], kv_norm_g[l], idx_k_norm_g[l], idx_k_norm_b[l], w_uk[l], w_uv[l],
                   rel_bias[l], w_branch_a[l], w_branch_b[l], w_out[l], ln1_g[l], ln1_b[l],
                   w_router[l], b_router[l], w_gate[l], b_gate[l], w_up[l], b_up[l], w_down[l],
                   b_down[l], ln2_g[l], ln2_b[l])
    return x
```

```python
import functools

import jax
import jax.numpy as jnp
import numpy as np
from jax import lax
from jax.experimental import pallas as pl
from jax.experimental.pallas import tpu as pltpu

F32 = jnp.float32
BF16 = jnp.bfloat16
I32 = jnp.int32

D_MODEL = 1024
CHUNK = 64
Q_BLOCK = 128
A_HEADS = 8
A_HEAD_DIM = 64
A_KV_RANK = 128
IDX_HEADS = 8
IDX_DIM = 64
IDX_TOPK = 256
IDX_SCALE = (IDX_HEADS * IDX_DIM) ** -0.5
B_HEADS = 8
B_HEAD_DIM = 64
B_LEFT_CHUNKS = 8
REL_CLIP = 128
N_EXPERTS = 32
TOP_K = 4
D_FF = 1024
SWIGLU_LIMIT = 7.0
SWIGLU_ALPHA = 1.702
DEPTH = 1
DN_ALPHA = (2.0 * DEPTH) ** 0.25
LN_EPS = 1e-5

A_WIDTH = A_HEADS * A_HEAD_DIM
B_WIDTH = B_HEADS * B_HEAD_DIM
SPLITS = [A_WIDTH, A_KV_RANK, IDX_HEADS * IDX_DIM, IDX_DIM, IDX_HEADS,
          B_WIDTH, B_WIDTH, B_WIDTH, D_MODEL, D_MODEL]

LANES = 128
SUBLANES = 8
KEY_TILE = 256
B_QTILE = 256
B_PAD = B_LEFT_CHUNKS * CHUNK
B_WIN = B_PAD + B_QTILE
TOKEN_TILE = 512
MOE_BLOCK = 512
FINAL_TILE = 256
ROW_TILES = D_MODEL // LANES
DMA_GROUP = 4
INT_MIN = -2 ** 31
VMEM_LIMIT = 56 * 1024 * 1024

A_SLOPES = [2.0 ** (-8.0 * (h + 1) / A_HEADS) for h in range(A_HEADS)]

_C_QA = 0
_C_QI = _C_QA + A_WIDTH
_C_CKV = _C_QI + IDX_HEADS * IDX_DIM
_C_KI = _C_CKV + A_KV_RANK
_C_WI = _C_KI + LANES
_C_QKVB = _C_WI + LANES
_C_END = _C_QKVB + 3 * B_WIDTH


def _params(sem, vmem=VMEM_LIMIT):
    return pltpu.CompilerParams(dimension_semantics=sem, vmem_limit_bytes=vmem)


def _full(a):
    return pl.BlockSpec(a.shape, lambda *_: (0,) * a.ndim)


def _proj_kernel(x_ref, w_ref, wuk_ref, kvg_ref, kig_ref, kib_ref,
                 qlat_ref, qi_ref, ckv_ref, ki_ref, wi_ref, qb_ref, kb_ref, vb_ref):
    xb = x_ref[...].astype(BF16)

    def mm(c0, c1):
        return jnp.dot(xb, w_ref[:, c0:c1], preferred_element_type=F32)

    qa = mm(_C_QA, _C_QI)
    qlat = jnp.dot(qa.astype(BF16), wuk_ref[...], preferred_element_type=F32)
    qlat_ref[...] = (qlat * (A_HEAD_DIM ** -0.5)).astype(BF16)
    qi_ref[...] = mm(_C_QI, _C_CKV).astype(BF16)

    ckv = mm(_C_CKV, _C_KI)
    ms = jnp.mean(ckv * ckv, axis=-1, keepdims=True)
    ckv_ref[...] = ((ckv * lax.rsqrt(ms + LN_EPS)) * kvg_ref[...]).astype(BF16)

    ki = mm(_C_KI, _C_WI)[:, :IDX_DIM]
    mu = jnp.mean(ki, axis=-1, keepdims=True)
    var = jnp.mean(jnp.square(ki - mu), axis=-1, keepdims=True)
    ki_ref[...] = (((ki - mu) * lax.rsqrt(var + LN_EPS)) * kig_ref[...] + kib_ref[...]).astype(BF16)

    wi_ref[...] = mm(_C_WI, _C_QKVB)[:, :IDX_HEADS]
    qb_ref[...] = mm(_C_QKVB, _C_QKVB + B_WIDTH).astype(BF16)
    kb_ref[...] = mm(_C_QKVB + B_WIDTH, _C_QKVB + 2 * B_WIDTH).astype(BF16)
    vb_ref[...] = mm(_C_QKVB + 2 * B_WIDTH, _C_END).astype(BF16)


def _projection(xf, w_mix, wuk_bd, kvg, kig, kib):
    n, tm = xf.shape[0], TOKEN_TILE
    row = lambda w: pl.BlockSpec((tm, w), lambda i: (i, 0))
    outs = [(A_HEADS * A_KV_RANK, BF16), (IDX_HEADS * IDX_DIM, BF16), (A_KV_RANK, BF16),
            (IDX_DIM, BF16), (IDX_HEADS, F32), (B_WIDTH, BF16), (B_WIDTH, BF16), (B_WIDTH, BF16)]
    return pl.pallas_call(
        _proj_kernel,
        out_shape=[jax.ShapeDtypeStruct((n, w), dt) for w, dt in outs],
        grid=(n // tm,),
        in_specs=[row(D_MODEL), _full(w_mix), _full(wuk_bd), _full(kvg), _full(kig), _full(kib)],
        out_specs=[row(w) for w, _ in outs],
        compiler_params=_params(("parallel",)),
        name="proj",
    )(xf, w_mix, wuk_bd, kvg, kig, kib)


def _mixer_a_kernel(kidx_ref, ckv_ref, ckvt_ref, qi_ref, wit_ref, qlat_ref, wuv_ref,
                    ya_ref, keys_ref, s_ref, acc_ref, *, seq, topk):
    i = pl.program_id(1)
    nkt = i // 2 + 1
    nsub = KEY_TILE // SUBLANES
    lane = lax.broadcasted_iota(I32, (1, LANES), 1)
    qpos = i * Q_BLOCK + lane
    qchunk = qpos >> 6
    row_iota = lax.broadcasted_iota(I32, (KEY_TILE, LANES), 0)

    qi_blk = qi_ref[0, 0]
    w_all = wit_ref[0] * IDX_SCALE

    def idx_body(kt, carry):
        lg = pl.dot(kidx_ref[0, kt], qi_blk, trans_b=True)
        isc = jnp.zeros((KEY_TILE, LANES), F32)
        for h in range(IDX_HEADS):
            isc = isc + jnp.maximum(lg[:, h * LANES:(h + 1) * LANES], 0.0) * w_all[h:h + 1, :]
        bits = lax.bitcast_convert_type(isc + 0.0, I32)
        key = bits ^ ((bits >> 31) & 0x7FFFFFFF)
        spos = kt * KEY_TILE + row_iota
        key = jnp.where((spos >> 6) <= qchunk, key, INT_MIN)
        keys_ref[kt] = key
        return carry

    lax.fori_loop(0, nkt, idx_body, 0)

    def count(pred):
        def body(kt, acc):
            spos = kt * KEY_TILE + row_iota
            m = jnp.where(pred(keys_ref[kt], spos), 1, 0).astype(I32)
            return acc + jnp.sum(m.reshape(nsub, SUBLANES, LANES), axis=0)
        acc = lax.fori_loop(0, nkt, body, jnp.zeros((SUBLANES, LANES), I32))
        return jnp.sum(acc, axis=0, keepdims=True)

    def bit_body(it, carry):
        thr, cge = carry
        cand = thr ^ lax.shift_left(jnp.int32(1), 31 - it)
        cnt = count(lambda kv, spos: kv >= cand)
        ok = cnt >= topk
        return jnp.where(ok, cand, thr), jnp.where(ok, cnt, cge)

    thr0 = jnp.full((1, LANES), INT_MIN, I32)
    thr, cge = lax.fori_loop(0, 32, bit_body, (thr0, jnp.zeros((1, LANES), I32)))

    nbits = int(np.log2(seq)) + 1
    has_thr = thr > INT_MIN
    need_tie = jnp.max(jnp.where(has_thr, cge, 0)) > topk

    def tie_fn():
        room = topk - count(lambda kv, spos: kv > thr)

        def jb(it, cut):
            cand = cut | lax.shift_left(jnp.int32(1), nbits - 1 - it)
            f = count(lambda kv, spos: (kv == thr) & (spos < cand))
            return jnp.where(f <= room, cand, cut)
        return lax.fori_loop(0, nbits, jb, jnp.zeros((1, LANES), I32))

    cut = lax.cond(need_tie, tie_fn, lambda: jnp.full((1, LANES), 2 * seq, I32))
    cut = jnp.where(has_thr, cut, 0)

    qlat = qlat_ref[0]
    qr = jnp.concatenate([qlat[:, h * LANES:(h + 1) * LANES] for h in range(A_HEADS)], axis=0)

    def sc_body(kt, m):
        st = pl.dot(ckv_ref[0, kt], qr, trans_b=True)
        kv = keys_ref[kt]
        spos = kt * KEY_TILE + row_iota
        sel = (kv > thr) | ((kv == thr) & (spos < cut))
        dist = jnp.abs(qpos - spos).astype(F32)
        out = []
        for h in range(A_HEADS):
            sh = st[:, h * LANES:(h + 1) * LANES] - A_SLOPES[h] * dist
            sh = jnp.where(sel, sh, -jnp.inf)
            s_ref[kt, :, h * LANES:(h + 1) * LANES] = sh
            out.append(jnp.maximum(m[h], jnp.max(sh.reshape(nsub, SUBLANES, LANES), axis=0)))
        return tuple(out)

    m8 = lax.fori_loop(0, nkt, sc_body,
                       tuple(jnp.full((SUBLANES, LANES), -jnp.inf, F32) for _ in range(A_HEADS)))
    mx = [jnp.max(m, axis=0, keepdims=True) for m in m8]

    acc_ref[...] = jnp.zeros_like(acc_ref)

    def pv_body(kt, l):
        out, ps = [], []
        for h in range(A_HEADS):
            p = jnp.exp(s_ref[kt, :, h * LANES:(h + 1) * LANES] - mx[h])
            out.append(l[h] + jnp.sum(p.reshape(nsub, SUBLANES, LANES), axis=0))
            ps.append(p.astype(BF16))
        acc_ref[...] += jnp.dot(ckvt_ref[0, kt], jnp.concatenate(ps, axis=1),
                                preferred_element_type=F32)
        return tuple(out)

    l8 = lax.fori_loop(0, nkt, pv_body,
                       tuple(jnp.zeros((SUBLANES, LANES), F32) for _ in range(A_HEADS)))
    inv = [1.0 / jnp.sum(l, axis=0, keepdims=True) for l in l8]

    for j in range(A_HEADS // 2):
        y = None
        for h in (2 * j, 2 * j + 1):
            o = (acc_ref[:, h * LANES:(h + 1) * LANES] * inv[h]).astype(BF16)
            t = pl.dot(o, wuv_ref[h], trans_a=True)
            y = t if y is None else y + t
        ya_ref[0, :, j * LANES:(j + 1) * LANES] = y.astype(BF16)


def _mixer_a(kidx, ckv, ckvt, qi_r, wit, qlat, wuv_pad, bsz, seq):
    nqb = seq // Q_BLOCK
    nkt = seq // KEY_TILE
    topk = min(IDX_TOPK, seq // 4)
    kern = functools.partial(_mixer_a_kernel, seq=seq, topk=topk)
    return pl.pallas_call(
        kern,
        out_shape=jax.ShapeDtypeStruct((bsz, seq, A_WIDTH), BF16),
        grid=(bsz, nqb),
        in_specs=[
            pl.BlockSpec((1, nkt, KEY_TILE, IDX_DIM), lambda b, i: (b, 0, 0, 0)),
            pl.BlockSpec((1, nkt, KEY_TILE, A_KV_RANK), lambda b, i: (b, 0, 0, 0)),
            pl.BlockSpec((1, nkt, A_KV_RANK, KEY_TILE), lambda b, i: (b, 0, 0, 0)),
            pl.BlockSpec((1, 1, IDX_HEADS * Q_BLOCK, IDX_DIM), lambda b, i: (b, i, 0, 0)),
            pl.BlockSpec((1, IDX_HEADS, Q_BLOCK), lambda b, i: (b, 0, i)),
            pl.BlockSpec((1, Q_BLOCK, A_HEADS * A_KV_RANK), lambda b, i: (b, i, 0)),
            pl.BlockSpec(wuv_pad.shape, lambda b, i: (0, 0, 0)),
        ],
        out_specs=pl.BlockSpec((1, Q_BLOCK, A_WIDTH), lambda b, i: (b, i, 0)),
        scratch_shapes=[
            pltpu.VMEM((nkt, KEY_TILE, LANES), I32),
            pltpu.VMEM((nkt, KEY_TILE, A_HEADS * LANES), F32),
            pltpu.VMEM((A_KV_RANK, A_HEADS * LANES), F32),
        ],
        compiler_params=_params(("parallel", "arbitrary")),
        name="mixer_a",
    )(kidx, ckv, ckvt, qi_r, wit, qlat, wuv_pad)


def _mixer_b_kernel(q_ref, k_ref, v_ref, bias_ref, yb_ref):
    t0 = pl.multiple_of(pl.program_id(1) * B_QTILE, B_QTILE)
    kw = k_ref[0, pl.ds(t0, B_WIN), :]
    vw = v_ref[0, pl.ds(t0, B_WIN), :]
    q = q_ref[0]
    kvalid = (t0 - B_PAD + lax.broadcasted_iota(I32, (1, B_WIN), 1)) >= 0
    lane = lax.broadcasted_iota(I32, (1, LANES), 1)
    scale = B_HEAD_DIM ** -0.5
    for j in range(B_HEADS // 2):
        sl = slice(j * LANES, (j + 1) * LANES)
        qs, ks, vs = q[:, sl], kw[:, sl], vw[:, sl]
        outs = []
        for hh in range(2):
            mine = (lane >= B_HEAD_DIM) if hh else (lane < B_HEAD_DIM)
            qm = jnp.where(mine, qs, jnp.zeros_like(qs))
            s = pl.dot(qm, ks, trans_b=True) * scale + bias_ref[2 * j + hh]
            s = jnp.where(kvalid, s, -jnp.inf)
            m = jnp.max(s, axis=-1, keepdims=True)
            p = jnp.exp(s - m)
            l = jnp.sum(p, axis=-1, keepdims=True)
            outs.append(jnp.dot(p.astype(BF16), vs, preferred_element_type=F32) / l)
        yb_ref[0, :, sl] = jnp.where(lane < B_HEAD_DIM, outs[0], outs[1]).astype(BF16)


def _band_bias(rel_bias):
    tq = np.arange(B_QTILE)[:, None]
    j = np.arange(B_WIN)[None, :]
    band = (j // CHUNK >= tq // CHUNK) & (j // CHUNK <= tq // CHUNK + B_LEFT_CHUNKS)
    m = np.arange(B_QTILE - 1 + B_WIN)
    line = rel_bias[:, np.clip(B_WIN - 1 - m, -REL_CLIP, REL_CLIP) + REL_CLIP].astype(F32)
    table = jnp.stack([line[:, B_QTILE - 1 - t:B_QTILE - 1 - t + B_WIN] for t in range(B_QTILE)], axis=1)
    return jnp.where(jnp.asarray(band)[None], table, -jnp.inf)


def _mixer_b(qb, kb_pad, vb_pad, bias, bsz, seq):
    return pl.pallas_call(
        _mixer_b_kernel,
        out_shape=jax.ShapeDtypeStruct((bsz, seq, B_WIDTH), BF16),
        grid=(bsz, seq // B_QTILE),
        in_specs=[
            pl.BlockSpec((1, B_QTILE, B_WIDTH), lambda b, i: (b, i, 0)),
            pl.BlockSpec((1, seq + B_PAD, B_WIDTH), lambda b, i: (b, 0, 0)),
            pl.BlockSpec((1, seq + B_PAD, B_WIDTH), lambda b, i: (b, 0, 0)),
            pl.BlockSpec(bias.shape, lambda b, i: (0, 0, 0)),
        ],
        out_specs=pl.BlockSpec((1, B_QTILE, B_WIDTH), lambda b, i: (b, i, 0)),
        compiler_params=_params(("parallel", "arbitrary")),
        name="mixer_b",
    )(qb, kb_pad, vb_pad, bias)


def _layer_norm(z, g, b):
    mu = jnp.mean(z, axis=-1, keepdims=True)
    var = jnp.mean(jnp.square(z - mu), axis=-1, keepdims=True)
    return ((z - mu) * lax.rsqrt(var + LN_EPS)) * g + b


def _merge_kernel(x_ref, ya_ref, yb_ref, wg_ref, wba_ref, wbb_ref, wo_ref, g_ref, b_ref,
                  wr_ref, br_ref, x1t_ref, topi_ref, gate_ref):
    tm = x_ref.shape[0]
    x = x_ref[...]
    xb = x.astype(BF16)
    ga = jnp.dot(xb, wg_ref[:, :D_MODEL], preferred_element_type=F32)
    a = jnp.dot(ya_ref[...], wba_ref[...], preferred_element_type=F32)
    merged = jax.nn.sigmoid(ga) * a
    gb = jnp.dot(xb, wg_ref[:, D_MODEL:], preferred_element_type=F32)
    b = jnp.dot(yb_ref[...], wbb_ref[...], preferred_element_type=F32)
    merged = merged + jax.nn.sigmoid(gb) * b
    out = jnp.dot(merged.astype(BF16), wo_ref[...], preferred_element_type=F32)
    x1 = _layer_norm(DN_ALPHA * x + out, g_ref[...], b_ref[...])
    for c in range(ROW_TILES):
        x1t_ref[pl.ds(c, tm, stride=ROW_TILES), :] = x1[:, c * LANES:(c + 1) * LANES]

    lg = pl.dot(wr_ref[...], x1.astype(BF16), trans_b=True) + br_ref[...]
    eidx = lax.broadcasted_iota(I32, lg.shape, 0)
    vals, idxs = [], []
    for _ in range(TOP_K):
        m = jnp.max(lg, axis=0, keepdims=True)
        sel = jnp.min(jnp.where(lg == m, eidx, N_EXPERTS), axis=0, keepdims=True)
        vals.append(m)
        idxs.append(sel)
        lg = jnp.where(eidx == sel, -jnp.inf, lg)
    es = [jnp.exp(v - vals[0]) for v in vals]
    tot = es[0] + es[1] + es[2] + es[3]
    topi_ref[...] = jnp.concatenate(idxs, axis=0)
    gate_ref[...] = jnp.concatenate([e / tot for e in es], axis=0)


def _merge(xf, ya, yb, wg, wba, wbb, wo, g1, b1, wr_t, br):
    n, tm = xf.shape[0], TOKEN_TILE
    row = lambda w: pl.BlockSpec((tm, w), lambda i: (i, 0))
    col = pl.BlockSpec((TOP_K, tm), lambda i: (0, i))
    return pl.pallas_call(
        _merge_kernel,
        out_shape=[jax.ShapeDtypeStruct((n * ROW_TILES, LANES), F32),
                   jax.ShapeDtypeStruct((TOP_K, n), I32),
                   jax.ShapeDtypeStruct((TOP_K, n), F32)],
        grid=(n // tm,),
        in_specs=[row(D_MODEL), row(A_WIDTH), row(B_WIDTH), _full(wg), _full(wba), _full(wbb),
                  _full(wo), _full(g1), _full(b1), _full(wr_t), _full(br)],
        out_specs=[pl.BlockSpec((tm * ROW_TILES, LANES), lambda i: (i, 0)), col, col],
        compiler_params=_params(("parallel",)),
        name="merge",
    )(xf, ya, yb, wg, wba, wbb, wo, g1, b1, wr_t, br)


def _one_hot_rows(topi, k, width):
    eidx = lax.broadcasted_iota(I32, (N_EXPERTS, width), 0)
    return eidx == topi[k:k + 1, :]


def _route_kernel(topi_ref, tri_ref, rank_ref, cnt_ref, run_ref):
    @pl.when(pl.program_id(0) == 0)
    def _():
        run_ref[...] = jnp.zeros_like(run_ref)

    topi = topi_ref[...]
    tr = topi.shape[1]
    hot = [_one_hot_rows(topi, k, tr) for k in range(TOP_K)]
    oh = jnp.concatenate([jnp.where(h, 1.0, 0.0).astype(BF16) for h in hot], axis=0)
    prefix = jnp.dot(oh, tri_ref[...], preferred_element_type=F32)
    base = run_ref[...]
    ranks = []
    for k in range(TOP_K):
        tbl = prefix[k * N_EXPERTS:(k + 1) * N_EXPERTS, :] + base
        ranks.append(jnp.sum(jnp.where(hot[k], tbl, 0.0), axis=0, keepdims=True))
        base = base + jnp.sum(jnp.where(hot[k], 1.0, 0.0), axis=1, keepdims=True)
    rank_ref[...] = jnp.concatenate(ranks, axis=0).astype(I32)
    run_ref[...] = base
    cnt_ref[...] = base.astype(I32)


def _route(topi):
    n, tr = topi.shape[1], TOKEN_TILE
    tri = (np.arange(tr)[:, None] < np.arange(tr)[None, :]).astype(np.float32)
    tri = jnp.asarray(tri, BF16)
    col = pl.BlockSpec((TOP_K, tr), lambda i: (0, i))
    return pl.pallas_call(
        _route_kernel,
        out_shape=[jax.ShapeDtypeStruct((TOP_K, n), I32), jax.ShapeDtypeStruct((N_EXPERTS, 1), I32)],
        grid=(n // tr,),
        in_specs=[col, _full(tri)],
        out_specs=[col, pl.BlockSpec((N_EXPERTS, 1), lambda i: (0, 0))],
        scratch_shapes=[pltpu.VMEM((N_EXPERTS, 1), F32)],
        compiler_params=_params(("arbitrary",)),
        name="route",
    )(topi, tri)


def _positions_kernel(topi_ref, rank_ref, pstv_ref, pos_ref):
    topi = topi_ref[...]
    rows = []
    for k in range(TOP_K):
        start = jnp.sum(jnp.where(_one_hot_rows(topi, k, topi.shape[1]), pstv_ref[...], 0),
                        axis=0, keepdims=True)
        rows.append(rank_ref[k:k + 1, :] + start)
    pos_ref[...] = jnp.concatenate(rows, axis=0)


def _positions(topi, rank, pstarts):
    n, td = topi.shape[1], TOKEN_TILE
    col = pl.BlockSpec((TOP_K, td), lambda i: (0, i))
    pstv = pstarts.reshape(N_EXPERTS, 1)
    return pl.pallas_call(
        _positions_kernel,
        out_shape=jax.ShapeDtypeStruct((TOP_K, n), I32),
        grid=(n // td,),
        in_specs=[col, col, _full(pstv)],
        out_specs=col,
        compiler_params=_params(("parallel",)),
        name="positions",
    )(topi, rank, pstv)


def _dispatch_kernel(cnt_ref, pst_ref, pos_hbm, x_hbm, xs_hbm,
                     pos_smem, xbuf, zbuf, sem_p, sem_x, sem_d, sem_z):
    i = pl.program_id(0)
    n = pl.num_programs(0)
    ng = pos_smem.shape[1]
    td = ng * DMA_GROUP
    tile_rows = td * ROW_TILES

    def pos_copy(step, slot):
        return pltpu.make_async_copy(pos_hbm.at[pl.ds(step * ng, ng)], pos_smem.at[slot], sem_p.at[slot])

    def x_copy(step, slot):
        return pltpu.make_async_copy(x_hbm.at[pl.ds(step * tile_rows, tile_rows)], xbuf.at[slot],
                                     sem_x.at[slot])

    @pl.when(i == 0)
    def _():
        pos_copy(0, 0).start()
        x_copy(0, 0).start()

    @pl.when(i + 1 < n)
    def _():
        pos_copy(i + 1, (i + 1) % 2).start()
        x_copy(i + 1, (i + 1) % 3).start()

    slot = i % 2
    xslot = i % 3
    pos_copy(i, slot).wait()
    x_copy(i, xslot).wait()

    def step_bytes(s):
        rows_ = TOP_K * tile_rows
        return pltpu.make_async_copy(xs_hbm.at[pl.ds(0, rows_)], xs_hbm.at[pl.ds(0, rows_)], sem_d.at[s])

    def body(g, c):
        dst = [pos_smem[slot, g, j] * ROW_TILES for j in range(DMA_GROUP * TOP_K)]
        for u in range(DMA_GROUP):
            src = xbuf.at[xslot, pl.ds((g * DMA_GROUP + u) * ROW_TILES, ROW_TILES)]
            for k in range(TOP_K):
                pltpu.make_async_copy(src, xs_hbm.at[pl.ds(dst[u * TOP_K + k], ROW_TILES)],
                                      sem_d.at[xslot]).start()
        return c
    lax.fori_loop(0, ng, body, 0)

    @pl.when(i > 0)
    def _():
        step_bytes((i + 2) % 3).wait()

    @pl.when(i == n - 1)
    def _():
        step_bytes(xslot).wait()
        zbuf[...] = jnp.zeros_like(zbuf)

        def zero_copy(r):
            return pltpu.make_async_copy(zbuf, xs_hbm.at[pl.ds(r * ROW_TILES, ROW_TILES)], sem_z)

        def zero_rows(lo, hi):
            def start(r, c2):
                zero_copy(r).start()
                return c2

            def wait(r, c2):
                zero_copy(r).wait()
                return c2
            lax.fori_loop(lo, hi, start, 0)
            lax.fori_loop(lo, hi, wait, 0)

        def padded_end(e):
            return pst_ref[e] + (cnt_ref[e] + MOE_BLOCK - 1) // MOE_BLOCK * MOE_BLOCK

        def ebody(e, c):
            zero_rows(pst_ref[e] + cnt_ref[e], padded_end(e))
            return c
        lax.fori_loop(0, N_EXPERTS, ebody, 0)
        zero_rows(padded_end(N_EXPERTS - 1), xs_hbm.shape[0] // ROW_TILES)


def _dispatch(counts, pstarts, pos_g, x1t, cap):
    n, td = pos_g.shape[0] * DMA_GROUP, TOKEN_TILE
    any_spec = pl.BlockSpec(memory_space=pl.ANY)
    return pl.pallas_call(
        _dispatch_kernel,
        out_shape=jax.ShapeDtypeStruct((cap * ROW_TILES, LANES), F32),
        grid_spec=pltpu.PrefetchScalarGridSpec(
            num_scalar_prefetch=2,
            grid=(n // td,),
            in_specs=[any_spec, any_spec],
            out_specs=any_spec,
            scratch_shapes=[
                pltpu.SMEM((2, td // DMA_GROUP, DMA_GROUP * TOP_K), I32),
                pltpu.VMEM((3, td * ROW_TILES, LANES), F32),
                pltpu.VMEM((ROW_TILES, LANES), F32),
                pltpu.SemaphoreType.DMA((2,)),
                pltpu.SemaphoreType.DMA((3,)),
                pltpu.SemaphoreType.DMA((3,)),
                pltpu.SemaphoreType.DMA,
            ],
        ),
        compiler_params=_params(("arbitrary",)),
        name="dispatch",
    )(counts, pstarts, pos_g, x1t)


def _moe_kernel(be_ref, nu_ref, xs_ref, wg_ref, bg_ref, wu_ref, bu_ref, wd_ref, bd_ref, ys_ref):
    del be_ref
    blk = MOE_BLOCK
    used = pl.program_id(0) < nu_ref[0]

    @pl.when(used)
    def _():
        xb = jnp.concatenate([xs_ref[pl.ds(c, blk, stride=ROW_TILES), :].astype(BF16)
                              for c in range(ROW_TILES)], axis=1)
        hg = jnp.minimum(jnp.dot(xb, wg_ref[0], preferred_element_type=F32) + bg_ref[0], SWIGLU_LIMIT)
        hu = jnp.clip(jnp.dot(xb, wu_ref[0], preferred_element_type=F32) + bu_ref[0],
                      -SWIGLU_LIMIT, SWIGLU_LIMIT)
        h = (hu + 1.0) * (hg * jax.nn.sigmoid(SWIGLU_ALPHA * hg))
        y = jnp.dot(h.astype(BF16), wd_ref[0], preferred_element_type=F32) + bd_ref[0]
        for c in range(ROW_TILES):
            ys_ref[pl.ds(c, blk, stride=ROW_TILES), :] = y[:, c * LANES:(c + 1) * LANES]

    @pl.when(jnp.logical_not(used))
    def _():
        ys_ref[...] = jnp.zeros_like(ys_ref)


def _moe(blk_expert, nused, xs, wg, bg, wu, bu, wd, bd):
    blk = MOE_BLOCK
    nblk = xs.shape[0] // (blk * ROW_TILES)
    wspec = lambda a: pl.BlockSpec((1,) + a.shape[1:], lambda i, be, nu: (be[i], 0, 0))
    xspec = pl.BlockSpec((blk * ROW_TILES, LANES), lambda i, be, nu: (jnp.minimum(i, nu[0] - 1), 0))
    return pl.pallas_call(
        _moe_kernel,
        out_shape=jax.ShapeDtypeStruct(xs.shape, F32),
        grid_spec=pltpu.PrefetchScalarGridSpec(
            num_scalar_prefetch=2,
            grid=(nblk,),
            in_specs=[xspec, wspec(wg), wspec(bg), wspec(wu), wspec(bu), wspec(wd), wspec(bd)],
            out_specs=pl.BlockSpec((blk * ROW_TILES, LANES), lambda i, be, nu: (i, 0)),
        ),
        compiler_params=_params(("arbitrary",)),
        name="moe",
    )(blk_expert, nused, xs, wg, bg, wu, bu, wd, bd)


def _final_kernel(pos_hbm, gt_ref, x1t_ref, ys_hbm, g_ref, b_ref, o_ref,
                  pos_smem, ybuf, sem_p, sem_g):
    i = pl.program_id(0)
    n = pl.num_programs(0)
    ft = FINAL_TILE
    slot_rows = TOP_K * ft * ROW_TILES

    ng = ft // DMA_GROUP

    def pos_copy(tile, slot):
        return pltpu.make_async_copy(pos_hbm.at[pl.ds(tile * ng, ng)], pos_smem.at[slot], sem_p.at[slot])

    def gather_rows(pslot, yslot):
        def body(g, c):
            src = [pos_smem[pslot, g, j] * ROW_TILES for j in range(DMA_GROUP * TOP_K)]
            base = yslot * slot_rows + g * (DMA_GROUP * ROW_TILES)
            for u in range(DMA_GROUP):
                for k in range(TOP_K):
                    dst = base + (k * ft + u) * ROW_TILES
                    pltpu.make_async_copy(ys_hbm.at[pl.ds(src[u * TOP_K + k], ROW_TILES)],
                                          ybuf.at[pl.ds(dst, ROW_TILES)], sem_g.at[yslot]).start()
            return c
        lax.fori_loop(0, ng, body, 0)

    def gather_wait(yslot):
        pltpu.make_async_copy(ys_hbm.at[pl.ds(0, slot_rows)],
                              ybuf.at[pl.ds(yslot * slot_rows, slot_rows)], sem_g.at[yslot]).wait()

    @pl.when(i == 0)
    def _():
        c = pos_copy(0, 0)
        c.start()
        c.wait()
        gather_rows(0, 0)

        @pl.when(n > 1)
        def _():
            pos_copy(1, 1).start()

    @pl.when(i + 1 < n)
    def _():
        pos_copy(i + 1, (i + 1) % 3).wait()
        gather_rows((i + 1) % 3, (i + 1) % 2)

    @pl.when(i + 2 < n)
    def _():
        pos_copy(i + 2, (i + 2) % 3).start()

    slot = i % 2
    gather_wait(slot)
    g = gt_ref[...]
    zs = []
    for c in range(ROW_TILES):
        f = None
        for k in range(TOP_K):
            yk = ybuf[pl.ds(slot * slot_rows + k * ft * ROW_TILES + c, ft, stride=ROW_TILES), :]
            term = g[:, k:k + 1] * yk
            f = term if f is None else f + term
        zs.append(DN_ALPHA * x1t_ref[pl.ds(c, ft, stride=ROW_TILES), :] + f)
    o_ref[...] = _layer_norm(jnp.concatenate(zs, axis=1), g_ref[...], b_ref[...])


def _final(pos, gates_t, x1t, ys, g2, b2):
    ft = FINAL_TILE
    n = gates_t.shape[0]
    any_spec = pl.BlockSpec(memory_space=pl.ANY)
    return pl.pallas_call(
        _final_kernel,
        out_shape=jax.ShapeDtypeStruct((n, D_MODEL), F32),
        grid=(n // ft,),
        in_specs=[any_spec, pl.BlockSpec((ft, TOP_K), lambda i: (i, 0)),
                  pl.BlockSpec((ft * ROW_TILES, LANES), lambda i: (i, 0)), any_spec,
                  _full(g2), _full(b2)],
        out_specs=pl.BlockSpec((ft, D_MODEL), lambda i: (i, 0)),
        scratch_shapes=[
            pltpu.SMEM((3, ft // DMA_GROUP, DMA_GROUP * TOP_K), I32),
            pltpu.VMEM((2 * TOP_K * ft * ROW_TILES, LANES), F32),
            pltpu.SemaphoreType.DMA((3,)),
            pltpu.SemaphoreType.DMA((2,)),
        ],
        compiler_params=_params(("arbitrary",)),
        name="final",
    )(pos, gates_t, x1t, ys, g2, b2)


def _mix_weight(w_in):
    qa, ckv, qi, ki, wi, qb, kb, vb, _, _ = jnp.split(w_in, np.cumsum(SPLITS)[:-1].tolist(), axis=-1)
    pad = lambda a: jnp.pad(a, ((0, 0), (0, LANES - a.shape[1])))
    return jnp.concatenate([qa, qi, ckv, pad(ki), pad(wi), qb, kb, vb], axis=1).astype(BF16)


def _layer(x, w_in, kv_norm_g, idx_k_norm_g, idx_k_norm_b, w_uk, w_uv, rel_bias, w_branch_a,
           w_branch_b, w_out, ln1_g, ln1_b, w_router, b_router, w_gate, b_gate, w_up, b_up,
           w_down, b_down, ln2_g, ln2_b):
    bsz, seq, _ = x.shape
    n = bsz * seq
    assert seq % KEY_TILE == 0 and n % TOKEN_TILE == 0
    xf = x.reshape(n, D_MODEL)
    row = lambda v: v.reshape(1, -1).astype(F32)

    w_mix = _mix_weight(w_in)
    w_gates = w_in[:, sum(SPLITS[:8]):].astype(BF16)
    wuk_bd = jnp.zeros((A_WIDTH, A_HEADS * A_KV_RANK), F32)
    wuv_pad = jnp.zeros((A_HEADS, A_KV_RANK, LANES), F32)
    for h in range(A_HEADS):
        wuk_bd = wuk_bd.at[h * A_HEAD_DIM:(h + 1) * A_HEAD_DIM, h * A_KV_RANK:(h + 1) * A_KV_RANK].set(w_uk[h])
        c0 = (h % 2) * A_HEAD_DIM
        wuv_pad = wuv_pad.at[h, :, c0:c0 + A_HEAD_DIM].set(w_uv[h])
    wuk_bd, wuv_pad = wuk_bd.astype(BF16), wuv_pad.astype(BF16)

    qlat, qi, ckv, ki, wi, qb, kb, vb = _projection(
        xf, w_mix, wuk_bd, row(kv_norm_g), row(idx_k_norm_g), row(idx_k_norm_b))

    nqb, nkt = seq // Q_BLOCK, seq // KEY_TILE
    kidx = ki.reshape(bsz, nkt, KEY_TILE, IDX_DIM)
    ckv4 = ckv.reshape(bsz, nkt, KEY_TILE, A_KV_RANK)
    ckvt = ckv4.transpose(0, 1, 3, 2)
    qi_r = qi.reshape(bsz, nqb, Q_BLOCK, IDX_HEADS, IDX_DIM).transpose(0, 1, 3, 2, 4).reshape(
        bsz, nqb, IDX_HEADS * Q_BLOCK, IDX_DIM)
    wit = wi.reshape(bsz, seq, IDX_HEADS).transpose(0, 2, 1)
    ya = _mixer_a(kidx, ckv4, ckvt, qi_r, wit, qlat.reshape(bsz, seq, -1), wuv_pad, bsz, seq)

    front = lambda a: jnp.pad(a.reshape(bsz, seq, B_WIDTH), ((0, 0), (B_PAD, 0), (0, 0)))
    yb = _mixer_b(qb.reshape(bsz, seq, B_WIDTH), front(kb), front(vb), _band_bias(rel_bias), bsz, seq)

    x1t, topi, gates = _merge(
        xf, ya.reshape(n, A_WIDTH), yb.reshape(n, B_WIDTH), w_gates, w_branch_a.astype(BF16),
        w_branch_b.astype(BF16), w_out.astype(BF16), row(ln1_g), row(ln1_b),
        w_router.T.astype(BF16), b_router.reshape(-1, 1).astype(F32))

    blk = MOE_BLOCK
    cap = n * TOP_K + N_EXPERTS * blk
    nblk = cap // blk
    rank, cnt = _route(topi)
    counts = cnt[:, 0]
    padded = (counts + blk - 1) // blk * blk
    pends = jnp.cumsum(padded).astype(I32)
    pstarts = pends - padded
    nused = (pends[-1:] // blk).astype(I32)
    blk_start = jnp.arange(nblk, dtype=I32) * blk
    blk_expert = jnp.minimum(jnp.sum(blk_start[:, None] >= pends[None, :], axis=1), N_EXPERTS - 1).astype(I32)

    pos = _positions(topi, rank, pstarts)
    pos_g = pos.T.reshape(n // DMA_GROUP, DMA_GROUP * TOP_K)
    xs = _dispatch(counts, pstarts, pos_g, x1t, cap)
    ys = _moe(blk_expert, nused, xs, w_gate.astype(BF16), b_gate[:, None, :], w_up.astype(BF16),
              b_up[:, None, :], w_down.astype(BF16), b_down[:, None, :])
    out = _final(pos_g, gates.T, x1t, ys, row(ln2_g), row(ln2_b))
    return out.reshape(bsz, seq, D_MODEL)


def kernel(x, w_in, kv_norm_g, idx_k_norm_g, idx_k_norm_b, w_uk, w_uv, rel_bias, w_branch_a,
           w_branch_b, w_out, ln1_g, ln1_b, w_router, b_router, w_gate, b_gate, w_up, b_up,
           w_down, b_down, ln2_g, ln2_b):
    for l in range(DEPTH):
        x = _layer(x, w_in[l], kv_norm_g[l], idx_k_norm_g[l], idx_k_norm_b[l], w_uk[l], w_uv[l],
                   rel_bias[l], w_branch_a[l], w_branch_b[l], w_out[l], ln1_g[l], ln1_b[l],
                   w_router[l], b_router[l], w_gate[l], b_gate[l], w_up[l], b_up[l], w_down[l],
                   b_down[l], ln2_g[l], ln2_b[l])
    return x
```

```python
import functools

import jax
import jax.numpy as jnp
import numpy as np
from jax import lax
from jax.experimental import pallas as pl
from jax.experimental.pallas import tpu as pltpu

F32 = jnp.float32
BF16 = jnp.bfloat16
I32 = jnp.int32

D_MODEL = 1024
CHUNK = 64
Q_BLOCK = 128
A_HEADS = 8
A_HEAD_DIM = 64
A_KV_RANK = 128
IDX_HEADS = 8
IDX_DIM = 64
IDX_TOPK = 256
IDX_SCALE = (IDX_HEADS * IDX_DIM) ** -0.5
B_HEADS = 8
B_HEAD_DIM = 64
B_LEFT_CHUNKS = 8
REL_CLIP = 128
N_EXPERTS = 32
TOP_K = 4
D_FF = 1024
SWIGLU_LIMIT = 7.0
SWIGLU_ALPHA = 1.702
DEPTH = 1
DN_ALPHA = (2.0 * DEPTH) ** 0.25
LN_EPS = 1e-5

A_WIDTH = A_HEADS * A_HEAD_DIM
B_WIDTH = B_HEADS * B_HEAD_DIM
SPLITS = [A_WIDTH, A_KV_RANK, IDX_HEADS * IDX_DIM, IDX_DIM, IDX_HEADS,
          B_WIDTH, B_WIDTH, B_WIDTH, D_MODEL, D_MODEL]

LANES = 128
SUBLANES = 8
KEY_TILE = 256
B_QTILE = 256
B_PAD = B_LEFT_CHUNKS * CHUNK
B_WIN = B_PAD + B_QTILE
TOKEN_TILE = 512
MOE_BLOCK = 512
FINAL_TILE = 256
ROW_TILES = D_MODEL // LANES
DMA_GROUP = 4
INT_MIN = -2 ** 31
VMEM_LIMIT = 56 * 1024 * 1024

A_SLOPES = [2.0 ** (-8.0 * (h + 1) / A_HEADS) for h in range(A_HEADS)]

_C_QA = 0
_C_QI = _C_QA + A_WIDTH
_C_CKV = _C_QI + IDX_HEADS * IDX_DIM
_C_KI = _C_CKV + A_KV_RANK
_C_WI = _C_KI + LANES
_C_QKVB = _C_WI + LANES
_C_END = _C_QKVB + 3 * B_WIDTH


def _params(sem, vmem=VMEM_LIMIT):
    return pltpu.CompilerParams(dimension_semantics=sem, vmem_limit_bytes=vmem)


def _full(a):
    return pl.BlockSpec(a.shape, lambda *_: (0,) * a.ndim)


def _proj_kernel(x_ref, w_ref, wuk_ref, kvg_ref, kig_ref, kib_ref,
                 qlat_ref, qi_ref, ckv_ref, ki_ref, wi_ref, qb_ref, kb_ref, vb_ref):
    xb = x_ref[...].astype(BF16)

    def mm(c0, c1):
        return jnp.dot(xb, w_ref[:, c0:c1], preferred_element_type=F32)

    qa = mm(_C_QA, _C_QI)
    qlat = jnp.dot(qa.astype(BF16), wuk_ref[...], preferred_element_type=F32)
    qlat_ref[...] = (qlat * (A_HEAD_DIM ** -0.5)).astype(BF16)
    qi_ref[...] = mm(_C_QI, _C_CKV).astype(BF16)

    ckv = mm(_C_CKV, _C_KI)
    ms = jnp.mean(ckv * ckv, axis=-1, keepdims=True)
    ckv_ref[...] = ((ckv * lax.rsqrt(ms + LN_EPS)) * kvg_ref[...]).astype(BF16)

    ki = mm(_C_KI, _C_WI)[:, :IDX_DIM]
    mu = jnp.mean(ki, axis=-1, keepdims=True)
    var = jnp.mean(jnp.square(ki - mu), axis=-1, keepdims=True)
    ki_ref[...] = (((ki - mu) * lax.rsqrt(var + LN_EPS)) * kig_ref[...] + kib_ref[...]).astype(BF16)

    wi_ref[...] = mm(_C_WI, _C_QKVB)[:, :IDX_HEADS]
    qb_ref[...] = mm(_C_QKVB, _C_QKVB + B_WIDTH).astype(BF16)
    kb_ref[...] = mm(_C_QKVB + B_WIDTH, _C_QKVB + 2 * B_WIDTH).astype(BF16)
    vb_ref[...] = mm(_C_QKVB + 2 * B_WIDTH, _C_END).astype(BF16)


def _projection(xf, w_mix, wuk_bd, kvg, kig, kib):
    n, tm = xf.shape[0], TOKEN_TILE
    row = lambda w: pl.BlockSpec((tm, w), lambda i: (i, 0))
    outs = [(A_HEADS * A_KV_RANK, BF16), (IDX_HEADS * IDX_DIM, BF16), (A_KV_RANK, BF16),
            (IDX_DIM, BF16), (IDX_HEADS, F32), (B_WIDTH, BF16), (B_WIDTH, BF16), (B_WIDTH, BF16)]
    return pl.pallas_call(
        _proj_kernel,
        out_shape=[jax.ShapeDtypeStruct((n, w), dt) for w, dt in outs],
        grid=(n // tm,),
        in_specs=[row(D_MODEL), _full(w_mix), _full(wuk_bd), _full(kvg), _full(kig), _full(kib)],
        out_specs=[row(w) for w, _ in outs],
        compiler_params=_params(("parallel",)),
        name="proj",
    )(xf, w_mix, wuk_bd, kvg, kig, kib)


def _mixer_a_kernel(kidx_ref, ckv_ref, ckvt_ref, qi_ref, wit_ref, qlat_ref, wuv_ref,
                    ya_ref, keys_ref, s_ref, acc_ref, *, seq, topk):
    i = pl.program_id(1)
    nkt = i // 2 + 1
    nsub = KEY_TILE // SUBLANES
    lane = lax.broadcasted_iota(I32, (1, LANES), 1)
    qpos = i * Q_BLOCK + lane
    qchunk = qpos >> 6
    row_iota = lax.broadcasted_iota(I32, (KEY_TILE, LANES), 0)

    qi_blk = qi_ref[0, 0]
    w_all = wit_ref[0] * IDX_SCALE

    def idx_body(kt, carry):
        lg = pl.dot(kidx_ref[0, kt], qi_blk, trans_b=True)
        isc = jnp.zeros((KEY_TILE, LANES), F32)
        for h in range(IDX_HEADS):
            isc = isc + jnp.maximum(lg[:, h * LANES:(h + 1) * LANES], 0.0) * w_all[h:h + 1, :]
        bits = lax.bitcast_convert_type(isc + 0.0, I32)
        key = bits ^ ((bits >> 31) & 0x7FFFFFFF)
        spos = kt * KEY_TILE + row_iota
        key = jnp.where((spos >> 6) <= qchunk, key, INT_MIN)
        keys_ref[kt] = key
        return carry

    def over_tiles(body, init):
        def pair(j, c):
            return body(2 * j + 1, body(2 * j, c))
        c = lax.fori_loop(0, nkt // 2, pair, init)
        return lax.cond(nkt % 2 == 1, lambda c: body(nkt - 1, c), lambda c: c, c)

    over_tiles(idx_body, 0)

    def count(pred):
        def body(kt, acc):
            spos = kt * KEY_TILE + row_iota
            m = jnp.where(pred(keys_ref[kt], spos), 1, 0).astype(I32)
            return acc + jnp.sum(m.reshape(nsub, SUBLANES, LANES), axis=0)
        acc = over_tiles(body, jnp.zeros((SUBLANES, LANES), I32))
        return jnp.sum(acc, axis=0, keepdims=True)

    def bit_body(it, carry):
        thr, cge = carry
        cand = thr ^ lax.shift_left(jnp.int32(1), 31 - it)
        cnt = count(lambda kv, spos: kv >= cand)
        ok = cnt >= topk
        return jnp.where(ok, cand, thr), jnp.where(ok, cnt, cge)

    thr0 = jnp.full((1, LANES), INT_MIN, I32)
    thr, cge = lax.fori_loop(0, 32, bit_body, (thr0, jnp.zeros((1, LANES), I32)))

    nbits = int(np.log2(seq)) + 1
    has_thr = thr > INT_MIN
    need_tie = jnp.max(jnp.where(has_thr, cge, 0)) > topk

    def tie_fn():
        room = topk - count(lambda kv, spos: kv > thr)

        def jb(it, cut):
            cand = cut | lax.shift_left(jnp.int32(1), nbits - 1 - it)
            f = count(lambda kv, spos: (kv == thr) & (spos < cand))
            return jnp.where(f <= room, cand, cut)
        return lax.fori_loop(0, nbits, jb, jnp.zeros((1, LANES), I32))

    cut = lax.cond(need_tie, tie_fn, lambda: jnp.full((1, LANES), 2 * seq, I32))
    cut = jnp.where(has_thr, cut, 0)

    qlat = qlat_ref[0]
    qr = jnp.concatenate([qlat[:, h * LANES:(h + 1) * LANES] for h in range(A_HEADS)], axis=0)

    def sc_body(kt, m):
        st = pl.dot(ckv_ref[0, kt], qr, trans_b=True)
        kv = keys_ref[kt]
        spos = kt * KEY_TILE + row_iota
        sel = (kv > thr) | ((kv == thr) & (spos < cut))
        dist = jnp.abs(qpos - spos).astype(F32)
        out = []
        for h in range(A_HEADS):
            sh = st[:, h * LANES:(h + 1) * LANES] - A_SLOPES[h] * dist
            sh = jnp.where(sel, sh, -jnp.inf)
            s_ref[kt, :, h * LANES:(h + 1) * LANES] = sh
            out.append(jnp.maximum(m[h], jnp.max(sh.reshape(nsub, SUBLANES, LANES), axis=0)))
        return tuple(out)

    m8 = over_tiles(sc_body, tuple(jnp.full((SUBLANES, LANES), -jnp.inf, F32) for _ in range(A_HEADS)))
    mx = [jnp.max(m, axis=0, keepdims=True) for m in m8]

    acc_ref[...] = jnp.zeros_like(acc_ref)

    def pv_body(kt, l):
        out, ps = [], []
        for h in range(A_HEADS):
            p = jnp.exp(s_ref[kt, :, h * LANES:(h + 1) * LANES] - mx[h])
            out.append(l[h] + jnp.sum(p.reshape(nsub, SUBLANES, LANES), axis=0))
            ps.append(p.astype(BF16))
        acc_ref[...] += jnp.dot(ckvt_ref[0, kt], jnp.concatenate(ps, axis=1),
                                preferred_element_type=F32)
        return tuple(out)

    l8 = over_tiles(pv_body, tuple(jnp.zeros((SUBLANES, LANES), F32) for _ in range(A_HEADS)))
    inv = [1.0 / jnp.sum(l, axis=0, keepdims=True) for l in l8]

    for j in range(A_HEADS // 2):
        y = None
        for h in (2 * j, 2 * j + 1):
            o = (acc_ref[:, h * LANES:(h + 1) * LANES] * inv[h]).astype(BF16)
            t = pl.dot(o, wuv_ref[h], trans_a=True)
            y = t if y is None else y + t
        ya_ref[0, :, j * LANES:(j + 1) * LANES] = y.astype(BF16)


def _mixer_a(kidx, ckv, ckvt, qi_r, wit, qlat, wuv_pad, bsz, seq):
    nqb = seq // Q_BLOCK
    nkt = seq // KEY_TILE
    topk = min(IDX_TOPK, seq // 4)
    kern = functools.partial(_mixer_a_kernel, seq=seq, topk=topk)
    return pl.pallas_call(
        kern,
        out_shape=jax.ShapeDtypeStruct((bsz, seq, A_WIDTH), BF16),
        grid=(bsz, nqb),
        in_specs=[
            pl.BlockSpec((1, nkt, KEY_TILE, IDX_DIM), lambda b, i: (b, 0, 0, 0)),
            pl.BlockSpec((1, nkt, KEY_TILE, A_KV_RANK), lambda b, i: (b, 0, 0, 0)),
            pl.BlockSpec((1, nkt, A_KV_RANK, KEY_TILE), lambda b, i: (b, 0, 0, 0)),
            pl.BlockSpec((1, 1, IDX_HEADS * Q_BLOCK, IDX_DIM), lambda b, i: (b, i, 0, 0)),
            pl.BlockSpec((1, IDX_HEADS, Q_BLOCK), lambda b, i: (b, 0, i)),
            pl.BlockSpec((1, Q_BLOCK, A_HEADS * A_KV_RANK), lambda b, i: (b, i, 0)),
            pl.BlockSpec(wuv_pad.shape, lambda b, i: (0, 0, 0)),
        ],
        out_specs=pl.BlockSpec((1, Q_BLOCK, A_WIDTH), lambda b, i: (b, i, 0)),
        scratch_shapes=[
            pltpu.VMEM((nkt, KEY_TILE, LANES), I32),
            pltpu.VMEM((nkt, KEY_TILE, A_HEADS * LANES), F32),
            pltpu.VMEM((A_KV_RANK, A_HEADS * LANES), F32),
        ],
        compiler_params=_params(("parallel", "arbitrary")),
        name="mixer_a",
    )(kidx, ckv, ckvt, qi_r, wit, qlat, wuv_pad)


def _mixer_b_kernel(q_ref, k_ref, v_ref, bias_ref, yb_ref):
    t0 = pl.multiple_of(pl.program_id(1) * B_QTILE, B_QTILE)
    kw = k_ref[0, pl.ds(t0, B_WIN), :]
    vw = v_ref[0, pl.ds(t0, B_WIN), :]
    q = q_ref[0]
    kvalid = (t0 - B_PAD + lax.broadcasted_iota(I32, (1, B_WIN), 1)) >= 0
    lane = lax.broadcasted_iota(I32, (1, LANES), 1)
    scale = B_HEAD_DIM ** -0.5
    for j in range(B_HEADS // 2):
        sl = slice(j * LANES, (j + 1) * LANES)
        qs, ks, vs = q[:, sl], kw[:, sl], vw[:, sl]
        outs = []
        for hh in range(2):
            mine = (lane >= B_HEAD_DIM) if hh else (lane < B_HEAD_DIM)
            qm = jnp.where(mine, qs, jnp.zeros_like(qs))
            s = pl.dot(qm, ks, trans_b=True) * scale + bias_ref[2 * j + hh]
            s = jnp.where(kvalid, s, -jnp.inf)
            m = jnp.max(s, axis=-1, keepdims=True)
            p = jnp.exp(s - m)
            l = jnp.sum(p, axis=-1, keepdims=True)
            outs.append(jnp.dot(p.astype(BF16), vs, preferred_element_type=F32) / l)
        yb_ref[0, :, sl] = jnp.where(lane < B_HEAD_DIM, outs[0], outs[1]).astype(BF16)


def _band_bias(rel_bias):
    tq = np.arange(B_QTILE)[:, None]
    j = np.arange(B_WIN)[None, :]
    band = (j // CHUNK >= tq // CHUNK) & (j // CHUNK <= tq // CHUNK + B_LEFT_CHUNKS)
    m = np.arange(B_QTILE - 1 + B_WIN)
    line = rel_bias[:, np.clip(B_WIN - 1 - m, -REL_CLIP, REL_CLIP) + REL_CLIP].astype(F32)
    table = jnp.stack([line[:, B_QTILE - 1 - t:B_QTILE - 1 - t + B_WIN] for t in range(B_QTILE)], axis=1)
    return jnp.where(jnp.asarray(band)[None], table, -jnp.inf)


def _mixer_b(qb, kb_pad, vb_pad, bias, bsz, seq):
    return pl.pallas_call(
        _mixer_b_kernel,
        out_shape=jax.ShapeDtypeStruct((bsz, seq, B_WIDTH), BF16),
        grid=(bsz, seq // B_QTILE),
        in_specs=[
            pl.BlockSpec((1, B_QTILE, B_WIDTH), lambda b, i: (b, i, 0)),
            pl.BlockSpec((1, seq + B_PAD, B_WIDTH), lambda b, i: (b, 0, 0)),
            pl.BlockSpec((1, seq + B_PAD, B_WIDTH), lambda b, i: (b, 0, 0)),
            pl.BlockSpec(bias.shape, lambda b, i: (0, 0, 0)),
        ],
        out_specs=pl.BlockSpec((1, B_QTILE, B_WIDTH), lambda b, i: (b, i, 0)),
        compiler_params=_params(("parallel", "arbitrary")),
        name="mixer_b",
    )(qb, kb_pad, vb_pad, bias)


def _layer_norm(z, g, b):
    mu = jnp.mean(z, axis=-1, keepdims=True)
    var = jnp.mean(jnp.square(z - mu), axis=-1, keepdims=True)
    return ((z - mu) * lax.rsqrt(var + LN_EPS)) * g + b


def _merge_kernel(x_ref, ya_ref, yb_ref, wg_ref, wba_ref, wbb_ref, wo_ref, g_ref, b_ref,
                  wr_ref, br_ref, x1t_ref, topi_ref, gate_ref):
    tm = x_ref.shape[0]
    x = x_ref[...]
    xb = x.astype(BF16)
    ga = jnp.dot(xb, wg_ref[:, :D_MODEL], preferred_element_type=F32)
    a = jnp.dot(ya_ref[...], wba_ref[...], preferred_element_type=F32)
    merged = jax.nn.sigmoid(ga) * a
    gb = jnp.dot(xb, wg_ref[:, D_MODEL:], preferred_element_type=F32)
    b = jnp.dot(yb_ref[...], wbb_ref[...], preferred_element_type=F32)
    merged = merged + jax.nn.sigmoid(gb) * b
    out = jnp.dot(merged.astype(BF16), wo_ref[...], preferred_element_type=F32)
    x1 = _layer_norm(DN_ALPHA * x + out, g_ref[...], b_ref[...])
    for c in range(ROW_TILES):
        x1t_ref[pl.ds(c, tm, stride=ROW_TILES), :] = x1[:, c * LANES:(c + 1) * LANES]

    lg = pl.dot(wr_ref[...], x1.astype(BF16), trans_b=True) + br_ref[...]
    eidx = lax.broadcasted_iota(I32, lg.shape, 0)
    vals, idxs = [], []
    for _ in range(TOP_K):
        m = jnp.max(lg, axis=0, keepdims=True)
        sel = jnp.min(jnp.where(lg == m, eidx, N_EXPERTS), axis=0, keepdims=True)
        vals.append(m)
        idxs.append(sel)
        lg = jnp.where(eidx == sel, -jnp.inf, lg)
    es = [jnp.exp(v - vals[0]) for v in vals]
    tot = es[0] + es[1] + es[2] + es[3]
    topi_ref[...] = jnp.concatenate(idxs, axis=0)
    gate_ref[...] = jnp.concatenate([e / tot for e in es], axis=0)


def _merge(xf, ya, yb, wg, wba, wbb, wo, g1, b1, wr_t, br):
    n, tm = xf.shape[0], TOKEN_TILE
    row = lambda w: pl.BlockSpec((tm, w), lambda i: (i, 0))
    col = pl.BlockSpec((TOP_K, tm), lambda i: (0, i))
    return pl.pallas_call(
        _merge_kernel,
        out_shape=[jax.ShapeDtypeStruct((n * ROW_TILES, LANES), F32),
                   jax.ShapeDtypeStruct((TOP_K, n), I32),
                   jax.ShapeDtypeStruct((TOP_K, n), F32)],
        grid=(n // tm,),
        in_specs=[row(D_MODEL), row(A_WIDTH), row(B_WIDTH), _full(wg), _full(wba), _full(wbb),
                  _full(wo), _full(g1), _full(b1), _full(wr_t), _full(br)],
        out_specs=[pl.BlockSpec((tm * ROW_TILES, LANES), lambda i: (i, 0)), col, col],
        compiler_params=_params(("parallel",)),
        name="merge",
    )(xf, ya, yb, wg, wba, wbb, wo, g1, b1, wr_t, br)


def _one_hot_rows(topi, k, width):
    eidx = lax.broadcasted_iota(I32, (N_EXPERTS, width), 0)
    return eidx == topi[k:k + 1, :]


def _route_kernel(topi_ref, tri_ref, rank_ref, cnt_ref, run_ref):
    @pl.when(pl.program_id(0) == 0)
    def _():
        run_ref[...] = jnp.zeros_like(run_ref)

    topi = topi_ref[...]
    tr = topi.shape[1]
    hot = [_one_hot_rows(topi, k, tr) for k in range(TOP_K)]
    oh = jnp.concatenate([jnp.where(h, 1.0, 0.0).astype(BF16) for h in hot], axis=0)
    prefix = jnp.dot(oh, tri_ref[...], preferred_element_type=F32)
    base = run_ref[...]
    ranks = []
    for k in range(TOP_K):
        tbl = prefix[k * N_EXPERTS:(k + 1) * N_EXPERTS, :] + base
        ranks.append(jnp.sum(jnp.where(hot[k], tbl, 0.0), axis=0, keepdims=True))
        base = base + jnp.sum(jnp.where(hot[k], 1.0, 0.0), axis=1, keepdims=True)
    rank_ref[...] = jnp.concatenate(ranks, axis=0).astype(I32)
    run_ref[...] = base
    cnt_ref[...] = base.astype(I32)


def _route(topi):
    n, tr = topi.shape[1], TOKEN_TILE
    tri = (np.arange(tr)[:, None] < np.arange(tr)[None, :]).astype(np.float32)
    tri = jnp.asarray(tri, BF16)
    col = pl.BlockSpec((TOP_K, tr), lambda i: (0, i))
    return pl.pallas_call(
        _route_kernel,
        out_shape=[jax.ShapeDtypeStruct((TOP_K, n), I32), jax.ShapeDtypeStruct((N_EXPERTS, 1), I32)],
        grid=(n // tr,),
        in_specs=[col, _full(tri)],
        out_specs=[col, pl.BlockSpec((N_EXPERTS, 1), lambda i: (0, 0))],
        scratch_shapes=[pltpu.VMEM((N_EXPERTS, 1), F32)],
        compiler_params=_params(("arbitrary",)),
        name="route",
    )(topi, tri)


def _positions_kernel(topi_ref, rank_ref, pstv_ref, pos_ref):
    topi = topi_ref[...]
    rows = []
    for k in range(TOP_K):
        start = jnp.sum(jnp.where(_one_hot_rows(topi, k, topi.shape[1]), pstv_ref[...], 0),
                        axis=0, keepdims=True)
        rows.append(rank_ref[k:k + 1, :] + start)
    pos_ref[...] = jnp.concatenate(rows, axis=0)


def _positions(topi, rank, pstarts):
    n, td = topi.shape[1], TOKEN_TILE
    col = pl.BlockSpec((TOP_K, td), lambda i: (0, i))
    pstv = pstarts.reshape(N_EXPERTS, 1)
    return pl.pallas_call(
        _positions_kernel,
        out_shape=jax.ShapeDtypeStruct((TOP_K, n), I32),
        grid=(n // td,),
        in_specs=[col, col, _full(pstv)],
        out_specs=col,
        compiler_params=_params(("parallel",)),
        name="positions",
    )(topi, rank, pstv)


def _dispatch_kernel(cnt_ref, pst_ref, pos_hbm, x_hbm, xs_hbm,
                     pos_smem, xbuf, zbuf, sem_p, sem_x, sem_d, sem_z):
    i = pl.program_id(0)
    n = pl.num_programs(0)
    ng = pos_smem.shape[1]
    td = ng * DMA_GROUP
    tile_rows = td * ROW_TILES

    def pos_copy(step, slot):
        return pltpu.make_async_copy(pos_hbm.at[pl.ds(step * ng, ng)], pos_smem.at[slot], sem_p.at[slot])

    def x_copy(step, slot):
        return pltpu.make_async_copy(x_hbm.at[pl.ds(step * tile_rows, tile_rows)], xbuf.at[slot],
                                     sem_x.at[slot])

    @pl.when(i == 0)
    def _():
        pos_copy(0, 0).start()
        x_copy(0, 0).start()

    @pl.when(i + 1 < n)
    def _():
        pos_copy(i + 1, (i + 1) % 2).start()
        x_copy(i + 1, (i + 1) % 3).start()

    slot = i % 2
    xslot = i % 3
    pos_copy(i, slot).wait()
    x_copy(i, xslot).wait()

    def step_bytes(s):
        rows_ = TOP_K * tile_rows
        return pltpu.make_async_copy(xs_hbm.at[pl.ds(0, rows_)], xs_hbm.at[pl.ds(0, rows_)], sem_d.at[s])

    def body(g, c):
        dst = [pos_smem[slot, g, j] * ROW_TILES for j in range(DMA_GROUP * TOP_K)]
        for u in range(DMA_GROUP):
            src = xbuf.at[xslot, pl.ds((g * DMA_GROUP + u) * ROW_TILES, ROW_TILES)]
            for k in range(TOP_K):
                pltpu.make_async_copy(src, xs_hbm.at[pl.ds(dst[u * TOP_K + k], ROW_TILES)],
                                      sem_d.at[xslot]).start()
        return c
    lax.fori_loop(0, ng, body, 0)

    @pl.when(i > 0)
    def _():
        step_bytes((i + 2) % 3).wait()

    @pl.when(i == n - 1)
    def _():
        step_bytes(xslot).wait()
        zbuf[...] = jnp.zeros_like(zbuf)

        def zero_copy(r):
            return pltpu.make_async_copy(zbuf, xs_hbm.at[pl.ds(r * ROW_TILES, ROW_TILES)], sem_z)

        def zero_rows(lo, hi):
            def start(r, c2):
                zero_copy(r).start()
                return c2

            def wait(r, c2):
                zero_copy(r).wait()
                return c2
            lax.fori_loop(lo, hi, start, 0)
            lax.fori_loop(lo, hi, wait, 0)

        def padded_end(e):
            return pst_ref[e] + (cnt_ref[e] + MOE_BLOCK - 1) // MOE_BLOCK * MOE_BLOCK

        def ebody(e, c):
            zero_rows(pst_ref[e] + cnt_ref[e], padded_end(e))
            return c
        lax.fori_loop(0, N_EXPERTS, ebody, 0)
        zero_rows(padded_end(N_EXPERTS - 1), xs_hbm.shape[0] // ROW_TILES)


def _dispatch(counts, pstarts, pos_g, x1t, cap):
    n, td = pos_g.shape[0] * DMA_GROUP, TOKEN_TILE
    any_spec = pl.BlockSpec(memory_space=pl.ANY)
    return pl.pallas_call(
        _dispatch_kernel,
        out_shape=jax.ShapeDtypeStruct((cap * ROW_TILES, LANES), F32),
        grid_spec=pltpu.PrefetchScalarGridSpec(
            num_scalar_prefetch=2,
            grid=(n // td,),
            in_specs=[any_spec, any_spec],
            out_specs=any_spec,
            scratch_shapes=[
                pltpu.SMEM((2, td // DMA_GROUP, DMA_GROUP * TOP_K), I32),
                pltpu.VMEM((3, td * ROW_TILES, LANES), F32),
                pltpu.VMEM((ROW_TILES, LANES), F32),
                pltpu.SemaphoreType.DMA((2,)),
                pltpu.SemaphoreType.DMA((3,)),
                pltpu.SemaphoreType.DMA((3,)),
                pltpu.SemaphoreType.DMA,
            ],
        ),
        compiler_params=_params(("arbitrary",)),
        name="dispatch",
    )(counts, pstarts, pos_g, x1t)


def _moe_kernel(be_ref, nu_ref, xs_ref, wg_ref, bg_ref, wu_ref, bu_ref, wd_ref, bd_ref, ys_ref):
    del be_ref
    blk = MOE_BLOCK
    used = pl.program_id(0) < nu_ref[0]

    @pl.when(used)
    def _():
        xb = jnp.concatenate([xs_ref[pl.ds(c, blk, stride=ROW_TILES), :].astype(BF16)
                              for c in range(ROW_TILES)], axis=1)
        hg = jnp.minimum(jnp.dot(xb, wg_ref[0], preferred_element_type=F32) + bg_ref[0], SWIGLU_LIMIT)
        hu = jnp.clip(jnp.dot(xb, wu_ref[0], preferred_element_type=F32) + bu_ref[0],
                      -SWIGLU_LIMIT, SWIGLU_LIMIT)
        h = (hu + 1.0) * (hg * jax.nn.sigmoid(SWIGLU_ALPHA * hg))
        y = jnp.dot(h.astype(BF16), wd_ref[0], preferred_element_type=F32) + bd_ref[0]
        for c in range(ROW_TILES):
            ys_ref[pl.ds(c, blk, stride=ROW_TILES), :] = y[:, c * LANES:(c + 1) * LANES]

    @pl.when(jnp.logical_not(used))
    def _():
        ys_ref[...] = jnp.zeros_like(ys_ref)


def _moe(blk_expert, nused, xs, wg, bg, wu, bu, wd, bd):
    blk = MOE_BLOCK
    nblk = xs.shape[0] // (blk * ROW_TILES)
    wspec = lambda a: pl.BlockSpec((1,) + a.shape[1:], lambda i, be, nu: (be[i], 0, 0))
    xspec = pl.BlockSpec((blk * ROW_TILES, LANES), lambda i, be, nu: (jnp.minimum(i, nu[0] - 1), 0))
    return pl.pallas_call(
        _moe_kernel,
        out_shape=jax.ShapeDtypeStruct(xs.shape, F32),
        grid_spec=pltpu.PrefetchScalarGridSpec(
            num_scalar_prefetch=2,
            grid=(nblk,),
            in_specs=[xspec, wspec(wg), wspec(bg), wspec(wu), wspec(bu), wspec(wd), wspec(bd)],
            out_specs=pl.BlockSpec((blk * ROW_TILES, LANES), lambda i, be, nu: (i, 0)),
        ),
        compiler_params=_params(("arbitrary",)),
        name="moe",
    )(blk_expert, nused, xs, wg, bg, wu, bu, wd, bd)


def _final_kernel(pos_hbm, gt_ref, x1t_ref, ys_hbm, g_ref, b_ref, o_ref,
                  pos_smem, ybuf, sem_p, sem_g):
    i = pl.program_id(0)
    n = pl.num_programs(0)
    ft = FINAL_TILE
    slot_rows = TOP_K * ft * ROW_TILES

    ng = ft // DMA_GROUP

    def pos_copy(tile, slot):
        return pltpu.make_async_copy(pos_hbm.at[pl.ds(tile * ng, ng)], pos_smem.at[slot], sem_p.at[slot])

    def gather_rows(pslot, yslot):
        def body(g, c):
            src = [pos_smem[pslot, g, j] * ROW_TILES for j in range(DMA_GROUP * TOP_K)]
            base = yslot * slot_rows + g * (DMA_GROUP * ROW_TILES)
            for u in range(DMA_GROUP):
                for k in range(TOP_K):
                    dst = base + (k * ft + u) * ROW_TILES
                    pltpu.make_async_copy(ys_hbm.at[pl.ds(src[u * TOP_K + k], ROW_TILES)],
                                          ybuf.at[pl.ds(dst, ROW_TILES)], sem_g.at[yslot]).start()
            return c
        lax.fori_loop(0, ng, body, 0)

    def gather_wait(yslot):
        pltpu.make_async_copy(ys_hbm.at[pl.ds(0, slot_rows)],
                              ybuf.at[pl.ds(yslot * slot_rows, slot_rows)], sem_g.at[yslot]).wait()

    @pl.when(i == 0)
    def _():
        c = pos_copy(0, 0)
        c.start()
        c.wait()
        gather_rows(0, 0)

        @pl.when(n > 1)
        def _():
            pos_copy(1, 1).start()

    @pl.when(i + 1 < n)
    def _():
        pos_copy(i + 1, (i + 1) % 3).wait()
        gather_rows((i + 1) % 3, (i + 1) % 2)

    @pl.when(i + 2 < n)
    def _():
        pos_copy(i + 2, (i + 2) % 3).start()

    slot = i % 2
    gather_wait(slot)
    g = gt_ref[...]
    zs = []
    for c in range(ROW_TILES):
        f = None
        for k in range(TOP_K):
            yk = ybuf[pl.ds(slot * slot_rows + k * ft * ROW_TILES + c, ft, stride=ROW_TILES), :]
            term = g[:, k:k + 1] * yk
            f = term if f is None else f + term
        zs.append(DN_ALPHA * x1t_ref[pl.ds(c, ft, stride=ROW_TILES), :] + f)
    o_ref[...] = _layer_norm(jnp.concatenate(zs, axis=1), g_ref[...], b_ref[...])


def _final(pos, gates_t, x1t, ys, g2, b2):
    ft = FINAL_TILE
    n = gates_t.shape[0]
    any_spec = pl.BlockSpec(memory_space=pl.ANY)
    return pl.pallas_call(
        _final_kernel,
        out_shape=jax.ShapeDtypeStruct((n, D_MODEL), F32),
        grid=(n // ft,),
        in_specs=[any_spec, pl.BlockSpec((ft, TOP_K), lambda i: (i, 0)),
                  pl.BlockSpec((ft * ROW_TILES, LANES), lambda i: (i, 0)), any_spec,
                  _full(g2), _full(b2)],
        out_specs=pl.BlockSpec((ft, D_MODEL), lambda i: (i, 0)),
        scratch_shapes=[
            pltpu.SMEM((3, ft // DMA_GROUP, DMA_GROUP * TOP_K), I32),
            pltpu.VMEM((2 * TOP_K * ft * ROW_TILES, LANES), F32),
            pltpu.SemaphoreType.DMA((3,)),
            pltpu.SemaphoreType.DMA((2,)),
        ],
        compiler_params=_params(("arbitrary",)),
        name="final",
    )(pos, gates_t, x1t, ys, g2, b2)


def _mix_weight(w_in):
    qa, ckv, qi, ki, wi, qb, kb, vb, _, _ = jnp.split(w_in, np.cumsum(SPLITS)[:-1].tolist(), axis=-1)
    pad = lambda a: jnp.pad(a, ((0, 0), (0, LANES - a.shape[1])))
    return jnp.concatenate([qa, qi, ckv, pad(ki), pad(wi), qb, kb, vb], axis=1).astype(BF16)


def _layer(x, w_in, kv_norm_g, idx_k_norm_g, idx_k_norm_b, w_uk, w_uv, rel_bias, w_branch_a,
           w_branch_b, w_out, ln1_g, ln1_b, w_router, b_router, w_gate, b_gate, w_up, b_up,
           w_down, b_down, ln2_g, ln2_b):
    bsz, seq, _ = x.shape
    n = bsz * seq
    assert seq % KEY_TILE == 0 and n % TOKEN_TILE == 0
    xf = x.reshape(n, D_MODEL)
    row = lambda v: v.reshape(1, -1).astype(F32)

    w_mix = _mix_weight(w_in)
    w_gates = w_in[:, sum(SPLITS[:8]):].astype(BF16)
    wuk_bd = jnp.zeros((A_WIDTH, A_HEADS * A_KV_RANK), F32)
    wuv_pad = jnp.zeros((A_HEADS, A_KV_RANK, LANES), F32)
    for h in range(A_HEADS):
        wuk_bd = wuk_bd.at[h * A_HEAD_DIM:(h + 1) * A_HEAD_DIM, h * A_KV_RANK:(h + 1) * A_KV_RANK].set(w_uk[h])
        c0 = (h % 2) * A_HEAD_DIM
        wuv_pad = wuv_pad.at[h, :, c0:c0 + A_HEAD_DIM].set(w_uv[h])
    wuk_bd, wuv_pad = wuk_bd.astype(BF16), wuv_pad.astype(BF16)

    qlat, qi, ckv, ki, wi, qb, kb, vb = _projection(
        xf, w_mix, wuk_bd, row(kv_norm_g), row(idx_k_norm_g), row(idx_k_norm_b))

    nqb, nkt = seq // Q_BLOCK, seq // KEY_TILE
    kidx = ki.reshape(bsz, nkt, KEY_TILE, IDX_DIM)
    ckv4 = ckv.reshape(bsz, nkt, KEY_TILE, A_KV_RANK)
    ckvt = ckv4.transpose(0, 1, 3, 2)
    qi_r = qi.reshape(bsz, nqb, Q_BLOCK, IDX_HEADS, IDX_DIM).transpose(0, 1, 3, 2, 4).reshape(
        bsz, nqb, IDX_HEADS * Q_BLOCK, IDX_DIM)
    wit = wi.reshape(bsz, seq, IDX_HEADS).transpose(0, 2, 1)
    ya = _mixer_a(kidx, ckv4, ckvt, qi_r, wit, qlat.reshape(bsz, seq, -1), wuv_pad, bsz, seq)

    front = lambda a: jnp.pad(a.reshape(bsz, seq, B_WIDTH), ((0, 0), (B_PAD, 0), (0, 0)))
    yb = _mixer_b(qb.reshape(bsz, seq, B_WIDTH), front(kb), front(vb), _band_bias(rel_bias), bsz, seq)

    x1t, topi, gates = _merge(
        xf, ya.reshape(n, A_WIDTH), yb.reshape(n, B_WIDTH), w_gates, w_branch_a.astype(BF16),
        w_branch_b.astype(BF16), w_out.astype(BF16), row(ln1_g), row(ln1_b),
        w_router.T.astype(BF16), b_router.reshape(-1, 1).astype(F32))

    blk = MOE_BLOCK
    cap = n * TOP_K + N_EXPERTS * blk
    nblk = cap // blk
    rank, cnt = _route(topi)
    counts = cnt[:, 0]
    padded = (counts + blk - 1) // blk * blk
    pends = jnp.cumsum(padded).astype(I32)
    pstarts = pends - padded
    nused = (pends[-1:] // blk).astype(I32)
    blk_start = jnp.arange(nblk, dtype=I32) * blk
    blk_expert = jnp.minimum(jnp.sum(blk_start[:, None] >= pends[None, :], axis=1), N_EXPERTS - 1).astype(I32)

    pos = _positions(topi, rank, pstarts)
    pos_g = pos.T.reshape(n // DMA_GROUP, DMA_GROUP * TOP_K)
    xs = _dispatch(counts, pstarts, pos_g, x1t, cap)
    ys = _moe(blk_expert, nused, xs, w_gate.astype(BF16), b_gate[:, None, :], w_up.astype(BF16),
              b_up[:, None, :], w_down.astype(BF16), b_down[:, None, :])
    out = _final(pos_g, gates.T, x1t, ys, row(ln2_g), row(ln2_b))
    return out.reshape(bsz, seq, D_MODEL)


def kernel(x, w_in, kv_norm_g, idx_k_norm_g, idx_k_norm_b, w_uk, w_uv, rel_bias, w_branch_a,
           w_branch_b, w_out, ln1_g, ln1_b, w_router, b_router, w_gate, b_gate, w_up, b_up,
           w_down, b_down, ln2_g, ln2_b):
    for l in range(DEPTH):
        x = _layer(x, w_in[l], kv_norm_g[l], idx_k_norm_g[l], idx_k_norm_b[l], w_uk[l], w_uv[l],
                   rel_bias[l], w_branch_a[l], w_branch_b[l], w_out[l], ln1_g[l], ln1_b[l],
                   w_router[l], b_router[l], w_gate[l], b_gate[l], w_up[l], b_up[l], w_down[l],
                   b_down[l], ln2_g[l], ln2_b[l])
    return x
```

```python
import functools

import jax
import jax.numpy as jnp
import numpy as np
from jax import lax
from jax.experimental import pallas as pl
from jax.experimental.pallas import tpu as pltpu

F32 = jnp.float32
BF16 = jnp.bfloat16
I32 = jnp.int32
I16 = jnp.int16

D_MODEL = 1024
CHUNK = 64
Q_BLOCK = 128
A_HEADS = 8
A_HEAD_DIM = 64
A_KV_RANK = 128
IDX_HEADS = 8
IDX_DIM = 64
IDX_TOPK = 256
IDX_SCALE = (IDX_HEADS * IDX_DIM) ** -0.5
B_HEADS = 8
B_HEAD_DIM = 64
B_LEFT_CHUNKS = 8
REL_CLIP = 128
N_EXPERTS = 32
TOP_K = 4
D_FF = 1024
SWIGLU_LIMIT = 7.0
SWIGLU_ALPHA = 1.702
DEPTH = 1
DN_ALPHA = (2.0 * DEPTH) ** 0.25
LN_EPS = 1e-5

A_WIDTH = A_HEADS * A_HEAD_DIM
B_WIDTH = B_HEADS * B_HEAD_DIM
SPLITS = [A_WIDTH, A_KV_RANK, IDX_HEADS * IDX_DIM, IDX_DIM, IDX_HEADS,
          B_WIDTH, B_WIDTH, B_WIDTH, D_MODEL, D_MODEL]

LANES = 128
SUBLANES = 8
KEY_TILE = 256
B_QTILE = 256
B_PAD = B_LEFT_CHUNKS * CHUNK
B_WIN = B_PAD + B_QTILE
TOKEN_TILE = 512
MOE_BLOCK = 512
FINAL_TILE = 256
ROW_TILES = D_MODEL // LANES
DMA_GROUP = 4
INT_MIN = -2 ** 31
HALF = 2 ** 15
VMEM_LIMIT = 56 * 1024 * 1024

A_SLOPES = [2.0 ** (-8.0 * (h + 1) / A_HEADS) for h in range(A_HEADS)]

_C_QA = 0
_C_QI = _C_QA + A_WIDTH
_C_CKV = _C_QI + IDX_HEADS * IDX_DIM
_C_KI = _C_CKV + A_KV_RANK
_C_WI = _C_KI + LANES
_C_QKVB = _C_WI + LANES
_C_END = _C_QKVB + 3 * B_WIDTH


def _params(sem, vmem=VMEM_LIMIT):
    return pltpu.CompilerParams(dimension_semantics=sem, vmem_limit_bytes=vmem)


def _full(a):
    return pl.BlockSpec(a.shape, lambda *_: (0,) * a.ndim)


def _proj_kernel(x_ref, w_ref, wuk_ref, kvg_ref, kig_ref, kib_ref,
                 qlat_ref, qi_ref, ckv_ref, ki_ref, wi_ref, qb_ref, kb_ref, vb_ref):
    xb = x_ref[...].astype(BF16)

    def mm(c0, c1):
        return jnp.dot(xb, w_ref[:, c0:c1], preferred_element_type=F32)

    qa = mm(_C_QA, _C_QI)
    qlat = jnp.dot(qa.astype(BF16), wuk_ref[...], preferred_element_type=F32)
    qlat_ref[...] = (qlat * (A_HEAD_DIM ** -0.5)).astype(BF16)
    qi_ref[...] = mm(_C_QI, _C_CKV).astype(BF16)

    ckv = mm(_C_CKV, _C_KI)
    ms = jnp.mean(ckv * ckv, axis=-1, keepdims=True)
    ckv_ref[...] = ((ckv * lax.rsqrt(ms + LN_EPS)) * kvg_ref[...]).astype(BF16)

    ki = mm(_C_KI, _C_WI)[:, :IDX_DIM]
    mu = jnp.mean(ki, axis=-1, keepdims=True)
    var = jnp.mean(jnp.square(ki - mu), axis=-1, keepdims=True)
    ki_ref[...] = (((ki - mu) * lax.rsqrt(var + LN_EPS)) * kig_ref[...] + kib_ref[...]).astype(BF16)

    wi_ref[...] = mm(_C_WI, _C_QKVB)[:, :IDX_HEADS]
    qb_ref[...] = mm(_C_QKVB, _C_QKVB + B_WIDTH).astype(BF16)
    kb_ref[...] = mm(_C_QKVB + B_WIDTH, _C_QKVB + 2 * B_WIDTH).astype(BF16)
    vb_ref[...] = mm(_C_QKVB + 2 * B_WIDTH, _C_END).astype(BF16)


def _projection(xf, w_mix, wuk_bd, kvg, kig, kib):
    n, tm = xf.shape[0], TOKEN_TILE
    row = lambda w: pl.BlockSpec((tm, w), lambda i: (i, 0))
    outs = [(A_HEADS * A_KV_RANK, BF16), (IDX_HEADS * IDX_DIM, BF16), (A_KV_RANK, BF16),
            (IDX_DIM, BF16), (IDX_HEADS, F32), (B_WIDTH, BF16), (B_WIDTH, BF16), (B_WIDTH, BF16)]
    return pl.pallas_call(
        _proj_kernel,
        out_shape=[jax.ShapeDtypeStruct((n, w), dt) for w, dt in outs],
        grid=(n // tm,),
        in_specs=[row(D_MODEL), _full(w_mix), _full(wuk_bd), _full(kvg), _full(kig), _full(kib)],
        out_specs=[row(w) for w, _ in outs],
        compiler_params=_params(("parallel",)),
        name="proj",
    )(xf, w_mix, wuk_bd, kvg, kig, kib)


def _mixer_a_kernel(kidx_ref, ckv_ref, ckvt_ref, qi_ref, wit_ref, qlat_ref, wuv_ref,
                    ya_ref, keys_ref, hi_ref, lo_ref, s_ref, acc_ref, *, seq, topk):
    i = pl.program_id(1)
    nkt = i // 2 + 1
    nsub = KEY_TILE // SUBLANES
    nsub16 = KEY_TILE // (2 * SUBLANES)
    lane = lax.broadcasted_iota(I32, (1, LANES), 1)
    qpos = i * Q_BLOCK + lane
    qchunk = qpos >> 6
    row_iota = lax.broadcasted_iota(I32, (KEY_TILE, LANES), 0)

    qi_blk = qi_ref[0, 0]
    w_all = wit_ref[0] * IDX_SCALE

    def idx_body(kt, carry):
        lg = pl.dot(kidx_ref[0, kt], qi_blk, trans_b=True)
        isc = jnp.zeros((KEY_TILE, LANES), F32)
        for h in range(IDX_HEADS):
            isc = isc + jnp.maximum(lg[:, h * LANES:(h + 1) * LANES], 0.0) * w_all[h:h + 1, :]
        bits = lax.bitcast_convert_type(isc + 0.0, I32)
        key = bits ^ ((bits >> 31) & 0x7FFFFFFF)
        spos = kt * KEY_TILE + row_iota
        key = jnp.where((spos >> 6) <= qchunk, key, INT_MIN)
        keys_ref[kt] = key
        hi_ref[kt] = (key >> 16).astype(I16)
        lo_ref[kt] = ((key & 0xFFFF) - HALF).astype(I16)
        return carry

    def over_tiles(body, init):
        def pair(j, c):
            return body(2 * j + 1, body(2 * j, c))
        c = lax.fori_loop(0, nkt // 2, pair, init)
        return lax.cond(nkt % 2 == 1, lambda c: body(nkt - 1, c), lambda c: c, c)

    over_tiles(idx_body, 0)

    def count(pred):
        def body(kt, acc):
            spos = kt * KEY_TILE + row_iota
            m = jnp.where(pred(keys_ref[kt], spos), 1, 0).astype(I32)
            return acc + jnp.sum(m.reshape(nsub, SUBLANES, LANES), axis=0)
        acc = over_tiles(body, jnp.zeros((SUBLANES, LANES), I32))
        return jnp.sum(acc, axis=0, keepdims=True)

    rows16 = 2 * SUBLANES

    def packed(v):
        return jnp.broadcast_to(v, (rows16, LANES)).astype(I16)

    def count16(ref, pred):
        def body(kt, acc):
            ones = [jnp.where(pred(ref[kt, j * rows16:(j + 1) * rows16, :]), jnp.int16(1), jnp.int16(0))
                    for j in range(nsub16)]
            while len(ones) > 1:
                ones = [a + b for a, b in zip(ones[::2], ones[1::2])]
            return acc + ones[0]
        acc = over_tiles(body, jnp.zeros((rows16, LANES), I16))
        return jnp.sum(acc.astype(I32), axis=0, keepdims=True)

    def select16(ref, want, count_all):
        def bit_body(it, carry):
            u, cge = carry
            cand = u | lax.shift_left(jnp.int32(1), 15 - it)
            c16 = packed(cand - HALF)
            cnt = count16(ref, lambda v: v >= c16)
            ok = cnt >= want
            return jnp.where(ok, cand, u), jnp.where(ok, cnt, cge)
        return lax.fori_loop(0, 16, bit_body, (jnp.zeros((1, LANES), I32), count_all))

    nkeys = jnp.full((1, LANES), nkt * KEY_TILE, I32)
    want = jnp.full((1, LANES), topk, I32)
    u_hi, cge_hi = select16(hi_ref, want, nkeys)
    t_hi = packed(u_hi - HALF)
    cgt_hi = count16(hi_ref, lambda v: v > t_hi)

    def lo_body(kt, carry):
        for j in range(nsub16):
            rows = slice(j * rows16, (j + 1) * rows16)
            lo_ref[kt, rows, :] = jnp.where(hi_ref[kt, rows, :] == t_hi, lo_ref[kt, rows, :],
                                            jnp.int16(-HALF))
        return carry
    over_tiles(lo_body, 0)
    u_lo, cge_lo = select16(lo_ref, want - cgt_hi, cge_hi - cgt_hi)
    thr = lax.shift_left(u_hi - HALF, 16) | u_lo
    cge = cgt_hi + cge_lo

    nbits = int(np.log2(seq)) + 1
    has_thr = thr > INT_MIN
    need_tie = jnp.max(jnp.where(has_thr, cge, 0)) > topk

    def tie_fn():
        room = topk - count(lambda kv, spos: kv > thr)

        def jb(it, cut):
            cand = cut | lax.shift_left(jnp.int32(1), nbits - 1 - it)
            f = count(lambda kv, spos: (kv == thr) & (spos < cand))
            return jnp.where(f <= room, cand, cut)
        return lax.fori_loop(0, nbits, jb, jnp.zeros((1, LANES), I32))

    cut = lax.cond(need_tie, tie_fn, lambda: jnp.full((1, LANES), 2 * seq, I32))
    cut = jnp.where(has_thr, cut, 0)

    qlat = qlat_ref[0]
    qr = jnp.concatenate([qlat[:, h * LANES:(h + 1) * LANES] for h in range(A_HEADS)], axis=0)

    def sc_body(kt, m):
        st = pl.dot(ckv_ref[0, kt], qr, trans_b=True)
        kv = keys_ref[kt]
        spos = kt * KEY_TILE + row_iota
        sel = (kv > thr) | ((kv == thr) & (spos < cut))
        dist = jnp.abs(qpos - spos).astype(F32)
        out = []
        for h in range(A_HEADS):
            sh = st[:, h * LANES:(h + 1) * LANES] - A_SLOPES[h] * dist
            sh = jnp.where(sel, sh, -jnp.inf)
            s_ref[kt, :, h * LANES:(h + 1) * LANES] = sh
            out.append(jnp.maximum(m[h], jnp.max(sh.reshape(nsub, SUBLANES, LANES), axis=0)))
        return tuple(out)

    m8 = over_tiles(sc_body, tuple(jnp.full((SUBLANES, LANES), -jnp.inf, F32) for _ in range(A_HEADS)))
    mx = [jnp.max(m, axis=0, keepdims=True) for m in m8]

    acc_ref[...] = jnp.zeros_like(acc_ref)

    def pv_body(kt, l):
        out, ps = [], []
        for h in range(A_HEADS):
            p = jnp.exp(s_ref[kt, :, h * LANES:(h + 1) * LANES] - mx[h])
            out.append(l[h] + jnp.sum(p.reshape(nsub, SUBLANES, LANES), axis=0))
            ps.append(p.astype(BF16))
        acc_ref[...] += jnp.dot(ckvt_ref[0, kt], jnp.concatenate(ps, axis=1),
                                preferred_element_type=F32)
        return tuple(out)

    l8 = over_tiles(pv_body, tuple(jnp.zeros((SUBLANES, LANES), F32) for _ in range(A_HEADS)))
    inv = [1.0 / jnp.sum(l, axis=0, keepdims=True) for l in l8]

    for j in range(A_HEADS // 2):
        y = None
        for h in (2 * j, 2 * j + 1):
            o = (acc_ref[:, h * LANES:(h + 1) * LANES] * inv[h]).astype(BF16)
            t = pl.dot(o, wuv_ref[h], trans_a=True)
            y = t if y is None else y + t
        ya_ref[0, :, j * LANES:(j + 1) * LANES] = y.astype(BF16)


def _mixer_a(kidx, ckv, ckvt, qi_r, wit, qlat, wuv_pad, bsz, seq):
    nqb = seq // Q_BLOCK
    nkt = seq // KEY_TILE
    topk = min(IDX_TOPK, seq // 4)
    kern = functools.partial(_mixer_a_kernel, seq=seq, topk=topk)
    return pl.pallas_call(
        kern,
        out_shape=jax.ShapeDtypeStruct((bsz, seq, A_WIDTH), BF16),
        grid=(bsz, nqb),
        in_specs=[
            pl.BlockSpec((1, nkt, KEY_TILE, IDX_DIM), lambda b, i: (b, 0, 0, 0)),
            pl.BlockSpec((1, nkt, KEY_TILE, A_KV_RANK), lambda b, i: (b, 0, 0, 0)),
            pl.BlockSpec((1, nkt, A_KV_RANK, KEY_TILE), lambda b, i: (b, 0, 0, 0)),
            pl.BlockSpec((1, 1, IDX_HEADS * Q_BLOCK, IDX_DIM), lambda b, i: (b, i, 0, 0)),
            pl.BlockSpec((1, IDX_HEADS, Q_BLOCK), lambda b, i: (b, 0, i)),
            pl.BlockSpec((1, Q_BLOCK, A_HEADS * A_KV_RANK), lambda b, i: (b, i, 0)),
            pl.BlockSpec(wuv_pad.shape, lambda b, i: (0, 0, 0)),
        ],
        out_specs=pl.BlockSpec((1, Q_BLOCK, A_WIDTH), lambda b, i: (b, i, 0)),
        scratch_shapes=[
            pltpu.VMEM((nkt, KEY_TILE, LANES), I32),
            pltpu.VMEM((nkt, KEY_TILE, LANES), I16),
            pltpu.VMEM((nkt, KEY_TILE, LANES), I16),
            pltpu.VMEM((nkt, KEY_TILE, A_HEADS * LANES), F32),
            pltpu.VMEM((A_KV_RANK, A_HEADS * LANES), F32),
        ],
        compiler_params=_params(("parallel", "arbitrary")),
        name="mixer_a",
    )(kidx, ckv, ckvt, qi_r, wit, qlat, wuv_pad)


def _mixer_b_kernel(q_ref, k_ref, v_ref, bias_ref, yb_ref):
    t0 = pl.multiple_of(pl.program_id(1) * B_QTILE, B_QTILE)
    kw = k_ref[0, pl.ds(t0, B_WIN), :]
    vw = v_ref[0, pl.ds(t0, B_WIN), :]
    q = q_ref[0]
    kvalid = (t0 - B_PAD + lax.broadcasted_iota(I32, (1, B_WIN), 1)) >= 0
    lane = lax.broadcasted_iota(I32, (1, LANES), 1)
    scale = B_HEAD_DIM ** -0.5
    for j in range(B_HEADS // 2):
        sl = slice(j * LANES, (j + 1) * LANES)
        qs, ks, vs = q[:, sl], kw[:, sl], vw[:, sl]
        outs = []
        for hh in range(2):
            mine = (lane >= B_HEAD_DIM) if hh else (lane < B_HEAD_DIM)
            qm = jnp.where(mine, qs, jnp.zeros_like(qs))
            s = pl.dot(qm, ks, trans_b=True) * scale + bias_ref[2 * j + hh]
            s = jnp.where(kvalid, s, -jnp.inf)
            m = jnp.max(s, axis=-1, keepdims=True)
            p = jnp.exp(s - m)
            l = jnp.sum(p, axis=-1, keepdims=True)
            outs.append(jnp.dot(p.astype(BF16), vs, preferred_element_type=F32) / l)
        yb_ref[0, :, sl] = jnp.where(lane < B_HEAD_DIM, outs[0], outs[1]).astype(BF16)


def _band_bias(rel_bias):
    tq = np.arange(B_QTILE)[:, None]
    j = np.arange(B_WIN)[None, :]
    band = (j // CHUNK >= tq // CHUNK) & (j // CHUNK <= tq // CHUNK + B_LEFT_CHUNKS)
    m = np.arange(B_QTILE - 1 + B_WIN)
    line = rel_bias[:, np.clip(B_WIN - 1 - m, -REL_CLIP, REL_CLIP) + REL_CLIP].astype(F32)
    table = jnp.stack([line[:, B_QTILE - 1 - t:B_QTILE - 1 - t + B_WIN] for t in range(B_QTILE)], axis=1)
    return jnp.where(jnp.asarray(band)[None], table, -jnp.inf)


def _mixer_b(qb, kb_pad, vb_pad, bias, bsz, seq):
    return pl.pallas_call(
        _mixer_b_kernel,
        out_shape=jax.ShapeDtypeStruct((bsz, seq, B_WIDTH), BF16),
        grid=(bsz, seq // B_QTILE),
        in_specs=[
            pl.BlockSpec((1, B_QTILE, B_WIDTH), lambda b, i: (b, i, 0)),
            pl.BlockSpec((1, seq + B_PAD, B_WIDTH), lambda b, i: (b, 0, 0)),
            pl.BlockSpec((1, seq + B_PAD, B_WIDTH), lambda b, i: (b, 0, 0)),
            pl.BlockSpec(bias.shape, lambda b, i: (0, 0, 0)),
        ],
        out_specs=pl.BlockSpec((1, B_QTILE, B_WIDTH), lambda b, i: (b, i, 0)),
        compiler_params=_params(("parallel", "arbitrary")),
        name="mixer_b",
    )(qb, kb_pad, vb_pad, bias)


def _layer_norm(z, g, b):
    mu = jnp.mean(z, axis=-1, keepdims=True)
    var = jnp.mean(jnp.square(z - mu), axis=-1, keepdims=True)
    return ((z - mu) * lax.rsqrt(var + LN_EPS)) * g + b


def _merge_kernel(x_ref, ya_ref, yb_ref, wg_ref, wba_ref, wbb_ref, wo_ref, g_ref, b_ref,
                  wr_ref, br_ref, x1t_ref, topi_ref, gate_ref):
    tm = x_ref.shape[0]
    x = x_ref[...]
    xb = x.astype(BF16)
    ga = jnp.dot(xb, wg_ref[:, :D_MODEL], preferred_element_type=F32)
    a = jnp.dot(ya_ref[...], wba_ref[...], preferred_element_type=F32)
    merged = jax.nn.sigmoid(ga) * a
    gb = jnp.dot(xb, wg_ref[:, D_MODEL:], preferred_element_type=F32)
    b = jnp.dot(yb_ref[...], wbb_ref[...], preferred_element_type=F32)
    merged = merged + jax.nn.sigmoid(gb) * b
    out = jnp.dot(merged.astype(BF16), wo_ref[...], preferred_element_type=F32)
    x1 = _layer_norm(DN_ALPHA * x + out, g_ref[...], b_ref[...])
    for c in range(ROW_TILES):
        x1t_ref[pl.ds(c, tm, stride=ROW_TILES), :] = x1[:, c * LANES:(c + 1) * LANES]

    lg = pl.dot(wr_ref[...], x1.astype(BF16), trans_b=True) + br_ref[...]
    eidx = lax.broadcasted_iota(I32, lg.shape, 0)
    vals, idxs = [], []
    for _ in range(TOP_K):
        m = jnp.max(lg, axis=0, keepdims=True)
        sel = jnp.min(jnp.where(lg == m, eidx, N_EXPERTS), axis=0, keepdims=True)
        vals.append(m)
        idxs.append(sel)
        lg = jnp.where(eidx == sel, -jnp.inf, lg)
    es = [jnp.exp(v - vals[0]) for v in vals]
    tot = es[0] + es[1] + es[2] + es[3]
    topi_ref[...] = jnp.concatenate(idxs, axis=0)
    gate_ref[...] = jnp.concatenate([e / tot for e in es], axis=0)


def _merge(xf, ya, yb, wg, wba, wbb, wo, g1, b1, wr_t, br):
    n, tm = xf.shape[0], TOKEN_TILE
    row = lambda w: pl.BlockSpec((tm, w), lambda i: (i, 0))
    col = pl.BlockSpec((TOP_K, tm), lambda i: (0, i))
    return pl.pallas_call(
        _merge_kernel,
        out_shape=[jax.ShapeDtypeStruct((n * ROW_TILES, LANES), F32),
                   jax.ShapeDtypeStruct((TOP_K, n), I32),
                   jax.ShapeDtypeStruct((TOP_K, n), F32)],
        grid=(n // tm,),
        in_specs=[row(D_MODEL), row(A_WIDTH), row(B_WIDTH), _full(wg), _full(wba), _full(wbb),
                  _full(wo), _full(g1), _full(b1), _full(wr_t), _full(br)],
        out_specs=[pl.BlockSpec((tm * ROW_TILES, LANES), lambda i: (i, 0)), col, col],
        compiler_params=_params(("parallel",)),
        name="merge",
    )(xf, ya, yb, wg, wba, wbb, wo, g1, b1, wr_t, br)


def _one_hot_rows(topi, k, width):
    eidx = lax.broadcasted_iota(I32, (N_EXPERTS, width), 0)
    return eidx == topi[k:k + 1, :]


def _route_kernel(topi_ref, tri_ref, rank_ref, cnt_ref, run_ref):
    @pl.when(pl.program_id(0) == 0)
    def _():
        run_ref[...] = jnp.zeros_like(run_ref)

    topi = topi_ref[...]
    tr = topi.shape[1]
    hot = [_one_hot_rows(topi, k, tr) for k in range(TOP_K)]
    oh = jnp.concatenate([jnp.where(h, 1.0, 0.0).astype(BF16) for h in hot], axis=0)
    prefix = jnp.dot(oh, tri_ref[...], preferred_element_type=F32)
    base = run_ref[...]
    ranks = []
    for k in range(TOP_K):
        tbl = prefix[k * N_EXPERTS:(k + 1) * N_EXPERTS, :] + base
        ranks.append(jnp.sum(jnp.where(hot[k], tbl, 0.0), axis=0, keepdims=True))
        base = base + jnp.sum(jnp.where(hot[k], 1.0, 0.0), axis=1, keepdims=True)
    rank_ref[...] = jnp.concatenate(ranks, axis=0).astype(I32)
    run_ref[...] = base
    cnt_ref[...] = base.astype(I32)


def _route(topi):
    n, tr = topi.shape[1], TOKEN_TILE
    tri = (np.arange(tr)[:, None] < np.arange(tr)[None, :]).astype(np.float32)
    tri = jnp.asarray(tri, BF16)
    col = pl.BlockSpec((TOP_K, tr), lambda i: (0, i))
    return pl.pallas_call(
        _route_kernel,
        out_shape=[jax.ShapeDtypeStruct((TOP_K, n), I32), jax.ShapeDtypeStruct((N_EXPERTS, 1), I32)],
        grid=(n // tr,),
        in_specs=[col, _full(tri)],
        out_specs=[col, pl.BlockSpec((N_EXPERTS, 1), lambda i: (0, 0))],
        scratch_shapes=[pltpu.VMEM((N_EXPERTS, 1), F32)],
        compiler_params=_params(("arbitrary",)),
        name="route",
    )(topi, tri)


def _positions_kernel(topi_ref, rank_ref, pstv_ref, pos_ref):
    topi = topi_ref[...]
    rows = []
    for k in range(TOP_K):
        start = jnp.sum(jnp.where(_one_hot_rows(topi, k, topi.shape[1]), pstv_ref[...], 0),
                        axis=0, keepdims=True)
        rows.append(rank_ref[k:k + 1, :] + start)
    pos_ref[...] = jnp.concatenate(rows, axis=0)


def _positions(topi, rank, pstarts):
    n, td = topi.shape[1], TOKEN_TILE
    col = pl.BlockSpec((TOP_K, td), lambda i: (0, i))
    pstv = pstarts.reshape(N_EXPERTS, 1)
    return pl.pallas_call(
        _positions_kernel,
        out_shape=jax.ShapeDtypeStruct((TOP_K, n), I32),
        grid=(n // td,),
        in_specs=[col, col, _full(pstv)],
        out_specs=col,
        compiler_params=_params(("parallel",)),
        name="positions",
    )(topi, rank, pstv)


def _dispatch_kernel(cnt_ref, pst_ref, pos_hbm, x_hbm, xs_hbm,
                     pos_smem, xbuf, zbuf, sem_p, sem_x, sem_d, sem_z):
    i = pl.program_id(0)
    n = pl.num_programs(0)
    ng = pos_smem.shape[1]
    td = ng * DMA_GROUP
    tile_rows = td * ROW_TILES

    def pos_copy(step, slot):
        return pltpu.make_async_copy(pos_hbm.at[pl.ds(step * ng, ng)], pos_smem.at[slot], sem_p.at[slot])

    def x_copy(step, slot):
        return pltpu.make_async_copy(x_hbm.at[pl.ds(step * tile_rows, tile_rows)], xbuf.at[slot],
                                     sem_x.at[slot])

    @pl.when(i == 0)
    def _():
        pos_copy(0, 0).start()
        x_copy(0, 0).start()

    @pl.when(i + 1 < n)
    def _():
        pos_copy(i + 1, (i + 1) % 2).start()
        x_copy(i + 1, (i + 1) % 3).start()

    slot = i % 2
    xslot = i % 3
    pos_copy(i, slot).wait()
    x_copy(i, xslot).wait()

    def step_bytes(s):
        rows_ = TOP_K * tile_rows
        return pltpu.make_async_copy(xs_hbm.at[pl.ds(0, rows_)], xs_hbm.at[pl.ds(0, rows_)], sem_d.at[s])

    def body(g, c):
        dst = [pos_smem[slot, g, j] * ROW_TILES for j in range(DMA_GROUP * TOP_K)]
        for u in range(DMA_GROUP):
            src = xbuf.at[xslot, pl.ds((g * DMA_GROUP + u) * ROW_TILES, ROW_TILES)]
            for k in range(TOP_K):
                pltpu.make_async_copy(src, xs_hbm.at[pl.ds(dst[u * TOP_K + k], ROW_TILES)],
                                      sem_d.at[xslot]).start(priority=k % 2)
        return c
    lax.fori_loop(0, ng, body, 0)

    @pl.when(i > 0)
    def _():
        step_bytes((i + 2) % 3).wait()

    @pl.when(i == n - 1)
    def _():
        step_bytes(xslot).wait()
        zbuf[...] = jnp.zeros_like(zbuf)

        def zero_copy(r):
            return pltpu.make_async_copy(zbuf, xs_hbm.at[pl.ds(r * ROW_TILES, ROW_TILES)], sem_z)

        def zero_rows(lo, hi):
            def start(r, c2):
                zero_copy(r).start()
                return c2

            def wait(r, c2):
                zero_copy(r).wait()
                return c2
            lax.fori_loop(lo, hi, start, 0)
            lax.fori_loop(lo, hi, wait, 0)

        def padded_end(e):
            return pst_ref[e] + (cnt_ref[e] + MOE_BLOCK - 1) // MOE_BLOCK * MOE_BLOCK

        def ebody(e, c):
            zero_rows(pst_ref[e] + cnt_ref[e], padded_end(e))
            return c
        lax.fori_loop(0, N_EXPERTS, ebody, 0)
        zero_rows(padded_end(N_EXPERTS - 1), xs_hbm.shape[0] // ROW_TILES)


def _dispatch(counts, pstarts, pos_g, x1t, cap):
    n, td = pos_g.shape[0] * DMA_GROUP, TOKEN_TILE
    any_spec = pl.BlockSpec(memory_space=pl.ANY)
    return pl.pallas_call(
        _dispatch_kernel,
        out_shape=jax.ShapeDtypeStruct((cap * ROW_TILES, LANES), F32),
        grid_spec=pltpu.PrefetchScalarGridSpec(
            num_scalar_prefetch=2,
            grid=(n // td,),
            in_specs=[any_spec, any_spec],
            out_specs=any_spec,
            scratch_shapes=[
                pltpu.SMEM((2, td // DMA_GROUP, DMA_GROUP * TOP_K), I32),
                pltpu.VMEM((3, td * ROW_TILES, LANES), F32),
                pltpu.VMEM((ROW_TILES, LANES), F32),
                pltpu.SemaphoreType.DMA((2,)),
                pltpu.SemaphoreType.DMA((3,)),
                pltpu.SemaphoreType.DMA((3,)),
                pltpu.SemaphoreType.DMA,
            ],
        ),
        compiler_params=_params(("arbitrary",)),
        name="dispatch",
    )(counts, pstarts, pos_g, x1t)


def _moe_kernel(be_ref, nu_ref, xs_ref, wg_ref, bg_ref, wu_ref, bu_ref, wd_ref, bd_ref, ys_ref):
    del be_ref
    blk = MOE_BLOCK
    used = pl.program_id(0) < nu_ref[0]

    @pl.when(used)
    def _():
        xb = jnp.concatenate([xs_ref[pl.ds(c, blk, stride=ROW_TILES), :].astype(BF16)
                              for c in range(ROW_TILES)], axis=1)
        hg = jnp.minimum(jnp.dot(xb, wg_ref[0], preferred_element_type=F32) + bg_ref[0], SWIGLU_LIMIT)
        hu = jnp.clip(jnp.dot(xb, wu_ref[0], preferred_element_type=F32) + bu_ref[0],
                      -SWIGLU_LIMIT, SWIGLU_LIMIT)
        h = (hu + 1.0) * (hg * jax.nn.sigmoid(SWIGLU_ALPHA * hg))
        y = jnp.dot(h.astype(BF16), wd_ref[0], preferred_element_type=F32) + bd_ref[0]
        for c in range(ROW_TILES):
            ys_ref[pl.ds(c, blk, stride=ROW_TILES), :] = y[:, c * LANES:(c + 1) * LANES]

    @pl.when(jnp.logical_not(used))
    def _():
        ys_ref[...] = jnp.zeros_like(ys_ref)


def _moe(blk_expert, nused, xs, wg, bg, wu, bu, wd, bd):
    blk = MOE_BLOCK
    nblk = xs.shape[0] // (blk * ROW_TILES)
    wspec = lambda a: pl.BlockSpec((1,) + a.shape[1:], lambda i, be, nu: (be[i], 0, 0))
    xspec = pl.BlockSpec((blk * ROW_TILES, LANES), lambda i, be, nu: (jnp.minimum(i, nu[0] - 1), 0))
    return pl.pallas_call(
        _moe_kernel,
        out_shape=jax.ShapeDtypeStruct(xs.shape, F32),
        grid_spec=pltpu.PrefetchScalarGridSpec(
            num_scalar_prefetch=2,
            grid=(nblk,),
            in_specs=[xspec, wspec(wg), wspec(bg), wspec(wu), wspec(bu), wspec(wd), wspec(bd)],
            out_specs=pl.BlockSpec((blk * ROW_TILES, LANES), lambda i, be, nu: (i, 0)),
        ),
        compiler_params=_params(("arbitrary",)),
        name="moe",
    )(blk_expert, nused, xs, wg, bg, wu, bu, wd, bd)


def _final_kernel(pos_hbm, gt_ref, x1t_ref, ys_hbm, g_ref, b_ref, o_ref,
                  pos_smem, ybuf, sem_p, sem_g):
    i = pl.program_id(0)
    n = pl.num_programs(0)
    ft = FINAL_TILE
    slot_rows = TOP_K * ft * ROW_TILES

    ng = ft // DMA_GROUP

    def pos_copy(tile, slot):
        return pltpu.make_async_copy(pos_hbm.at[pl.ds(tile * ng, ng)], pos_smem.at[slot], sem_p.at[slot])

    def gather_rows(pslot, yslot):
        def body(g, c):
            src = [pos_smem[pslot, g, j] * ROW_TILES for j in range(DMA_GROUP * TOP_K)]
            base = yslot * slot_rows + g * (DMA_GROUP * ROW_TILES)
            for u in range(DMA_GROUP):
                for k in range(TOP_K):
                    dst = base + (k * ft + u) * ROW_TILES
                    pltpu.make_async_copy(ys_hbm.at[pl.ds(src[u * TOP_K + k], ROW_TILES)],
                                          ybuf.at[pl.ds(dst, ROW_TILES)],
                                          sem_g.at[yslot]).start(priority=k % 2)
            return c
        lax.fori_loop(0, ng, body, 0)

    def gather_wait(yslot):
        pltpu.make_async_copy(ys_hbm.at[pl.ds(0, slot_rows)],
                              ybuf.at[pl.ds(yslot * slot_rows, slot_rows)], sem_g.at[yslot]).wait()

    @pl.when(i == 0)
    def _():
        c = pos_copy(0, 0)
        c.start()
        c.wait()
        gather_rows(0, 0)

        @pl.when(n > 1)
        def _():
            pos_copy(1, 1).start()

    @pl.when(i + 1 < n)
    def _():
        pos_copy(i + 1, (i + 1) % 3).wait()
        gather_rows((i + 1) % 3, (i + 1) % 2)

    @pl.when(i + 2 < n)
    def _():
        pos_copy(i + 2, (i + 2) % 3).start()

    slot = i % 2
    gather_wait(slot)
    g = gt_ref[...]
    zs = []
    for c in range(ROW_TILES):
        f = None
        for k in range(TOP_K):
            yk = ybuf[pl.ds(slot * slot_rows + k * ft * ROW_TILES + c, ft, stride=ROW_TILES), :]
            term = g[:, k:k + 1] * yk
            f = term if f is None else f + term
        zs.append(DN_ALPHA * x1t_ref[pl.ds(c, ft, stride=ROW_TILES), :] + f)
    o_ref[...] = _layer_norm(jnp.concatenate(zs, axis=1), g_ref[...], b_ref[...])


def _final(pos, gates_t, x1t, ys, g2, b2):
    ft = FINAL_TILE
    n = gates_t.shape[0]
    any_spec = pl.BlockSpec(memory_space=pl.ANY)
    return pl.pallas_call(
        _final_kernel,
        out_shape=jax.ShapeDtypeStruct((n, D_MODEL), F32),
        grid=(n // ft,),
        in_specs=[any_spec, pl.BlockSpec((ft, TOP_K), lambda i: (i, 0)),
                  pl.BlockSpec((ft * ROW_TILES, LANES), lambda i: (i, 0)), any_spec,
                  _full(g2), _full(b2)],
        out_specs=pl.BlockSpec((ft, D_MODEL), lambda i: (i, 0)),
        scratch_shapes=[
            pltpu.SMEM((3, ft // DMA_GROUP, DMA_GROUP * TOP_K), I32),
            pltpu.VMEM((2 * TOP_K * ft * ROW_TILES, LANES), F32),
            pltpu.SemaphoreType.DMA((3,)),
            pltpu.SemaphoreType.DMA((2,)),
        ],
        compiler_params=_params(("arbitrary",)),
        name="final",
    )(pos, gates_t, x1t, ys, g2, b2)


def _mix_weight(w_in):
    qa, ckv, qi, ki, wi, qb, kb, vb, _, _ = jnp.split(w_in, np.cumsum(SPLITS)[:-1].tolist(), axis=-1)
    pad = lambda a: jnp.pad(a, ((0, 0), (0, LANES - a.shape[1])))
    return jnp.concatenate([qa, qi, ckv, pad(ki), pad(wi), qb, kb, vb], axis=1).astype(BF16)


def _layer(x, w_in, kv_norm_g, idx_k_norm_g, idx_k_norm_b, w_uk, w_uv, rel_bias, w_branch_a,
           w_branch_b, w_out, ln1_g, ln1_b, w_router, b_router, w_gate, b_gate, w_up, b_up,
           w_down, b_down, ln2_g, ln2_b):
    bsz, seq, _ = x.shape
    n = bsz * seq
    assert seq % KEY_TILE == 0 and n % TOKEN_TILE == 0
    xf = x.reshape(n, D_MODEL)
    row = lambda v: v.reshape(1, -1).astype(F32)

    w_mix = _mix_weight(w_in)
    w_gates = w_in[:, sum(SPLITS[:8]):].astype(BF16)
    wuk_bd = jnp.zeros((A_WIDTH, A_HEADS * A_KV_RANK), F32)
    wuv_pad = jnp.zeros((A_HEADS, A_KV_RANK, LANES), F32)
    for h in range(A_HEADS):
        wuk_bd = wuk_bd.at[h * A_HEAD_DIM:(h + 1) * A_HEAD_DIM, h * A_KV_RANK:(h + 1) * A_KV_RANK].set(w_uk[h])
        c0 = (h % 2) * A_HEAD_DIM
        wuv_pad = wuv_pad.at[h, :, c0:c0 + A_HEAD_DIM].set(w_uv[h])
    wuk_bd, wuv_pad = wuk_bd.astype(BF16), wuv_pad.astype(BF16)

    qlat, qi, ckv, ki, wi, qb, kb, vb = _projection(
        xf, w_mix, wuk_bd, row(kv_norm_g), row(idx_k_norm_g), row(idx_k_norm_b))

    nqb, nkt = seq // Q_BLOCK, seq // KEY_TILE
    kidx = ki.reshape(bsz, nkt, KEY_TILE, IDX_DIM)
    ckv4 = ckv.reshape(bsz, nkt, KEY_TILE, A_KV_RANK)
    ckvt = ckv4.transpose(0, 1, 3, 2)
    qi_r = qi.reshape(bsz, nqb, Q_BLOCK, IDX_HEADS, IDX_DIM).transpose(0, 1, 3, 2, 4).reshape(
        bsz, nqb, IDX_HEADS * Q_BLOCK, IDX_DIM)
    wit = wi.reshape(bsz, seq, IDX_HEADS).transpose(0, 2, 1)
    ya = _mixer_a(kidx, ckv4, ckvt, qi_r, wit, qlat.reshape(bsz, seq, -1), wuv_pad, bsz, seq)

    front = lambda a: jnp.pad(a.reshape(bsz, seq, B_WIDTH), ((0, 0), (B_PAD, 0), (0, 0)))
    yb = _mixer_b(qb.reshape(bsz, seq, B_WIDTH), front(kb), front(vb), _band_bias(rel_bias), bsz, seq)

    x1t, topi, gates = _merge(
        xf, ya.reshape(n, A_WIDTH), yb.reshape(n, B_WIDTH), w_gates, w_branch_a.astype(BF16),
        w_branch_b.astype(BF16), w_out.astype(BF16), row(ln1_g), row(ln1_b),
        w_router.T.astype(BF16), b_router.reshape(-1, 1).astype(F32))

    blk = MOE_BLOCK
    cap = n * TOP_K + N_EXPERTS * blk
    nblk = cap // blk
    rank, cnt = _route(topi)
    counts = cnt[:, 0]
    padded = (counts + blk - 1) // blk * blk
    pends = jnp.cumsum(padded).astype(I32)
    pstarts = pends - padded
    nused = (pends[-1:] // blk).astype(I32)
    blk_start = jnp.arange(nblk, dtype=I32) * blk
    blk_expert = jnp.minimum(jnp.sum(blk_start[:, None] >= pends[None, :], axis=1), N_EXPERTS - 1).astype(I32)

    pos = _positions(topi, rank, pstarts)
    pos_g = pos.T.reshape(n // DMA_GROUP, DMA_GROUP * TOP_K)
    xs = _dispatch(counts, pstarts, pos_g, x1t, cap)
    ys = _moe(blk_expert, nused, xs, w_gate.astype(BF16), b_gate[:, None, :], w_up.astype(BF16),
              b_up[:, None, :], w_down.astype(BF16), b_down[:, None, :])
    out = _final(pos_g, gates.T, x1t, ys, row(ln2_g), row(ln2_b))
    return out.reshape(bsz, seq, D_MODEL)


def kernel(x, w_in, kv_norm_g, idx_k_norm_g, idx_k_norm_b, w_uk, w_uv, rel_bias, w_branch_a,
           w_branch_b, w_out, ln1_g, ln1_b, w_router, b_router, w_gate, b_gate, w_up, b_up,
           w_down, b_down, ln2_g, ln2_b):
    for l in range(DEPTH):
        x = _layer(x, w_in[l], kv_norm_g[l], idx_k_norm_g[l], idx_k_norm_b[l], w_uk[l], w_uv[l],
                   rel_bias[l], w_branch_a[l], w_branch_b[l], w_out[l], ln1_g[l], ln1_b[l],
                   w_router[l], b_router[l], w_gate[l], b_gate[l], w_up[l], b_up[l], w_down[l],
                   b_down[l], ln2_g[l], ln2_b[l])
    return x
```

```python
import functools

import jax
import jax.numpy as jnp
import numpy as np
from jax import lax
from jax.experimental import pallas as pl
from jax.experimental.pallas import tpu as pltpu

F32 = jnp.float32
BF16 = jnp.bfloat16
I32 = jnp.int32

D_MODEL = 1024
CHUNK = 64
Q_BLOCK = 128
A_HEADS = 8
A_HEAD_DIM = 64
A_KV_RANK = 128
IDX_HEADS = 8
IDX_DIM = 64
IDX_TOPK = 256
IDX_SCALE = (IDX_HEADS * IDX_DIM) ** -0.5
B_HEADS = 8
B_HEAD_DIM = 64
B_LEFT_CHUNKS = 8
REL_CLIP = 128
N_EXPERTS = 32
TOP_K = 4
D_FF = 1024
SWIGLU_LIMIT = 7.0
SWIGLU_ALPHA = 1.702
DEPTH = 1
DN_ALPHA = (2.0 * DEPTH) ** 0.25
LN_EPS = 1e-5

A_WIDTH = A_HEADS * A_HEAD_DIM
B_WIDTH = B_HEADS * B_HEAD_DIM
SPLITS = [A_WIDTH, A_KV_RANK, IDX_HEADS * IDX_DIM, IDX_DIM, IDX_HEADS,
          B_WIDTH, B_WIDTH, B_WIDTH, D_MODEL, D_MODEL]

LANES = 128
SUBLANES = 8
KEY_TILE = 256
B_QTILE = 256
B_PAD = B_LEFT_CHUNKS * CHUNK
B_WIN = B_PAD + B_QTILE
TOKEN_TILE = 512
MOE_BLOCK = 512
FINAL_TILE = 256
ROW_TILES = D_MODEL // LANES
DMA_GROUP = 4
INT_MIN = -2 ** 31
VMEM_LIMIT = 56 * 1024 * 1024

A_SLOPES = [2.0 ** (-8.0 * (h + 1) / A_HEADS) for h in range(A_HEADS)]

_C_QA = 0
_C_QI = _C_QA + A_WIDTH
_C_CKV = _C_QI + IDX_HEADS * IDX_DIM
_C_KI = _C_CKV + A_KV_RANK
_C_WI = _C_KI + LANES
_C_QKVB = _C_WI + LANES
_C_END = _C_QKVB + 3 * B_WIDTH


def _params(sem, vmem=VMEM_LIMIT):
    return pltpu.CompilerParams(dimension_semantics=sem, vmem_limit_bytes=vmem)


def _full(a):
    return pl.BlockSpec(a.shape, lambda *_: (0,) * a.ndim)


def _proj_kernel(x_ref, w_ref, wuk_ref, kvg_ref, kig_ref, kib_ref,
                 qlat_ref, qi_ref, ckv_ref, ki_ref, wi_ref, qb_ref, kb_ref, vb_ref):
    xb = x_ref[...].astype(BF16)

    def mm(c0, c1):
        return jnp.dot(xb, w_ref[:, c0:c1], preferred_element_type=F32)

    qa = mm(_C_QA, _C_QI)
    qlat = jnp.dot(qa.astype(BF16), wuk_ref[...], preferred_element_type=F32)
    qlat_ref[...] = (qlat * (A_HEAD_DIM ** -0.5)).astype(BF16)
    qi_ref[...] = mm(_C_QI, _C_CKV).astype(BF16)

    ckv = mm(_C_CKV, _C_KI)
    ms = jnp.mean(ckv * ckv, axis=-1, keepdims=True)
    ckv_ref[...] = ((ckv * lax.rsqrt(ms + LN_EPS)) * kvg_ref[...]).astype(BF16)

    ki = mm(_C_KI, _C_WI)[:, :IDX_DIM]
    mu = jnp.mean(ki, axis=-1, keepdims=True)
    var = jnp.mean(jnp.square(ki - mu), axis=-1, keepdims=True)
    ki_ref[...] = (((ki - mu) * lax.rsqrt(var + LN_EPS)) * kig_ref[...] + kib_ref[...]).astype(BF16)

    wi_ref[...] = mm(_C_WI, _C_QKVB)[:, :IDX_HEADS]
    qb_ref[...] = mm(_C_QKVB, _C_QKVB + B_WIDTH).astype(BF16)
    kb_ref[...] = mm(_C_QKVB + B_WIDTH, _C_QKVB + 2 * B_WIDTH).astype(BF16)
    vb_ref[...] = mm(_C_QKVB + 2 * B_WIDTH, _C_END).astype(BF16)


def _projection(xf, w_mix, wuk_bd, kvg, kig, kib):
    n, tm = xf.shape[0], TOKEN_TILE
    row = lambda w: pl.BlockSpec((tm, w), lambda i: (i, 0))
    outs = [(A_HEADS * A_KV_RANK, BF16), (IDX_HEADS * IDX_DIM, BF16), (A_KV_RANK, BF16),
            (IDX_DIM, BF16), (IDX_HEADS, F32), (B_WIDTH, BF16), (B_WIDTH, BF16), (B_WIDTH, BF16)]
    return pl.pallas_call(
        _proj_kernel,
        out_shape=[jax.ShapeDtypeStruct((n, w), dt) for w, dt in outs],
        grid=(n // tm,),
        in_specs=[row(D_MODEL), _full(w_mix), _full(wuk_bd), _full(kvg), _full(kig), _full(kib)],
        out_specs=[row(w) for w, _ in outs],
        compiler_params=_params(("parallel",)),
        name="proj",
    )(xf, w_mix, wuk_bd, kvg, kig, kib)


def _mixer_a_kernel(kidx_ref, ckv_ref, ckvt_ref, qi_ref, wit_ref, qlat_ref, wuv_ref,
                    ya_ref, keys_ref, s_ref, acc_ref, *, seq, topk):
    i = pl.program_id(1)
    nkt = i // 2 + 1
    nsub = KEY_TILE // SUBLANES
    lane = lax.broadcasted_iota(I32, (1, LANES), 1)
    qpos = i * Q_BLOCK + lane
    qchunk = qpos >> 6
    row_iota = lax.broadcasted_iota(I32, (KEY_TILE, LANES), 0)

    qi_blk = qi_ref[0, 0]
    w_all = wit_ref[0] * IDX_SCALE

    def idx_body(kt, carry):
        lg = jnp.dot(kidx_ref[0, kt], qi_blk, preferred_element_type=F32)
        isc = jnp.zeros((KEY_TILE, LANES), F32)
        for h in range(IDX_HEADS):
            isc = isc + jnp.maximum(lg[:, h * LANES:(h + 1) * LANES], 0.0) * w_all[h:h + 1, :]
        bits = lax.bitcast_convert_type(isc + 0.0, I32)
        key = bits ^ ((bits >> 31) & 0x7FFFFFFF)
        spos = kt * KEY_TILE + row_iota
        key = jnp.where((spos >> 6) <= qchunk, key, INT_MIN)
        keys_ref[kt] = key
        return carry

    def over_tiles(body, init):
        def pair(j, c):
            return body(2 * j + 1, body(2 * j, c))
        c = lax.fori_loop(0, nkt // 2, pair, init)
        return lax.cond(nkt % 2 == 1, lambda c: body(nkt - 1, c), lambda c: c, c)

    over_tiles(idx_body, 0)

    def count(pred):
        def body(kt, acc):
            spos = kt * KEY_TILE + row_iota
            m = jnp.where(pred(keys_ref[kt], spos), 1, 0).astype(I32)
            return acc + jnp.sum(m.reshape(nsub, SUBLANES, LANES), axis=0)
        acc = over_tiles(body, jnp.zeros((SUBLANES, LANES), I32))
        return jnp.sum(acc, axis=0, keepdims=True)

    def bit_body(it, carry):
        thr, cge = carry
        cand = thr ^ lax.shift_left(jnp.int32(1), 31 - it)
        cnt = count(lambda kv, spos: kv >= cand)
        ok = cnt >= topk
        return jnp.where(ok, cand, thr), jnp.where(ok, cnt, cge)

    thr0 = jnp.full((1, LANES), INT_MIN, I32)
    thr, cge = lax.fori_loop(0, 32, bit_body, (thr0, jnp.zeros((1, LANES), I32)))

    nbits = int(np.log2(seq)) + 1
    has_thr = thr > INT_MIN
    need_tie = jnp.max(jnp.where(has_thr, cge, 0)) > topk

    def tie_fn():
        room = topk - count(lambda kv, spos: kv > thr)

        def jb(it, cut):
            cand = cut | lax.shift_left(jnp.int32(1), nbits - 1 - it)
            f = count(lambda kv, spos: (kv == thr) & (spos < cand))
            return jnp.where(f <= room, cand, cut)
        return lax.fori_loop(0, nbits, jb, jnp.zeros((1, LANES), I32))

    cut = lax.cond(need_tie, tie_fn, lambda: jnp.full((1, LANES), 2 * seq, I32))
    cut = jnp.where(has_thr, cut, 0)

    qr = qlat_ref[0, 0]

    def sc_body(kt, m):
        st = jnp.dot(ckv_ref[0, kt], qr, preferred_element_type=F32)
        kv = keys_ref[kt]
        spos = kt * KEY_TILE + row_iota
        sel = (kv > thr) | ((kv == thr) & (spos < cut))
        dist = jnp.abs(qpos - spos).astype(F32)
        out = []
        for h in range(A_HEADS):
            sh = st[:, h * LANES:(h + 1) * LANES] - A_SLOPES[h] * dist
            sh = jnp.where(sel, sh, -jnp.inf)
            s_ref[kt, :, h * LANES:(h + 1) * LANES] = sh
            out.append(jnp.maximum(m[h], jnp.max(sh.reshape(nsub, SUBLANES, LANES), axis=0)))
        return tuple(out)

    m8 = over_tiles(sc_body, tuple(jnp.full((SUBLANES, LANES), -jnp.inf, F32) for _ in range(A_HEADS)))
    mx = [jnp.max(m, axis=0, keepdims=True) for m in m8]

    acc_ref[...] = jnp.zeros_like(acc_ref)

    def pv_body(kt, l):
        out, ps = [], []
        for h in range(A_HEADS):
            p = jnp.exp(s_ref[kt, :, h * LANES:(h + 1) * LANES] - mx[h])
            out.append(l[h] + jnp.sum(p.reshape(nsub, SUBLANES, LANES), axis=0))
            ps.append(p.astype(BF16))
        acc_ref[...] += jnp.dot(ckvt_ref[0, kt], jnp.concatenate(ps, axis=1),
                                preferred_element_type=F32)
        return tuple(out)

    l8 = over_tiles(pv_body, tuple(jnp.zeros((SUBLANES, LANES), F32) for _ in range(A_HEADS)))
    inv = [1.0 / jnp.sum(l, axis=0, keepdims=True) for l in l8]

    for j in range(A_HEADS // 2):
        y = None
        for h in (2 * j, 2 * j + 1):
            o = (acc_ref[:, h * LANES:(h + 1) * LANES] * inv[h]).astype(BF16)
            t = pl.dot(o, wuv_ref[h], trans_a=True)
            y = t if y is None else y + t
        ya_ref[0, :, j * LANES:(j + 1) * LANES] = y.astype(BF16)


def _mixer_a(kidx, ckv, ckvt, qi_r, wit, qlat, wuv_pad, bsz, seq):
    nqb = seq // Q_BLOCK
    nkt = seq // KEY_TILE
    topk = min(IDX_TOPK, seq // 4)
    kern = functools.partial(_mixer_a_kernel, seq=seq, topk=topk)
    return pl.pallas_call(
        kern,
        out_shape=jax.ShapeDtypeStruct((bsz, seq, A_WIDTH), BF16),
        grid=(bsz, nqb),
        in_specs=[
            pl.BlockSpec((1, nkt, KEY_TILE, IDX_DIM), lambda b, i: (b, 0, 0, 0)),
            pl.BlockSpec((1, nkt, KEY_TILE, A_KV_RANK), lambda b, i: (b, 0, 0, 0)),
            pl.BlockSpec((1, nkt, A_KV_RANK, KEY_TILE), lambda b, i: (b, 0, 0, 0)),
            pl.BlockSpec((1, 1, IDX_DIM, IDX_HEADS * Q_BLOCK), lambda b, i: (b, i, 0, 0)),
            pl.BlockSpec((1, IDX_HEADS, Q_BLOCK), lambda b, i: (b, 0, i)),
            pl.BlockSpec((1, 1, A_KV_RANK, A_HEADS * Q_BLOCK), lambda b, i: (b, i, 0, 0)),
            pl.BlockSpec(wuv_pad.shape, lambda b, i: (0, 0, 0)),
        ],
        out_specs=pl.BlockSpec((1, Q_BLOCK, A_WIDTH), lambda b, i: (b, i, 0)),
        scratch_shapes=[
            pltpu.VMEM((nkt, KEY_TILE, LANES), I32),
            pltpu.VMEM((nkt, KEY_TILE, A_HEADS * LANES), F32),
            pltpu.VMEM((A_KV_RANK, A_HEADS * LANES), F32),
        ],
        compiler_params=_params(("parallel", "arbitrary")),
        name="mixer_a",
    )(kidx, ckv, ckvt, qi_r, wit, qlat, wuv_pad)


def _mixer_b_kernel(q_ref, k_ref, v_ref, bias_ref, yb_ref):
    t0 = pl.multiple_of(pl.program_id(1) * B_QTILE, B_QTILE)
    kw = k_ref[0, pl.ds(t0, B_WIN), :]
    vw = v_ref[0, pl.ds(t0, B_WIN), :]
    q = q_ref[0]
    kvalid = (t0 - B_PAD + lax.broadcasted_iota(I32, (1, B_WIN), 1)) >= 0
    lane = lax.broadcasted_iota(I32, (1, LANES), 1)
    scale = B_HEAD_DIM ** -0.5
    for j in range(B_HEADS // 2):
        sl = slice(j * LANES, (j + 1) * LANES)
        qs, ks, vs = q[:, sl], kw[:, sl], vw[:, sl]
        outs = []
        for hh in range(2):
            mine = (lane >= B_HEAD_DIM) if hh else (lane < B_HEAD_DIM)
            qm = jnp.where(mine, qs, jnp.zeros_like(qs))
            s = pl.dot(qm, ks, trans_b=True) * scale + bias_ref[2 * j + hh]
            s = jnp.where(kvalid, s, -jnp.inf)
            m = jnp.max(s, axis=-1, keepdims=True)
            p = jnp.exp(s - m)
            l = jnp.sum(p, axis=-1, keepdims=True)
            outs.append(jnp.dot(p.astype(BF16), vs, preferred_element_type=F32) / l)
        yb_ref[0, :, sl] = jnp.where(lane < B_HEAD_DIM, outs[0], outs[1]).astype(BF16)


def _band_bias(rel_bias):
    tq = np.arange(B_QTILE)[:, None]
    j = np.arange(B_WIN)[None, :]
    band = (j // CHUNK >= tq // CHUNK) & (j // CHUNK <= tq // CHUNK + B_LEFT_CHUNKS)
    m = np.arange(B_QTILE - 1 + B_WIN)
    line = rel_bias[:, np.clip(B_WIN - 1 - m, -REL_CLIP, REL_CLIP) + REL_CLIP].astype(F32)
    table = jnp.stack([line[:, B_QTILE - 1 - t:B_QTILE - 1 - t + B_WIN] for t in range(B_QTILE)], axis=1)
    return jnp.where(jnp.asarray(band)[None], table, -jnp.inf)


def _mixer_b(qb, kb_pad, vb_pad, bias, bsz, seq):
    return pl.pallas_call(
        _mixer_b_kernel,
        out_shape=jax.ShapeDtypeStruct((bsz, seq, B_WIDTH), BF16),
        grid=(bsz, seq // B_QTILE),
        in_specs=[
            pl.BlockSpec((1, B_QTILE, B_WIDTH), lambda b, i: (b, i, 0)),
            pl.BlockSpec((1, seq + B_PAD, B_WIDTH), lambda b, i: (b, 0, 0)),
            pl.BlockSpec((1, seq + B_PAD, B_WIDTH), lambda b, i: (b, 0, 0)),
            pl.BlockSpec(bias.shape, lambda b, i: (0, 0, 0)),
        ],
        out_specs=pl.BlockSpec((1, B_QTILE, B_WIDTH), lambda b, i: (b, i, 0)),
        compiler_params=_params(("parallel", "arbitrary")),
        name="mixer_b",
    )(qb, kb_pad, vb_pad, bias)


def _layer_norm(z, g, b):
    mu = jnp.mean(z, axis=-1, keepdims=True)
    var = jnp.mean(jnp.square(z - mu), axis=-1, keepdims=True)
    return ((z - mu) * lax.rsqrt(var + LN_EPS)) * g + b


def _merge_kernel(x_ref, ya_ref, yb_ref, wg_ref, wba_ref, wbb_ref, wo_ref, g_ref, b_ref,
                  wr_ref, br_ref, x1t_ref, topi_ref, gate_ref):
    tm = x_ref.shape[0]
    x = x_ref[...]
    xb = x.astype(BF16)
    ga = jnp.dot(xb, wg_ref[:, :D_MODEL], preferred_element_type=F32)
    a = jnp.dot(ya_ref[...], wba_ref[...], preferred_element_type=F32)
    merged = jax.nn.sigmoid(ga) * a
    gb = jnp.dot(xb, wg_ref[:, D_MODEL:], preferred_element_type=F32)
    b = jnp.dot(yb_ref[...], wbb_ref[...], preferred_element_type=F32)
    merged = merged + jax.nn.sigmoid(gb) * b
    out = jnp.dot(merged.astype(BF16), wo_ref[...], preferred_element_type=F32)
    x1 = _layer_norm(DN_ALPHA * x + out, g_ref[...], b_ref[...])
    for c in range(ROW_TILES):
        x1t_ref[pl.ds(c, tm, stride=ROW_TILES), :] = x1[:, c * LANES:(c + 1) * LANES]

    lg = pl.dot(wr_ref[...], x1.astype(BF16), trans_b=True) + br_ref[...]
    eidx = lax.broadcasted_iota(I32, lg.shape, 0)
    vals, idxs = [], []
    for _ in range(TOP_K):
        m = jnp.max(lg, axis=0, keepdims=True)
        sel = jnp.min(jnp.where(lg == m, eidx, N_EXPERTS), axis=0, keepdims=True)
        vals.append(m)
        idxs.append(sel)
        lg = jnp.where(eidx == sel, -jnp.inf, lg)
    es = [jnp.exp(v - vals[0]) for v in vals]
    tot = es[0] + es[1] + es[2] + es[3]
    topi_ref[...] = jnp.concatenate(idxs, axis=0)
    gate_ref[...] = jnp.concatenate([e / tot for e in es], axis=0)


def _merge(xf, ya, yb, wg, wba, wbb, wo, g1, b1, wr_t, br):
    n, tm = xf.shape[0], TOKEN_TILE
    row = lambda w: pl.BlockSpec((tm, w), lambda i: (i, 0))
    col = pl.BlockSpec((TOP_K, tm), lambda i: (0, i))
    return pl.pallas_call(
        _merge_kernel,
        out_shape=[jax.ShapeDtypeStruct((n * ROW_TILES, LANES), F32),
                   jax.ShapeDtypeStruct((TOP_K, n), I32),
                   jax.ShapeDtypeStruct((TOP_K, n), F32)],
        grid=(n // tm,),
        in_specs=[row(D_MODEL), row(A_WIDTH), row(B_WIDTH), _full(wg), _full(wba), _full(wbb),
                  _full(wo), _full(g1), _full(b1), _full(wr_t), _full(br)],
        out_specs=[pl.BlockSpec((tm * ROW_TILES, LANES), lambda i: (i, 0)), col, col],
        compiler_params=_params(("parallel",)),
        name="merge",
    )(xf, ya, yb, wg, wba, wbb, wo, g1, b1, wr_t, br)


def _one_hot_rows(topi, k, width):
    eidx = lax.broadcasted_iota(I32, (N_EXPERTS, width), 0)
    return eidx == topi[k:k + 1, :]


def _route_kernel(topi_ref, tri_ref, rank_ref, cnt_ref, run_ref):
    @pl.when(pl.program_id(0) == 0)
    def _():
        run_ref[...] = jnp.zeros_like(run_ref)

    topi = topi_ref[...]
    tr = topi.shape[1]
    hot = [_one_hot_rows(topi, k, tr) for k in range(TOP_K)]
    oh = jnp.concatenate([jnp.where(h, 1.0, 0.0).astype(BF16) for h in hot], axis=0)
    prefix = jnp.dot(oh, tri_ref[...], preferred_element_type=F32)
    base = run_ref[...]
    ranks = []
    for k in range(TOP_K):
        tbl = prefix[k * N_EXPERTS:(k + 1) * N_EXPERTS, :] + base
        ranks.append(jnp.sum(jnp.where(hot[k], tbl, 0.0), axis=0, keepdims=True))
        base = base + jnp.sum(jnp.where(hot[k], 1.0, 0.0), axis=1, keepdims=True)
    rank_ref[...] = jnp.concatenate(ranks, axis=0).astype(I32)
    run_ref[...] = base
    cnt_ref[...] = base.astype(I32)


def _route(topi):
    n, tr = topi.shape[1], TOKEN_TILE
    tri = (np.arange(tr)[:, None] < np.arange(tr)[None, :]).astype(np.float32)
    tri = jnp.asarray(tri, BF16)
    col = pl.BlockSpec((TOP_K, tr), lambda i: (0, i))
    return pl.pallas_call(
        _route_kernel,
        out_shape=[jax.ShapeDtypeStruct((TOP_K, n), I32), jax.ShapeDtypeStruct((N_EXPERTS, 1), I32)],
        grid=(n // tr,),
        in_specs=[col, _full(tri)],
        out_specs=[col, pl.BlockSpec((N_EXPERTS, 1), lambda i: (0, 0))],
        scratch_shapes=[pltpu.VMEM((N_EXPERTS, 1), F32)],
        compiler_params=_params(("arbitrary",)),
        name="route",
    )(topi, tri)


def _positions_kernel(topi_ref, rank_ref, pstv_ref, pos_ref):
    topi = topi_ref[...]
    rows = []
    for k in range(TOP_K):
        start = jnp.sum(jnp.where(_one_hot_rows(topi, k, topi.shape[1]), pstv_ref[...], 0),
                        axis=0, keepdims=True)
        rows.append(rank_ref[k:k + 1, :] + start)
    pos_ref[...] = jnp.concatenate(rows, axis=0)


def _positions(topi, rank, pstarts):
    n, td = topi.shape[1], TOKEN_TILE
    col = pl.BlockSpec((TOP_K, td), lambda i: (0, i))
    pstv = pstarts.reshape(N_EXPERTS, 1)
    return pl.pallas_call(
        _positions_kernel,
        out_shape=jax.ShapeDtypeStruct((TOP_K, n), I32),
        grid=(n // td,),
        in_specs=[col, col, _full(pstv)],
        out_specs=col,
        compiler_params=_params(("parallel",)),
        name="positions",
    )(topi, rank, pstv)


def _dispatch_kernel(cnt_ref, pst_ref, pos_hbm, x_hbm, xs_hbm,
                     pos_smem, xbuf, zbuf, sem_p, sem_x, sem_d, sem_z):
    i = pl.program_id(0)
    n = pl.num_programs(0)
    ng = pos_smem.shape[1]
    td = ng * DMA_GROUP
    tile_rows = td * ROW_TILES

    def pos_copy(step, slot):
        return pltpu.make_async_copy(pos_hbm.at[pl.ds(step * ng, ng)], pos_smem.at[slot], sem_p.at[slot])

    def x_copy(step, slot):
        return pltpu.make_async_copy(x_hbm.at[pl.ds(step * tile_rows, tile_rows)], xbuf.at[slot],
                                     sem_x.at[slot])

    @pl.when(i == 0)
    def _():
        pos_copy(0, 0).start()
        x_copy(0, 0).start()

    @pl.when(i + 1 < n)
    def _():
        pos_copy(i + 1, (i + 1) % 2).start()
        x_copy(i + 1, (i + 1) % 3).start()

    slot = i % 2
    xslot = i % 3
    pos_copy(i, slot).wait()
    x_copy(i, xslot).wait()

    def step_bytes(s):
        rows_ = TOP_K * tile_rows
        return pltpu.make_async_copy(xs_hbm.at[pl.ds(0, rows_)], xs_hbm.at[pl.ds(0, rows_)], sem_d.at[s])

    def body(g, c):
        dst = [pos_smem[slot, g, j] * ROW_TILES for j in range(DMA_GROUP * TOP_K)]
        for u in range(DMA_GROUP):
            src = xbuf.at[xslot, pl.ds((g * DMA_GROUP + u) * ROW_TILES, ROW_TILES)]
            for k in range(TOP_K):
                pltpu.make_async_copy(src, xs_hbm.at[pl.ds(dst[u * TOP_K + k], ROW_TILES)],
                                      sem_d.at[xslot]).start(priority=k % 2)
        return c
    lax.fori_loop(0, ng, body, 0)

    @pl.when(i > 0)
    def _():
        step_bytes((i + 2) % 3).wait()

    @pl.when(i == n - 1)
    def _():
        step_bytes(xslot).wait()
        zbuf[...] = jnp.zeros_like(zbuf)

        def zero_copy(r):
            return pltpu.make_async_copy(zbuf, xs_hbm.at[pl.ds(r * ROW_TILES, ROW_TILES)], sem_z)

        def zero_rows(lo, hi):
            def start(r, c2):
                zero_copy(r).start()
                return c2

            def wait(r, c2):
                zero_copy(r).wait()
                return c2
            lax.fori_loop(lo, hi, start, 0)
            lax.fori_loop(lo, hi, wait, 0)

        def padded_end(e):
            return pst_ref[e] + (cnt_ref[e] + MOE_BLOCK - 1) // MOE_BLOCK * MOE_BLOCK

        def ebody(e, c):
            zero_rows(pst_ref[e] + cnt_ref[e], padded_end(e))
            return c
        lax.fori_loop(0, N_EXPERTS, ebody, 0)
        zero_rows(padded_end(N_EXPERTS - 1), xs_hbm.shape[0] // ROW_TILES)


def _dispatch(counts, pstarts, pos_g, x1t, cap):
    n, td = pos_g.shape[0] * DMA_GROUP, TOKEN_TILE
    any_spec = pl.BlockSpec(memory_space=pl.ANY)
    return pl.pallas_call(
        _dispatch_kernel,
        out_shape=jax.ShapeDtypeStruct((cap * ROW_TILES, LANES), F32),
        grid_spec=pltpu.PrefetchScalarGridSpec(
            num_scalar_prefetch=2,
            grid=(n // td,),
            in_specs=[any_spec, any_spec],
            out_specs=any_spec,
            scratch_shapes=[
                pltpu.SMEM((2, td // DMA_GROUP, DMA_GROUP * TOP_K), I32),
                pltpu.VMEM((3, td * ROW_TILES, LANES), F32),
                pltpu.VMEM((ROW_TILES, LANES), F32),
                pltpu.SemaphoreType.DMA((2,)),
                pltpu.SemaphoreType.DMA((3,)),
                pltpu.SemaphoreType.DMA((3,)),
                pltpu.SemaphoreType.DMA,
            ],
        ),
        compiler_params=_params(("arbitrary",)),
        name="dispatch",
    )(counts, pstarts, pos_g, x1t)


def _moe_kernel(be_ref, nu_ref, xs_ref, wg_ref, bg_ref, wu_ref, bu_ref, wd_ref, bd_ref, ys_ref):
    del be_ref
    blk = MOE_BLOCK
    used = pl.program_id(0) < nu_ref[0]

    @pl.when(used)
    def _():
        xb = jnp.concatenate([xs_ref[pl.ds(c, blk, stride=ROW_TILES), :].astype(BF16)
                              for c in range(ROW_TILES)], axis=1)
        hg = jnp.minimum(jnp.dot(xb, wg_ref[0], preferred_element_type=F32) + bg_ref[0], SWIGLU_LIMIT)
        hu = jnp.clip(jnp.dot(xb, wu_ref[0], preferred_element_type=F32) + bu_ref[0],
                      -SWIGLU_LIMIT, SWIGLU_LIMIT)
        h = (hu + 1.0) * (hg * jax.nn.sigmoid(SWIGLU_ALPHA * hg))
        y = jnp.dot(h.astype(BF16), wd_ref[0], preferred_element_type=F32) + bd_ref[0]
        for c in range(ROW_TILES):
            ys_ref[pl.ds(c, blk, stride=ROW_TILES), :] = y[:, c * LANES:(c + 1) * LANES]

    @pl.when(jnp.logical_not(used))
    def _():
        ys_ref[...] = jnp.zeros_like(ys_ref)


def _moe(blk_expert, nused, xs, wg, bg, wu, bu, wd, bd):
    blk = MOE_BLOCK
    nblk = xs.shape[0] // (blk * ROW_TILES)
    wspec = lambda a: pl.BlockSpec((1,) + a.shape[1:], lambda i, be, nu: (be[i], 0, 0))
    xspec = pl.BlockSpec((blk * ROW_TILES, LANES), lambda i, be, nu: (jnp.minimum(i, nu[0] - 1), 0))
    return pl.pallas_call(
        _moe_kernel,
        out_shape=jax.ShapeDtypeStruct(xs.shape, F32),
        grid_spec=pltpu.PrefetchScalarGridSpec(
            num_scalar_prefetch=2,
            grid=(nblk,),
            in_specs=[xspec, wspec(wg), wspec(bg), wspec(wu), wspec(bu), wspec(wd), wspec(bd)],
            out_specs=pl.BlockSpec((blk * ROW_TILES, LANES), lambda i, be, nu: (i, 0)),
        ),
        compiler_params=_params(("arbitrary",)),
        name="moe",
    )(blk_expert, nused, xs, wg, bg, wu, bu, wd, bd)


def _final_kernel(pos_hbm, gt_ref, x1t_ref, ys_hbm, g_ref, b_ref, o_ref,
                  pos_smem, ybuf, sem_p, sem_g):
    i = pl.program_id(0)
    n = pl.num_programs(0)
    ft = FINAL_TILE
    slot_rows = TOP_K * ft * ROW_TILES

    ng = ft // DMA_GROUP

    def pos_copy(tile, slot):
        return pltpu.make_async_copy(pos_hbm.at[pl.ds(tile * ng, ng)], pos_smem.at[slot], sem_p.at[slot])

    def gather_rows(pslot, yslot):
        def body(g, c):
            src = [pos_smem[pslot, g, j] * ROW_TILES for j in range(DMA_GROUP * TOP_K)]
            base = yslot * slot_rows + g * (DMA_GROUP * ROW_TILES)
            for u in range(DMA_GROUP):
                for k in range(TOP_K):
                    dst = base + (k * ft + u) * ROW_TILES
                    pltpu.make_async_copy(ys_hbm.at[pl.ds(src[u * TOP_K + k], ROW_TILES)],
                                          ybuf.at[pl.ds(dst, ROW_TILES)],
                                          sem_g.at[yslot]).start(priority=k % 2)
            return c
        lax.fori_loop(0, ng, body, 0)

    def gather_wait(yslot):
        pltpu.make_async_copy(ys_hbm.at[pl.ds(0, slot_rows)],
                              ybuf.at[pl.ds(yslot * slot_rows, slot_rows)], sem_g.at[yslot]).wait()

    @pl.when(i == 0)
    def _():
        c = pos_copy(0, 0)
        c.start()
        c.wait()
        gather_rows(0, 0)

        @pl.when(n > 1)
        def _():
            pos_copy(1, 1).start()

    @pl.when(i + 1 < n)
    def _():
        pos_copy(i + 1, (i + 1) % 3).wait()
        gather_rows((i + 1) % 3, (i + 1) % 2)

    @pl.when(i + 2 < n)
    def _():
        pos_copy(i + 2, (i + 2) % 3).start()

    slot = i % 2
    gather_wait(slot)
    g = gt_ref[...]
    zs = []
    for c in range(ROW_TILES):
        f = None
        for k in range(TOP_K):
            yk = ybuf[pl.ds(slot * slot_rows + k * ft * ROW_TILES + c, ft, stride=ROW_TILES), :]
            term = g[:, k:k + 1] * yk
            f = term if f is None else f + term
        zs.append(DN_ALPHA * x1t_ref[pl.ds(c, ft, stride=ROW_TILES), :] + f)
    o_ref[...] = _layer_norm(jnp.concatenate(zs, axis=1), g_ref[...], b_ref[...])


def _final(pos, gates_t, x1t, ys, g2, b2):
    ft = FINAL_TILE
    n = gates_t.shape[0]
    any_spec = pl.BlockSpec(memory_space=pl.ANY)
    return pl.pallas_call(
        _final_kernel,
        out_shape=jax.ShapeDtypeStruct((n, D_MODEL), F32),
        grid=(n // ft,),
        in_specs=[any_spec, pl.BlockSpec((ft, TOP_K), lambda i: (i, 0)),
                  pl.BlockSpec((ft * ROW_TILES, LANES), lambda i: (i, 0)), any_spec,
                  _full(g2), _full(b2)],
        out_specs=pl.BlockSpec((ft, D_MODEL), lambda i: (i, 0)),
        scratch_shapes=[
            pltpu.SMEM((3, ft // DMA_GROUP, DMA_GROUP * TOP_K), I32),
            pltpu.VMEM((2 * TOP_K * ft * ROW_TILES, LANES), F32),
            pltpu.SemaphoreType.DMA((3,)),
            pltpu.SemaphoreType.DMA((2,)),
        ],
        compiler_params=_params(("arbitrary",)),
        name="final",
    )(pos, gates_t, x1t, ys, g2, b2)


def _mix_weight(w_in):
    qa, ckv, qi, ki, wi, qb, kb, vb, _, _ = jnp.split(w_in, np.cumsum(SPLITS)[:-1].tolist(), axis=-1)
    pad = lambda a: jnp.pad(a, ((0, 0), (0, LANES - a.shape[1])))
    return jnp.concatenate([qa, qi, ckv, pad(ki), pad(wi), qb, kb, vb], axis=1).astype(BF16)


def _layer(x, w_in, kv_norm_g, idx_k_norm_g, idx_k_norm_b, w_uk, w_uv, rel_bias, w_branch_a,
           w_branch_b, w_out, ln1_g, ln1_b, w_router, b_router, w_gate, b_gate, w_up, b_up,
           w_down, b_down, ln2_g, ln2_b):
    bsz, seq, _ = x.shape
    n = bsz * seq
    assert seq % KEY_TILE == 0 and n % TOKEN_TILE == 0
    xf = x.reshape(n, D_MODEL)
    row = lambda v: v.reshape(1, -1).astype(F32)

    w_mix = _mix_weight(w_in)
    w_gates = w_in[:, sum(SPLITS[:8]):].astype(BF16)
    wuk_bd = jnp.zeros((A_WIDTH, A_HEADS * A_KV_RANK), F32)
    wuv_pad = jnp.zeros((A_HEADS, A_KV_RANK, LANES), F32)
    for h in range(A_HEADS):
        wuk_bd = wuk_bd.at[h * A_HEAD_DIM:(h + 1) * A_HEAD_DIM, h * A_KV_RANK:(h + 1) * A_KV_RANK].set(w_uk[h])
        c0 = (h % 2) * A_HEAD_DIM
        wuv_pad = wuv_pad.at[h, :, c0:c0 + A_HEAD_DIM].set(w_uv[h])
    wuk_bd, wuv_pad = wuk_bd.astype(BF16), wuv_pad.astype(BF16)

    qlat, qi, ckv, ki, wi, qb, kb, vb = _projection(
        xf, w_mix, wuk_bd, row(kv_norm_g), row(idx_k_norm_g), row(idx_k_norm_b))

    nqb, nkt = seq // Q_BLOCK, seq // KEY_TILE
    kidx = ki.reshape(bsz, nkt, KEY_TILE, IDX_DIM)
    ckv4 = ckv.reshape(bsz, nkt, KEY_TILE, A_KV_RANK)
    ckvt = ckv4.transpose(0, 1, 3, 2)
    qi_r = qi.reshape(bsz, nqb, Q_BLOCK, IDX_HEADS, IDX_DIM).transpose(0, 1, 4, 3, 2).reshape(
        bsz, nqb, IDX_DIM, IDX_HEADS * Q_BLOCK)
    wit = wi.reshape(bsz, seq, IDX_HEADS).transpose(0, 2, 1)
    qlat_r = qlat.reshape(bsz, nqb, Q_BLOCK, A_HEADS, A_KV_RANK).transpose(0, 1, 4, 3, 2).reshape(
        bsz, nqb, A_KV_RANK, A_HEADS * Q_BLOCK)
    ya = _mixer_a(kidx, ckv4, ckvt, qi_r, wit, qlat_r, wuv_pad, bsz, seq)

    front = lambda a: jnp.pad(a.reshape(bsz, seq, B_WIDTH), ((0, 0), (B_PAD, 0), (0, 0)))
    yb = _mixer_b(qb.reshape(bsz, seq, B_WIDTH), front(kb), front(vb), _band_bias(rel_bias), bsz, seq)

    x1t, topi, gates = _merge(
        xf, ya.reshape(n, A_WIDTH), yb.reshape(n, B_WIDTH), w_gates, w_branch_a.astype(BF16),
        w_branch_b.astype(BF16), w_out.astype(BF16), row(ln1_g), row(ln1_b),
        w_router.T.astype(BF16), b_router.reshape(-1, 1).astype(F32))

    blk = MOE_BLOCK
    cap = n * TOP_K + N_EXPERTS * blk
    nblk = cap // blk
    rank, cnt = _route(topi)
    counts = cnt[:, 0]
    padded = (counts + blk - 1) // blk * blk
    pends = jnp.cumsum(padded).astype(I32)
    pstarts = pends - padded
    nused = (pends[-1:] // blk).astype(I32)
    blk_start = jnp.arange(nblk, dtype=I32) * blk
    blk_expert = jnp.minimum(jnp.sum(blk_start[:, None] >= pends[None, :], axis=1), N_EXPERTS - 1).astype(I32)

    pos = _positions(topi, rank, pstarts)
    pos_g = pos.T.reshape(n // DMA_GROUP, DMA_GROUP * TOP_K)
    xs = _dispatch(counts, pstarts, pos_g, x1t, cap)
    ys = _moe(blk_expert, nused, xs, w_gate.astype(BF16), b_gate[:, None, :], w_up.astype(BF16),
              b_up[:, None, :], w_down.astype(BF16), b_down[:, None, :])
    out = _final(pos_g, gates.T, x1t, ys, row(ln2_g), row(ln2_b))
    return out.reshape(bsz, seq, D_MODEL)


def kernel(x, w_in, kv_norm_g, idx_k_norm_g, idx_k_norm_b, w_uk, w_uv, rel_bias, w_branch_a,
           w_branch_b, w_out, ln1_g, ln1_b, w_router, b_router, w_gate, b_gate, w_up, b_up,
           w_down, b_down, ln2_g, ln2_b):
    for l in range(DEPTH):
        x = _layer(x, w_in[l], kv_norm_g[l], idx_k_norm_g[l], idx_k_norm_b[l], w_uk[l], w_uv[l],
                   rel_bias[l], w_branch_a[l], w_branch_b[l], w_out[l], ln1_g[l], ln1_b[l],
                   w_router[l], b_router[l], w_gate[l], b_gate[l], w_up[l], b_up[l], w_down[l],
                   b_down[l], ln2_g[l], ln2_b[l])
    return x
```

```python
import functools

import jax
import jax.numpy as jnp
import numpy as np
from jax import lax
from jax.experimental import pallas as pl
from jax.experimental.pallas import tpu as pltpu

F32 = jnp.float32
BF16 = jnp.bfloat16
I32 = jnp.int32

D_MODEL = 1024
CHUNK = 64
Q_BLOCK = 128
A_HEADS = 8
A_HEAD_DIM = 64
A_KV_RANK = 128
IDX_HEADS = 8
IDX_DIM = 64
IDX_TOPK = 256
IDX_SCALE = (IDX_HEADS * IDX_DIM) ** -0.5
B_HEADS = 8
B_HEAD_DIM = 64
B_LEFT_CHUNKS = 8
REL_CLIP = 128
N_EXPERTS = 32
TOP_K = 4
D_FF = 1024
SWIGLU_LIMIT = 7.0
SWIGLU_ALPHA = 1.702
DEPTH = 1
DN_ALPHA = (2.0 * DEPTH) ** 0.25
LN_EPS = 1e-5

A_WIDTH = A_HEADS * A_HEAD_DIM
B_WIDTH = B_HEADS * B_HEAD_DIM
SPLITS = [A_WIDTH, A_KV_RANK, IDX_HEADS * IDX_DIM, IDX_DIM, IDX_HEADS,
          B_WIDTH, B_WIDTH, B_WIDTH, D_MODEL, D_MODEL]

LANES = 128
SUBLANES = 8
KEY_TILE = 256
B_QTILE = 256
B_PAD = B_LEFT_CHUNKS * CHUNK
B_WIN = B_PAD + B_QTILE
TOKEN_TILE = 512
MOE_BLOCK = 512
FINAL_TILE = 256
ROW_TILES = D_MODEL // LANES
DMA_GROUP = 4
INT_MIN = -2 ** 31
VMEM_LIMIT = 56 * 1024 * 1024

A_SLOPES = [2.0 ** (-8.0 * (h + 1) / A_HEADS) for h in range(A_HEADS)]

_C_QA = 0
_C_QI = _C_QA + A_WIDTH
_C_CKV = _C_QI + IDX_HEADS * IDX_DIM
_C_KI = _C_CKV + A_KV_RANK
_C_WI = _C_KI + LANES
_C_QKVB = _C_WI + LANES
_C_END = _C_QKVB + 3 * B_WIDTH


def _params(sem, vmem=VMEM_LIMIT):
    return pltpu.CompilerParams(dimension_semantics=sem, vmem_limit_bytes=vmem)


def _full(a):
    return pl.BlockSpec(a.shape, lambda *_: (0,) * a.ndim)


def _proj_kernel(x_ref, w_ref, wuk_ref, kvg_ref, kig_ref, kib_ref,
                 qlat_ref, qi_ref, ckv_ref, ki_ref, wi_ref, qb_ref, kb_ref, vb_ref):
    xb = x_ref[...].astype(BF16)

    def mm(c0, c1):
        return jnp.dot(xb, w_ref[:, c0:c1], preferred_element_type=F32)

    qa = mm(_C_QA, _C_QI)
    qlat = jnp.dot(qa.astype(BF16), wuk_ref[...], preferred_element_type=F32)
    qlat_ref[...] = (qlat * (A_HEAD_DIM ** -0.5)).astype(BF16)
    qi_ref[...] = mm(_C_QI, _C_CKV).astype(BF16)

    ckv = mm(_C_CKV, _C_KI)
    ms = jnp.mean(ckv * ckv, axis=-1, keepdims=True)
    ckv_ref[...] = ((ckv * lax.rsqrt(ms + LN_EPS)) * kvg_ref[...]).astype(BF16)

    ki = mm(_C_KI, _C_WI)[:, :IDX_DIM]
    mu = jnp.mean(ki, axis=-1, keepdims=True)
    var = jnp.mean(jnp.square(ki - mu), axis=-1, keepdims=True)
    ki_ref[...] = (((ki - mu) * lax.rsqrt(var + LN_EPS)) * kig_ref[...] + kib_ref[...]).astype(BF16)

    wi_ref[...] = mm(_C_WI, _C_QKVB)[:, :IDX_HEADS]
    qb_ref[...] = mm(_C_QKVB, _C_QKVB + B_WIDTH).astype(BF16)
    kb_ref[...] = mm(_C_QKVB + B_WIDTH, _C_QKVB + 2 * B_WIDTH).astype(BF16)
    vb_ref[...] = mm(_C_QKVB + 2 * B_WIDTH, _C_END).astype(BF16)


def _projection(xf, w_mix, wuk_bd, kvg, kig, kib):
    n, tm = xf.shape[0], TOKEN_TILE
    row = lambda w: pl.BlockSpec((tm, w), lambda i: (i, 0))
    outs = [(A_HEADS * A_KV_RANK, BF16), (IDX_HEADS * IDX_DIM, BF16), (A_KV_RANK, BF16),
            (IDX_DIM, BF16), (IDX_HEADS, F32), (B_WIDTH, BF16), (B_WIDTH, BF16), (B_WIDTH, BF16)]
    return pl.pallas_call(
        _proj_kernel,
        out_shape=[jax.ShapeDtypeStruct((n, w), dt) for w, dt in outs],
        grid=(n // tm,),
        in_specs=[row(D_MODEL), _full(w_mix), _full(wuk_bd), _full(kvg), _full(kig), _full(kib)],
        out_specs=[row(w) for w, _ in outs],
        compiler_params=_params(("parallel",)),
        name="proj",
    )(xf, w_mix, wuk_bd, kvg, kig, kib)


def _mixer_a_kernel(kidx_ref, ckv_ref, ckvt_ref, qi_ref, wit_ref, qlat_ref, wuv_ref,
                    ya_ref, keys_ref, s_ref, acc_ref, *, seq, topk):
    i = pl.program_id(1)
    nkt = i // 2 + 1
    nsub = KEY_TILE // SUBLANES
    lane = lax.broadcasted_iota(I32, (1, LANES), 1)
    qpos = i * Q_BLOCK + lane
    qchunk = qpos >> 6
    row_iota = lax.broadcasted_iota(I32, (KEY_TILE, LANES), 0)

    qi_blk = qi_ref[0, 0]
    w_all = wit_ref[0] * IDX_SCALE

    def idx_body(kt, carry):
        lg = pl.dot(kidx_ref[0, kt], qi_blk, trans_b=True)
        isc = jnp.zeros((KEY_TILE, LANES), F32)
        for h in range(IDX_HEADS):
            isc = isc + jnp.maximum(lg[:, h * LANES:(h + 1) * LANES], 0.0) * w_all[h:h + 1, :]
        bits = lax.bitcast_convert_type(isc + 0.0, I32)
        key = bits ^ ((bits >> 31) & 0x7FFFFFFF)
        spos = kt * KEY_TILE + row_iota
        key = jnp.where((spos >> 6) <= qchunk, key, INT_MIN)
        keys_ref[kt] = key
        return carry

    def over_tiles(body, init):
        def pair(j, c):
            return body(2 * j + 1, body(2 * j, c))
        c = lax.fori_loop(0, nkt // 2, pair, init)
        return lax.cond(nkt % 2 == 1, lambda c: body(nkt - 1, c), lambda c: c, c)

    over_tiles(idx_body, 0)

    def count(pred):
        def body(kt, acc):
            spos = kt * KEY_TILE + row_iota
            m = jnp.where(pred(keys_ref[kt], spos), 1, 0).astype(I32)
            return acc + jnp.sum(m.reshape(nsub, SUBLANES, LANES), axis=0)
        acc = over_tiles(body, jnp.zeros((SUBLANES, LANES), I32))
        return jnp.sum(acc, axis=0, keepdims=True)

    def bit_body(it, carry):
        thr, cge = carry
        cand = thr ^ lax.shift_left(jnp.int32(1), 31 - it)
        cnt = count(lambda kv, spos: kv >= cand)
        ok = cnt >= topk
        return jnp.where(ok, cand, thr), jnp.where(ok, cnt, cge)

    thr0 = jnp.full((1, LANES), INT_MIN, I32)
    thr, cge = lax.fori_loop(0, 32, bit_body, (thr0, jnp.zeros((1, LANES), I32)))

    nbits = int(np.log2(seq)) + 1
    has_thr = thr > INT_MIN
    need_tie = jnp.max(jnp.where(has_thr, cge, 0)) > topk

    def tie_fn():
        room = topk - count(lambda kv, spos: kv > thr)

        def jb(it, cut):
            cand = cut | lax.shift_left(jnp.int32(1), nbits - 1 - it)
            f = count(lambda kv, spos: (kv == thr) & (spos < cand))
            return jnp.where(f <= room, cand, cut)
        return lax.fori_loop(0, nbits, jb, jnp.zeros((1, LANES), I32))

    cut = lax.cond(need_tie, tie_fn, lambda: jnp.full((1, LANES), 2 * seq, I32))
    cut = jnp.where(has_thr, cut, 0)

    qlat = qlat_ref[0]
    qr = jnp.concatenate([qlat[:, h * LANES:(h + 1) * LANES] for h in range(A_HEADS)], axis=0)

    def sc_body(kt, m):
        st = pl.dot(ckv_ref[0, kt], qr, trans_b=True)
        kv = keys_ref[kt]
        spos = kt * KEY_TILE + row_iota
        sel = (kv > thr) | ((kv == thr) & (spos < cut))
        dist = jnp.abs(qpos - spos).astype(F32)
        out = []
        for h in range(A_HEADS):
            sh = st[:, h * LANES:(h + 1) * LANES] - A_SLOPES[h] * dist
            sh = jnp.where(sel, sh, -jnp.inf)
            s_ref[kt, :, h * LANES:(h + 1) * LANES] = sh
            out.append(jnp.maximum(m[h], jnp.max(sh.reshape(nsub, SUBLANES, LANES), axis=0)))
        return tuple(out)

    m8 = over_tiles(sc_body, tuple(jnp.full((SUBLANES, LANES), -jnp.inf, F32) for _ in range(A_HEADS)))
    mx = [jnp.max(m, axis=0, keepdims=True) for m in m8]

    acc_ref[...] = jnp.zeros_like(acc_ref)

    def pv_body(kt, l):
        out, ps = [], []
        for h in range(A_HEADS):
            p = jnp.exp(s_ref[kt, :, h * LANES:(h + 1) * LANES] - mx[h])
            out.append(l[h] + jnp.sum(p.reshape(nsub, SUBLANES, LANES), axis=0))
            ps.append(p.astype(BF16))
        acc_ref[...] += jnp.dot(ckvt_ref[0, kt], jnp.concatenate(ps, axis=1),
                                preferred_element_type=F32)
        return tuple(out)

    l8 = over_tiles(pv_body, tuple(jnp.zeros((SUBLANES, LANES), F32) for _ in range(A_HEADS)))
    inv = [1.0 / jnp.sum(l, axis=0, keepdims=True) for l in l8]

    for j in range(A_HEADS // 2):
        y = None
        for h in (2 * j, 2 * j + 1):
            o = (acc_ref[:, h * LANES:(h + 1) * LANES] * inv[h]).astype(BF16)
            t = pl.dot(o, wuv_ref[h], trans_a=True)
            y = t if y is None else y + t
        ya_ref[0, :, j * LANES:(j + 1) * LANES] = y.astype(BF16)


def _mixer_a(kidx, ckv, ckvt, qi_r, wit, qlat, wuv_pad, bsz, seq):
    nqb = seq // Q_BLOCK
    nkt = seq // KEY_TILE
    topk = min(IDX_TOPK, seq // 4)
    kern = functools.partial(_mixer_a_kernel, seq=seq, topk=topk)
    return pl.pallas_call(
        kern,
        out_shape=jax.ShapeDtypeStruct((bsz, seq, A_WIDTH), BF16),
        grid=(bsz, nqb),
        in_specs=[
            pl.BlockSpec((1, nkt, KEY_TILE, IDX_DIM), lambda b, i: (b, 0, 0, 0)),
            pl.BlockSpec((1, nkt, KEY_TILE, A_KV_RANK), lambda b, i: (b, 0, 0, 0)),
            pl.BlockSpec((1, nkt, A_KV_RANK, KEY_TILE), lambda b, i: (b, 0, 0, 0)),
            pl.BlockSpec((1, 1, IDX_HEADS * Q_BLOCK, IDX_DIM), lambda b, i: (b, i, 0, 0)),
            pl.BlockSpec((1, IDX_HEADS, Q_BLOCK), lambda b, i: (b, 0, i)),
            pl.BlockSpec((1, Q_BLOCK, A_HEADS * A_KV_RANK), lambda b, i: (b, i, 0)),
            pl.BlockSpec(wuv_pad.shape, lambda b, i: (0, 0, 0)),
        ],
        out_specs=pl.BlockSpec((1, Q_BLOCK, A_WIDTH), lambda b, i: (b, i, 0)),
        scratch_shapes=[
            pltpu.VMEM((nkt, KEY_TILE, LANES), I32),
            pltpu.VMEM((nkt, KEY_TILE, A_HEADS * LANES), F32),
            pltpu.VMEM((A_KV_RANK, A_HEADS * LANES), F32),
        ],
        compiler_params=_params(("parallel", "arbitrary")),
        name="mixer_a",
    )(kidx, ckv, ckvt, qi_r, wit, qlat, wuv_pad)


def _mixer_b_kernel(q_ref, k_ref, v_ref, bias_ref, yb_ref):
    t0 = pl.multiple_of(pl.program_id(1) * B_QTILE, B_QTILE)
    kw = k_ref[0, pl.ds(t0, B_WIN), :]
    vw = v_ref[0, pl.ds(t0, B_WIN), :]
    q = q_ref[0] * jnp.asarray(B_HEAD_DIM ** -0.5, BF16)
    lane = lax.broadcasted_iota(I32, (1, LANES), 1)

    def attend(has_front_padding):
        kvalid = (t0 - B_PAD + lax.broadcasted_iota(I32, (1, B_WIN), 1)) >= 0
        for j in range(B_HEADS // 2):
            sl = slice(j * LANES, (j + 1) * LANES)
            qs, ks, vs = q[:, sl], kw[:, sl], vw[:, sl]
            outs = []
            for hh in range(2):
                mine = (lane >= B_HEAD_DIM) if hh else (lane < B_HEAD_DIM)
                qm = jnp.where(mine, qs, jnp.zeros_like(qs))
                s = pl.dot(qm, ks, trans_b=True) + bias_ref[2 * j + hh]
                if has_front_padding:
                    s = jnp.where(kvalid, s, -jnp.inf)
                m = jnp.max(s, axis=-1, keepdims=True)
                p = jnp.exp(s - m)
                l = jnp.sum(p, axis=-1, keepdims=True)
                outs.append(jnp.dot(p.astype(BF16), vs, preferred_element_type=F32) / l)
            yb_ref[0, :, sl] = jnp.where(lane < B_HEAD_DIM, outs[0], outs[1]).astype(BF16)

    lax.cond(t0 < B_PAD, lambda: attend(True), lambda: attend(False))


def _band_bias(rel_bias):
    tq = np.arange(B_QTILE)[:, None]
    j = np.arange(B_WIN)[None, :]
    band = (j // CHUNK >= tq // CHUNK) & (j // CHUNK <= tq // CHUNK + B_LEFT_CHUNKS)
    m = np.arange(B_QTILE - 1 + B_WIN)
    line = rel_bias[:, np.clip(B_WIN - 1 - m, -REL_CLIP, REL_CLIP) + REL_CLIP].astype(F32)
    table = jnp.stack([line[:, B_QTILE - 1 - t:B_QTILE - 1 - t + B_WIN] for t in range(B_QTILE)], axis=1)
    return jnp.where(jnp.asarray(band)[None], table, -jnp.inf)


def _mixer_b(qb, kb_pad, vb_pad, bias, bsz, seq):
    return pl.pallas_call(
        _mixer_b_kernel,
        out_shape=jax.ShapeDtypeStruct((bsz, seq, B_WIDTH), BF16),
        grid=(bsz, seq // B_QTILE),
        in_specs=[
            pl.BlockSpec((1, B_QTILE, B_WIDTH), lambda b, i: (b, i, 0)),
            pl.BlockSpec((1, seq + B_PAD, B_WIDTH), lambda b, i: (b, 0, 0)),
            pl.BlockSpec((1, seq + B_PAD, B_WIDTH), lambda b, i: (b, 0, 0)),
            pl.BlockSpec(bias.shape, lambda b, i: (0, 0, 0)),
        ],
        out_specs=pl.BlockSpec((1, B_QTILE, B_WIDTH), lambda b, i: (b, i, 0)),
        compiler_params=_params(("parallel", "arbitrary")),
        name="mixer_b",
    )(qb, kb_pad, vb_pad, bias)


def _layer_norm(z, g, b):
    mu = jnp.mean(z, axis=-1, keepdims=True)
    var = jnp.mean(jnp.square(z - mu), axis=-1, keepdims=True)
    return ((z - mu) * lax.rsqrt(var + LN_EPS)) * g + b


def _merge_kernel(x_ref, ya_ref, yb_ref, wg_ref, wba_ref, wbb_ref, wo_ref, g_ref, b_ref,
                  wr_ref, br_ref, x1t_ref, topi_ref, gate_ref):
    tm = x_ref.shape[0]
    x = x_ref[...]
    xb = x.astype(BF16)
    ga = jnp.dot(xb, wg_ref[:, :D_MODEL], preferred_element_type=F32)
    a = jnp.dot(ya_ref[...], wba_ref[...], preferred_element_type=F32)
    merged = jax.nn.sigmoid(ga) * a
    gb = jnp.dot(xb, wg_ref[:, D_MODEL:], preferred_element_type=F32)
    b = jnp.dot(yb_ref[...], wbb_ref[...], preferred_element_type=F32)
    merged = merged + jax.nn.sigmoid(gb) * b
    out = jnp.dot(merged.astype(BF16), wo_ref[...], preferred_element_type=F32)
    x1 = _layer_norm(DN_ALPHA * x + out, g_ref[...], b_ref[...])
    for c in range(ROW_TILES):
        x1t_ref[pl.ds(c, tm, stride=ROW_TILES), :] = x1[:, c * LANES:(c + 1) * LANES]

    lg = pl.dot(wr_ref[...], x1.astype(BF16), trans_b=True) + br_ref[...]
    eidx = lax.broadcasted_iota(I32, lg.shape, 0)
    vals, idxs = [], []
    for _ in range(TOP_K):
        m = jnp.max(lg, axis=0, keepdims=True)
        sel = jnp.min(jnp.where(lg == m, eidx, N_EXPERTS), axis=0, keepdims=True)
        vals.append(m)
        idxs.append(sel)
        lg = jnp.where(eidx == sel, -jnp.inf, lg)
    es = [jnp.exp(v - vals[0]) for v in vals]
    tot = es[0] + es[1] + es[2] + es[3]
    topi_ref[...] = jnp.concatenate(idxs, axis=0)
    gate_ref[...] = jnp.concatenate([e / tot for e in es], axis=0)


def _merge(xf, ya, yb, wg, wba, wbb, wo, g1, b1, wr_t, br):
    n, tm = xf.shape[0], TOKEN_TILE
    row = lambda w: pl.BlockSpec((tm, w), lambda i: (i, 0))
    col = pl.BlockSpec((TOP_K, tm), lambda i: (0, i))
    return pl.pallas_call(
        _merge_kernel,
        out_shape=[jax.ShapeDtypeStruct((n * ROW_TILES, LANES), F32),
                   jax.ShapeDtypeStruct((TOP_K, n), I32),
                   jax.ShapeDtypeStruct((TOP_K, n), F32)],
        grid=(n // tm,),
        in_specs=[row(D_MODEL), row(A_WIDTH), row(B_WIDTH), _full(wg), _full(wba), _full(wbb),
                  _full(wo), _full(g1), _full(b1), _full(wr_t), _full(br)],
        out_specs=[pl.BlockSpec((tm * ROW_TILES, LANES), lambda i: (i, 0)), col, col],
        compiler_params=_params(("parallel",)),
        name="merge",
    )(xf, ya, yb, wg, wba, wbb, wo, g1, b1, wr_t, br)


def _one_hot_rows(topi, k, width):
    eidx = lax.broadcasted_iota(I32, (N_EXPERTS, width), 0)
    return eidx == topi[k:k + 1, :]


def _route_kernel(topi_ref, tri_ref, rank_ref, cnt_ref, run_ref):
    @pl.when(pl.program_id(0) == 0)
    def _():
        run_ref[...] = jnp.zeros_like(run_ref)

    topi = topi_ref[...]
    tr = topi.shape[1]
    hot = [_one_hot_rows(topi, k, tr) for k in range(TOP_K)]
    oh = jnp.concatenate([jnp.where(h, 1.0, 0.0).astype(BF16) for h in hot], axis=0)
    prefix = jnp.dot(oh, tri_ref[...], preferred_element_type=F32)
    base = run_ref[...]
    ranks = []
    for k in range(TOP_K):
        tbl = prefix[k * N_EXPERTS:(k + 1) * N_EXPERTS, :] + base
        ranks.append(jnp.sum(jnp.where(hot[k], tbl, 0.0), axis=0, keepdims=True))
        base = base + jnp.sum(jnp.where(hot[k], 1.0, 0.0), axis=1, keepdims=True)
    rank_ref[...] = jnp.concatenate(ranks, axis=0).astype(I32)
    run_ref[...] = base
    cnt_ref[...] = base.astype(I32)


def _route(topi):
    n, tr = topi.shape[1], TOKEN_TILE
    tri = (np.arange(tr)[:, None] < np.arange(tr)[None, :]).astype(np.float32)
    tri = jnp.asarray(tri, BF16)
    col = pl.BlockSpec((TOP_K, tr), lambda i: (0, i))
    return pl.pallas_call(
        _route_kernel,
        out_shape=[jax.ShapeDtypeStruct((TOP_K, n), I32), jax.ShapeDtypeStruct((N_EXPERTS, 1), I32)],
        grid=(n // tr,),
        in_specs=[col, _full(tri)],
        out_specs=[col, pl.BlockSpec((N_EXPERTS, 1), lambda i: (0, 0))],
        scratch_shapes=[pltpu.VMEM((N_EXPERTS, 1), F32)],
        compiler_params=_params(("arbitrary",)),
        name="route",
    )(topi, tri)


def _positions_kernel(topi_ref, rank_ref, pstv_ref, pos_ref):
    topi = topi_ref[...]
    rows = []
    for k in range(TOP_K):
        start = jnp.sum(jnp.where(_one_hot_rows(topi, k, topi.shape[1]), pstv_ref[...], 0),
                        axis=0, keepdims=True)
        rows.append(rank_ref[k:k + 1, :] + start)
    pos_ref[...] = jnp.concatenate(rows, axis=0)


def _positions(topi, rank, pstarts):
    n, td = topi.shape[1], TOKEN_TILE
    col = pl.BlockSpec((TOP_K, td), lambda i: (0, i))
    pstv = pstarts.reshape(N_EXPERTS, 1)
    return pl.pallas_call(
        _positions_kernel,
        out_shape=jax.ShapeDtypeStruct((TOP_K, n), I32),
        grid=(n // td,),
        in_specs=[col, col, _full(pstv)],
        out_specs=col,
        compiler_params=_params(("parallel",)),
        name="positions",
    )(topi, rank, pstv)


def _dispatch_kernel(cnt_ref, pst_ref, pos_hbm, x_hbm, xs_hbm,
                     pos_smem, xbuf, zbuf, sem_p, sem_x, sem_d, sem_z):
    i = pl.program_id(0)
    n = pl.num_programs(0)
    ng = pos_smem.shape[1]
    td = ng * DMA_GROUP
    tile_rows = td * ROW_TILES

    def pos_copy(step, slot):
        return pltpu.make_async_copy(pos_hbm.at[pl.ds(step * ng, ng)], pos_smem.at[slot], sem_p.at[slot])

    def x_copy(step, slot):
        return pltpu.make_async_copy(x_hbm.at[pl.ds(step * tile_rows, tile_rows)], xbuf.at[slot],
                                     sem_x.at[slot])

    @pl.when(i == 0)
    def _():
        pos_copy(0, 0).start()
        x_copy(0, 0).start()

    @pl.when(i + 1 < n)
    def _():
        pos_copy(i + 1, (i + 1) % 2).start()
        x_copy(i + 1, (i + 1) % 3).start()

    slot = i % 2
    xslot = i % 3
    pos_copy(i, slot).wait()
    x_copy(i, xslot).wait()

    def step_bytes(s):
        rows_ = TOP_K * tile_rows
        return pltpu.make_async_copy(xs_hbm.at[pl.ds(0, rows_)], xs_hbm.at[pl.ds(0, rows_)], sem_d.at[s])

    def body(g, c):
        dst = [pos_smem[slot, g, j] * ROW_TILES for j in range(DMA_GROUP * TOP_K)]
        for u in range(DMA_GROUP):
            src = xbuf.at[xslot, pl.ds((g * DMA_GROUP + u) * ROW_TILES, ROW_TILES)]
            for k in range(TOP_K):
                pltpu.make_async_copy(src, xs_hbm.at[pl.ds(dst[u * TOP_K + k], ROW_TILES)],
                                      sem_d.at[xslot]).start(priority=k % 2)
        return c
    lax.fori_loop(0, ng, body, 0)

    @pl.when(i > 0)
    def _():
        step_bytes((i + 2) % 3).wait()

    @pl.when(i == n - 1)
    def _():
        step_bytes(xslot).wait()
        zbuf[...] = jnp.zeros_like(zbuf)

        def zero_copy(r):
            return pltpu.make_async_copy(zbuf, xs_hbm.at[pl.ds(r * ROW_TILES, ROW_TILES)], sem_z)

        def zero_rows(lo, hi):
            def start(r, c2):
                zero_copy(r).start()
                return c2

            def wait(r, c2):
                zero_copy(r).wait()
                return c2
            lax.fori_loop(lo, hi, start, 0)
            lax.fori_loop(lo, hi, wait, 0)

        def padded_end(e):
            return pst_ref[e] + (cnt_ref[e] + MOE_BLOCK - 1) // MOE_BLOCK * MOE_BLOCK

        def ebody(e, c):
            zero_rows(pst_ref[e] + cnt_ref[e], padded_end(e))
            return c
        lax.fori_loop(0, N_EXPERTS, ebody, 0)
        zero_rows(padded_end(N_EXPERTS - 1), xs_hbm.shape[0] // ROW_TILES)


def _dispatch(counts, pstarts, pos_g, x1t, cap):
    n, td = pos_g.shape[0] * DMA_GROUP, TOKEN_TILE
    any_spec = pl.BlockSpec(memory_space=pl.ANY)
    return pl.pallas_call(
        _dispatch_kernel,
        out_shape=jax.ShapeDtypeStruct((cap * ROW_TILES, LANES), F32),
        grid_spec=pltpu.PrefetchScalarGridSpec(
            num_scalar_prefetch=2,
            grid=(n // td,),
            in_specs=[any_spec, any_spec],
            out_specs=any_spec,
            scratch_shapes=[
                pltpu.SMEM((2, td // DMA_GROUP, DMA_GROUP * TOP_K), I32),
                pltpu.VMEM((3, td * ROW_TILES, LANES), F32),
                pltpu.VMEM((ROW_TILES, LANES), F32),
                pltpu.SemaphoreType.DMA((2,)),
                pltpu.SemaphoreType.DMA((3,)),
                pltpu.SemaphoreType.DMA((3,)),
                pltpu.SemaphoreType.DMA,
            ],
        ),
        compiler_params=_params(("arbitrary",)),
        name="dispatch",
    )(counts, pstarts, pos_g, x1t)


def _moe_kernel(be_ref, nu_ref, xs_ref, wg_ref, bg_ref, wu_ref, bu_ref, wd_ref, bd_ref, ys_ref,
                wg_bf, wu_bf, wd_bf):
    i = pl.program_id(0)
    blk = MOE_BLOCK
    used = i < nu_ref[0]

    @pl.when(jnp.logical_or(i == 0, be_ref[i] != be_ref[jnp.maximum(i - 1, 0)]))
    def _():
        wg_bf[...] = wg_ref[0].astype(BF16)
        wu_bf[...] = wu_ref[0].astype(BF16)
        wd_bf[...] = wd_ref[0].astype(BF16)

    @pl.when(used)
    def _():
        xb = jnp.concatenate([xs_ref[pl.ds(c, blk, stride=ROW_TILES), :].astype(BF16)
                              for c in range(ROW_TILES)], axis=1)
        hg = jnp.minimum(jnp.dot(xb, wg_bf[...], preferred_element_type=F32) + bg_ref[0], SWIGLU_LIMIT)
        hu = jnp.clip(jnp.dot(xb, wu_bf[...], preferred_element_type=F32) + bu_ref[0],
                      -SWIGLU_LIMIT, SWIGLU_LIMIT)
        h = (hu + 1.0) * (hg * jax.nn.sigmoid(SWIGLU_ALPHA * hg))
        y = jnp.dot(h.astype(BF16), wd_bf[...], preferred_element_type=F32) + bd_ref[0]
        for c in range(ROW_TILES):
            ys_ref[pl.ds(c, blk, stride=ROW_TILES), :] = y[:, c * LANES:(c + 1) * LANES]

    @pl.when(jnp.logical_not(used))
    def _():
        ys_ref[...] = jnp.zeros_like(ys_ref)


def _moe(blk_expert, nused, xs, wg, bg, wu, bu, wd, bd):
    blk = MOE_BLOCK
    nblk = xs.shape[0] // (blk * ROW_TILES)
    wspec = lambda a: pl.BlockSpec((1,) + a.shape[1:], lambda i, be, nu: (be[i], 0, 0))
    xspec = pl.BlockSpec((blk * ROW_TILES, LANES), lambda i, be, nu: (jnp.minimum(i, nu[0] - 1), 0))
    return pl.pallas_call(
        _moe_kernel,
        out_shape=jax.ShapeDtypeStruct(xs.shape, F32),
        grid_spec=pltpu.PrefetchScalarGridSpec(
            num_scalar_prefetch=2,
            grid=(nblk,),
            in_specs=[xspec, wspec(wg), wspec(bg), wspec(wu), wspec(bu), wspec(wd), wspec(bd)],
            out_specs=pl.BlockSpec((blk * ROW_TILES, LANES), lambda i, be, nu: (i, 0)),
            scratch_shapes=[pltpu.VMEM(wg.shape[1:], BF16), pltpu.VMEM(wu.shape[1:], BF16),
                            pltpu.VMEM(wd.shape[1:], BF16)],
        ),
        compiler_params=_params(("arbitrary",)),
        name="moe",
    )(blk_expert, nused, xs, wg, bg, wu, bu, wd, bd)


def _final_kernel(pos_hbm, gt_ref, x1t_ref, ys_hbm, g_ref, b_ref, o_ref,
                  pos_smem, ybuf, sem_p, sem_g):
    i = pl.program_id(0)
    n = pl.num_programs(0)
    ft = FINAL_TILE
    slot_rows = TOP_K * ft * ROW_TILES

    ng = ft // DMA_GROUP

    def pos_copy(tile, slot):
        return pltpu.make_async_copy(pos_hbm.at[pl.ds(tile * ng, ng)], pos_smem.at[slot], sem_p.at[slot])

    def gather_rows(pslot, yslot):
        def body(g, c):
            src = [pos_smem[pslot, g, j] * ROW_TILES for j in range(DMA_GROUP * TOP_K)]
            base = yslot * slot_rows + g * (DMA_GROUP * ROW_TILES)
            for u in range(DMA_GROUP):
                for k in range(TOP_K):
                    dst = base + (k * ft + u) * ROW_TILES
                    pltpu.make_async_copy(ys_hbm.at[pl.ds(src[u * TOP_K + k], ROW_TILES)],
                                          ybuf.at[pl.ds(dst, ROW_TILES)],
                                          sem_g.at[yslot]).start(priority=k % 2)
            return c
        lax.fori_loop(0, ng, body, 0)

    def gather_wait(yslot):
        pltpu.make_async_copy(ys_hbm.at[pl.ds(0, slot_rows)],
                              ybuf.at[pl.ds(yslot * slot_rows, slot_rows)], sem_g.at[yslot]).wait()

    @pl.when(i == 0)
    def _():
        c = pos_copy(0, 0)
        c.start()
        c.wait()
        gather_rows(0, 0)

        @pl.when(n > 1)
        def _():
            pos_copy(1, 1).start()

    @pl.when(i + 1 < n)
    def _():
        pos_copy(i + 1, (i + 1) % 3).wait()
        gather_rows((i + 1) % 3, (i + 1) % 2)

    @pl.when(i + 2 < n)
    def _():
        pos_copy(i + 2, (i + 2) % 3).start()

    slot = i % 2
    gather_wait(slot)
    g = gt_ref[...]
    zs = []
    for c in range(ROW_TILES):
        f = None
        for k in range(TOP_K):
            yk = ybuf[pl.ds(slot * slot_rows + k * ft * ROW_TILES + c, ft, stride=ROW_TILES), :]
            term = g[:, k:k + 1] * yk
            f = term if f is None else f + term
        zs.append(DN_ALPHA * x1t_ref[pl.ds(c, ft, stride=ROW_TILES), :] + f)
    o_ref[...] = _layer_norm(jnp.concatenate(zs, axis=1), g_ref[...], b_ref[...])


def _final(pos, gates_t, x1t, ys, g2, b2):
    ft = FINAL_TILE
    n = gates_t.shape[0]
    any_spec = pl.BlockSpec(memory_space=pl.ANY)
    return pl.pallas_call(
        _final_kernel,
        out_shape=jax.ShapeDtypeStruct((n, D_MODEL), F32),
        grid=(n // ft,),
        in_specs=[any_spec, pl.BlockSpec((ft, TOP_K), lambda i: (i, 0)),
                  pl.BlockSpec((ft * ROW_TILES, LANES), lambda i: (i, 0)), any_spec,
                  _full(g2), _full(b2)],
        out_specs=pl.BlockSpec((ft, D_MODEL), lambda i: (i, 0)),
        scratch_shapes=[
            pltpu.SMEM((3, ft // DMA_GROUP, DMA_GROUP * TOP_K), I32),
            pltpu.VMEM((2 * TOP_K * ft * ROW_TILES, LANES), F32),
            pltpu.SemaphoreType.DMA((3,)),
            pltpu.SemaphoreType.DMA((2,)),
        ],
        compiler_params=_params(("arbitrary",)),
        name="final",
    )(pos, gates_t, x1t, ys, g2, b2)


def _mix_weight(w_in):
    qa, ckv, qi, ki, wi, qb, kb, vb, _, _ = jnp.split(w_in, np.cumsum(SPLITS)[:-1].tolist(), axis=-1)
    pad = lambda a: jnp.pad(a, ((0, 0), (0, LANES - a.shape[1])))
    return jnp.concatenate([qa, qi, ckv, pad(ki), pad(wi), qb, kb, vb], axis=1).astype(BF16)


def _layer(x, w_in, kv_norm_g, idx_k_norm_g, idx_k_norm_b, w_uk, w_uv, rel_bias, w_branch_a,
           w_branch_b, w_out, ln1_g, ln1_b, w_router, b_router, w_gate, b_gate, w_up, b_up,
           w_down, b_down, ln2_g, ln2_b):
    bsz, seq, _ = x.shape
    n = bsz * seq
    assert seq % KEY_TILE == 0 and n % TOKEN_TILE == 0
    xf = x.reshape(n, D_MODEL)
    row = lambda v: v.reshape(1, -1).astype(F32)

    w_mix = _mix_weight(w_in)
    w_gates = w_in[:, sum(SPLITS[:8]):].astype(BF16)
    wuk_bd = jnp.zeros((A_WIDTH, A_HEADS * A_KV_RANK), F32)
    wuv_pad = jnp.zeros((A_HEADS, A_KV_RANK, LANES), F32)
    for h in range(A_HEADS):
        wuk_bd = wuk_bd.at[h * A_HEAD_DIM:(h + 1) * A_HEAD_DIM, h * A_KV_RANK:(h + 1) * A_KV_RANK].set(w_uk[h])
        c0 = (h % 2) * A_HEAD_DIM
        wuv_pad = wuv_pad.at[h, :, c0:c0 + A_HEAD_DIM].set(w_uv[h])
    wuk_bd, wuv_pad = wuk_bd.astype(BF16), wuv_pad.astype(BF16)

    qlat, qi, ckv, ki, wi, qb, kb, vb = _projection(
        xf, w_mix, wuk_bd, row(kv_norm_g), row(idx_k_norm_g), row(idx_k_norm_b))

    nqb, nkt = seq // Q_BLOCK, seq // KEY_TILE
    kidx = ki.reshape(bsz, nkt, KEY_TILE, IDX_DIM)
    ckv4 = ckv.reshape(bsz, nkt, KEY_TILE, A_KV_RANK)
    ckvt = ckv4.transpose(0, 1, 3, 2)
    qi_r = qi.reshape(bsz, nqb, Q_BLOCK, IDX_HEADS, IDX_DIM).transpose(0, 1, 3, 2, 4).reshape(
        bsz, nqb, IDX_HEADS * Q_BLOCK, IDX_DIM)
    wit = wi.reshape(bsz, seq, IDX_HEADS).transpose(0, 2, 1)
    ya = _mixer_a(kidx, ckv4, ckvt, qi_r, wit, qlat.reshape(bsz, seq, -1), wuv_pad, bsz, seq)

    front = lambda a: jnp.pad(a.reshape(bsz, seq, B_WIDTH), ((0, 0), (B_PAD, 0), (0, 0)))
    yb = _mixer_b(qb.reshape(bsz, seq, B_WIDTH), front(kb), front(vb), _band_bias(rel_bias), bsz, seq)

    x1t, topi, gates = _merge(
        xf, ya.reshape(n, A_WIDTH), yb.reshape(n, B_WIDTH), w_gates, w_branch_a.astype(BF16),
        w_branch_b.astype(BF16), w_out.astype(BF16), row(ln1_g), row(ln1_b),
        w_router.T.astype(BF16), b_router.reshape(-1, 1).astype(F32))

    blk = MOE_BLOCK
    cap = n * TOP_K + N_EXPERTS * blk
    nblk = cap // blk
    rank, cnt = _route(topi)
    counts = cnt[:, 0]
    padded = (counts + blk - 1) // blk * blk
    pends = jnp.cumsum(padded).astype(I32)
    pstarts = pends - padded
    nused = (pends[-1:] // blk).astype(I32)
    blk_start = jnp.arange(nblk, dtype=I32) * blk
    blk_expert = jnp.minimum(jnp.sum(blk_start[:, None] >= pends[None, :], axis=1), N_EXPERTS - 1).astype(I32)

    pos = _positions(topi, rank, pstarts)
    pos_g = pos.T.reshape(n // DMA_GROUP, DMA_GROUP * TOP_K)
    xs = _dispatch(counts, pstarts, pos_g, x1t, cap)
    ys = _moe(blk_expert, nused, xs, w_gate, b_gate[:, None, :], w_up, b_up[:, None, :],
              w_down, b_down[:, None, :])
    out = _final(pos_g, gates.T, x1t, ys, row(ln2_g), row(ln2_b))
    return out.reshape(bsz, seq, D_MODEL)


def kernel(x, w_in, kv_norm_g, idx_k_norm_g, idx_k_norm_b, w_uk, w_uv, rel_bias, w_branch_a,
           w_branch_b, w_out, ln1_g, ln1_b, w_router, b_router, w_gate, b_gate, w_up, b_up,
           w_down, b_down, ln2_g, ln2_b):
    for l in range(DEPTH):
        x = _layer(x, w_in[l], kv_norm_g[l], idx_k_norm_g[l], idx_k_norm_b[l], w_uk[l], w_uv[l],
                   rel_bias[l], w_branch_a[l], w_branch_b[l], w_out[l], ln1_g[l], ln1_b[l],
                   w_router[l], b_router[l], w_gate[l], b_gate[l], w_up[l], b_up[l], w_down[l],
                   b_down[l], ln2_g[l], ln2_b[l])
    return x
```

```python
import functools

import jax
import jax.numpy as jnp
import numpy as np
from jax import lax
from jax.experimental import pallas as pl
from jax.experimental.pallas import tpu as pltpu

F32 = jnp.float32
BF16 = jnp.bfloat16
I32 = jnp.int32

D_MODEL = 1024
CHUNK = 64
Q_BLOCK = 128
A_HEADS = 8
A_HEAD_DIM = 64
A_KV_RANK = 128
IDX_HEADS = 8
IDX_DIM = 64
IDX_TOPK = 256
IDX_SCALE = (IDX_HEADS * IDX_DIM) ** -0.5
B_HEADS = 8
B_HEAD_DIM = 64
B_LEFT_CHUNKS = 8
REL_CLIP = 128
N_EXPERTS = 32
TOP_K = 4
D_FF = 1024
SWIGLU_LIMIT = 7.0
SWIGLU_ALPHA = 1.702
DEPTH = 1
DN_ALPHA = (2.0 * DEPTH) ** 0.25
LN_EPS = 1e-5

A_WIDTH = A_HEADS * A_HEAD_DIM
B_WIDTH = B_HEADS * B_HEAD_DIM
SPLITS = [A_WIDTH, A_KV_RANK, IDX_HEADS * IDX_DIM, IDX_DIM, IDX_HEADS,
          B_WIDTH, B_WIDTH, B_WIDTH, D_MODEL, D_MODEL]

LANES = 128
SUBLANES = 8
KEY_TILE = 256
B_QTILE = 256
B_PAD = B_LEFT_CHUNKS * CHUNK
B_WIN = B_PAD + B_QTILE
TOKEN_TILE = 512
MOE_BLOCK = 512
FINAL_TILE = 256
ROW_TILES = D_MODEL // LANES
DMA_GROUP = 4
INT_MIN = -2 ** 31
VMEM_LIMIT = 56 * 1024 * 1024

A_SLOPES = [2.0 ** (-8.0 * (h + 1) / A_HEADS) for h in range(A_HEADS)]

_C_QA = 0
_C_QI = _C_QA + A_WIDTH
_C_CKV = _C_QI + IDX_HEADS * IDX_DIM
_C_KI = _C_CKV + A_KV_RANK
_C_WI = _C_KI + LANES
_C_QKVB = _C_WI + LANES
_C_END = _C_QKVB + 3 * B_WIDTH


def _params(sem, vmem=VMEM_LIMIT):
    return pltpu.CompilerParams(dimension_semantics=sem, vmem_limit_bytes=vmem)


def _full(a):
    return pl.BlockSpec(a.shape, lambda *_: (0,) * a.ndim)


def _proj_kernel(x_ref, w_ref, wuk_ref, kvg_ref, kig_ref, kib_ref,
                 qlat_ref, qi_ref, ckv_ref, ki_ref, wi_ref, qb_ref, kb_ref, vb_ref):
    xb = x_ref[...].astype(BF16)

    def mm(c0, c1):
        return jnp.dot(xb, w_ref[:, c0:c1], preferred_element_type=F32)

    qa = mm(_C_QA, _C_QI)
    qlat = jnp.dot(qa.astype(BF16), wuk_ref[...], preferred_element_type=F32)
    qlat_ref[...] = (qlat * (A_HEAD_DIM ** -0.5)).astype(BF16)
    qi_ref[...] = mm(_C_QI, _C_CKV).astype(BF16)

    ckv = mm(_C_CKV, _C_KI)
    ms = jnp.mean(ckv * ckv, axis=-1, keepdims=True)
    ckv_ref[...] = ((ckv * lax.rsqrt(ms + LN_EPS)) * kvg_ref[...]).astype(BF16)

    ki = mm(_C_KI, _C_WI)[:, :IDX_DIM]
    mu = jnp.mean(ki, axis=-1, keepdims=True)
    var = jnp.mean(jnp.square(ki - mu), axis=-1, keepdims=True)
    ki_ref[...] = (((ki - mu) * lax.rsqrt(var + LN_EPS)) * kig_ref[...] + kib_ref[...]).astype(BF16)

    wi_ref[...] = mm(_C_WI, _C_QKVB)[:, :IDX_HEADS]
    qb_ref[...] = mm(_C_QKVB, _C_QKVB + B_WIDTH).astype(BF16)
    kb_ref[...] = mm(_C_QKVB + B_WIDTH, _C_QKVB + 2 * B_WIDTH).astype(BF16)
    vb_ref[...] = mm(_C_QKVB + 2 * B_WIDTH, _C_END).astype(BF16)


def _projection(xf, w_mix, wuk_bd, kvg, kig, kib):
    n, tm = xf.shape[0], TOKEN_TILE
    row = lambda w: pl.BlockSpec((tm, w), lambda i: (i, 0))
    outs = [(A_HEADS * A_KV_RANK, BF16), (IDX_HEADS * IDX_DIM, BF16), (A_KV_RANK, BF16),
            (IDX_DIM, BF16), (IDX_HEADS, F32), (B_WIDTH, BF16), (B_WIDTH, BF16), (B_WIDTH, BF16)]
    return pl.pallas_call(
        _proj_kernel,
        out_shape=[jax.ShapeDtypeStruct((n, w), dt) for w, dt in outs],
        grid=(n // tm,),
        in_specs=[row(D_MODEL), _full(w_mix), _full(wuk_bd), _full(kvg), _full(kig), _full(kib)],
        out_specs=[row(w) for w, _ in outs],
        compiler_params=_params(("parallel",)),
        name="proj",
    )(xf, w_mix, wuk_bd, kvg, kig, kib)


def _mixer_a_kernel(kidx_ref, ckv_ref, ckvt_ref, qi_ref, wit_ref, qlat_ref, wuv_ref,
                    ya_ref, keys_ref, acc_ref, *, seq, topk):
    i = pl.program_id(1)
    nkt = i // 2 + 1
    nsub = KEY_TILE // SUBLANES
    lane = lax.broadcasted_iota(I32, (1, LANES), 1)
    qpos = i * Q_BLOCK + lane
    qchunk = qpos >> 6
    row_iota = lax.broadcasted_iota(I32, (KEY_TILE, LANES), 0)

    qi_blk = qi_ref[0, 0]
    w_all = wit_ref[0] * IDX_SCALE

    def idx_body(kt, carry):
        lg = pl.dot(kidx_ref[0, kt], qi_blk, trans_b=True)
        isc = jnp.zeros((KEY_TILE, LANES), F32)
        for h in range(IDX_HEADS):
            isc = isc + jnp.maximum(lg[:, h * LANES:(h + 1) * LANES], 0.0) * w_all[h:h + 1, :]
        bits = lax.bitcast_convert_type(isc + 0.0, I32)
        key = bits ^ ((bits >> 31) & 0x7FFFFFFF)
        spos = kt * KEY_TILE + row_iota
        key = jnp.where((spos >> 6) <= qchunk, key, INT_MIN)
        keys_ref[kt] = key
        return carry

    def over_tiles(body, init):
        def pair(j, c):
            return body(2 * j + 1, body(2 * j, c))
        c = lax.fori_loop(0, nkt // 2, pair, init)
        return lax.cond(nkt % 2 == 1, lambda c: body(nkt - 1, c), lambda c: c, c)

    over_tiles(idx_body, 0)

    def count(pred):
        def body(kt, acc):
            spos = kt * KEY_TILE + row_iota
            m = jnp.where(pred(keys_ref[kt], spos), 1, 0).astype(I32)
            return acc + jnp.sum(m.reshape(nsub, SUBLANES, LANES), axis=0)
        acc = over_tiles(body, jnp.zeros((SUBLANES, LANES), I32))
        return jnp.sum(acc, axis=0, keepdims=True)

    def bit_body(it, carry):
        thr, cge = carry
        cand = thr ^ lax.shift_left(jnp.int32(1), 31 - it)
        cnt = count(lambda kv, spos: kv >= cand)
        ok = cnt >= topk
        return jnp.where(ok, cand, thr), jnp.where(ok, cnt, cge)

    thr0 = jnp.full((1, LANES), INT_MIN, I32)
    thr, cge = lax.fori_loop(0, 32, bit_body, (thr0, jnp.zeros((1, LANES), I32)))

    nbits = int(np.log2(seq)) + 1
    has_thr = thr > INT_MIN
    need_tie = jnp.max(jnp.where(has_thr, cge, 0)) > topk

    def tie_fn():
        room = topk - count(lambda kv, spos: kv > thr)

        def jb(it, cut):
            cand = cut | lax.shift_left(jnp.int32(1), nbits - 1 - it)
            f = count(lambda kv, spos: (kv == thr) & (spos < cand))
            return jnp.where(f <= room, cand, cut)
        return lax.fori_loop(0, nbits, jb, jnp.zeros((1, LANES), I32))

    cut = lax.cond(need_tie, tie_fn, lambda: jnp.full((1, LANES), 2 * seq, I32))
    cut = jnp.where(has_thr, cut, 0)

    qlat = qlat_ref[0]
    qr = jnp.concatenate([qlat[:, h * LANES:(h + 1) * LANES] for h in range(A_HEADS)], axis=0)
    acc_ref[...] = jnp.zeros_like(acc_ref)

    def col_reduce(x, op):
        return op(op(x.reshape(nsub, SUBLANES, LANES), axis=0), axis=0, keepdims=True)

    def att_body(kt, carry):
        m, l = carry
        st = pl.dot(ckv_ref[0, kt], qr, trans_b=True)
        kv = keys_ref[kt]
        spos = kt * KEY_TILE + row_iota
        sel = (kv > thr) | ((kv == thr) & (spos < cut))
        dist = jnp.abs(qpos - spos).astype(F32)
        m_new, l_new, alphas, ps = [], [], [], []
        for h in range(A_HEADS):
            sh = st[:, h * LANES:(h + 1) * LANES] - A_SLOPES[h] * dist
            sh = jnp.where(sel, sh, -jnp.inf)
            mn = jnp.maximum(m[h], col_reduce(sh, jnp.max))
            ms = jnp.where(mn == -jnp.inf, 0.0, mn)
            p = jnp.exp(sh - ms)
            a = jnp.exp(m[h] - ms)
            m_new.append(mn)
            l_new.append(a * l[h] + col_reduce(p, jnp.sum))
            alphas.append(a)
            ps.append(p.astype(BF16))
        acc_ref[...] = acc_ref[...] * jnp.concatenate(alphas, axis=1) + jnp.dot(
            ckvt_ref[0, kt], jnp.concatenate(ps, axis=1), preferred_element_type=F32)
        return tuple(m_new), tuple(l_new)

    _, l8 = over_tiles(att_body, (tuple(jnp.full((1, LANES), -jnp.inf, F32) for _ in range(A_HEADS)),
                                  tuple(jnp.zeros((1, LANES), F32) for _ in range(A_HEADS))))
    inv = [1.0 / l for l in l8]

    for j in range(A_HEADS // 2):
        y = None
        for h in (2 * j, 2 * j + 1):
            o = (acc_ref[:, h * LANES:(h + 1) * LANES] * inv[h]).astype(BF16)
            t = pl.dot(o, wuv_ref[h], trans_a=True)
            y = t if y is None else y + t
        ya_ref[0, :, j * LANES:(j + 1) * LANES] = y.astype(BF16)


def _mixer_a(kidx, ckv, ckvt, qi_r, wit, qlat, wuv_pad, bsz, seq):
    nqb = seq // Q_BLOCK
    nkt = seq // KEY_TILE
    topk = min(IDX_TOPK, seq // 4)
    kern = functools.partial(_mixer_a_kernel, seq=seq, topk=topk)
    return pl.pallas_call(
        kern,
        out_shape=jax.ShapeDtypeStruct((bsz, seq, A_WIDTH), BF16),
        grid=(bsz, nqb),
        in_specs=[
            pl.BlockSpec((1, nkt, KEY_TILE, IDX_DIM), lambda b, i: (b, 0, 0, 0)),
            pl.BlockSpec((1, nkt, KEY_TILE, A_KV_RANK), lambda b, i: (b, 0, 0, 0)),
            pl.BlockSpec((1, nkt, A_KV_RANK, KEY_TILE), lambda b, i: (b, 0, 0, 0)),
            pl.BlockSpec((1, 1, IDX_HEADS * Q_BLOCK, IDX_DIM), lambda b, i: (b, i, 0, 0)),
            pl.BlockSpec((1, IDX_HEADS, Q_BLOCK), lambda b, i: (b, 0, i)),
            pl.BlockSpec((1, Q_BLOCK, A_HEADS * A_KV_RANK), lambda b, i: (b, i, 0)),
            pl.BlockSpec(wuv_pad.shape, lambda b, i: (0, 0, 0)),
        ],
        out_specs=pl.BlockSpec((1, Q_BLOCK, A_WIDTH), lambda b, i: (b, i, 0)),
        scratch_shapes=[
            pltpu.VMEM((nkt, KEY_TILE, LANES), I32),
            pltpu.VMEM((A_KV_RANK, A_HEADS * LANES), F32),
        ],
        compiler_params=_params(("parallel", "arbitrary")),
        name="mixer_a",
    )(kidx, ckv, ckvt, qi_r, wit, qlat, wuv_pad)


def _mixer_b_kernel(q_ref, k_ref, v_ref, bias_ref, yb_ref):
    t0 = pl.multiple_of(pl.program_id(1) * B_QTILE, B_QTILE)
    kw = k_ref[0, pl.ds(t0, B_WIN), :]
    vw = v_ref[0, pl.ds(t0, B_WIN), :]
    q = q_ref[0]
    kvalid = (t0 - B_PAD + lax.broadcasted_iota(I32, (1, B_WIN), 1)) >= 0
    lane = lax.broadcasted_iota(I32, (1, LANES), 1)
    scale = B_HEAD_DIM ** -0.5
    for j in range(B_HEADS // 2):
        sl = slice(j * LANES, (j + 1) * LANES)
        qs, ks, vs = q[:, sl], kw[:, sl], vw[:, sl]
        outs = []
        for hh in range(2):
            mine = (lane >= B_HEAD_DIM) if hh else (lane < B_HEAD_DIM)
            qm = jnp.where(mine, qs, jnp.zeros_like(qs))
            s = pl.dot(qm, ks, trans_b=True) * scale + bias_ref[2 * j + hh]
            s = jnp.where(kvalid, s, -jnp.inf)
            m = jnp.max(s, axis=-1, keepdims=True)
            p = jnp.exp(s - m)
            l = jnp.sum(p, axis=-1, keepdims=True)
            outs.append(jnp.dot(p.astype(BF16), vs, preferred_element_type=F32) / l)
        yb_ref[0, :, sl] = jnp.where(lane < B_HEAD_DIM, outs[0], outs[1]).astype(BF16)


def _band_bias(rel_bias):
    tq = np.arange(B_QTILE)[:, None]
    j = np.arange(B_WIN)[None, :]
    band = (j // CHUNK >= tq // CHUNK) & (j // CHUNK <= tq // CHUNK + B_LEFT_CHUNKS)
    m = np.arange(B_QTILE - 1 + B_WIN)
    line = rel_bias[:, np.clip(B_WIN - 1 - m, -REL_CLIP, REL_CLIP) + REL_CLIP].astype(F32)
    table = jnp.stack([line[:, B_QTILE - 1 - t:B_QTILE - 1 - t + B_WIN] for t in range(B_QTILE)], axis=1)
    return jnp.where(jnp.asarray(band)[None], table, -jnp.inf)


def _mixer_b(qb, kb_pad, vb_pad, bias, bsz, seq):
    return pl.pallas_call(
        _mixer_b_kernel,
        out_shape=jax.ShapeDtypeStruct((bsz, seq, B_WIDTH), BF16),
        grid=(bsz, seq // B_QTILE),
        in_specs=[
            pl.BlockSpec((1, B_QTILE, B_WIDTH), lambda b, i: (b, i, 0)),
            pl.BlockSpec((1, seq + B_PAD, B_WIDTH), lambda b, i: (b, 0, 0)),
            pl.BlockSpec((1, seq + B_PAD, B_WIDTH), lambda b, i: (b, 0, 0)),
            pl.BlockSpec(bias.shape, lambda b, i: (0, 0, 0)),
        ],
        out_specs=pl.BlockSpec((1, B_QTILE, B_WIDTH), lambda b, i: (b, i, 0)),
        compiler_params=_params(("parallel", "arbitrary")),
        name="mixer_b",
    )(qb, kb_pad, vb_pad, bias)


def _layer_norm(z, g, b):
    mu = jnp.mean(z, axis=-1, keepdims=True)
    var = jnp.mean(jnp.square(z - mu), axis=-1, keepdims=True)
    return ((z - mu) * lax.rsqrt(var + LN_EPS)) * g + b


def _merge_kernel(x_ref, ya_ref, yb_ref, wg_ref, wba_ref, wbb_ref, wo_ref, g_ref, b_ref,
                  wr_ref, br_ref, x1t_ref, topi_ref, gate_ref):
    tm = x_ref.shape[0]
    x = x_ref[...]
    xb = x.astype(BF16)
    ga = jnp.dot(xb, wg_ref[:, :D_MODEL], preferred_element_type=F32)
    a = jnp.dot(ya_ref[...], wba_ref[...], preferred_element_type=F32)
    merged = jax.nn.sigmoid(ga) * a
    gb = jnp.dot(xb, wg_ref[:, D_MODEL:], preferred_element_type=F32)
    b = jnp.dot(yb_ref[...], wbb_ref[...], preferred_element_type=F32)
    merged = merged + jax.nn.sigmoid(gb) * b
    out = jnp.dot(merged.astype(BF16), wo_ref[...], preferred_element_type=F32)
    x1 = _layer_norm(DN_ALPHA * x + out, g_ref[...], b_ref[...])
    for c in range(ROW_TILES):
        x1t_ref[pl.ds(c, tm, stride=ROW_TILES), :] = x1[:, c * LANES:(c + 1) * LANES]

    lg = pl.dot(wr_ref[...], x1.astype(BF16), trans_b=True) + br_ref[...]
    eidx = lax.broadcasted_iota(I32, lg.shape, 0)
    vals, idxs = [], []
    for _ in range(TOP_K):
        m = jnp.max(lg, axis=0, keepdims=True)
        sel = jnp.min(jnp.where(lg == m, eidx, N_EXPERTS), axis=0, keepdims=True)
        vals.append(m)
        idxs.append(sel)
        lg = jnp.where(eidx == sel, -jnp.inf, lg)
    es = [jnp.exp(v - vals[0]) for v in vals]
    tot = es[0] + es[1] + es[2] + es[3]
    topi_ref[...] = jnp.concatenate(idxs, axis=0)
    gate_ref[...] = jnp.concatenate([e / tot for e in es], axis=0)


def _merge(xf, ya, yb, wg, wba, wbb, wo, g1, b1, wr_t, br):
    n, tm = xf.shape[0], TOKEN_TILE
    row = lambda w: pl.BlockSpec((tm, w), lambda i: (i, 0))
    col = pl.BlockSpec((TOP_K, tm), lambda i: (0, i))
    return pl.pallas_call(
        _merge_kernel,
        out_shape=[jax.ShapeDtypeStruct((n * ROW_TILES, LANES), F32),
                   jax.ShapeDtypeStruct((TOP_K, n), I32),
                   jax.ShapeDtypeStruct((TOP_K, n), F32)],
        grid=(n // tm,),
        in_specs=[row(D_MODEL), row(A_WIDTH), row(B_WIDTH), _full(wg), _full(wba), _full(wbb),
                  _full(wo), _full(g1), _full(b1), _full(wr_t), _full(br)],
        out_specs=[pl.BlockSpec((tm * ROW_TILES, LANES), lambda i: (i, 0)), col, col],
        compiler_params=_params(("parallel",)),
        name="merge",
    )(xf, ya, yb, wg, wba, wbb, wo, g1, b1, wr_t, br)


def _one_hot_rows(topi, k, width):
    eidx = lax.broadcasted_iota(I32, (N_EXPERTS, width), 0)
    return eidx == topi[k:k + 1, :]


def _route_kernel(topi_ref, tri_ref, rank_ref, cnt_ref, run_ref):
    @pl.when(pl.program_id(0) == 0)
    def _():
        run_ref[...] = jnp.zeros_like(run_ref)

    topi = topi_ref[...]
    tr = topi.shape[1]
    hot = [_one_hot_rows(topi, k, tr) for k in range(TOP_K)]
    oh = jnp.concatenate([jnp.where(h, 1.0, 0.0).astype(BF16) for h in hot], axis=0)
    prefix = jnp.dot(oh, tri_ref[...], preferred_element_type=F32)
    base = run_ref[...]
    ranks = []
    for k in range(TOP_K):
        tbl = prefix[k * N_EXPERTS:(k + 1) * N_EXPERTS, :] + base
        ranks.append(jnp.sum(jnp.where(hot[k], tbl, 0.0), axis=0, keepdims=True))
        base = base + jnp.sum(jnp.where(hot[k], 1.0, 0.0), axis=1, keepdims=True)
    rank_ref[...] = jnp.concatenate(ranks, axis=0).astype(I32)
    run_ref[...] = base
    cnt_ref[...] = base.astype(I32)


def _route(topi):
    n, tr = topi.shape[1], TOKEN_TILE
    tri = (np.arange(tr)[:, None] < np.arange(tr)[None, :]).astype(np.float32)
    tri = jnp.asarray(tri, BF16)
    col = pl.BlockSpec((TOP_K, tr), lambda i: (0, i))
    return pl.pallas_call(
        _route_kernel,
        out_shape=[jax.ShapeDtypeStruct((TOP_K, n), I32), jax.ShapeDtypeStruct((N_EXPERTS, 1), I32)],
        grid=(n // tr,),
        in_specs=[col, _full(tri)],
        out_specs=[col, pl.BlockSpec((N_EXPERTS, 1), lambda i: (0, 0))],
        scratch_shapes=[pltpu.VMEM((N_EXPERTS, 1), F32)],
        compiler_params=_params(("arbitrary",)),
        name="route",
    )(topi, tri)


def _positions_kernel(topi_ref, rank_ref, pstv_ref, pos_ref):
    topi = topi_ref[...]
    rows = []
    for k in range(TOP_K):
        start = jnp.sum(jnp.where(_one_hot_rows(topi, k, topi.shape[1]), pstv_ref[...], 0),
                        axis=0, keepdims=True)
        rows.append(rank_ref[k:k + 1, :] + start)
    pos_ref[...] = jnp.concatenate(rows, axis=0)


def _positions(topi, rank, pstarts):
    n, td = topi.shape[1], TOKEN_TILE
    col = pl.BlockSpec((TOP_K, td), lambda i: (0, i))
    pstv = pstarts.reshape(N_EXPERTS, 1)
    return pl.pallas_call(
        _positions_kernel,
        out_shape=jax.ShapeDtypeStruct((TOP_K, n), I32),
        grid=(n // td,),
        in_specs=[col, col, _full(pstv)],
        out_specs=col,
        compiler_params=_params(("parallel",)),
        name="positions",
    )(topi, rank, pstv)


def _dispatch_kernel(cnt_ref, pst_ref, pos_hbm, x_hbm, xs_hbm,
                     pos_smem, xbuf, zbuf, sem_p, sem_x, sem_d, sem_z):
    i = pl.program_id(0)
    n = pl.num_programs(0)
    ng = pos_smem.shape[1]
    td = ng * DMA_GROUP
    tile_rows = td * ROW_TILES

    def pos_copy(step, slot):
        return pltpu.make_async_copy(pos_hbm.at[pl.ds(step * ng, ng)], pos_smem.at[slot], sem_p.at[slot])

    def x_copy(step, slot):
        return pltpu.make_async_copy(x_hbm.at[pl.ds(step * tile_rows, tile_rows)], xbuf.at[slot],
                                     sem_x.at[slot])

    @pl.when(i == 0)
    def _():
        pos_copy(0, 0).start()
        x_copy(0, 0).start()

    @pl.when(i + 1 < n)
    def _():
        pos_copy(i + 1, (i + 1) % 2).start()
        x_copy(i + 1, (i + 1) % 3).start()

    slot = i % 2
    xslot = i % 3
    pos_copy(i, slot).wait()
    x_copy(i, xslot).wait()

    def step_bytes(s):
        rows_ = TOP_K * tile_rows
        return pltpu.make_async_copy(xs_hbm.at[pl.ds(0, rows_)], xs_hbm.at[pl.ds(0, rows_)], sem_d.at[s])

    def body(g, c):
        dst = [pos_smem[slot, g, j] * ROW_TILES for j in range(DMA_GROUP * TOP_K)]
        for u in range(DMA_GROUP):
            src = xbuf.at[xslot, pl.ds((g * DMA_GROUP + u) * ROW_TILES, ROW_TILES)]
            for k in range(TOP_K):
                pltpu.make_async_copy(src, xs_hbm.at[pl.ds(dst[u * TOP_K + k], ROW_TILES)],
                                      sem_d.at[xslot]).start(priority=k % 2)
        return c
    lax.fori_loop(0, ng, body, 0)

    @pl.when(i > 0)
    def _():
        step_bytes((i + 2) % 3).wait()

    @pl.when(i == n - 1)
    def _():
        step_bytes(xslot).wait()
        zbuf[...] = jnp.zeros_like(zbuf)

        def zero_copy(r):
            return pltpu.make_async_copy(zbuf, xs_hbm.at[pl.ds(r * ROW_TILES, ROW_TILES)], sem_z)

        def zero_rows(lo, hi):
            def start(r, c2):
                zero_copy(r).start()
                return c2

            def wait(r, c2):
                zero_copy(r).wait()
                return c2
            lax.fori_loop(lo, hi, start, 0)
            lax.fori_loop(lo, hi, wait, 0)

        def padded_end(e):
            return pst_ref[e] + (cnt_ref[e] + MOE_BLOCK - 1) // MOE_BLOCK * MOE_BLOCK

        def ebody(e, c):
            zero_rows(pst_ref[e] + cnt_ref[e], padded_end(e))
            return c
        lax.fori_loop(0, N_EXPERTS, ebody, 0)
        zero_rows(padded_end(N_EXPERTS - 1), xs_hbm.shape[0] // ROW_TILES)


def _dispatch(counts, pstarts, pos_g, x1t, cap):
    n, td = pos_g.shape[0] * DMA_GROUP, TOKEN_TILE
    any_spec = pl.BlockSpec(memory_space=pl.ANY)
    return pl.pallas_call(
        _dispatch_kernel,
        out_shape=jax.ShapeDtypeStruct((cap * ROW_TILES, LANES), F32),
        grid_spec=pltpu.PrefetchScalarGridSpec(
            num_scalar_prefetch=2,
            grid=(n // td,),
            in_specs=[any_spec, any_spec],
            out_specs=any_spec,
            scratch_shapes=[
                pltpu.SMEM((2, td // DMA_GROUP, DMA_GROUP * TOP_K), I32),
                pltpu.VMEM((3, td * ROW_TILES, LANES), F32),
                pltpu.VMEM((ROW_TILES, LANES), F32),
                pltpu.SemaphoreType.DMA((2,)),
                pltpu.SemaphoreType.DMA((3,)),
                pltpu.SemaphoreType.DMA((3,)),
                pltpu.SemaphoreType.DMA,
            ],
        ),
        compiler_params=_params(("arbitrary",)),
        name="dispatch",
    )(counts, pstarts, pos_g, x1t)


def _moe_kernel(be_ref, nu_ref, xs_ref, wg_ref, bg_ref, wu_ref, bu_ref, wd_ref, bd_ref, ys_ref,
                wg_bf, wu_bf, wd_bf):
    i = pl.program_id(0)
    blk = MOE_BLOCK
    used = i < nu_ref[0]

    @pl.when(jnp.logical_or(i == 0, be_ref[i] != be_ref[jnp.maximum(i - 1, 0)]))
    def _():
        wg_bf[...] = wg_ref[0].astype(BF16)
        wu_bf[...] = wu_ref[0].astype(BF16)
        wd_bf[...] = wd_ref[0].astype(BF16)

    @pl.when(used)
    def _():
        xb = jnp.concatenate([xs_ref[pl.ds(c, blk, stride=ROW_TILES), :].astype(BF16)
                              for c in range(ROW_TILES)], axis=1)
        hg = jnp.minimum(jnp.dot(xb, wg_bf[...], preferred_element_type=F32) + bg_ref[0], SWIGLU_LIMIT)
        hu = jnp.clip(jnp.dot(xb, wu_bf[...], preferred_element_type=F32) + bu_ref[0],
                      -SWIGLU_LIMIT, SWIGLU_LIMIT)
        h = (hu + 1.0) * (hg * jax.nn.sigmoid(SWIGLU_ALPHA * hg))
        y = jnp.dot(h.astype(BF16), wd_bf[...], preferred_element_type=F32) + bd_ref[0]
        for c in range(ROW_TILES):
            ys_ref[pl.ds(c, blk, stride=ROW_TILES), :] = y[:, c * LANES:(c + 1) * LANES]

    @pl.when(jnp.logical_not(used))
    def _():
        ys_ref[...] = jnp.zeros_like(ys_ref)


def _moe(blk_expert, nused, xs, wg, bg, wu, bu, wd, bd):
    blk = MOE_BLOCK
    nblk = xs.shape[0] // (blk * ROW_TILES)
    wspec = lambda a: pl.BlockSpec((1,) + a.shape[1:], lambda i, be, nu: (be[i], 0, 0))
    xspec = pl.BlockSpec((blk * ROW_TILES, LANES), lambda i, be, nu: (jnp.minimum(i, nu[0] - 1), 0))
    return pl.pallas_call(
        _moe_kernel,
        out_shape=jax.ShapeDtypeStruct(xs.shape, F32),
        grid_spec=pltpu.PrefetchScalarGridSpec(
            num_scalar_prefetch=2,
            grid=(nblk,),
            in_specs=[xspec, wspec(wg), wspec(bg), wspec(wu), wspec(bu), wspec(wd), wspec(bd)],
            out_specs=pl.BlockSpec((blk * ROW_TILES, LANES), lambda i, be, nu: (i, 0)),
            scratch_shapes=[pltpu.VMEM(wg.shape[1:], BF16), pltpu.VMEM(wu.shape[1:], BF16),
                            pltpu.VMEM(wd.shape[1:], BF16)],
        ),
        compiler_params=_params(("arbitrary",)),
        name="moe",
    )(blk_expert, nused, xs, wg, bg, wu, bu, wd, bd)


def _final_kernel(pos_hbm, gt_ref, x1t_ref, ys_hbm, g_ref, b_ref, o_ref,
                  pos_smem, ybuf, sem_p, sem_g):
    i = pl.program_id(0)
    n = pl.num_programs(0)
    ft = FINAL_TILE
    slot_rows = TOP_K * ft * ROW_TILES

    ng = ft // DMA_GROUP

    def pos_copy(tile, slot):
        return pltpu.make_async_copy(pos_hbm.at[pl.ds(tile * ng, ng)], pos_smem.at[slot], sem_p.at[slot])

    def gather_rows(pslot, yslot):
        def body(g, c):
            src = [pos_smem[pslot, g, j] * ROW_TILES for j in range(DMA_GROUP * TOP_K)]
            base = yslot * slot_rows + g * (DMA_GROUP * ROW_TILES)
            for u in range(DMA_GROUP):
                for k in range(TOP_K):
                    dst = base + (k * ft + u) * ROW_TILES
                    pltpu.make_async_copy(ys_hbm.at[pl.ds(src[u * TOP_K + k], ROW_TILES)],
                                          ybuf.at[pl.ds(dst, ROW_TILES)],
                                          sem_g.at[yslot]).start(priority=k % 2)
            return c
        lax.fori_loop(0, ng, body, 0)

    def gather_wait(yslot):
        pltpu.make_async_copy(ys_hbm.at[pl.ds(0, slot_rows)],
                              ybuf.at[pl.ds(yslot * slot_rows, slot_rows)], sem_g.at[yslot]).wait()

    @pl.when(i == 0)
    def _():
        c = pos_copy(0, 0)
        c.start()
        c.wait()
        gather_rows(0, 0)

        @pl.when(n > 1)
        def _():
            pos_copy(1, 1).start()

    @pl.when(i + 1 < n)
    def _():
        pos_copy(i + 1, (i + 1) % 3).wait()
        gather_rows((i + 1) % 3, (i + 1) % 2)

    @pl.when(i + 2 < n)
    def _():
        pos_copy(i + 2, (i + 2) % 3).start()

    slot = i % 2
    gather_wait(slot)
    g = gt_ref[...]
    zs = []
    for c in range(ROW_TILES):
        f = None
        for k in range(TOP_K):
            yk = ybuf[pl.ds(slot * slot_rows + k * ft * ROW_TILES + c, ft, stride=ROW_TILES), :]
            term = g[:, k:k + 1] * yk
            f = term if f is None else f + term
        zs.append(DN_ALPHA * x1t_ref[pl.ds(c, ft, stride=ROW_TILES), :] + f)
    o_ref[...] = _layer_norm(jnp.concatenate(zs, axis=1), g_ref[...], b_ref[...])


def _final(pos, gates_t, x1t, ys, g2, b2):
    ft = FINAL_TILE
    n = gates_t.shape[0]
    any_spec = pl.BlockSpec(memory_space=pl.ANY)
    return pl.pallas_call(
        _final_kernel,
        out_shape=jax.ShapeDtypeStruct((n, D_MODEL), F32),
        grid=(n // ft,),
        in_specs=[any_spec, pl.BlockSpec((ft, TOP_K), lambda i: (i, 0)),
                  pl.BlockSpec((ft * ROW_TILES, LANES), lambda i: (i, 0)), any_spec,
                  _full(g2), _full(b2)],
        out_specs=pl.BlockSpec((ft, D_MODEL), lambda i: (i, 0)),
        scratch_shapes=[
            pltpu.SMEM((3, ft // DMA_GROUP, DMA_GROUP * TOP_K), I32),
            pltpu.VMEM((2 * TOP_K * ft * ROW_TILES, LANES), F32),
            pltpu.SemaphoreType.DMA((3,)),
            pltpu.SemaphoreType.DMA((2,)),
        ],
        compiler_params=_params(("arbitrary",)),
        name="final",
    )(pos, gates_t, x1t, ys, g2, b2)


def _mix_weight(w_in):
    qa, ckv, qi, ki, wi, qb, kb, vb, _, _ = jnp.split(w_in, np.cumsum(SPLITS)[:-1].tolist(), axis=-1)
    pad = lambda a: jnp.pad(a, ((0, 0), (0, LANES - a.shape[1])))
    return jnp.concatenate([qa, qi, ckv, pad(ki), pad(wi), qb, kb, vb], axis=1).astype(BF16)


def _layer(x, w_in, kv_norm_g, idx_k_norm_g, idx_k_norm_b, w_uk, w_uv, rel_bias, w_branch_a,
           w_branch_b, w_out, ln1_g, ln1_b, w_router, b_router, w_gate, b_gate, w_up, b_up,
           w_down, b_down, ln2_g, ln2_b):
    bsz, seq, _ = x.shape
    n = bsz * seq
    assert seq % KEY_TILE == 0 and n % TOKEN_TILE == 0
    xf = x.reshape(n, D_MODEL)
    row = lambda v: v.reshape(1, -1).astype(F32)

    w_mix = _mix_weight(w_in)
    w_gates = w_in[:, sum(SPLITS[:8]):].astype(BF16)
    wuk_bd = jnp.zeros((A_WIDTH, A_HEADS * A_KV_RANK), F32)
    wuv_pad = jnp.zeros((A_HEADS, A_KV_RANK, LANES), F32)
    for h in range(A_HEADS):
        wuk_bd = wuk_bd.at[h * A_HEAD_DIM:(h + 1) * A_HEAD_DIM, h * A_KV_RANK:(h + 1) * A_KV_RANK].set(w_uk[h])
        c0 = (h % 2) * A_HEAD_DIM
        wuv_pad = wuv_pad.at[h, :, c0:c0 + A_HEAD_DIM].set(w_uv[h])
    wuk_bd, wuv_pad = wuk_bd.astype(BF16), wuv_pad.astype(BF16)

    qlat, qi, ckv, ki, wi, qb, kb, vb = _projection(
        xf, w_mix, wuk_bd, row(kv_norm_g), row(idx_k_norm_g), row(idx_k_norm_b))

    nqb, nkt = seq // Q_BLOCK, seq // KEY_TILE
    kidx = ki.reshape(bsz, nkt, KEY_TILE, IDX_DIM)
    ckv4 = ckv.reshape(bsz, nkt, KEY_TILE, A_KV_RANK)
    ckvt = ckv4.transpose(0, 1, 3, 2)
    qi_r = qi.reshape(bsz, nqb, Q_BLOCK, IDX_HEADS, IDX_DIM).transpose(0, 1, 3, 2, 4).reshape(
        bsz, nqb, IDX_HEADS * Q_BLOCK, IDX_DIM)
    wit = wi.reshape(bsz, seq, IDX_HEADS).transpose(0, 2, 1)
    ya = _mixer_a(kidx, ckv4, ckvt, qi_r, wit, qlat.reshape(bsz, seq, -1), wuv_pad, bsz, seq)

    front = lambda a: jnp.pad(a.reshape(bsz, seq, B_WIDTH), ((0, 0), (B_PAD, 0), (0, 0)))
    yb = _mixer_b(qb.reshape(bsz, seq, B_WIDTH), front(kb), front(vb), _band_bias(rel_bias), bsz, seq)

    x1t, topi, gates = _merge(
        xf, ya.reshape(n, A_WIDTH), yb.reshape(n, B_WIDTH), w_gates, w_branch_a.astype(BF16),
        w_branch_b.astype(BF16), w_out.astype(BF16), row(ln1_g), row(ln1_b),
        w_router.T.astype(BF16), b_router.reshape(-1, 1).astype(F32))

    blk = MOE_BLOCK
    cap = n * TOP_K + N_EXPERTS * blk
    nblk = cap // blk
    rank, cnt = _route(topi)
    counts = cnt[:, 0]
    padded = (counts + blk - 1) // blk * blk
    pends = jnp.cumsum(padded).astype(I32)
    pstarts = pends - padded
    nused = (pends[-1:] // blk).astype(I32)
    blk_start = jnp.arange(nblk, dtype=I32) * blk
    blk_expert = jnp.minimum(jnp.sum(blk_start[:, None] >= pends[None, :], axis=1), N_EXPERTS - 1).astype(I32)

    pos = _positions(topi, rank, pstarts)
    pos_g = pos.T.reshape(n // DMA_GROUP, DMA_GROUP * TOP_K)
    xs = _dispatch(counts, pstarts, pos_g, x1t, cap)
    ys = _moe(blk_expert, nused, xs, w_gate, b_gate[:, None, :], w_up, b_up[:, None, :],
              w_down, b_down[:, None, :])
    out = _final(pos_g, gates.T, x1t, ys, row(ln2_g), row(ln2_b))
    return out.reshape(bsz, seq, D_MODEL)


def kernel(x, w_in, kv_norm_g, idx_k_norm_g, idx_k_norm_b, w_uk, w_uv, rel_bias, w_branch_a,
           w_branch_b, w_out, ln1_g, ln1_b, w_router, b_router, w_gate, b_gate, w_up, b_up,
           w_down, b_down, ln2_g, ln2_b):
    for l in range(DEPTH):
        x = _layer(x, w_in[l], kv_norm_g[l], idx_k_norm_g[l], idx_k_norm_b[l], w_uk[l], w_uv[l],
                   rel_bias[l], w_branch_a[l], w_branch_b[l], w_out[l], ln1_g[l], ln1_b[l],
                   w_router[l], b_router[l], w_gate[l], b_gate[l], w_up[l], b_up[l], w_down[l],
                   b_down[l], ln2_g[l], ln2_b[l])
    return x
```

```python
import functools

import jax
import jax.numpy as jnp
import numpy as np
from jax import lax
from jax.experimental import pallas as pl
from jax.experimental.pallas import tpu as pltpu

F32 = jnp.float32
BF16 = jnp.bfloat16
I32 = jnp.int32

D_MODEL = 1024
CHUNK = 64
Q_BLOCK = 128
A_HEADS = 8
A_HEAD_DIM = 64
A_KV_RANK = 128
IDX_HEADS = 8
IDX_DIM = 64
IDX_TOPK = 256
IDX_SCALE = (IDX_HEADS * IDX_DIM) ** -0.5
B_HEADS = 8
B_HEAD_DIM = 64
B_LEFT_CHUNKS = 8
REL_CLIP = 128
N_EXPERTS = 32
TOP_K = 4
D_FF = 1024
SWIGLU_LIMIT = 7.0
SWIGLU_ALPHA = 1.702
DEPTH = 1
DN_ALPHA = (2.0 * DEPTH) ** 0.25
LN_EPS = 1e-5

A_WIDTH = A_HEADS * A_HEAD_DIM
B_WIDTH = B_HEADS * B_HEAD_DIM
SPLITS = [A_WIDTH, A_KV_RANK, IDX_HEADS * IDX_DIM, IDX_DIM, IDX_HEADS,
          B_WIDTH, B_WIDTH, B_WIDTH, D_MODEL, D_MODEL]

LANES = 128
SUBLANES = 8
KEY_TILE = 256
B_QTILE = 256
B_PAD = B_LEFT_CHUNKS * CHUNK
B_WIN = B_PAD + B_QTILE
TOKEN_TILE = 512
MATMUL_TILE = 1024
MOE_BLOCK = 512
FINAL_TILE = 512
ROW_TILES = D_MODEL // LANES
DMA_GROUP = 4
INT_MIN = -2 ** 31
VMEM_LIMIT = 56 * 1024 * 1024

A_SLOPES = [2.0 ** (-8.0 * (h + 1) / A_HEADS) for h in range(A_HEADS)]

_C_QA = 0
_C_QI = _C_QA + A_WIDTH
_C_CKV = _C_QI + IDX_HEADS * IDX_DIM
_C_KI = _C_CKV + A_KV_RANK
_C_WI = _C_KI + LANES
_C_QKVB = _C_WI + LANES
_C_END = _C_QKVB + 3 * B_WIDTH


def _params(sem, vmem=VMEM_LIMIT):
    return pltpu.CompilerParams(dimension_semantics=sem, vmem_limit_bytes=vmem)


def _full(a):
    return pl.BlockSpec(a.shape, lambda *_: (0,) * a.ndim)


def _proj_kernel(x_ref, w_ref, wuk_ref, kvg_ref, kig_ref, kib_ref,
                 qlat_ref, qi_ref, ckv_ref, ki_ref, wi_ref, qb_ref, kb_ref, vb_ref):
    xb = x_ref[...].astype(BF16)

    def mm(c0, c1):
        return jnp.dot(xb, w_ref[:, c0:c1], preferred_element_type=F32)

    qa = mm(_C_QA, _C_QI)
    qlat = jnp.dot(qa.astype(BF16), wuk_ref[...], preferred_element_type=F32)
    qlat_ref[...] = (qlat * (A_HEAD_DIM ** -0.5)).astype(BF16)
    qi_ref[...] = mm(_C_QI, _C_CKV).astype(BF16)

    ckv = mm(_C_CKV, _C_KI)
    ms = jnp.mean(ckv * ckv, axis=-1, keepdims=True)
    ckv_ref[...] = ((ckv * lax.rsqrt(ms + LN_EPS)) * kvg_ref[...]).astype(BF16)

    ki = mm(_C_KI, _C_WI)[:, :IDX_DIM]
    mu = jnp.mean(ki, axis=-1, keepdims=True)
    var = jnp.mean(jnp.square(ki - mu), axis=-1, keepdims=True)
    ki_ref[...] = (((ki - mu) * lax.rsqrt(var + LN_EPS)) * kig_ref[...] + kib_ref[...]).astype(BF16)

    wi_ref[...] = mm(_C_WI, _C_QKVB)[:, :IDX_HEADS]
    qb_ref[...] = mm(_C_QKVB, _C_QKVB + B_WIDTH).astype(BF16)
    kb_ref[...] = mm(_C_QKVB + B_WIDTH, _C_QKVB + 2 * B_WIDTH).astype(BF16)
    vb_ref[...] = mm(_C_QKVB + 2 * B_WIDTH, _C_END).astype(BF16)


def _projection(xf, w_mix, wuk_bd, kvg, kig, kib):
    n, tm = xf.shape[0], min(MATMUL_TILE, xf.shape[0])
    row = lambda w: pl.BlockSpec((tm, w), lambda i: (i, 0))
    outs = [(A_HEADS * A_KV_RANK, BF16), (IDX_HEADS * IDX_DIM, BF16), (A_KV_RANK, BF16),
            (IDX_DIM, BF16), (IDX_HEADS, F32), (B_WIDTH, BF16), (B_WIDTH, BF16), (B_WIDTH, BF16)]
    return pl.pallas_call(
        _proj_kernel,
        out_shape=[jax.ShapeDtypeStruct((n, w), dt) for w, dt in outs],
        grid=(n // tm,),
        in_specs=[row(D_MODEL), _full(w_mix), _full(wuk_bd), _full(kvg), _full(kig), _full(kib)],
        out_specs=[row(w) for w, _ in outs],
        compiler_params=_params(("parallel",)),
        name="proj",
    )(xf, w_mix, wuk_bd, kvg, kig, kib)


def _mixer_a_kernel(kidx_ref, ckv_ref, ckvt_ref, qi_ref, wit_ref, qlat_ref, wuv_ref,
                    ya_ref, keys_ref, s_ref, acc_ref, *, seq, topk):
    i = pl.program_id(1)
    nkt = i // 2 + 1
    nsub = KEY_TILE // SUBLANES
    lane = lax.broadcasted_iota(I32, (1, LANES), 1)
    qpos = i * Q_BLOCK + lane
    qchunk = qpos >> 6
    row_iota = lax.broadcasted_iota(I32, (KEY_TILE, LANES), 0)

    qi_blk = qi_ref[0, 0]
    w_all = wit_ref[0] * IDX_SCALE

    def idx_body(kt, carry):
        lg = pl.dot(kidx_ref[0, kt], qi_blk, trans_b=True)
        isc = jnp.zeros((KEY_TILE, LANES), F32)
        for h in range(IDX_HEADS):
            isc = isc + jnp.maximum(lg[:, h * LANES:(h + 1) * LANES], 0.0) * w_all[h:h + 1, :]
        bits = lax.bitcast_convert_type(isc + 0.0, I32)
        key = bits ^ ((bits >> 31) & 0x7FFFFFFF)
        spos = kt * KEY_TILE + row_iota
        key = jnp.where((spos >> 6) <= qchunk, key, INT_MIN)
        keys_ref[kt] = key
        return carry

    def over_tiles(body, init):
        def pair(j, c):
            return body(2 * j + 1, body(2 * j, c))
        c = lax.fori_loop(0, nkt // 2, pair, init)
        return lax.cond(nkt % 2 == 1, lambda c: body(nkt - 1, c), lambda c: c, c)

    over_tiles(idx_body, 0)

    def count(pred):
        def body(kt, acc):
            spos = kt * KEY_TILE + row_iota
            m = jnp.where(pred(keys_ref[kt], spos), 1, 0).astype(I32)
            return acc + jnp.sum(m.reshape(nsub, SUBLANES, LANES), axis=0)
        acc = over_tiles(body, jnp.zeros((SUBLANES, LANES), I32))
        return jnp.sum(acc, axis=0, keepdims=True)

    def bit_body(it, carry):
        thr, cge = carry
        cand = thr ^ lax.shift_left(jnp.int32(1), 31 - it)
        cnt = count(lambda kv, spos: kv >= cand)
        ok = cnt >= topk
        return jnp.where(ok, cand, thr), jnp.where(ok, cnt, cge)

    thr0 = jnp.full((1, LANES), INT_MIN, I32)
    thr, cge = lax.fori_loop(0, 32, bit_body, (thr0, jnp.zeros((1, LANES), I32)))

    nbits = int(np.log2(seq)) + 1
    has_thr = thr > INT_MIN
    need_tie = jnp.max(jnp.where(has_thr, cge, 0)) > topk

    def tie_fn():
        room = topk - count(lambda kv, spos: kv > thr)

        def jb(it, cut):
            cand = cut | lax.shift_left(jnp.int32(1), nbits - 1 - it)
            f = count(lambda kv, spos: (kv == thr) & (spos < cand))
            return jnp.where(f <= room, cand, cut)
        return lax.fori_loop(0, nbits, jb, jnp.zeros((1, LANES), I32))

    cut = lax.cond(need_tie, tie_fn, lambda: jnp.full((1, LANES), 2 * seq, I32))
    cut = jnp.where(has_thr, cut, 0)

    qlat = qlat_ref[0]
    qr = jnp.concatenate([qlat[:, h * LANES:(h + 1) * LANES] for h in range(A_HEADS)], axis=0)

    def sc_body(kt, m):
        st = pl.dot(ckv_ref[0, kt], qr, trans_b=True)
        kv = keys_ref[kt]
        spos = kt * KEY_TILE + row_iota
        sel = (kv > thr) | ((kv == thr) & (spos < cut))
        dist = jnp.abs(qpos - spos).astype(F32)
        out = []
        for h in range(A_HEADS):
            sh = st[:, h * LANES:(h + 1) * LANES] - A_SLOPES[h] * dist
            sh = jnp.where(sel, sh, -jnp.inf)
            s_ref[kt, :, h * LANES:(h + 1) * LANES] = sh
            out.append(jnp.maximum(m[h], jnp.max(sh.reshape(nsub, SUBLANES, LANES), axis=0)))
        return tuple(out)

    m8 = over_tiles(sc_body, tuple(jnp.full((SUBLANES, LANES), -jnp.inf, F32) for _ in range(A_HEADS)))
    mx = [jnp.max(m, axis=0, keepdims=True) for m in m8]

    acc_ref[...] = jnp.zeros_like(acc_ref)

    def pv_body(kt, l):
        out, ps = [], []
        for h in range(A_HEADS):
            p = jnp.exp(s_ref[kt, :, h * LANES:(h + 1) * LANES] - mx[h])
            out.append(l[h] + jnp.sum(p.reshape(nsub, SUBLANES, LANES), axis=0))
            ps.append(p.astype(BF16))
        acc_ref[...] += jnp.dot(ckvt_ref[0, kt], jnp.concatenate(ps, axis=1),
                                preferred_element_type=F32)
        return tuple(out)

    l8 = over_tiles(pv_body, tuple(jnp.zeros((SUBLANES, LANES), F32) for _ in range(A_HEADS)))
    inv = [1.0 / jnp.sum(l, axis=0, keepdims=True) for l in l8]

    for j in range(A_HEADS // 2):
        y = None
        for h in (2 * j, 2 * j + 1):
            o = (acc_ref[:, h * LANES:(h + 1) * LANES] * inv[h]).astype(BF16)
            t = pl.dot(o, wuv_ref[h], trans_a=True)
            y = t if y is None else y + t
        ya_ref[0, :, j * LANES:(j + 1) * LANES] = y.astype(BF16)


def _mixer_a(kidx, ckv, ckvt, qi_r, wit, qlat, wuv_pad, bsz, seq):
    nqb = seq // Q_BLOCK
    nkt = seq // KEY_TILE
    topk = min(IDX_TOPK, seq // 4)
    kern = functools.partial(_mixer_a_kernel, seq=seq, topk=topk)
    return pl.pallas_call(
        kern,
        out_shape=jax.ShapeDtypeStruct((bsz, seq, A_WIDTH), BF16),
        grid=(bsz, nqb),
        in_specs=[
            pl.BlockSpec((1, nkt, KEY_TILE, IDX_DIM), lambda b, i: (b, 0, 0, 0)),
            pl.BlockSpec((1, nkt, KEY_TILE, A_KV_RANK), lambda b, i: (b, 0, 0, 0)),
            pl.BlockSpec((1, nkt, A_KV_RANK, KEY_TILE), lambda b, i: (b, 0, 0, 0)),
            pl.BlockSpec((1, 1, IDX_HEADS * Q_BLOCK, IDX_DIM), lambda b, i: (b, i, 0, 0)),
            pl.BlockSpec((1, IDX_HEADS, Q_BLOCK), lambda b, i: (b, 0, i)),
            pl.BlockSpec((1, Q_BLOCK, A_HEADS * A_KV_RANK), lambda b, i: (b, i, 0)),
            pl.BlockSpec(wuv_pad.shape, lambda b, i: (0, 0, 0)),
        ],
        out_specs=pl.BlockSpec((1, Q_BLOCK, A_WIDTH), lambda b, i: (b, i, 0)),
        scratch_shapes=[
            pltpu.VMEM((nkt, KEY_TILE, LANES), I32),
            pltpu.VMEM((nkt, KEY_TILE, A_HEADS * LANES), F32),
            pltpu.VMEM((A_KV_RANK, A_HEADS * LANES), F32),
        ],
        compiler_params=_params(("parallel", "arbitrary")),
        name="mixer_a",
    )(kidx, ckv, ckvt, qi_r, wit, qlat, wuv_pad)


def _mixer_b_kernel(q_ref, k_ref, v_ref, bias_ref, yb_ref):
    t0 = pl.multiple_of(pl.program_id(1) * B_QTILE, B_QTILE)
    kw = k_ref[0, pl.ds(t0, B_WIN), :]
    vw = v_ref[0, pl.ds(t0, B_WIN), :]
    q = q_ref[0]
    kvalid = (t0 - B_PAD + lax.broadcasted_iota(I32, (1, B_WIN), 1)) >= 0
    lane = lax.broadcasted_iota(I32, (1, LANES), 1)
    scale = B_HEAD_DIM ** -0.5
    for j in range(B_HEADS // 2):
        sl = slice(j * LANES, (j + 1) * LANES)
        qs, ks, vs = q[:, sl], kw[:, sl], vw[:, sl]
        outs = []
        for hh in range(2):
            mine = (lane >= B_HEAD_DIM) if hh else (lane < B_HEAD_DIM)
            qm = jnp.where(mine, qs, jnp.zeros_like(qs))
            s = pl.dot(qm, ks, trans_b=True) * scale + bias_ref[2 * j + hh]
            s = jnp.where(kvalid, s, -jnp.inf)
            m = jnp.max(s, axis=-1, keepdims=True)
            p = jnp.exp(s - m)
            l = jnp.sum(p, axis=-1, keepdims=True)
            outs.append(jnp.dot(p.astype(BF16), vs, preferred_element_type=F32) / l)
        yb_ref[0, :, sl] = jnp.where(lane < B_HEAD_DIM, outs[0], outs[1]).astype(BF16)


def _band_bias(rel_bias):
    tq = np.arange(B_QTILE)[:, None]
    j = np.arange(B_WIN)[None, :]
    band = (j // CHUNK >= tq // CHUNK) & (j // CHUNK <= tq // CHUNK + B_LEFT_CHUNKS)
    m = np.arange(B_QTILE - 1 + B_WIN)
    line = rel_bias[:, np.clip(B_WIN - 1 - m, -REL_CLIP, REL_CLIP) + REL_CLIP].astype(F32)
    table = jnp.stack([line[:, B_QTILE - 1 - t:B_QTILE - 1 - t + B_WIN] for t in range(B_QTILE)], axis=1)
    return jnp.where(jnp.asarray(band)[None], table, -jnp.inf)


def _mixer_b(qb, kb_pad, vb_pad, bias, bsz, seq):
    return pl.pallas_call(
        _mixer_b_kernel,
        out_shape=jax.ShapeDtypeStruct((bsz, seq, B_WIDTH), BF16),
        grid=(bsz, seq // B_QTILE),
        in_specs=[
            pl.BlockSpec((1, B_QTILE, B_WIDTH), lambda b, i: (b, i, 0)),
            pl.BlockSpec((1, seq + B_PAD, B_WIDTH), lambda b, i: (b, 0, 0)),
            pl.BlockSpec((1, seq + B_PAD, B_WIDTH), lambda b, i: (b, 0, 0)),
            pl.BlockSpec(bias.shape, lambda b, i: (0, 0, 0)),
        ],
        out_specs=pl.BlockSpec((1, B_QTILE, B_WIDTH), lambda b, i: (b, i, 0)),
        compiler_params=_params(("parallel", "arbitrary")),
        name="mixer_b",
    )(qb, kb_pad, vb_pad, bias)


def _layer_norm(z, g, b):
    mu = jnp.mean(z, axis=-1, keepdims=True)
    var = jnp.mean(jnp.square(z - mu), axis=-1, keepdims=True)
    return ((z - mu) * lax.rsqrt(var + LN_EPS)) * g + b


def _merge_kernel(x_ref, ya_ref, yb_ref, wg_ref, wba_ref, wbb_ref, wo_ref, g_ref, b_ref,
                  wr_ref, br_ref, x1t_ref, topi_ref, gate_ref):
    tm = x_ref.shape[0]
    x = x_ref[...]
    xb = x.astype(BF16)
    ga = jnp.dot(xb, wg_ref[:, :D_MODEL], preferred_element_type=F32)
    a = jnp.dot(ya_ref[...], wba_ref[...], preferred_element_type=F32)
    merged = jax.nn.sigmoid(ga) * a
    gb = jnp.dot(xb, wg_ref[:, D_MODEL:], preferred_element_type=F32)
    b = jnp.dot(yb_ref[...], wbb_ref[...], preferred_element_type=F32)
    merged = merged + jax.nn.sigmoid(gb) * b
    out = jnp.dot(merged.astype(BF16), wo_ref[...], preferred_element_type=F32)
    x1 = _layer_norm(DN_ALPHA * x + out, g_ref[...], b_ref[...])
    for c in range(ROW_TILES):
        x1t_ref[pl.ds(c, tm, stride=ROW_TILES), :] = x1[:, c * LANES:(c + 1) * LANES]

    lg = pl.dot(wr_ref[...], x1.astype(BF16), trans_b=True) + br_ref[...]
    eidx = lax.broadcasted_iota(I32, lg.shape, 0)
    vals, idxs = [], []
    for _ in range(TOP_K):
        m = jnp.max(lg, axis=0, keepdims=True)
        sel = jnp.min(jnp.where(lg == m, eidx, N_EXPERTS), axis=0, keepdims=True)
        vals.append(m)
        idxs.append(sel)
        lg = jnp.where(eidx == sel, -jnp.inf, lg)
    es = [jnp.exp(v - vals[0]) for v in vals]
    tot = es[0] + es[1] + es[2] + es[3]
    topi_ref[...] = jnp.concatenate(idxs, axis=0)
    gate_ref[...] = jnp.concatenate([e / tot for e in es], axis=0)


def _merge(xf, ya, yb, wg, wba, wbb, wo, g1, b1, wr_t, br):
    n, tm = xf.shape[0], min(MATMUL_TILE, xf.shape[0])
    row = lambda w: pl.BlockSpec((tm, w), lambda i: (i, 0))
    col = pl.BlockSpec((TOP_K, tm), lambda i: (0, i))
    return pl.pallas_call(
        _merge_kernel,
        out_shape=[jax.ShapeDtypeStruct((n * ROW_TILES, LANES), F32),
                   jax.ShapeDtypeStruct((TOP_K, n), I32),
                   jax.ShapeDtypeStruct((TOP_K, n), F32)],
        grid=(n // tm,),
        in_specs=[row(D_MODEL), row(A_WIDTH), row(B_WIDTH), _full(wg), _full(wba), _full(wbb),
                  _full(wo), _full(g1), _full(b1), _full(wr_t), _full(br)],
        out_specs=[pl.BlockSpec((tm * ROW_TILES, LANES), lambda i: (i, 0)), col, col],
        compiler_params=_params(("parallel",)),
        name="merge",
    )(xf, ya, yb, wg, wba, wbb, wo, g1, b1, wr_t, br)


def _one_hot_rows(topi, k, width):
    eidx = lax.broadcasted_iota(I32, (N_EXPERTS, width), 0)
    return eidx == topi[k:k + 1, :]


def _route_kernel(topi_ref, tri_ref, rank_ref, cnt_ref, run_ref):
    @pl.when(pl.program_id(0) == 0)
    def _():
        run_ref[...] = jnp.zeros_like(run_ref)

    topi = topi_ref[...]
    tr = topi.shape[1]
    hot = [_one_hot_rows(topi, k, tr) for k in range(TOP_K)]
    oh = jnp.concatenate([jnp.where(h, 1.0, 0.0).astype(BF16) for h in hot], axis=0)
    prefix = jnp.dot(oh, tri_ref[...], preferred_element_type=F32)
    base = run_ref[...]
    ranks = []
    for k in range(TOP_K):
        tbl = prefix[k * N_EXPERTS:(k + 1) * N_EXPERTS, :] + base
        ranks.append(jnp.sum(jnp.where(hot[k], tbl, 0.0), axis=0, keepdims=True))
        base = base + jnp.sum(jnp.where(hot[k], 1.0, 0.0), axis=1, keepdims=True)
    rank_ref[...] = jnp.concatenate(ranks, axis=0).astype(I32)
    run_ref[...] = base
    cnt_ref[...] = base.astype(I32)


def _route(topi):
    n, tr = topi.shape[1], TOKEN_TILE
    tri = (np.arange(tr)[:, None] < np.arange(tr)[None, :]).astype(np.float32)
    tri = jnp.asarray(tri, BF16)
    col = pl.BlockSpec((TOP_K, tr), lambda i: (0, i))
    return pl.pallas_call(
        _route_kernel,
        out_shape=[jax.ShapeDtypeStruct((TOP_K, n), I32), jax.ShapeDtypeStruct((N_EXPERTS, 1), I32)],
        grid=(n // tr,),
        in_specs=[col, _full(tri)],
        out_specs=[col, pl.BlockSpec((N_EXPERTS, 1), lambda i: (0, 0))],
        scratch_shapes=[pltpu.VMEM((N_EXPERTS, 1), F32)],
        compiler_params=_params(("arbitrary",)),
        name="route",
    )(topi, tri)


def _positions_kernel(topi_ref, rank_ref, pstv_ref, pos_ref):
    topi = topi_ref[...]
    rows = []
    for k in range(TOP_K):
        start = jnp.sum(jnp.where(_one_hot_rows(topi, k, topi.shape[1]), pstv_ref[...], 0),
                        axis=0, keepdims=True)
        rows.append(rank_ref[k:k + 1, :] + start)
    pos_ref[...] = jnp.concatenate(rows, axis=0)


def _positions(topi, rank, pstarts):
    n, td = topi.shape[1], TOKEN_TILE
    col = pl.BlockSpec((TOP_K, td), lambda i: (0, i))
    pstv = pstarts.reshape(N_EXPERTS, 1)
    return pl.pallas_call(
        _positions_kernel,
        out_shape=jax.ShapeDtypeStruct((TOP_K, n), I32),
        grid=(n // td,),
        in_specs=[col, col, _full(pstv)],
        out_specs=col,
        compiler_params=_params(("parallel",)),
        name="positions",
    )(topi, rank, pstv)


def _dispatch_kernel(cnt_ref, pst_ref, pos_hbm, x_hbm, xs_hbm,
                     pos_smem, xbuf, zbuf, sem_p, sem_x, sem_d, sem_z):
    i = pl.program_id(0)
    n = pl.num_programs(0)
    ng = pos_smem.shape[1]
    td = ng * DMA_GROUP
    tile_rows = td * ROW_TILES

    def pos_copy(step, slot):
        return pltpu.make_async_copy(pos_hbm.at[pl.ds(step * ng, ng)], pos_smem.at[slot], sem_p.at[slot])

    def x_copy(step, slot):
        return pltpu.make_async_copy(x_hbm.at[pl.ds(step * tile_rows, tile_rows)], xbuf.at[slot],
                                     sem_x.at[slot])

    @pl.when(i == 0)
    def _():
        pos_copy(0, 0).start()
        x_copy(0, 0).start()

    @pl.when(i + 1 < n)
    def _():
        pos_copy(i + 1, (i + 1) % 2).start()
        x_copy(i + 1, (i + 1) % 3).start()

    slot = i % 2
    xslot = i % 3
    pos_copy(i, slot).wait()
    x_copy(i, xslot).wait()

    def step_bytes(s):
        rows_ = TOP_K * tile_rows
        return pltpu.make_async_copy(xs_hbm.at[pl.ds(0, rows_)], xs_hbm.at[pl.ds(0, rows_)], sem_d.at[s])

    def body(g, c):
        dst = [pos_smem[slot, g, j] * ROW_TILES for j in range(DMA_GROUP * TOP_K)]
        for u in range(DMA_GROUP):
            src = xbuf.at[xslot, pl.ds((g * DMA_GROUP + u) * ROW_TILES, ROW_TILES)]
            for k in range(TOP_K):
                pltpu.make_async_copy(src, xs_hbm.at[pl.ds(dst[u * TOP_K + k], ROW_TILES)],
                                      sem_d.at[xslot]).start(priority=k % 2)
        return c
    lax.fori_loop(0, ng, body, 0)

    @pl.when(i > 0)
    def _():
        step_bytes((i + 2) % 3).wait()

    @pl.when(i == n - 1)
    def _():
        step_bytes(xslot).wait()
        zbuf[...] = jnp.zeros_like(zbuf)

        def zero_copy(r):
            return pltpu.make_async_copy(zbuf, xs_hbm.at[pl.ds(r * ROW_TILES, ROW_TILES)], sem_z)

        def zero_rows(lo, hi):
            def start(r, c2):
                zero_copy(r).start()
                return c2

            def wait(r, c2):
                zero_copy(r).wait()
                return c2
            lax.fori_loop(lo, hi, start, 0)
            lax.fori_loop(lo, hi, wait, 0)

        def padded_end(e):
            return pst_ref[e] + (cnt_ref[e] + MOE_BLOCK - 1) // MOE_BLOCK * MOE_BLOCK

        def ebody(e, c):
            zero_rows(pst_ref[e] + cnt_ref[e], padded_end(e))
            return c
        lax.fori_loop(0, N_EXPERTS, ebody, 0)
        zero_rows(padded_end(N_EXPERTS - 1), xs_hbm.shape[0] // ROW_TILES)


def _dispatch(counts, pstarts, pos_g, x1t, cap):
    n, td = pos_g.shape[0] * DMA_GROUP, TOKEN_TILE
    any_spec = pl.BlockSpec(memory_space=pl.ANY)
    return pl.pallas_call(
        _dispatch_kernel,
        out_shape=jax.ShapeDtypeStruct((cap * ROW_TILES, LANES), F32),
        grid_spec=pltpu.PrefetchScalarGridSpec(
            num_scalar_prefetch=2,
            grid=(n // td,),
            in_specs=[any_spec, any_spec],
            out_specs=any_spec,
            scratch_shapes=[
                pltpu.SMEM((2, td // DMA_GROUP, DMA_GROUP * TOP_K), I32),
                pltpu.VMEM((3, td * ROW_TILES, LANES), F32),
                pltpu.VMEM((ROW_TILES, LANES), F32),
                pltpu.SemaphoreType.DMA((2,)),
                pltpu.SemaphoreType.DMA((3,)),
                pltpu.SemaphoreType.DMA((3,)),
                pltpu.SemaphoreType.DMA,
            ],
        ),
        compiler_params=_params(("arbitrary",)),
        name="dispatch",
    )(counts, pstarts, pos_g, x1t)


def _moe_kernel(be_ref, nu_ref, xs_ref, wg_ref, bg_ref, wu_ref, bu_ref, wd_ref, bd_ref, ys_ref,
                wg_bf, wu_bf, wd_bf):
    i = pl.program_id(0)
    blk = MOE_BLOCK
    used = i < nu_ref[0]

    @pl.when(jnp.logical_or(i == 0, be_ref[i] != be_ref[jnp.maximum(i - 1, 0)]))
    def _():
        wg_bf[...] = wg_ref[0].astype(BF16)
        wu_bf[...] = wu_ref[0].astype(BF16)
        wd_bf[...] = wd_ref[0].astype(BF16)

    @pl.when(used)
    def _():
        xb = jnp.concatenate([xs_ref[pl.ds(c, blk, stride=ROW_TILES), :].astype(BF16)
                              for c in range(ROW_TILES)], axis=1)
        hg = jnp.minimum(jnp.dot(xb, wg_bf[...], preferred_element_type=F32) + bg_ref[0], SWIGLU_LIMIT)
        hu = jnp.clip(jnp.dot(xb, wu_bf[...], preferred_element_type=F32) + bu_ref[0],
                      -SWIGLU_LIMIT, SWIGLU_LIMIT)
        h = (hu + 1.0) * (hg * jax.nn.sigmoid(SWIGLU_ALPHA * hg))
        y = jnp.dot(h.astype(BF16), wd_bf[...], preferred_element_type=F32) + bd_ref[0]
        for c in range(ROW_TILES):
            ys_ref[pl.ds(c, blk, stride=ROW_TILES), :] = y[:, c * LANES:(c + 1) * LANES]

    @pl.when(jnp.logical_not(used))
    def _():
        ys_ref[...] = jnp.zeros_like(ys_ref)


def _moe(blk_expert, nused, xs, wg, bg, wu, bu, wd, bd):
    blk = MOE_BLOCK
    nblk = xs.shape[0] // (blk * ROW_TILES)
    wspec = lambda a: pl.BlockSpec((1,) + a.shape[1:], lambda i, be, nu: (be[i], 0, 0))
    xspec = pl.BlockSpec((blk * ROW_TILES, LANES), lambda i, be, nu: (jnp.minimum(i, nu[0] - 1), 0))
    return pl.pallas_call(
        _moe_kernel,
        out_shape=jax.ShapeDtypeStruct(xs.shape, F32),
        grid_spec=pltpu.PrefetchScalarGridSpec(
            num_scalar_prefetch=2,
            grid=(nblk,),
            in_specs=[xspec, wspec(wg), wspec(bg), wspec(wu), wspec(bu), wspec(wd), wspec(bd)],
            out_specs=pl.BlockSpec((blk * ROW_TILES, LANES), lambda i, be, nu: (i, 0)),
            scratch_shapes=[pltpu.VMEM(wg.shape[1:], BF16), pltpu.VMEM(wu.shape[1:], BF16),
                            pltpu.VMEM(wd.shape[1:], BF16)],
        ),
        compiler_params=_params(("arbitrary",)),
        name="moe",
    )(blk_expert, nused, xs, wg, bg, wu, bu, wd, bd)


def _final_kernel(pos_hbm, gt_ref, x1t_ref, ys_hbm, g_ref, b_ref, o_ref,
                  pos_smem, ybuf, sem_p, sem_g):
    i = pl.program_id(0)
    n = pl.num_programs(0)
    ft = FINAL_TILE
    slot_rows = TOP_K * ft * ROW_TILES

    ng = ft // DMA_GROUP

    def pos_copy(tile, slot):
        return pltpu.make_async_copy(pos_hbm.at[pl.ds(tile * ng, ng)], pos_smem.at[slot], sem_p.at[slot])

    def gather_rows(pslot, yslot):
        def body(g, c):
            src = [pos_smem[pslot, g, j] * ROW_TILES for j in range(DMA_GROUP * TOP_K)]
            base = yslot * slot_rows + g * (DMA_GROUP * ROW_TILES)
            for u in range(DMA_GROUP):
                for k in range(TOP_K):
                    dst = base + (k * ft + u) * ROW_TILES
                    pltpu.make_async_copy(ys_hbm.at[pl.ds(src[u * TOP_K + k], ROW_TILES)],
                                          ybuf.at[pl.ds(dst, ROW_TILES)],
                                          sem_g.at[yslot]).start(priority=k % 2)
            return c
        lax.fori_loop(0, ng, body, 0)

    def gather_wait(yslot):
        pltpu.make_async_copy(ys_hbm.at[pl.ds(0, slot_rows)],
                              ybuf.at[pl.ds(yslot * slot_rows, slot_rows)], sem_g.at[yslot]).wait()

    @pl.when(i == 0)
    def _():
        c = pos_copy(0, 0)
        c.start()
        c.wait()
        gather_rows(0, 0)

        @pl.when(n > 1)
        def _():
            pos_copy(1, 1).start()

    @pl.when(i + 1 < n)
    def _():
        pos_copy(i + 1, (i + 1) % 3).wait()
        gather_rows((i + 1) % 3, (i + 1) % 2)

    @pl.when(i + 2 < n)
    def _():
        pos_copy(i + 2, (i + 2) % 3).start()

    slot = i % 2
    gather_wait(slot)
    g = gt_ref[...]
    zs = []
    for c in range(ROW_TILES):
        f = None
        for k in range(TOP_K):
            yk = ybuf[pl.ds(slot * slot_rows + k * ft * ROW_TILES + c, ft, stride=ROW_TILES), :]
            term = g[:, k:k + 1] * yk
            f = term if f is None else f + term
        zs.append(DN_ALPHA * x1t_ref[pl.ds(c, ft, stride=ROW_TILES), :] + f)
    o_ref[...] = _layer_norm(jnp.concatenate(zs, axis=1), g_ref[...], b_ref[...])


def _final(pos, gates_t, x1t, ys, g2, b2):
    ft = FINAL_TILE
    n = gates_t.shape[0]
    any_spec = pl.BlockSpec(memory_space=pl.ANY)
    return pl.pallas_call(
        _final_kernel,
        out_shape=jax.ShapeDtypeStruct((n, D_MODEL), F32),
        grid=(n // ft,),
        in_specs=[any_spec, pl.BlockSpec((ft, TOP_K), lambda i: (i, 0)),
                  pl.BlockSpec((ft * ROW_TILES, LANES), lambda i: (i, 0)), any_spec,
                  _full(g2), _full(b2)],
        out_specs=pl.BlockSpec((ft, D_MODEL), lambda i: (i, 0)),
        scratch_shapes=[
            pltpu.SMEM((3, ft // DMA_GROUP, DMA_GROUP * TOP_K), I32),
            pltpu.VMEM((2 * TOP_K * ft * ROW_TILES, LANES), F32),
            pltpu.SemaphoreType.DMA((3,)),
            pltpu.SemaphoreType.DMA((2,)),
        ],
        compiler_params=_params(("arbitrary",)),
        name="final",
    )(pos, gates_t, x1t, ys, g2, b2)


def _mix_weight(w_in):
    qa, ckv, qi, ki, wi, qb, kb, vb, _, _ = jnp.split(w_in, np.cumsum(SPLITS)[:-1].tolist(), axis=-1)
    pad = lambda a: jnp.pad(a, ((0, 0), (0, LANES - a.shape[1])))
    return jnp.concatenate([qa, qi, ckv, pad(ki), pad(wi), qb, kb, vb], axis=1).astype(BF16)


def _layer(x, w_in, kv_norm_g, idx_k_norm_g, idx_k_norm_b, w_uk, w_uv, rel_bias, w_branch_a,
           w_branch_b, w_out, ln1_g, ln1_b, w_router, b_router, w_gate, b_gate, w_up, b_up,
           w_down, b_down, ln2_g, ln2_b):
    bsz, seq, _ = x.shape
    n = bsz * seq
    assert seq % KEY_TILE == 0 and n % TOKEN_TILE == 0
    xf = x.reshape(n, D_MODEL)
    row = lambda v: v.reshape(1, -1).astype(F32)

    w_mix = _mix_weight(w_in)
    w_gates = w_in[:, sum(SPLITS[:8]):].astype(BF16)
    wuk_bd = jnp.zeros((A_WIDTH, A_HEADS * A_KV_RANK), F32)
    wuv_pad = jnp.zeros((A_HEADS, A_KV_RANK, LANES), F32)
    for h in range(A_HEADS):
        wuk_bd = wuk_bd.at[h * A_HEAD_DIM:(h + 1) * A_HEAD_DIM, h * A_KV_RANK:(h + 1) * A_KV_RANK].set(w_uk[h])
        c0 = (h % 2) * A_HEAD_DIM
        wuv_pad = wuv_pad.at[h, :, c0:c0 + A_HEAD_DIM].set(w_uv[h])
    wuk_bd, wuv_pad = wuk_bd.astype(BF16), wuv_pad.astype(BF16)

    qlat, qi, ckv, ki, wi, qb, kb, vb = _projection(
        xf, w_mix, wuk_bd, row(kv_norm_g), row(idx_k_norm_g), row(idx_k_norm_b))

    nqb, nkt = seq // Q_BLOCK, seq // KEY_TILE
    kidx = ki.reshape(bsz, nkt, KEY_TILE, IDX_DIM)
    ckv4 = ckv.reshape(bsz, nkt, KEY_TILE, A_KV_RANK)
    ckvt = ckv4.transpose(0, 1, 3, 2)
    qi_r = qi.reshape(bsz, nqb, Q_BLOCK, IDX_HEADS, IDX_DIM).transpose(0, 1, 3, 2, 4).reshape(
        bsz, nqb, IDX_HEADS * Q_BLOCK, IDX_DIM)
    wit = wi.reshape(bsz, seq, IDX_HEADS).transpose(0, 2, 1)
    ya = _mixer_a(kidx, ckv4, ckvt, qi_r, wit, qlat.reshape(bsz, seq, -1), wuv_pad, bsz, seq)

    front = lambda a: jnp.pad(a.reshape(bsz, seq, B_WIDTH), ((0, 0), (B_PAD, 0), (0, 0)))
    yb = _mixer_b(qb.reshape(bsz, seq, B_WIDTH), front(kb), front(vb), _band_bias(rel_bias), bsz, seq)

    x1t, topi, gates = _merge(
        xf, ya.reshape(n, A_WIDTH), yb.reshape(n, B_WIDTH), w_gates, w_branch_a.astype(BF16),
        w_branch_b.astype(BF16), w_out.astype(BF16), row(ln1_g), row(ln1_b),
        w_router.T.astype(BF16), b_router.reshape(-1, 1).astype(F32))

    blk = MOE_BLOCK
    cap = n * TOP_K + N_EXPERTS * blk
    nblk = cap // blk
    rank, cnt = _route(topi)
    counts = cnt[:, 0]
    padded = (counts + blk - 1) // blk * blk
    pends = jnp.cumsum(padded).astype(I32)
    pstarts = pends - padded
    nused = (pends[-1:] // blk).astype(I32)
    blk_start = jnp.arange(nblk, dtype=I32) * blk
    blk_expert = jnp.minimum(jnp.sum(blk_start[:, None] >= pends[None, :], axis=1), N_EXPERTS - 1).astype(I32)

    pos = _positions(topi, rank, pstarts)
    pos_g = pos.T.reshape(n // DMA_GROUP, DMA_GROUP * TOP_K)
    xs = _dispatch(counts, pstarts, pos_g, x1t, cap)
    ys = _moe(blk_expert, nused, xs, w_gate, b_gate[:, None, :], w_up, b_up[:, None, :],
              w_down, b_down[:, None, :])
    out = _final(pos_g, gates.T, x1t, ys, row(ln2_g), row(ln2_b))
    return out.reshape(bsz, seq, D_MODEL)


def kernel(x, w_in, kv_norm_g, idx_k_norm_g, idx_k_norm_b, w_uk, w_uv, rel_bias, w_branch_a,
           w_branch_b, w_out, ln1_g, ln1_b, w_router, b_router, w_gate, b_gate, w_up, b_up,
           w_down, b_down, ln2_g, ln2_b):
    for l in range(DEPTH):
        x = _layer(x, w_in[l], kv_norm_g[l], idx_k_norm_g[l], idx_k_norm_b[l], w_uk[l], w_uv[l],
                   rel_bias[l], w_branch_a[l], w_branch_b[l], w_out[l], ln1_g[l], ln1_b[l],
                   w_router[l], b_router[l], w_gate[l], b_gate[l], w_up[l], b_up[l], w_down[l],
                   b_down[l], ln2_g[l], ln2_b[l])
    return x
```

```python
import functools

import jax
import jax.numpy as jnp
import numpy as np
from jax import lax
from jax.experimental import pallas as pl
from jax.experimental.pallas import tpu as pltpu

F32 = jnp.float32
BF16 = jnp.bfloat16
I32 = jnp.int32

D_MODEL = 1024
CHUNK = 64
Q_BLOCK = 128
A_HEADS = 8
A_HEAD_DIM = 64
A_KV_RANK = 128
IDX_HEADS = 8
IDX_DIM = 64
IDX_TOPK = 256
IDX_SCALE = (IDX_HEADS * IDX_DIM) ** -0.5
B_HEADS = 8
B_HEAD_DIM = 64
B_LEFT_CHUNKS = 8
REL_CLIP = 128
N_EXPERTS = 32
TOP_K = 4
D_FF = 1024
SWIGLU_LIMIT = 7.0
SWIGLU_ALPHA = 1.702
DEPTH = 1
DN_ALPHA = (2.0 * DEPTH) ** 0.25
LN_EPS = 1e-5

A_WIDTH = A_HEADS * A_HEAD_DIM
B_WIDTH = B_HEADS * B_HEAD_DIM
SPLITS = [A_WIDTH, A_KV_RANK, IDX_HEADS * IDX_DIM, IDX_DIM, IDX_HEADS,
          B_WIDTH, B_WIDTH, B_WIDTH, D_MODEL, D_MODEL]

LANES = 128
SUBLANES = 8
KEY_TILE = 256
B_QTILE = 128
B_PAD = B_LEFT_CHUNKS * CHUNK
B_WIN = B_PAD + B_QTILE
TOKEN_TILE = 512
MATMUL_TILE = 1024
MOE_BLOCK = 512
FINAL_TILE = 512
ROW_TILES = D_MODEL // LANES
DMA_GROUP = 4
INT_MIN = -2 ** 31
VMEM_LIMIT = 56 * 1024 * 1024

A_SLOPES = [2.0 ** (-8.0 * (h + 1) / A_HEADS) for h in range(A_HEADS)]

_C_QA = 0
_C_QI = _C_QA + A_WIDTH
_C_CKV = _C_QI + IDX_HEADS * IDX_DIM
_C_KI = _C_CKV + A_KV_RANK
_C_WI = _C_KI + LANES
_C_QKVB = _C_WI + LANES
_C_END = _C_QKVB + 3 * B_WIDTH


def _params(sem, vmem=VMEM_LIMIT):
    return pltpu.CompilerParams(dimension_semantics=sem, vmem_limit_bytes=vmem)


def _full(a):
    return pl.BlockSpec(a.shape, lambda *_: (0,) * a.ndim)


def _proj_kernel(x_ref, w_ref, wuk_ref, kvg_ref, kig_ref, kib_ref,
                 qlat_ref, qi_ref, ckv_ref, ki_ref, wi_ref, qb_ref, kb_ref, vb_ref):
    xb = x_ref[...].astype(BF16)

    def mm(c0, c1):
        return jnp.dot(xb, w_ref[:, c0:c1], preferred_element_type=F32)

    qa = mm(_C_QA, _C_QI)
    qlat = jnp.dot(qa.astype(BF16), wuk_ref[...], preferred_element_type=F32)
    qlat_ref[...] = (qlat * (A_HEAD_DIM ** -0.5)).astype(BF16)
    qi_ref[...] = mm(_C_QI, _C_CKV).astype(BF16)

    ckv = mm(_C_CKV, _C_KI)
    ms = jnp.mean(ckv * ckv, axis=-1, keepdims=True)
    ckv_ref[...] = ((ckv * lax.rsqrt(ms + LN_EPS)) * kvg_ref[...]).astype(BF16)

    ki = mm(_C_KI, _C_WI)[:, :IDX_DIM]
    mu = jnp.mean(ki, axis=-1, keepdims=True)
    var = jnp.mean(jnp.square(ki - mu), axis=-1, keepdims=True)
    ki_ref[...] = (((ki - mu) * lax.rsqrt(var + LN_EPS)) * kig_ref[...] + kib_ref[...]).astype(BF16)

    wi_ref[...] = mm(_C_WI, _C_QKVB)[:, :IDX_HEADS]
    qb_ref[...] = mm(_C_QKVB, _C_QKVB + B_WIDTH).astype(BF16)
    kb_ref[...] = mm(_C_QKVB + B_WIDTH, _C_QKVB + 2 * B_WIDTH).astype(BF16)
    vb_ref[...] = mm(_C_QKVB + 2 * B_WIDTH, _C_END).astype(BF16)


def _projection(xf, w_mix, wuk_bd, kvg, kig, kib):
    n, tm = xf.shape[0], min(MATMUL_TILE, xf.shape[0])
    row = lambda w: pl.BlockSpec((tm, w), lambda i: (i, 0))
    outs = [(A_HEADS * A_KV_RANK, BF16), (IDX_HEADS * IDX_DIM, BF16), (A_KV_RANK, BF16),
            (IDX_DIM, BF16), (IDX_HEADS, F32), (B_WIDTH, BF16), (B_WIDTH, BF16), (B_WIDTH, BF16)]
    return pl.pallas_call(
        _proj_kernel,
        out_shape=[jax.ShapeDtypeStruct((n, w), dt) for w, dt in outs],
        grid=(n // tm,),
        in_specs=[row(D_MODEL), _full(w_mix), _full(wuk_bd), _full(kvg), _full(kig), _full(kib)],
        out_specs=[row(w) for w, _ in outs],
        compiler_params=_params(("parallel",)),
        name="proj",
    )(xf, w_mix, wuk_bd, kvg, kig, kib)


def _mixer_a_kernel(kidx_ref, ckv_ref, ckvt_ref, qi_ref, wit_ref, qlat_ref, wuv_ref,
                    ya_ref, keys_ref, s_ref, acc_ref, *, seq, topk):
    i = pl.program_id(1)
    nkt = i // 2 + 1
    nsub = KEY_TILE // SUBLANES
    lane = lax.broadcasted_iota(I32, (1, LANES), 1)
    qpos = i * Q_BLOCK + lane
    qchunk = qpos >> 6
    row_iota = lax.broadcasted_iota(I32, (KEY_TILE, LANES), 0)

    qi_blk = qi_ref[0, 0]
    w_all = wit_ref[0] * IDX_SCALE

    def idx_body(kt, carry):
        lg = pl.dot(kidx_ref[0, kt], qi_blk, trans_b=True)
        isc = jnp.zeros((KEY_TILE, LANES), F32)
        for h in range(IDX_HEADS):
            isc = isc + jnp.maximum(lg[:, h * LANES:(h + 1) * LANES], 0.0) * w_all[h:h + 1, :]
        bits = lax.bitcast_convert_type(isc + 0.0, I32)
        key = bits ^ ((bits >> 31) & 0x7FFFFFFF)
        spos = kt * KEY_TILE + row_iota
        key = jnp.where((spos >> 6) <= qchunk, key, INT_MIN)
        keys_ref[kt] = key
        return carry

    def over_tiles(body, init):
        def pair(j, c):
            return body(2 * j + 1, body(2 * j, c))
        c = lax.fori_loop(0, nkt // 2, pair, init)
        return lax.cond(nkt % 2 == 1, lambda c: body(nkt - 1, c), lambda c: c, c)

    over_tiles(idx_body, 0)

    def count(pred):
        def body(kt, acc):
            spos = kt * KEY_TILE + row_iota
            m = jnp.where(pred(keys_ref[kt], spos), 1, 0).astype(I32)
            return acc + jnp.sum(m.reshape(nsub, SUBLANES, LANES), axis=0)
        acc = over_tiles(body, jnp.zeros((SUBLANES, LANES), I32))
        return jnp.sum(acc, axis=0, keepdims=True)

    def bit_body(it, carry):
        thr, cge = carry
        cand = thr ^ lax.shift_left(jnp.int32(1), 31 - it)
        cnt = count(lambda kv, spos: kv >= cand)
        ok = cnt >= topk
        return jnp.where(ok, cand, thr), jnp.where(ok, cnt, cge)

    start = (jnp.full((1, LANES), INT_MIN, I32), jnp.zeros((1, LANES), I32))
    thr, cge = lax.cond(nkt * KEY_TILE > topk,
                        lambda: lax.fori_loop(0, 32, bit_body, start), lambda: start)

    nbits = int(np.log2(seq)) + 1
    has_thr = thr > INT_MIN
    need_tie = jnp.max(jnp.where(has_thr, cge, 0)) > topk

    def tie_fn():
        room = topk - count(lambda kv, spos: kv > thr)

        def jb(it, cut):
            cand = cut | lax.shift_left(jnp.int32(1), nbits - 1 - it)
            f = count(lambda kv, spos: (kv == thr) & (spos < cand))
            return jnp.where(f <= room, cand, cut)
        return lax.fori_loop(0, nbits, jb, jnp.zeros((1, LANES), I32))

    cut = lax.cond(need_tie, tie_fn, lambda: jnp.full((1, LANES), 2 * seq, I32))
    cut = jnp.where(has_thr, cut, 0)

    qlat = qlat_ref[0]
    qr = jnp.concatenate([qlat[:, h * LANES:(h + 1) * LANES] for h in range(A_HEADS)], axis=0)

    def sc_body(kt, m):
        st = pl.dot(ckv_ref[0, kt], qr, trans_b=True)
        kv = keys_ref[kt]
        spos = kt * KEY_TILE + row_iota
        sel = (kv > thr) | ((kv == thr) & (spos < cut))
        dist = jnp.abs(qpos - spos).astype(F32)
        out = []
        for h in range(A_HEADS):
            sh = st[:, h * LANES:(h + 1) * LANES] - A_SLOPES[h] * dist
            sh = jnp.where(sel, sh, -jnp.inf)
            s_ref[kt, :, h * LANES:(h + 1) * LANES] = sh
            out.append(jnp.maximum(m[h], jnp.max(sh.reshape(nsub, SUBLANES, LANES), axis=0)))
        return tuple(out)

    m8 = over_tiles(sc_body, tuple(jnp.full((SUBLANES, LANES), -jnp.inf, F32) for _ in range(A_HEADS)))
    mx = [jnp.max(m, axis=0, keepdims=True) for m in m8]

    acc_ref[...] = jnp.zeros_like(acc_ref)

    def pv_body(kt, l):
        out, ps = [], []
        for h in range(A_HEADS):
            p = jnp.exp(s_ref[kt, :, h * LANES:(h + 1) * LANES] - mx[h])
            out.append(l[h] + jnp.sum(p.reshape(nsub, SUBLANES, LANES), axis=0))
            ps.append(p.astype(BF16))
        acc_ref[...] += jnp.dot(ckvt_ref[0, kt], jnp.concatenate(ps, axis=1),
                                preferred_element_type=F32)
        return tuple(out)

    l8 = over_tiles(pv_body, tuple(jnp.zeros((SUBLANES, LANES), F32) for _ in range(A_HEADS)))
    inv = [1.0 / jnp.sum(l, axis=0, keepdims=True) for l in l8]

    for j in range(A_HEADS // 2):
        y = None
        for h in (2 * j, 2 * j + 1):
            o = (acc_ref[:, h * LANES:(h + 1) * LANES] * inv[h]).astype(BF16)
            t = pl.dot(o, wuv_ref[h], trans_a=True)
            y = t if y is None else y + t
        ya_ref[0, :, j * LANES:(j + 1) * LANES] = y.astype(BF16)


def _mixer_a(kidx, ckv, ckvt, qi_r, wit, qlat, wuv_pad, bsz, seq):
    nqb = seq // Q_BLOCK
    nkt = seq // KEY_TILE
    topk = min(IDX_TOPK, seq // 4)
    kern = functools.partial(_mixer_a_kernel, seq=seq, topk=topk)
    return pl.pallas_call(
        kern,
        out_shape=jax.ShapeDtypeStruct((bsz, seq, A_WIDTH), BF16),
        grid=(bsz, nqb),
        in_specs=[
            pl.BlockSpec((1, nkt, KEY_TILE, IDX_DIM), lambda b, i: (b, 0, 0, 0)),
            pl.BlockSpec((1, nkt, KEY_TILE, A_KV_RANK), lambda b, i: (b, 0, 0, 0)),
            pl.BlockSpec((1, nkt, A_KV_RANK, KEY_TILE), lambda b, i: (b, 0, 0, 0)),
            pl.BlockSpec((1, 1, IDX_HEADS * Q_BLOCK, IDX_DIM), lambda b, i: (b, i, 0, 0)),
            pl.BlockSpec((1, IDX_HEADS, Q_BLOCK), lambda b, i: (b, 0, i)),
            pl.BlockSpec((1, Q_BLOCK, A_HEADS * A_KV_RANK), lambda b, i: (b, i, 0)),
            pl.BlockSpec(wuv_pad.shape, lambda b, i: (0, 0, 0)),
        ],
        out_specs=pl.BlockSpec((1, Q_BLOCK, A_WIDTH), lambda b, i: (b, i, 0)),
        scratch_shapes=[
            pltpu.VMEM((nkt, KEY_TILE, LANES), I32),
            pltpu.VMEM((nkt, KEY_TILE, A_HEADS * LANES), F32),
            pltpu.VMEM((A_KV_RANK, A_HEADS * LANES), F32),
        ],
        compiler_params=_params(("parallel", "arbitrary")),
        name="mixer_a",
    )(kidx, ckv, ckvt, qi_r, wit, qlat, wuv_pad)


def _mixer_b_kernel(q_ref, k_ref, v_ref, bias_ref, yb_ref):
    t0 = pl.multiple_of(pl.program_id(1) * B_QTILE, B_QTILE)
    kw = k_ref[0, pl.ds(t0, B_WIN), :]
    vw = v_ref[0, pl.ds(t0, B_WIN), :]
    q = q_ref[0]
    kvalid = (t0 - B_PAD + lax.broadcasted_iota(I32, (1, B_WIN), 1)) >= 0
    lane = lax.broadcasted_iota(I32, (1, LANES), 1)
    scale = B_HEAD_DIM ** -0.5
    for j in range(B_HEADS // 2):
        sl = slice(j * LANES, (j + 1) * LANES)
        qs, ks, vs = q[:, sl], kw[:, sl], vw[:, sl]
        outs = []
        for hh in range(2):
            mine = (lane >= B_HEAD_DIM) if hh else (lane < B_HEAD_DIM)
            qm = jnp.where(mine, qs, jnp.zeros_like(qs))
            s = pl.dot(qm, ks, trans_b=True) * scale + bias_ref[2 * j + hh]
            s = jnp.where(kvalid, s, -jnp.inf)
            m = jnp.max(s, axis=-1, keepdims=True)
            p = jnp.exp(s - m)
            l = jnp.sum(p, axis=-1, keepdims=True)
            outs.append(jnp.dot(p.astype(BF16), vs, preferred_element_type=F32) / l)
        yb_ref[0, :, sl] = jnp.where(lane < B_HEAD_DIM, outs[0], outs[1]).astype(BF16)


def _band_bias(rel_bias):
    tq = np.arange(B_QTILE)[:, None]
    j = np.arange(B_WIN)[None, :]
    band = (j // CHUNK >= tq // CHUNK) & (j // CHUNK <= tq // CHUNK + B_LEFT_CHUNKS)
    m = np.arange(B_QTILE - 1 + B_WIN)
    line = rel_bias[:, np.clip(B_WIN - 1 - m, -REL_CLIP, REL_CLIP) + REL_CLIP].astype(F32)
    table = jnp.stack([line[:, B_QTILE - 1 - t:B_QTILE - 1 - t + B_WIN] for t in range(B_QTILE)], axis=1)
    return jnp.where(jnp.asarray(band)[None], table, -jnp.inf)


def _mixer_b(qb, kb_pad, vb_pad, bias, bsz, seq):
    return pl.pallas_call(
        _mixer_b_kernel,
        out_shape=jax.ShapeDtypeStruct((bsz, seq, B_WIDTH), BF16),
        grid=(bsz, seq // B_QTILE),
        in_specs=[
            pl.BlockSpec((1, B_QTILE, B_WIDTH), lambda b, i: (b, i, 0)),
            pl.BlockSpec((1, seq + B_PAD, B_WIDTH), lambda b, i: (b, 0, 0)),
            pl.BlockSpec((1, seq + B_PAD, B_WIDTH), lambda b, i: (b, 0, 0)),
            pl.BlockSpec(bias.shape, lambda b, i: (0, 0, 0)),
        ],
        out_specs=pl.BlockSpec((1, B_QTILE, B_WIDTH), lambda b, i: (b, i, 0)),
        compiler_params=_params(("parallel", "arbitrary")),
        name="mixer_b",
    )(qb, kb_pad, vb_pad, bias)


def _layer_norm(z, g, b):
    mu = jnp.mean(z, axis=-1, keepdims=True)
    var = jnp.mean(jnp.square(z - mu), axis=-1, keepdims=True)
    return ((z - mu) * lax.rsqrt(var + LN_EPS)) * g + b


def _merge_kernel(x_ref, ya_ref, yb_ref, wg_ref, wba_ref, wbb_ref, wo_ref, g_ref, b_ref,
                  wr_ref, br_ref, x1t_ref, topi_ref, gate_ref):
    tm = x_ref.shape[0]
    x = x_ref[...]
    xb = x.astype(BF16)
    ga = jnp.dot(xb, wg_ref[:, :D_MODEL], preferred_element_type=F32)
    a = jnp.dot(ya_ref[...], wba_ref[...], preferred_element_type=F32)
    merged = jax.nn.sigmoid(ga) * a
    gb = jnp.dot(xb, wg_ref[:, D_MODEL:], preferred_element_type=F32)
    b = jnp.dot(yb_ref[...], wbb_ref[...], preferred_element_type=F32)
    merged = merged + jax.nn.sigmoid(gb) * b
    out = jnp.dot(merged.astype(BF16), wo_ref[...], preferred_element_type=F32)
    x1 = _layer_norm(DN_ALPHA * x + out, g_ref[...], b_ref[...])
    for c in range(ROW_TILES):
        x1t_ref[pl.ds(c, tm, stride=ROW_TILES), :] = x1[:, c * LANES:(c + 1) * LANES]

    lg = pl.dot(wr_ref[...], x1.astype(BF16), trans_b=True) + br_ref[...]
    eidx = lax.broadcasted_iota(I32, lg.shape, 0)
    vals, idxs = [], []
    for _ in range(TOP_K):
        m = jnp.max(lg, axis=0, keepdims=True)
        sel = jnp.min(jnp.where(lg == m, eidx, N_EXPERTS), axis=0, keepdims=True)
        vals.append(m)
        idxs.append(sel)
        lg = jnp.where(eidx == sel, -jnp.inf, lg)
    es = [jnp.exp(v - vals[0]) for v in vals]
    tot = es[0] + es[1] + es[2] + es[3]
    topi_ref[...] = jnp.concatenate(idxs, axis=0)
    gate_ref[...] = jnp.concatenate([e / tot for e in es], axis=0)


def _merge(xf, ya, yb, wg, wba, wbb, wo, g1, b1, wr_t, br):
    n, tm = xf.shape[0], min(MATMUL_TILE, xf.shape[0])
    row = lambda w: pl.BlockSpec((tm, w), lambda i: (i, 0))
    col = pl.BlockSpec((TOP_K, tm), lambda i: (0, i))
    return pl.pallas_call(
        _merge_kernel,
        out_shape=[jax.ShapeDtypeStruct((n * ROW_TILES, LANES), F32),
                   jax.ShapeDtypeStruct((TOP_K, n), I32),
                   jax.ShapeDtypeStruct((TOP_K, n), F32)],
        grid=(n // tm,),
        in_specs=[row(D_MODEL), row(A_WIDTH), row(B_WIDTH), _full(wg), _full(wba), _full(wbb),
                  _full(wo), _full(g1), _full(b1), _full(wr_t), _full(br)],
        out_specs=[pl.BlockSpec((tm * ROW_TILES, LANES), lambda i: (i, 0)), col, col],
        compiler_params=_params(("parallel",)),
        name="merge",
    )(xf, ya, yb, wg, wba, wbb, wo, g1, b1, wr_t, br)


def _one_hot_rows(topi, k, width):
    eidx = lax.broadcasted_iota(I32, (N_EXPERTS, width), 0)
    return eidx == topi[k:k + 1, :]


def _route_kernel(topi_ref, tri_ref, rank_ref, cnt_ref, run_ref):
    @pl.when(pl.program_id(0) == 0)
    def _():
        run_ref[...] = jnp.zeros_like(run_ref)

    topi = topi_ref[...]
    tr = topi.shape[1]
    hot = [_one_hot_rows(topi, k, tr) for k in range(TOP_K)]
    oh = jnp.concatenate([jnp.where(h, 1.0, 0.0).astype(BF16) for h in hot], axis=0)
    prefix = jnp.dot(oh, tri_ref[...], preferred_element_type=F32)
    base = run_ref[...]
    ranks = []
    for k in range(TOP_K):
        tbl = prefix[k * N_EXPERTS:(k + 1) * N_EXPERTS, :] + base
        ranks.append(jnp.sum(jnp.where(hot[k], tbl, 0.0), axis=0, keepdims=True))
        base = base + jnp.sum(jnp.where(hot[k], 1.0, 0.0), axis=1, keepdims=True)
    rank_ref[...] = jnp.concatenate(ranks, axis=0).astype(I32)
    run_ref[...] = base
    cnt_ref[...] = base.astype(I32)


def _route(topi):
    n, tr = topi.shape[1], TOKEN_TILE
    tri = (np.arange(tr)[:, None] < np.arange(tr)[None, :]).astype(np.float32)
    tri = jnp.asarray(tri, BF16)
    col = pl.BlockSpec((TOP_K, tr), lambda i: (0, i))
    return pl.pallas_call(
        _route_kernel,
        out_shape=[jax.ShapeDtypeStruct((TOP_K, n), I32), jax.ShapeDtypeStruct((N_EXPERTS, 1), I32)],
        grid=(n // tr,),
        in_specs=[col, _full(tri)],
        out_specs=[col, pl.BlockSpec((N_EXPERTS, 1), lambda i: (0, 0))],
        scratch_shapes=[pltpu.VMEM((N_EXPERTS, 1), F32)],
        compiler_params=_params(("arbitrary",)),
        name="route",
    )(topi, tri)


def _positions_kernel(topi_ref, rank_ref, pstv_ref, pos_ref):
    topi = topi_ref[...]
    rows = []
    for k in range(TOP_K):
        start = jnp.sum(jnp.where(_one_hot_rows(topi, k, topi.shape[1]), pstv_ref[...], 0),
                        axis=0, keepdims=True)
        rows.append(rank_ref[k:k + 1, :] + start)
    pos_ref[...] = jnp.concatenate(rows, axis=0)


def _positions(topi, rank, pstarts):
    n, td = topi.shape[1], TOKEN_TILE
    col = pl.BlockSpec((TOP_K, td), lambda i: (0, i))
    pstv = pstarts.reshape(N_EXPERTS, 1)
    return pl.pallas_call(
        _positions_kernel,
        out_shape=jax.ShapeDtypeStruct((TOP_K, n), I32),
        grid=(n // td,),
        in_specs=[col, col, _full(pstv)],
        out_specs=col,
        compiler_params=_params(("parallel",)),
        name="positions",
    )(topi, rank, pstv)


def _dispatch_kernel(cnt_ref, pst_ref, pos_hbm, x_hbm, xs_hbm,
                     pos_smem, xbuf, zbuf, sem_p, sem_x, sem_d, sem_z):
    i = pl.program_id(0)
    n = pl.num_programs(0)
    ng = pos_smem.shape[1]
    td = ng * DMA_GROUP
    tile_rows = td * ROW_TILES

    def pos_copy(step, slot):
        return pltpu.make_async_copy(pos_hbm.at[pl.ds(step * ng, ng)], pos_smem.at[slot], sem_p.at[slot])

    def x_copy(step, slot):
        return pltpu.make_async_copy(x_hbm.at[pl.ds(step * tile_rows, tile_rows)], xbuf.at[slot],
                                     sem_x.at[slot])

    @pl.when(i == 0)
    def _():
        pos_copy(0, 0).start()
        x_copy(0, 0).start()

    @pl.when(i + 1 < n)
    def _():
        pos_copy(i + 1, (i + 1) % 2).start()
        x_copy(i + 1, (i + 1) % 3).start()

    slot = i % 2
    xslot = i % 3
    pos_copy(i, slot).wait()
    x_copy(i, xslot).wait()

    def step_bytes(s):
        rows_ = TOP_K * tile_rows
        return pltpu.make_async_copy(xs_hbm.at[pl.ds(0, rows_)], xs_hbm.at[pl.ds(0, rows_)], sem_d.at[s])

    def body(g, c):
        dst = [pos_smem[slot, g, j] * ROW_TILES for j in range(DMA_GROUP * TOP_K)]
        for u in range(DMA_GROUP):
            src = xbuf.at[xslot, pl.ds((g * DMA_GROUP + u) * ROW_TILES, ROW_TILES)]
            for k in range(TOP_K):
                pltpu.make_async_copy(src, xs_hbm.at[pl.ds(dst[u * TOP_K + k], ROW_TILES)],
                                      sem_d.at[xslot]).start(priority=k % 2)
        return c
    lax.fori_loop(0, ng, body, 0)

    @pl.when(i > 0)
    def _():
        step_bytes((i + 2) % 3).wait()

    @pl.when(i == n - 1)
    def _():
        step_bytes(xslot).wait()
        zbuf[...] = jnp.zeros_like(zbuf)

        def zero_copy(r):
            return pltpu.make_async_copy(zbuf, xs_hbm.at[pl.ds(r * ROW_TILES, ROW_TILES)], sem_z)

        def zero_rows(lo, hi):
            def start(r, c2):
                zero_copy(r).start()
                return c2

            def wait(r, c2):
                zero_copy(r).wait()
                return c2
            lax.fori_loop(lo, hi, start, 0)
            lax.fori_loop(lo, hi, wait, 0)

        def padded_end(e):
            return pst_ref[e] + (cnt_ref[e] + MOE_BLOCK - 1) // MOE_BLOCK * MOE_BLOCK

        def ebody(e, c):
            zero_rows(pst_ref[e] + cnt_ref[e], padded_end(e))
            return c
        lax.fori_loop(0, N_EXPERTS, ebody, 0)
        zero_rows(padded_end(N_EXPERTS - 1), xs_hbm.shape[0] // ROW_TILES)


def _dispatch(counts, pstarts, pos_g, x1t, cap):
    n, td = pos_g.shape[0] * DMA_GROUP, TOKEN_TILE
    any_spec = pl.BlockSpec(memory_space=pl.ANY)
    return pl.pallas_call(
        _dispatch_kernel,
        out_shape=jax.ShapeDtypeStruct((cap * ROW_TILES, LANES), F32),
        grid_spec=pltpu.PrefetchScalarGridSpec(
            num_scalar_prefetch=2,
            grid=(n // td,),
            in_specs=[any_spec, any_spec],
            out_specs=any_spec,
            scratch_shapes=[
                pltpu.SMEM((2, td // DMA_GROUP, DMA_GROUP * TOP_K), I32),
                pltpu.VMEM((3, td * ROW_TILES, LANES), F32),
                pltpu.VMEM((ROW_TILES, LANES), F32),
                pltpu.SemaphoreType.DMA((2,)),
                pltpu.SemaphoreType.DMA((3,)),
                pltpu.SemaphoreType.DMA((3,)),
                pltpu.SemaphoreType.DMA,
            ],
        ),
        compiler_params=_params(("arbitrary",)),
        name="dispatch",
    )(counts, pstarts, pos_g, x1t)


def _moe_kernel(be_ref, nu_ref, xs_ref, wg_ref, bg_ref, wu_ref, bu_ref, wd_ref, bd_ref, ys_ref,
                wg_bf, wu_bf, wd_bf):
    i = pl.program_id(0)
    blk = MOE_BLOCK
    used = i < nu_ref[0]

    @pl.when(jnp.logical_or(i == 0, be_ref[i] != be_ref[jnp.maximum(i - 1, 0)]))
    def _():
        wg_bf[...] = wg_ref[0].astype(BF16)
        wu_bf[...] = wu_ref[0].astype(BF16)
        wd_bf[...] = wd_ref[0].astype(BF16)

    @pl.when(used)
    def _():
        xb = jnp.concatenate([xs_ref[pl.ds(c, blk, stride=ROW_TILES), :].astype(BF16)
                              for c in range(ROW_TILES)], axis=1)
        hg = jnp.minimum(jnp.dot(xb, wg_bf[...], preferred_element_type=F32) + bg_ref[0], SWIGLU_LIMIT)
        hu = jnp.clip(jnp.dot(xb, wu_bf[...], preferred_element_type=F32) + bu_ref[0],
                      -SWIGLU_LIMIT, SWIGLU_LIMIT)
        h = (hu + 1.0) * (hg * jax.nn.sigmoid(SWIGLU_ALPHA * hg))
        y = jnp.dot(h.astype(BF16), wd_bf[...], preferred_element_type=F32) + bd_ref[0]
        for c in range(ROW_TILES):
            ys_ref[pl.ds(c, blk, stride=ROW_TILES), :] = y[:, c * LANES:(c + 1) * LANES]

    @pl.when(jnp.logical_not(used))
    def _():
        ys_ref[...] = jnp.zeros_like(ys_ref)


def _moe(blk_expert, nused, xs, wg, bg, wu, bu, wd, bd):
    blk = MOE_BLOCK
    nblk = xs.shape[0] // (blk * ROW_TILES)
    wspec = lambda a: pl.BlockSpec((1,) + a.shape[1:], lambda i, be, nu: (be[i], 0, 0))
    xspec = pl.BlockSpec((blk * ROW_TILES, LANES), lambda i, be, nu: (jnp.minimum(i, nu[0] - 1), 0))
    return pl.pallas_call(
        _moe_kernel,
        out_shape=jax.ShapeDtypeStruct(xs.shape, F32),
        grid_spec=pltpu.PrefetchScalarGridSpec(
            num_scalar_prefetch=2,
            grid=(nblk,),
            in_specs=[xspec, wspec(wg), wspec(bg), wspec(wu), wspec(bu), wspec(wd), wspec(bd)],
            out_specs=pl.BlockSpec((blk * ROW_TILES, LANES), lambda i, be, nu: (i, 0)),
            scratch_shapes=[pltpu.VMEM(wg.shape[1:], BF16), pltpu.VMEM(wu.shape[1:], BF16),
                            pltpu.VMEM(wd.shape[1:], BF16)],
        ),
        compiler_params=_params(("arbitrary",)),
        name="moe",
    )(blk_expert, nused, xs, wg, bg, wu, bu, wd, bd)


def _final_kernel(pos_hbm, gt_ref, x1t_ref, ys_hbm, g_ref, b_ref, o_ref,
                  pos_smem, ybuf, sem_p, sem_g):
    i = pl.program_id(0)
    n = pl.num_programs(0)
    ft = FINAL_TILE
    slot_rows = TOP_K * ft * ROW_TILES

    ng = ft // DMA_GROUP

    def pos_copy(tile, slot):
        return pltpu.make_async_copy(pos_hbm.at[pl.ds(tile * ng, ng)], pos_smem.at[slot], sem_p.at[slot])

    def gather_rows(pslot, yslot):
        def body(g, c):
            src = [pos_smem[pslot, g, j] * ROW_TILES for j in range(DMA_GROUP * TOP_K)]
            base = yslot * slot_rows + g * (DMA_GROUP * ROW_TILES)
            for u in range(DMA_GROUP):
                for k in range(TOP_K):
                    dst = base + (k * ft + u) * ROW_TILES
                    pltpu.make_async_copy(ys_hbm.at[pl.ds(src[u * TOP_K + k], ROW_TILES)],
                                          ybuf.at[pl.ds(dst, ROW_TILES)],
                                          sem_g.at[yslot]).start(priority=k % 2)
            return c
        lax.fori_loop(0, ng, body, 0)

    def gather_wait(yslot):
        pltpu.make_async_copy(ys_hbm.at[pl.ds(0, slot_rows)],
                              ybuf.at[pl.ds(yslot * slot_rows, slot_rows)], sem_g.at[yslot]).wait()

    @pl.when(i == 0)
    def _():
        c = pos_copy(0, 0)
        c.start()
        c.wait()
        gather_rows(0, 0)

        @pl.when(n > 1)
        def _():
            pos_copy(1, 1).start()

    @pl.when(i + 1 < n)
    def _():
        pos_copy(i + 1, (i + 1) % 3).wait()
        gather_rows((i + 1) % 3, (i + 1) % 2)

    @pl.when(i + 2 < n)
    def _():
        pos_copy(i + 2, (i + 2) % 3).start()

    slot = i % 2
    gather_wait(slot)
    g = gt_ref[...]
    zs = []
    for c in range(ROW_TILES):
        f = None
        for k in range(TOP_K):
            yk = ybuf[pl.ds(slot * slot_rows + k * ft * ROW_TILES + c, ft, stride=ROW_TILES), :]
            term = g[:, k:k + 1] * yk
            f = term if f is None else f + term
        zs.append(DN_ALPHA * x1t_ref[pl.ds(c, ft, stride=ROW_TILES), :] + f)
    o_ref[...] = _layer_norm(jnp.concatenate(zs, axis=1), g_ref[...], b_ref[...])


def _final(pos, gates_t, x1t, ys, g2, b2):
    ft = FINAL_TILE
    n = gates_t.shape[0]
    any_spec = pl.BlockSpec(memory_space=pl.ANY)
    return pl.pallas_call(
        _final_kernel,
        out_shape=jax.ShapeDtypeStruct((n, D_MODEL), F32),
        grid=(n // ft,),
        in_specs=[any_spec, pl.BlockSpec((ft, TOP_K), lambda i: (i, 0)),
                  pl.BlockSpec((ft * ROW_TILES, LANES), lambda i: (i, 0)), any_spec,
                  _full(g2), _full(b2)],
        out_specs=pl.BlockSpec((ft, D_MODEL), lambda i: (i, 0)),
        scratch_shapes=[
            pltpu.SMEM((3, ft // DMA_GROUP, DMA_GROUP * TOP_K), I32),
            pltpu.VMEM((2 * TOP_K * ft * ROW_TILES, LANES), F32),
            pltpu.SemaphoreType.DMA((3,)),
            pltpu.SemaphoreType.DMA((2,)),
        ],
        compiler_params=_params(("arbitrary",)),
        name="final",
    )(pos, gates_t, x1t, ys, g2, b2)


def _mix_weight(w_in):
    qa, ckv, qi, ki, wi, qb, kb, vb, _, _ = jnp.split(w_in, np.cumsum(SPLITS)[:-1].tolist(), axis=-1)
    pad = lambda a: jnp.pad(a, ((0, 0), (0, LANES - a.shape[1])))
    return jnp.concatenate([qa, qi, ckv, pad(ki), pad(wi), qb, kb, vb], axis=1).astype(BF16)


def _layer(x, w_in, kv_norm_g, idx_k_norm_g, idx_k_norm_b, w_uk, w_uv, rel_bias, w_branch_a,
           w_branch_b, w_out, ln1_g, ln1_b, w_router, b_router, w_gate, b_gate, w_up, b_up,
           w_down, b_down, ln2_g, ln2_b):
    bsz, seq, _ = x.shape
    n = bsz * seq
    assert seq % KEY_TILE == 0 and n % TOKEN_TILE == 0
    xf = x.reshape(n, D_MODEL)
    row = lambda v: v.reshape(1, -1).astype(F32)

    w_mix = _mix_weight(w_in)
    w_gates = w_in[:, sum(SPLITS[:8]):].astype(BF16)
    wuk_bd = jnp.zeros((A_WIDTH, A_HEADS * A_KV_RANK), F32)
    wuv_pad = jnp.zeros((A_HEADS, A_KV_RANK, LANES), F32)
    for h in range(A_HEADS):
        wuk_bd = wuk_bd.at[h * A_HEAD_DIM:(h + 1) * A_HEAD_DIM, h * A_KV_RANK:(h + 1) * A_KV_RANK].set(w_uk[h])
        c0 = (h % 2) * A_HEAD_DIM
        wuv_pad = wuv_pad.at[h, :, c0:c0 + A_HEAD_DIM].set(w_uv[h])
    wuk_bd, wuv_pad = wuk_bd.astype(BF16), wuv_pad.astype(BF16)

    qlat, qi, ckv, ki, wi, qb, kb, vb = _projection(
        xf, w_mix, wuk_bd, row(kv_norm_g), row(idx_k_norm_g), row(idx_k_norm_b))

    nqb, nkt = seq // Q_BLOCK, seq // KEY_TILE
    kidx = ki.reshape(bsz, nkt, KEY_TILE, IDX_DIM)
    ckv4 = ckv.reshape(bsz, nkt, KEY_TILE, A_KV_RANK)
    ckvt = ckv4.transpose(0, 1, 3, 2)
    qi_r = qi.reshape(bsz, nqb, Q_BLOCK, IDX_HEADS, IDX_DIM).transpose(0, 1, 3, 2, 4).reshape(
        bsz, nqb, IDX_HEADS * Q_BLOCK, IDX_DIM)
    wit = wi.reshape(bsz, seq, IDX_HEADS).transpose(0, 2, 1)
    ya = _mixer_a(kidx, ckv4, ckvt, qi_r, wit, qlat.reshape(bsz, seq, -1), wuv_pad, bsz, seq)

    front = lambda a: jnp.pad(a.reshape(bsz, seq, B_WIDTH), ((0, 0), (B_PAD, 0), (0, 0)))
    yb = _mixer_b(qb.reshape(bsz, seq, B_WIDTH), front(kb), front(vb), _band_bias(rel_bias), bsz, seq)

    x1t, topi, gates = _merge(
        xf, ya.reshape(n, A_WIDTH), yb.reshape(n, B_WIDTH), w_gates, w_branch_a.astype(BF16),
        w_branch_b.astype(BF16), w_out.astype(BF16), row(ln1_g), row(ln1_b),
        w_router.T.astype(BF16), b_router.reshape(-1, 1).astype(F32))

    blk = MOE_BLOCK
    cap = n * TOP_K + N_EXPERTS * blk
    nblk = cap // blk
    rank, cnt = _route(topi)
    counts = cnt[:, 0]
    padded = (counts + blk - 1) // blk * blk
    pends = jnp.cumsum(padded).astype(I32)
    pstarts = pends - padded
    nused = (pends[-1:] // blk).astype(I32)
    blk_start = jnp.arange(nblk, dtype=I32) * blk
    blk_expert = jnp.minimum(jnp.sum(blk_start[:, None] >= pends[None, :], axis=1), N_EXPERTS - 1).astype(I32)

    pos = _positions(topi, rank, pstarts)
    pos_g = pos.T.reshape(n // DMA_GROUP, DMA_GROUP * TOP_K)
    xs = _dispatch(counts, pstarts, pos_g, x1t, cap)
    ys = _moe(blk_expert, nused, xs, w_gate, b_gate[:, None, :], w_up, b_up[:, None, :],
              w_down, b_down[:, None, :])
    out = _final(pos_g, gates.T, x1t, ys, row(ln2_g), row(ln2_b))
    return out.reshape(bsz, seq, D_MODEL)


def kernel(x, w_in, kv_norm_g, idx_k_norm_g, idx_k_norm_b, w_uk, w_uv, rel_bias, w_branch_a,
           w_branch_b, w_out, ln1_g, ln1_b, w_router, b_router, w_gate, b_gate, w_up, b_up,
           w_down, b_down, ln2_g, ln2_b):
    for l in range(DEPTH):
        x = _layer(x, w_in[l], kv_norm_g[l], idx_k_norm_g[l], idx_k_norm_b[l], w_uk[l], w_uv[l],
                   rel_bias[l], w_branch_a[l], w_branch_b[l], w_out[l], ln1_g[l], ln1_b[l],
                   w_router[l], b_router[l], w_gate[l], b_gate[l], w_up[l], b_up[l], w_down[l],
                   b_down[l], ln2_g[l], ln2_b[l])
    return x
```

```python
import functools

import jax
import jax.numpy as jnp
import numpy as np
from jax import lax
from jax.experimental import pallas as pl
from jax.experimental.pallas import tpu as pltpu

F32 = jnp.float32
BF16 = jnp.bfloat16
I32 = jnp.int32

D_MODEL = 1024
CHUNK = 64
Q_BLOCK = 128
A_HEADS = 8
A_HEAD_DIM = 64
A_KV_RANK = 128
IDX_HEADS = 8
IDX_DIM = 64
IDX_TOPK = 256
IDX_SCALE = (IDX_HEADS * IDX_DIM) ** -0.5
B_HEADS = 8
B_HEAD_DIM = 64
B_LEFT_CHUNKS = 8
REL_CLIP = 128
N_EXPERTS = 32
TOP_K = 4
D_FF = 1024
SWIGLU_LIMIT = 7.0
SWIGLU_ALPHA = 1.702
DEPTH = 1
DN_ALPHA = (2.0 * DEPTH) ** 0.25
LN_EPS = 1e-5

A_WIDTH = A_HEADS * A_HEAD_DIM
B_WIDTH = B_HEADS * B_HEAD_DIM
SPLITS = [A_WIDTH, A_KV_RANK, IDX_HEADS * IDX_DIM, IDX_DIM, IDX_HEADS,
          B_WIDTH, B_WIDTH, B_WIDTH, D_MODEL, D_MODEL]

LANES = 128
SUBLANES = 8
KEY_TILE = 256
B_QTILE = 512
B_PAD = B_LEFT_CHUNKS * CHUNK
B_WIN = B_PAD + B_QTILE
TOKEN_TILE = 512
MATMUL_TILE = 1024
MOE_BLOCK = 512
FINAL_TILE = 512
ROW_TILES = D_MODEL // LANES
DMA_GROUP = 4
INT_MIN = -2 ** 31
VMEM_LIMIT = 56 * 1024 * 1024

A_SLOPES = [2.0 ** (-8.0 * (h + 1) / A_HEADS) for h in range(A_HEADS)]

_C_QA = 0
_C_QI = _C_QA + A_WIDTH
_C_CKV = _C_QI + IDX_HEADS * IDX_DIM
_C_KI = _C_CKV + A_KV_RANK
_C_WI = _C_KI + LANES
_C_QKVB = _C_WI + LANES
_C_END = _C_QKVB + 3 * B_WIDTH


def _params(sem, vmem=VMEM_LIMIT):
    return pltpu.CompilerParams(dimension_semantics=sem, vmem_limit_bytes=vmem)


def _full(a):
    return pl.BlockSpec(a.shape, lambda *_: (0,) * a.ndim)


def _proj_kernel(x_ref, w_ref, wuk_ref, kvg_ref, kig_ref, kib_ref,
                 qlat_ref, qi_ref, ckv_ref, ki_ref, wi_ref, qb_ref, kb_ref, vb_ref):
    xb = x_ref[...].astype(BF16)

    def mm(c0, c1):
        return jnp.dot(xb, w_ref[:, c0:c1], preferred_element_type=F32)

    qa = mm(_C_QA, _C_QI)
    qlat = jnp.dot(qa.astype(BF16), wuk_ref[...], preferred_element_type=F32)
    qlat_ref[...] = (qlat * (A_HEAD_DIM ** -0.5)).astype(BF16)
    qi_ref[...] = mm(_C_QI, _C_CKV).astype(BF16)

    ckv = mm(_C_CKV, _C_KI)
    ms = jnp.mean(ckv * ckv, axis=-1, keepdims=True)
    ckv_ref[...] = ((ckv * lax.rsqrt(ms + LN_EPS)) * kvg_ref[...]).astype(BF16)

    ki = mm(_C_KI, _C_WI)[:, :IDX_DIM]
    mu = jnp.mean(ki, axis=-1, keepdims=True)
    var = jnp.mean(jnp.square(ki - mu), axis=-1, keepdims=True)
    ki_ref[...] = (((ki - mu) * lax.rsqrt(var + LN_EPS)) * kig_ref[...] + kib_ref[...]).astype(BF16)

    wi_ref[...] = mm(_C_WI, _C_QKVB)[:, :IDX_HEADS]
    qb_ref[...] = mm(_C_QKVB, _C_QKVB + B_WIDTH).astype(BF16)
    kb_ref[...] = mm(_C_QKVB + B_WIDTH, _C_QKVB + 2 * B_WIDTH).astype(BF16)
    vb_ref[...] = mm(_C_QKVB + 2 * B_WIDTH, _C_END).astype(BF16)


def _projection(xf, w_mix, wuk_bd, kvg, kig, kib):
    n, tm = xf.shape[0], min(MATMUL_TILE, xf.shape[0])
    row = lambda w: pl.BlockSpec((tm, w), lambda i: (i, 0))
    outs = [(A_HEADS * A_KV_RANK, BF16), (IDX_HEADS * IDX_DIM, BF16), (A_KV_RANK, BF16),
            (IDX_DIM, BF16), (IDX_HEADS, F32), (B_WIDTH, BF16), (B_WIDTH, BF16), (B_WIDTH, BF16)]
    return pl.pallas_call(
        _proj_kernel,
        out_shape=[jax.ShapeDtypeStruct((n, w), dt) for w, dt in outs],
        grid=(n // tm,),
        in_specs=[row(D_MODEL), _full(w_mix), _full(wuk_bd), _full(kvg), _full(kig), _full(kib)],
        out_specs=[row(w) for w, _ in outs],
        compiler_params=_params(("parallel",)),
        name="proj",
    )(xf, w_mix, wuk_bd, kvg, kig, kib)


def _mixer_a_kernel(kidx_ref, ckv_ref, ckvt_ref, qi_ref, wit_ref, qlat_ref, wuv_ref,
                    ya_ref, keys_ref, s_ref, acc_ref, *, seq, topk):
    i = pl.program_id(1)
    nkt = i // 2 + 1
    nsub = KEY_TILE // SUBLANES
    lane = lax.broadcasted_iota(I32, (1, LANES), 1)
    qpos = i * Q_BLOCK + lane
    qchunk = qpos >> 6
    row_iota = lax.broadcasted_iota(I32, (KEY_TILE, LANES), 0)

    qi_blk = qi_ref[0, 0]
    w_all = wit_ref[0] * IDX_SCALE

    def idx_body(kt, carry):
        lg = pl.dot(kidx_ref[0, kt], qi_blk, trans_b=True)
        isc = jnp.zeros((KEY_TILE, LANES), F32)
        for h in range(IDX_HEADS):
            isc = isc + jnp.maximum(lg[:, h * LANES:(h + 1) * LANES], 0.0) * w_all[h:h + 1, :]
        bits = lax.bitcast_convert_type(isc + 0.0, I32)
        key = bits ^ ((bits >> 31) & 0x7FFFFFFF)
        spos = kt * KEY_TILE + row_iota
        key = jnp.where((spos >> 6) <= qchunk, key, INT_MIN)
        keys_ref[kt] = key
        return carry

    def over_tiles(body, init):
        def pair(j, c):
            return body(2 * j + 1, body(2 * j, c))
        c = lax.fori_loop(0, nkt // 2, pair, init)
        return lax.cond(nkt % 2 == 1, lambda c: body(nkt - 1, c), lambda c: c, c)

    over_tiles(idx_body, 0)

    def count(pred):
        def body(kt, acc):
            spos = kt * KEY_TILE + row_iota
            m = jnp.where(pred(keys_ref[kt], spos), 1, 0).astype(I32)
            return acc + jnp.sum(m.reshape(nsub, SUBLANES, LANES), axis=0)
        acc = over_tiles(body, jnp.zeros((SUBLANES, LANES), I32))
        return jnp.sum(acc, axis=0, keepdims=True)

    def bit_body(it, carry):
        thr, cge = carry
        cand = thr ^ lax.shift_left(jnp.int32(1), 31 - it)
        cnt = count(lambda kv, spos: kv >= cand)
        ok = cnt >= topk
        return jnp.where(ok, cand, thr), jnp.where(ok, cnt, cge)

    start = (jnp.full((1, LANES), INT_MIN, I32), jnp.zeros((1, LANES), I32))
    thr, cge = lax.cond(nkt * KEY_TILE > topk,
                        lambda: lax.fori_loop(0, 32, bit_body, start), lambda: start)

    nbits = int(np.log2(seq)) + 1
    has_thr = thr > INT_MIN
    need_tie = jnp.max(jnp.where(has_thr, cge, 0)) > topk

    def tie_fn():
        room = topk - count(lambda kv, spos: kv > thr)

        def jb(it, cut):
            cand = cut | lax.shift_left(jnp.int32(1), nbits - 1 - it)
            f = count(lambda kv, spos: (kv == thr) & (spos < cand))
            return jnp.where(f <= room, cand, cut)
        return lax.fori_loop(0, nbits, jb, jnp.zeros((1, LANES), I32))

    cut = lax.cond(need_tie, tie_fn, lambda: jnp.full((1, LANES), 2 * seq, I32))
    cut = jnp.where(has_thr, cut, 0)

    qlat = qlat_ref[0]
    qr = jnp.concatenate([qlat[:, h * LANES:(h + 1) * LANES] for h in range(A_HEADS)], axis=0)

    def sc_body(kt, m):
        st = pl.dot(ckv_ref[0, kt], qr, trans_b=True)
        kv = keys_ref[kt]
        spos = kt * KEY_TILE + row_iota
        sel = (kv > thr) | ((kv == thr) & (spos < cut))
        dist = jnp.abs(qpos - spos).astype(F32)
        out = []
        for h in range(A_HEADS):
            sh = st[:, h * LANES:(h + 1) * LANES] - A_SLOPES[h] * dist
            sh = jnp.where(sel, sh, -jnp.inf)
            s_ref[kt, :, h * LANES:(h + 1) * LANES] = sh
            out.append(jnp.maximum(m[h], jnp.max(sh.reshape(nsub, SUBLANES, LANES), axis=0)))
        return tuple(out)

    m8 = over_tiles(sc_body, tuple(jnp.full((SUBLANES, LANES), -jnp.inf, F32) for _ in range(A_HEADS)))
    mx = [jnp.max(m, axis=0, keepdims=True) for m in m8]

    acc_ref[...] = jnp.zeros_like(acc_ref)

    def pv_body(kt, l):
        out, ps = [], []
        for h in range(A_HEADS):
            p = jnp.exp(s_ref[kt, :, h * LANES:(h + 1) * LANES] - mx[h])
            out.append(l[h] + jnp.sum(p.reshape(nsub, SUBLANES, LANES), axis=0))
            ps.append(p.astype(BF16))
        acc_ref[...] += jnp.dot(ckvt_ref[0, kt], jnp.concatenate(ps, axis=1),
                                preferred_element_type=F32)
        return tuple(out)

    l8 = over_tiles(pv_body, tuple(jnp.zeros((SUBLANES, LANES), F32) for _ in range(A_HEADS)))
    inv = [1.0 / jnp.sum(l, axis=0, keepdims=True) for l in l8]

    for j in range(A_HEADS // 2):
        y = None
        for h in (2 * j, 2 * j + 1):
            o = (acc_ref[:, h * LANES:(h + 1) * LANES] * inv[h]).astype(BF16)
            t = pl.dot(o, wuv_ref[h], trans_a=True)
            y = t if y is None else y + t
        ya_ref[0, :, j * LANES:(j + 1) * LANES] = y.astype(BF16)


def _mixer_a(kidx, ckv, ckvt, qi_r, wit, qlat, wuv_pad, bsz, seq):
    nqb = seq // Q_BLOCK
    nkt = seq // KEY_TILE
    topk = min(IDX_TOPK, seq // 4)
    kern = functools.partial(_mixer_a_kernel, seq=seq, topk=topk)
    return pl.pallas_call(
        kern,
        out_shape=jax.ShapeDtypeStruct((bsz, seq, A_WIDTH), BF16),
        grid=(bsz, nqb),
        in_specs=[
            pl.BlockSpec((1, nkt, KEY_TILE, IDX_DIM), lambda b, i: (b, 0, 0, 0)),
            pl.BlockSpec((1, nkt, KEY_TILE, A_KV_RANK), lambda b, i: (b, 0, 0, 0)),
            pl.BlockSpec((1, nkt, A_KV_RANK, KEY_TILE), lambda b, i: (b, 0, 0, 0)),
            pl.BlockSpec((1, 1, IDX_HEADS * Q_BLOCK, IDX_DIM), lambda b, i: (b, i, 0, 0)),
            pl.BlockSpec((1, IDX_HEADS, Q_BLOCK), lambda b, i: (b, 0, i)),
            pl.BlockSpec((1, Q_BLOCK, A_HEADS * A_KV_RANK), lambda b, i: (b, i, 0)),
            pl.BlockSpec(wuv_pad.shape, lambda b, i: (0, 0, 0)),
        ],
        out_specs=pl.BlockSpec((1, Q_BLOCK, A_WIDTH), lambda b, i: (b, i, 0)),
        scratch_shapes=[
            pltpu.VMEM((nkt, KEY_TILE, LANES), I32),
            pltpu.VMEM((nkt, KEY_TILE, A_HEADS * LANES), F32),
            pltpu.VMEM((A_KV_RANK, A_HEADS * LANES), F32),
        ],
        compiler_params=_params(("parallel", "arbitrary")),
        name="mixer_a",
    )(kidx, ckv, ckvt, qi_r, wit, qlat, wuv_pad)


def _mixer_b_kernel(q_ref, k_ref, v_ref, bias_ref, yb_ref):
    t0 = pl.multiple_of(pl.program_id(1) * B_QTILE, B_QTILE)
    kw = k_ref[0, pl.ds(t0, B_WIN), :]
    vw = v_ref[0, pl.ds(t0, B_WIN), :]
    q = q_ref[0]
    kvalid = (t0 - B_PAD + lax.broadcasted_iota(I32, (1, B_WIN), 1)) >= 0
    lane = lax.broadcasted_iota(I32, (1, LANES), 1)
    scale = B_HEAD_DIM ** -0.5
    for j in range(B_HEADS // 2):
        sl = slice(j * LANES, (j + 1) * LANES)
        qs, ks, vs = q[:, sl], kw[:, sl], vw[:, sl]
        outs = []
        for hh in range(2):
            mine = (lane >= B_HEAD_DIM) if hh else (lane < B_HEAD_DIM)
            qm = jnp.where(mine, qs, jnp.zeros_like(qs))
            s = pl.dot(qm, ks, trans_b=True) * scale + bias_ref[2 * j + hh]
            s = jnp.where(kvalid, s, -jnp.inf)
            m = jnp.max(s, axis=-1, keepdims=True)
            p = jnp.exp(s - m)
            l = jnp.sum(p, axis=-1, keepdims=True)
            outs.append(jnp.dot(p.astype(BF16), vs, preferred_element_type=F32) / l)
        yb_ref[0, :, sl] = jnp.where(lane < B_HEAD_DIM, outs[0], outs[1]).astype(BF16)


def _band_bias(rel_bias):
    tq = np.arange(B_QTILE)[:, None]
    j = np.arange(B_WIN)[None, :]
    band = (j // CHUNK >= tq // CHUNK) & (j // CHUNK <= tq // CHUNK + B_LEFT_CHUNKS)
    m = np.arange(B_QTILE - 1 + B_WIN)
    line = rel_bias[:, np.clip(B_WIN - 1 - m, -REL_CLIP, REL_CLIP) + REL_CLIP].astype(F32)
    table = jnp.stack([line[:, B_QTILE - 1 - t:B_QTILE - 1 - t + B_WIN] for t in range(B_QTILE)], axis=1)
    return jnp.where(jnp.asarray(band)[None], table, -jnp.inf)


def _mixer_b(qb, kb_pad, vb_pad, bias, bsz, seq):
    return pl.pallas_call(
        _mixer_b_kernel,
        out_shape=jax.ShapeDtypeStruct((bsz, seq, B_WIDTH), BF16),
        grid=(bsz, seq // B_QTILE),
        in_specs=[
            pl.BlockSpec((1, B_QTILE, B_WIDTH), lambda b, i: (b, i, 0)),
            pl.BlockSpec((1, seq + B_PAD, B_WIDTH), lambda b, i: (b, 0, 0)),
            pl.BlockSpec((1, seq + B_PAD, B_WIDTH), lambda b, i: (b, 0, 0)),
            pl.BlockSpec(bias.shape, lambda b, i: (0, 0, 0), pipeline_mode=pl.Buffered(1)),
        ],
        out_specs=pl.BlockSpec((1, B_QTILE, B_WIDTH), lambda b, i: (b, i, 0)),
        compiler_params=_params(("parallel", "arbitrary")),
        name="mixer_b",
    )(qb, kb_pad, vb_pad, bias)


def _layer_norm(z, g, b):
    mu = jnp.mean(z, axis=-1, keepdims=True)
    var = jnp.mean(jnp.square(z - mu), axis=-1, keepdims=True)
    return ((z - mu) * lax.rsqrt(var + LN_EPS)) * g + b


def _merge_kernel(x_ref, ya_ref, yb_ref, wg_ref, wba_ref, wbb_ref, wo_ref, g_ref, b_ref,
                  wr_ref, br_ref, x1t_ref, topi_ref, gate_ref):
    tm = x_ref.shape[0]
    x = x_ref[...]
    xb = x.astype(BF16)
    ga = jnp.dot(xb, wg_ref[:, :D_MODEL], preferred_element_type=F32)
    a = jnp.dot(ya_ref[...], wba_ref[...], preferred_element_type=F32)
    merged = jax.nn.sigmoid(ga) * a
    gb = jnp.dot(xb, wg_ref[:, D_MODEL:], preferred_element_type=F32)
    b = jnp.dot(yb_ref[...], wbb_ref[...], preferred_element_type=F32)
    merged = merged + jax.nn.sigmoid(gb) * b
    out = jnp.dot(merged.astype(BF16), wo_ref[...], preferred_element_type=F32)
    x1 = _layer_norm(DN_ALPHA * x + out, g_ref[...], b_ref[...])
    for c in range(ROW_TILES):
        x1t_ref[pl.ds(c, tm, stride=ROW_TILES), :] = x1[:, c * LANES:(c + 1) * LANES]

    lg = pl.dot(wr_ref[...], x1.astype(BF16), trans_b=True) + br_ref[...]
    eidx = lax.broadcasted_iota(I32, lg.shape, 0)
    vals, idxs = [], []
    for _ in range(TOP_K):
        m = jnp.max(lg, axis=0, keepdims=True)
        sel = jnp.min(jnp.where(lg == m, eidx, N_EXPERTS), axis=0, keepdims=True)
        vals.append(m)
        idxs.append(sel)
        lg = jnp.where(eidx == sel, -jnp.inf, lg)
    es = [jnp.exp(v - vals[0]) for v in vals]
    tot = es[0] + es[1] + es[2] + es[3]
    topi_ref[...] = jnp.concatenate(idxs, axis=0)
    gate_ref[...] = jnp.concatenate([e / tot for e in es], axis=0)


def _merge(xf, ya, yb, wg, wba, wbb, wo, g1, b1, wr_t, br):
    n, tm = xf.shape[0], min(MATMUL_TILE, xf.shape[0])
    row = lambda w: pl.BlockSpec((tm, w), lambda i: (i, 0))
    col = pl.BlockSpec((TOP_K, tm), lambda i: (0, i))
    return pl.pallas_call(
        _merge_kernel,
        out_shape=[jax.ShapeDtypeStruct((n * ROW_TILES, LANES), F32),
                   jax.ShapeDtypeStruct((TOP_K, n), I32),
                   jax.ShapeDtypeStruct((TOP_K, n), F32)],
        grid=(n // tm,),
        in_specs=[row(D_MODEL), row(A_WIDTH), row(B_WIDTH), _full(wg), _full(wba), _full(wbb),
                  _full(wo), _full(g1), _full(b1), _full(wr_t), _full(br)],
        out_specs=[pl.BlockSpec((tm * ROW_TILES, LANES), lambda i: (i, 0)), col, col],
        compiler_params=_params(("parallel",)),
        name="merge",
    )(xf, ya, yb, wg, wba, wbb, wo, g1, b1, wr_t, br)


def _one_hot_rows(topi, k, width):
    eidx = lax.broadcasted_iota(I32, (N_EXPERTS, width), 0)
    return eidx == topi[k:k + 1, :]


def _route_kernel(topi_ref, tri_ref, rank_ref, cnt_ref, run_ref):
    @pl.when(pl.program_id(0) == 0)
    def _():
        run_ref[...] = jnp.zeros_like(run_ref)

    topi = topi_ref[...]
    tr = topi.shape[1]
    hot = [_one_hot_rows(topi, k, tr) for k in range(TOP_K)]
    oh = jnp.concatenate([jnp.where(h, 1.0, 0.0).astype(BF16) for h in hot], axis=0)
    prefix = jnp.dot(oh, tri_ref[...], preferred_element_type=F32)
    base = run_ref[...]
    ranks = []
    for k in range(TOP_K):
        tbl = prefix[k * N_EXPERTS:(k + 1) * N_EXPERTS, :] + base
        ranks.append(jnp.sum(jnp.where(hot[k], tbl, 0.0), axis=0, keepdims=True))
        base = base + jnp.sum(jnp.where(hot[k], 1.0, 0.0), axis=1, keepdims=True)
    rank_ref[...] = jnp.concatenate(ranks, axis=0).astype(I32)
    run_ref[...] = base
    cnt_ref[...] = base.astype(I32)


def _route(topi):
    n, tr = topi.shape[1], TOKEN_TILE
    tri = (np.arange(tr)[:, None] < np.arange(tr)[None, :]).astype(np.float32)
    tri = jnp.asarray(tri, BF16)
    col = pl.BlockSpec((TOP_K, tr), lambda i: (0, i))
    return pl.pallas_call(
        _route_kernel,
        out_shape=[jax.ShapeDtypeStruct((TOP_K, n), I32), jax.ShapeDtypeStruct((N_EXPERTS, 1), I32)],
        grid=(n // tr,),
        in_specs=[col, _full(tri)],
        out_specs=[col, pl.BlockSpec((N_EXPERTS, 1), lambda i: (0, 0))],
        scratch_shapes=[pltpu.VMEM((N_EXPERTS, 1), F32)],
        compiler_params=_params(("arbitrary",)),
        name="route",
    )(topi, tri)


def _positions_kernel(topi_ref, rank_ref, pstv_ref, pos_ref):
    topi = topi_ref[...]
    rows = []
    for k in range(TOP_K):
        start = jnp.sum(jnp.where(_one_hot_rows(topi, k, topi.shape[1]), pstv_ref[...], 0),
                        axis=0, keepdims=True)
        rows.append(rank_ref[k:k + 1, :] + start)
    pos_ref[...] = jnp.concatenate(rows, axis=0)


def _positions(topi, rank, pstarts):
    n, td = topi.shape[1], TOKEN_TILE
    col = pl.BlockSpec((TOP_K, td), lambda i: (0, i))
    pstv = pstarts.reshape(N_EXPERTS, 1)
    return pl.pallas_call(
        _positions_kernel,
        out_shape=jax.ShapeDtypeStruct((TOP_K, n), I32),
        grid=(n // td,),
        in_specs=[col, col, _full(pstv)],
        out_specs=col,
        compiler_params=_params(("parallel",)),
        name="positions",
    )(topi, rank, pstv)


def _dispatch_kernel(cnt_ref, pst_ref, pos_hbm, x_hbm, xs_hbm,
                     pos_smem, xbuf, zbuf, sem_p, sem_x, sem_d, sem_z):
    i = pl.program_id(0)
    n = pl.num_programs(0)
    ng = pos_smem.shape[1]
    td = ng * DMA_GROUP
    tile_rows = td * ROW_TILES

    def pos_copy(step, slot):
        return pltpu.make_async_copy(pos_hbm.at[pl.ds(step * ng, ng)], pos_smem.at[slot], sem_p.at[slot])

    def x_copy(step, slot):
        return pltpu.make_async_copy(x_hbm.at[pl.ds(step * tile_rows, tile_rows)], xbuf.at[slot],
                                     sem_x.at[slot])

    @pl.when(i == 0)
    def _():
        pos_copy(0, 0).start()
        x_copy(0, 0).start()

    @pl.when(i + 1 < n)
    def _():
        pos_copy(i + 1, (i + 1) % 2).start()
        x_copy(i + 1, (i + 1) % 3).start()

    slot = i % 2
    xslot = i % 3
    pos_copy(i, slot).wait()
    x_copy(i, xslot).wait()

    def step_bytes(s):
        rows_ = TOP_K * tile_rows
        return pltpu.make_async_copy(xs_hbm.at[pl.ds(0, rows_)], xs_hbm.at[pl.ds(0, rows_)], sem_d.at[s])

    def body(g, c):
        dst = [pos_smem[slot, g, j] * ROW_TILES for j in range(DMA_GROUP * TOP_K)]
        for u in range(DMA_GROUP):
            src = xbuf.at[xslot, pl.ds((g * DMA_GROUP + u) * ROW_TILES, ROW_TILES)]
            for k in range(TOP_K):
                pltpu.make_async_copy(src, xs_hbm.at[pl.ds(dst[u * TOP_K + k], ROW_TILES)],
                                      sem_d.at[xslot]).start(priority=k % 2)
        return c
    lax.fori_loop(0, ng, body, 0)

    @pl.when(i > 0)
    def _():
        step_bytes((i + 2) % 3).wait()

    @pl.when(i == n - 1)
    def _():
        step_bytes(xslot).wait()
        zbuf[...] = jnp.zeros_like(zbuf)

        def zero_copy(r):
            return pltpu.make_async_copy(zbuf, xs_hbm.at[pl.ds(r * ROW_TILES, ROW_TILES)], sem_z)

        def zero_rows(lo, hi):
            def start(r, c2):
                zero_copy(r).start()
                return c2

            def wait(r, c2):
                zero_copy(r).wait()
                return c2
            lax.fori_loop(lo, hi, start, 0)
            lax.fori_loop(lo, hi, wait, 0)

        def padded_end(e):
            return pst_ref[e] + (cnt_ref[e] + MOE_BLOCK - 1) // MOE_BLOCK * MOE_BLOCK

        def ebody(e, c):
            zero_rows(pst_ref[e] + cnt_ref[e], padded_end(e))
            return c
        lax.fori_loop(0, N_EXPERTS, ebody, 0)
        zero_rows(padded_end(N_EXPERTS - 1), xs_hbm.shape[0] // ROW_TILES)


def _dispatch(counts, pstarts, pos_g, x1t, cap):
    n, td = pos_g.shape[0] * DMA_GROUP, TOKEN_TILE
    any_spec = pl.BlockSpec(memory_space=pl.ANY)
    return pl.pallas_call(
        _dispatch_kernel,
        out_shape=jax.ShapeDtypeStruct((cap * ROW_TILES, LANES), F32),
        grid_spec=pltpu.PrefetchScalarGridSpec(
            num_scalar_prefetch=2,
            grid=(n // td,),
            in_specs=[any_spec, any_spec],
            out_specs=any_spec,
            scratch_shapes=[
                pltpu.SMEM((2, td // DMA_GROUP, DMA_GROUP * TOP_K), I32),
                pltpu.VMEM((3, td * ROW_TILES, LANES), F32),
                pltpu.VMEM((ROW_TILES, LANES), F32),
                pltpu.SemaphoreType.DMA((2,)),
                pltpu.SemaphoreType.DMA((3,)),
                pltpu.SemaphoreType.DMA((3,)),
                pltpu.SemaphoreType.DMA,
            ],
        ),
        compiler_params=_params(("arbitrary",)),
        name="dispatch",
    )(counts, pstarts, pos_g, x1t)


def _moe_kernel(be_ref, nu_ref, xs_ref, wg_ref, bg_ref, wu_ref, bu_ref, wd_ref, bd_ref, ys_ref,
                wg_bf, wu_bf, wd_bf):
    i = pl.program_id(0)
    blk = MOE_BLOCK
    used = i < nu_ref[0]

    @pl.when(jnp.logical_or(i == 0, be_ref[i] != be_ref[jnp.maximum(i - 1, 0)]))
    def _():
        wg_bf[...] = wg_ref[0].astype(BF16)
        wu_bf[...] = wu_ref[0].astype(BF16)
        wd_bf[...] = wd_ref[0].astype(BF16)

    @pl.when(used)
    def _():
        xb = jnp.concatenate([xs_ref[pl.ds(c, blk, stride=ROW_TILES), :].astype(BF16)
                              for c in range(ROW_TILES)], axis=1)
        hg = jnp.minimum(jnp.dot(xb, wg_bf[...], preferred_element_type=F32) + bg_ref[0], SWIGLU_LIMIT)
        hu = jnp.clip(jnp.dot(xb, wu_bf[...], preferred_element_type=F32) + bu_ref[0],
                      -SWIGLU_LIMIT, SWIGLU_LIMIT)
        h = (hu + 1.0) * (hg * jax.nn.sigmoid(SWIGLU_ALPHA * hg))
        y = jnp.dot(h.astype(BF16), wd_bf[...], preferred_element_type=F32) + bd_ref[0]
        for c in range(ROW_TILES):
            ys_ref[pl.ds(c, blk, stride=ROW_TILES), :] = y[:, c * LANES:(c + 1) * LANES]

    @pl.when(jnp.logical_not(used))
    def _():
        ys_ref[...] = jnp.zeros_like(ys_ref)


def _moe(blk_expert, nused, xs, wg, bg, wu, bu, wd, bd):
    blk = MOE_BLOCK
    nblk = xs.shape[0] // (blk * ROW_TILES)
    wspec = lambda a: pl.BlockSpec((1,) + a.shape[1:], lambda i, be, nu: (be[i], 0, 0))
    xspec = pl.BlockSpec((blk * ROW_TILES, LANES), lambda i, be, nu: (jnp.minimum(i, nu[0] - 1), 0))
    return pl.pallas_call(
        _moe_kernel,
        out_shape=jax.ShapeDtypeStruct(xs.shape, F32),
        grid_spec=pltpu.PrefetchScalarGridSpec(
            num_scalar_prefetch=2,
            grid=(nblk,),
            in_specs=[xspec, wspec(wg), wspec(bg), wspec(wu), wspec(bu), wspec(wd), wspec(bd)],
            out_specs=pl.BlockSpec((blk * ROW_TILES, LANES), lambda i, be, nu: (i, 0)),
            scratch_shapes=[pltpu.VMEM(wg.shape[1:], BF16), pltpu.VMEM(wu.shape[1:], BF16),
                            pltpu.VMEM(wd.shape[1:], BF16)],
        ),
        compiler_params=_params(("arbitrary",)),
        name="moe",
    )(blk_expert, nused, xs, wg, bg, wu, bu, wd, bd)


def _final_kernel(pos_hbm, gt_ref, x1t_ref, ys_hbm, g_ref, b_ref, o_ref,
                  pos_smem, ybuf, sem_p, sem_g):
    i = pl.program_id(0)
    n = pl.num_programs(0)
    ft = FINAL_TILE
    slot_rows = TOP_K * ft * ROW_TILES

    ng = ft // DMA_GROUP

    def pos_copy(tile, slot):
        return pltpu.make_async_copy(pos_hbm.at[pl.ds(tile * ng, ng)], pos_smem.at[slot], sem_p.at[slot])

    def gather_rows(pslot, yslot):
        def body(g, c):
            src = [pos_smem[pslot, g, j] * ROW_TILES for j in range(DMA_GROUP * TOP_K)]
            base = yslot * slot_rows + g * (DMA_GROUP * ROW_TILES)
            for u in range(DMA_GROUP):
                for k in range(TOP_K):
                    dst = base + (k * ft + u) * ROW_TILES
                    pltpu.make_async_copy(ys_hbm.at[pl.ds(src[u * TOP_K + k], ROW_TILES)],
                                          ybuf.at[pl.ds(dst, ROW_TILES)],
                                          sem_g.at[yslot]).start(priority=k % 2)
            return c
        lax.fori_loop(0, ng, body, 0)

    def gather_wait(yslot):
        pltpu.make_async_copy(ys_hbm.at[pl.ds(0, slot_rows)],
                              ybuf.at[pl.ds(yslot * slot_rows, slot_rows)], sem_g.at[yslot]).wait()

    @pl.when(i == 0)
    def _():
        c = pos_copy(0, 0)
        c.start()
        c.wait()
        gather_rows(0, 0)

        @pl.when(n > 1)
        def _():
            pos_copy(1, 1).start()

    @pl.when(i + 1 < n)
    def _():
        pos_copy(i + 1, (i + 1) % 3).wait()
        gather_rows((i + 1) % 3, (i + 1) % 2)

    @pl.when(i + 2 < n)
    def _():
        pos_copy(i + 2, (i + 2) % 3).start()

    slot = i % 2
    gather_wait(slot)
    g = gt_ref[...]
    zs = []
    for c in range(ROW_TILES):
        f = None
        for k in range(TOP_K):
            yk = ybuf[pl.ds(slot * slot_rows + k * ft * ROW_TILES + c, ft, stride=ROW_TILES), :]
            term = g[:, k:k + 1] * yk
            f = term if f is None else f + term
        zs.append(DN_ALPHA * x1t_ref[pl.ds(c, ft, stride=ROW_TILES), :] + f)
    o_ref[...] = _layer_norm(jnp.concatenate(zs, axis=1), g_ref[...], b_ref[...])


def _final(pos, gates_t, x1t, ys, g2, b2):
    ft = FINAL_TILE
    n = gates_t.shape[0]
    any_spec = pl.BlockSpec(memory_space=pl.ANY)
    return pl.pallas_call(
        _final_kernel,
        out_shape=jax.ShapeDtypeStruct((n, D_MODEL), F32),
        grid=(n // ft,),
        in_specs=[any_spec, pl.BlockSpec((ft, TOP_K), lambda i: (i, 0)),
                  pl.BlockSpec((ft * ROW_TILES, LANES), lambda i: (i, 0)), any_spec,
                  _full(g2), _full(b2)],
        out_specs=pl.BlockSpec((ft, D_MODEL), lambda i: (i, 0)),
        scratch_shapes=[
            pltpu.SMEM((3, ft // DMA_GROUP, DMA_GROUP * TOP_K), I32),
            pltpu.VMEM((2 * TOP_K * ft * ROW_TILES, LANES), F32),
            pltpu.SemaphoreType.DMA((3,)),
            pltpu.SemaphoreType.DMA((2,)),
        ],
        compiler_params=_params(("arbitrary",)),
        name="final",
    )(pos, gates_t, x1t, ys, g2, b2)


def _mix_weight(w_in):
    qa, ckv, qi, ki, wi, qb, kb, vb, _, _ = jnp.split(w_in, np.cumsum(SPLITS)[:-1].tolist(), axis=-1)
    pad = lambda a: jnp.pad(a, ((0, 0), (0, LANES - a.shape[1])))
    return jnp.concatenate([qa, qi, ckv, pad(ki), pad(wi), qb, kb, vb], axis=1).astype(BF16)


def _layer(x, w_in, kv_norm_g, idx_k_norm_g, idx_k_norm_b, w_uk, w_uv, rel_bias, w_branch_a,
           w_branch_b, w_out, ln1_g, ln1_b, w_router, b_router, w_gate, b_gate, w_up, b_up,
           w_down, b_down, ln2_g, ln2_b):
    bsz, seq, _ = x.shape
    n = bsz * seq
    assert seq % KEY_TILE == 0 and n % TOKEN_TILE == 0
    xf = x.reshape(n, D_MODEL)
    row = lambda v: v.reshape(1, -1).astype(F32)

    w_mix = _mix_weight(w_in)
    w_gates = w_in[:, sum(SPLITS[:8]):].astype(BF16)
    wuk_bd = jnp.zeros((A_WIDTH, A_HEADS * A_KV_RANK), F32)
    wuv_pad = jnp.zeros((A_HEADS, A_KV_RANK, LANES), F32)
    for h in range(A_HEADS):
        wuk_bd = wuk_bd.at[h * A_HEAD_DIM:(h + 1) * A_HEAD_DIM, h * A_KV_RANK:(h + 1) * A_KV_RANK].set(w_uk[h])
        c0 = (h % 2) * A_HEAD_DIM
        wuv_pad = wuv_pad.at[h, :, c0:c0 + A_HEAD_DIM].set(w_uv[h])
    wuk_bd, wuv_pad = wuk_bd.astype(BF16), wuv_pad.astype(BF16)

    qlat, qi, ckv, ki, wi, qb, kb, vb = _projection(
        xf, w_mix, wuk_bd, row(kv_norm_g), row(idx_k_norm_g), row(idx_k_norm_b))

    nqb, nkt = seq // Q_BLOCK, seq // KEY_TILE
    kidx = ki.reshape(bsz, nkt, KEY_TILE, IDX_DIM)
    ckv4 = ckv.reshape(bsz, nkt, KEY_TILE, A_KV_RANK)
    ckvt = ckv4.transpose(0, 1, 3, 2)
    qi_r = qi.reshape(bsz, nqb, Q_BLOCK, IDX_HEADS, IDX_DIM).transpose(0, 1, 3, 2, 4).reshape(
        bsz, nqb, IDX_HEADS * Q_BLOCK, IDX_DIM)
    wit = wi.reshape(bsz, seq, IDX_HEADS).transpose(0, 2, 1)
    ya = _mixer_a(kidx, ckv4, ckvt, qi_r, wit, qlat.reshape(bsz, seq, -1), wuv_pad, bsz, seq)

    front = lambda a: jnp.pad(a.reshape(bsz, seq, B_WIDTH), ((0, 0), (B_PAD, 0), (0, 0)))
    yb = _mixer_b(qb.reshape(bsz, seq, B_WIDTH), front(kb), front(vb), _band_bias(rel_bias), bsz, seq)

    x1t, topi, gates = _merge(
        xf, ya.reshape(n, A_WIDTH), yb.reshape(n, B_WIDTH), w_gates, w_branch_a.astype(BF16),
        w_branch_b.astype(BF16), w_out.astype(BF16), row(ln1_g), row(ln1_b),
        w_router.T.astype(BF16), b_router.reshape(-1, 1).astype(F32))

    blk = MOE_BLOCK
    cap = n * TOP_K + N_EXPERTS * blk
    nblk = cap // blk
    rank, cnt = _route(topi)
    counts = cnt[:, 0]
    padded = (counts + blk - 1) // blk * blk
    pends = jnp.cumsum(padded).astype(I32)
    pstarts = pends - padded
    nused = (pends[-1:] // blk).astype(I32)
    blk_start = jnp.arange(nblk, dtype=I32) * blk
    blk_expert = jnp.minimum(jnp.sum(blk_start[:, None] >= pends[None, :], axis=1), N_EXPERTS - 1).astype(I32)

    pos = _positions(topi, rank, pstarts)
    pos_g = pos.T.reshape(n // DMA_GROUP, DMA_GROUP * TOP_K)
    xs = _dispatch(counts, pstarts, pos_g, x1t, cap)
    ys = _moe(blk_expert, nused, xs, w_gate, b_gate[:, None, :], w_up, b_up[:, None, :],
              w_down, b_down[:, None, :])
    out = _final(pos_g, gates.T, x1t, ys, row(ln2_g), row(ln2_b))
    return out.reshape(bsz, seq, D_MODEL)


def kernel(x, w_in, kv_norm_g, idx_k_norm_g, idx_k_norm_b, w_uk, w_uv, rel_bias, w_branch_a,
           w_branch_b, w_out, ln1_g, ln1_b, w_router, b_router, w_gate, b_gate, w_up, b_up,
           w_down, b_down, ln2_g, ln2_b):
    for l in range(DEPTH):
        x = _layer(x, w_in[l], kv_norm_g[l], idx_k_norm_g[l], idx_k_norm_b[l], w_uk[l], w_uv[l],
                   rel_bias[l], w_branch_a[l], w_branch_b[l], w_out[l], ln1_g[l], ln1_b[l],
                   w_router[l], b_router[l], w_gate[l], b_gate[l], w_up[l], b_up[l], w_down[l],
                   b_down[l], ln2_g[l], ln2_b[l])
    return x
```

```python
import functools

import jax
import jax.numpy as jnp
import numpy as np
from jax import lax
from jax.experimental import pallas as pl
from jax.experimental.pallas import tpu as pltpu

F32 = jnp.float32
BF16 = jnp.bfloat16
I32 = jnp.int32

D_MODEL = 1024
CHUNK = 64
Q_BLOCK = 128
A_HEADS = 8
A_HEAD_DIM = 64
A_KV_RANK = 128
IDX_HEADS = 8
IDX_DIM = 64
IDX_TOPK = 256
IDX_SCALE = (IDX_HEADS * IDX_DIM) ** -0.5
B_HEADS = 8
B_HEAD_DIM = 64
B_LEFT_CHUNKS = 8
REL_CLIP = 128
N_EXPERTS = 32
TOP_K = 4
D_FF = 1024
SWIGLU_LIMIT = 7.0
SWIGLU_ALPHA = 1.702
DEPTH = 1
DN_ALPHA = (2.0 * DEPTH) ** 0.25
LN_EPS = 1e-5

A_WIDTH = A_HEADS * A_HEAD_DIM
B_WIDTH = B_HEADS * B_HEAD_DIM
SPLITS = [A_WIDTH, A_KV_RANK, IDX_HEADS * IDX_DIM, IDX_DIM, IDX_HEADS,
          B_WIDTH, B_WIDTH, B_WIDTH, D_MODEL, D_MODEL]

LANES = 128
SUBLANES = 8
KEY_TILE = 256
B_QTILE = 512
B_PAD = B_LEFT_CHUNKS * CHUNK
B_WIN = B_PAD + B_QTILE
TOKEN_TILE = 512
MATMUL_TILE = 1024
MOE_BLOCK = 512
FINAL_TILE = 512
ROW_TILES = D_MODEL // LANES
DMA_GROUP = 4
INT_MIN = -2 ** 31
VMEM_LIMIT = 56 * 1024 * 1024

A_SLOPES = [2.0 ** (-8.0 * (h + 1) / A_HEADS) for h in range(A_HEADS)]

_C_QA = 0
_C_QI = _C_QA + A_WIDTH
_C_CKV = _C_QI + IDX_HEADS * IDX_DIM
_C_KI = _C_CKV + A_KV_RANK
_C_WI = _C_KI + LANES
_C_QKVB = _C_WI + LANES
_C_END = _C_QKVB + 3 * B_WIDTH


def _params(sem, vmem=VMEM_LIMIT):
    return pltpu.CompilerParams(dimension_semantics=sem, vmem_limit_bytes=vmem)


def _full(a):
    return pl.BlockSpec(a.shape, lambda *_: (0,) * a.ndim)


def _proj_kernel(x_ref, w_ref, wuk_ref, kvg_ref, kig_ref, kib_ref,
                 qlat_ref, qi_ref, ckv_ref, ki_ref, wi_ref, qb_ref, kb_ref, vb_ref):
    xb = x_ref[...].astype(BF16)

    def mm(c0, c1):
        return jnp.dot(xb, w_ref[:, c0:c1], preferred_element_type=F32)

    qa = mm(_C_QA, _C_QI)
    qlat = jnp.dot(qa.astype(BF16), wuk_ref[...], preferred_element_type=F32)
    qlat_ref[...] = (qlat * (A_HEAD_DIM ** -0.5)).astype(BF16)
    qi_ref[...] = mm(_C_QI, _C_CKV).astype(BF16)

    ckv = mm(_C_CKV, _C_KI)
    ms = jnp.mean(ckv * ckv, axis=-1, keepdims=True)
    ckv_ref[...] = ((ckv * lax.rsqrt(ms + LN_EPS)) * kvg_ref[...]).astype(BF16)

    ki = mm(_C_KI, _C_WI)[:, :IDX_DIM]
    mu = jnp.mean(ki, axis=-1, keepdims=True)
    var = jnp.mean(jnp.square(ki - mu), axis=-1, keepdims=True)
    ki_ref[...] = (((ki - mu) * lax.rsqrt(var + LN_EPS)) * kig_ref[...] + kib_ref[...]).astype(BF16)

    wi_ref[...] = mm(_C_WI, _C_QKVB)[:, :IDX_HEADS]
    qb_ref[...] = mm(_C_QKVB, _C_QKVB + B_WIDTH).astype(BF16)
    kb_ref[...] = mm(_C_QKVB + B_WIDTH, _C_QKVB + 2 * B_WIDTH).astype(BF16)
    vb_ref[...] = mm(_C_QKVB + 2 * B_WIDTH, _C_END).astype(BF16)


def _projection(xf, w_mix, wuk_bd, kvg, kig, kib):
    n, tm = xf.shape[0], min(MATMUL_TILE, xf.shape[0])
    row = lambda w: pl.BlockSpec((tm, w), lambda i: (i, 0))
    outs = [(A_HEADS * A_KV_RANK, BF16), (IDX_HEADS * IDX_DIM, BF16), (A_KV_RANK, BF16),
            (IDX_DIM, BF16), (IDX_HEADS, F32), (B_WIDTH, BF16), (B_WIDTH, BF16), (B_WIDTH, BF16)]
    return pl.pallas_call(
        _proj_kernel,
        out_shape=[jax.ShapeDtypeStruct((n, w), dt) for w, dt in outs],
        grid=(n // tm,),
        in_specs=[row(D_MODEL), _full(w_mix), _full(wuk_bd), _full(kvg), _full(kig), _full(kib)],
        out_specs=[row(w) for w, _ in outs],
        compiler_params=_params(("parallel",)),
        name="proj",
    )(xf, w_mix, wuk_bd, kvg, kig, kib)


def _mixer_a_kernel(kidx_ref, ckv_ref, ckvt_ref, qi_ref, wit_ref, qlat_ref, wuv_ref,
                    ya_ref, keys_ref, s_ref, acc_ref, *, seq, topk):
    i = pl.program_id(1)
    nkt = i // 2 + 1
    nsub = KEY_TILE // SUBLANES
    lane = lax.broadcasted_iota(I32, (1, LANES), 1)
    qpos = i * Q_BLOCK + lane
    qchunk = qpos >> 6
    row_iota = lax.broadcasted_iota(I32, (KEY_TILE, LANES), 0)

    qi_blk = qi_ref[0, 0]
    w_all = wit_ref[0] * IDX_SCALE

    def idx_body(kt, carry):
        lg = pl.dot(kidx_ref[0, kt], qi_blk, trans_b=True)
        isc = jnp.zeros((KEY_TILE, LANES), F32)
        for h in range(IDX_HEADS):
            isc = isc + jnp.maximum(lg[:, h * LANES:(h + 1) * LANES], 0.0) * w_all[h:h + 1, :]
        bits = lax.bitcast_convert_type(isc + 0.0, I32)
        key = bits ^ ((bits >> 31) & 0x7FFFFFFF)
        spos = kt * KEY_TILE + row_iota
        key = jnp.where((spos >> 6) <= qchunk, key, INT_MIN)
        keys_ref[kt] = key
        return carry

    def over_tiles(body, init):
        def pair(j, c):
            return body(2 * j + 1, body(2 * j, c))
        c = lax.fori_loop(0, nkt // 2, pair, init)
        return lax.cond(nkt % 2 == 1, lambda c: body(nkt - 1, c), lambda c: c, c)

    over_tiles(idx_body, 0)

    def count(pred):
        def body(kt, acc):
            spos = kt * KEY_TILE + row_iota
            m = jnp.where(pred(keys_ref[kt], spos), 1, 0).astype(I32)
            return acc + jnp.sum(m.reshape(nsub, SUBLANES, LANES), axis=0)
        acc = over_tiles(body, jnp.zeros((SUBLANES, LANES), I32))
        return jnp.sum(acc, axis=0, keepdims=True)

    def bit_body(it, carry):
        thr, cge = carry
        cand = thr ^ lax.shift_left(jnp.int32(1), 31 - it)
        cnt = count(lambda kv, spos: kv >= cand)
        ok = cnt >= topk
        return jnp.where(ok, cand, thr), jnp.where(ok, cnt, cge)

    start = (jnp.full((1, LANES), INT_MIN, I32), jnp.zeros((1, LANES), I32))
    thr, cge = lax.cond(nkt * KEY_TILE > topk,
                        lambda: lax.fori_loop(0, 32, bit_body, start), lambda: start)

    nbits = int(np.log2(seq)) + 1
    has_thr = thr > INT_MIN
    need_tie = jnp.max(jnp.where(has_thr, cge, 0)) > topk

    def tie_fn():
        room = topk - count(lambda kv, spos: kv > thr)

        def jb(it, cut):
            cand = cut | lax.shift_left(jnp.int32(1), nbits - 1 - it)
            f = count(lambda kv, spos: (kv == thr) & (spos < cand))
            return jnp.where(f <= room, cand, cut)
        return lax.fori_loop(0, nbits, jb, jnp.zeros((1, LANES), I32))

    cut = lax.cond(need_tie, tie_fn, lambda: jnp.full((1, LANES), 2 * seq, I32))
    cut = jnp.where(has_thr, cut, 0)

    qlat = qlat_ref[0]
    qr = jnp.concatenate([qlat[:, h * LANES:(h + 1) * LANES] for h in range(A_HEADS)], axis=0)

    def sc_body(kt, m):
        st = pl.dot(ckv_ref[0, kt], qr, trans_b=True)
        kv = keys_ref[kt]
        spos = kt * KEY_TILE + row_iota
        sel = (kv > thr) | ((kv == thr) & (spos < cut))
        dist = jnp.abs(qpos - spos).astype(F32)
        out = []
        for h in range(A_HEADS):
            sh = st[:, h * LANES:(h + 1) * LANES] - A_SLOPES[h] * dist
            sh = jnp.where(sel, sh, -jnp.inf)
            s_ref[kt, :, h * LANES:(h + 1) * LANES] = sh
            out.append(jnp.maximum(m[h], jnp.max(sh.reshape(nsub, SUBLANES, LANES), axis=0)))
        return tuple(out)

    m8 = over_tiles(sc_body, tuple(jnp.full((SUBLANES, LANES), -jnp.inf, F32) for _ in range(A_HEADS)))
    mx = [jnp.max(m, axis=0, keepdims=True) for m in m8]

    acc_ref[...] = jnp.zeros_like(acc_ref)

    def pv_body(kt, l):
        out, ps = [], []
        for h in range(A_HEADS):
            p = jnp.exp(s_ref[kt, :, h * LANES:(h + 1) * LANES] - mx[h])
            out.append(l[h] + jnp.sum(p.reshape(nsub, SUBLANES, LANES), axis=0))
            ps.append(p.astype(BF16))
        acc_ref[...] += jnp.dot(ckvt_ref[0, kt], jnp.concatenate(ps, axis=1),
                                preferred_element_type=F32)
        return tuple(out)

    l8 = over_tiles(pv_body, tuple(jnp.zeros((SUBLANES, LANES), F32) for _ in range(A_HEADS)))
    inv = [1.0 / jnp.sum(l, axis=0, keepdims=True) for l in l8]

    for j in range(A_HEADS // 2):
        y = None
        for h in (2 * j, 2 * j + 1):
            o = (acc_ref[:, h * LANES:(h + 1) * LANES] * inv[h]).astype(BF16)
            t = pl.dot(o, wuv_ref[h], trans_a=True)
            y = t if y is None else y + t
        ya_ref[0, :, j * LANES:(j + 1) * LANES] = y.astype(BF16)


def _mixer_a(kidx, ckv, ckvt, qi_r, wit, qlat, wuv_pad, bsz, seq):
    nqb = seq // Q_BLOCK
    nkt = seq // KEY_TILE
    topk = min(IDX_TOPK, seq // 4)
    kern = functools.partial(_mixer_a_kernel, seq=seq, topk=topk)
    return pl.pallas_call(
        kern,
        out_shape=jax.ShapeDtypeStruct((bsz, seq, A_WIDTH), BF16),
        grid=(bsz, nqb),
        in_specs=[
            pl.BlockSpec((1, nkt, KEY_TILE, IDX_DIM), lambda b, i: (b, 0, 0, 0)),
            pl.BlockSpec((1, nkt, KEY_TILE, A_KV_RANK), lambda b, i: (b, 0, 0, 0)),
            pl.BlockSpec((1, nkt, A_KV_RANK, KEY_TILE), lambda b, i: (b, 0, 0, 0)),
            pl.BlockSpec((1, 1, IDX_HEADS * Q_BLOCK, IDX_DIM), lambda b, i: (b, i, 0, 0)),
            pl.BlockSpec((1, IDX_HEADS, Q_BLOCK), lambda b, i: (b, 0, i)),
            pl.BlockSpec((1, Q_BLOCK, A_HEADS * A_KV_RANK), lambda b, i: (b, i, 0)),
            pl.BlockSpec(wuv_pad.shape, lambda b, i: (0, 0, 0)),
        ],
        out_specs=pl.BlockSpec((1, Q_BLOCK, A_WIDTH), lambda b, i: (b, i, 0)),
        scratch_shapes=[
            pltpu.VMEM((nkt, KEY_TILE, LANES), I32),
            pltpu.VMEM((nkt, KEY_TILE, A_HEADS * LANES), F32),
            pltpu.VMEM((A_KV_RANK, A_HEADS * LANES), F32),
        ],
        compiler_params=_params(("parallel", "arbitrary")),
        name="mixer_a",
    )(kidx, ckv, ckvt, qi_r, wit, qlat, wuv_pad)


def _mixer_b_kernel(q_ref, k_ref, v_ref, bias_ref, yb_ref):
    t0 = pl.multiple_of(pl.program_id(1) * B_QTILE, B_QTILE)
    kw = k_ref[0, pl.ds(t0, B_WIN), :]
    vw = v_ref[0, pl.ds(t0, B_WIN), :]
    q = q_ref[0]
    kvalid = (t0 - B_PAD + lax.broadcasted_iota(I32, (1, B_WIN), 1)) >= 0
    lane = lax.broadcasted_iota(I32, (1, LANES), 1)
    scale = B_HEAD_DIM ** -0.5
    for j in range(B_HEADS // 2):
        sl = slice(j * LANES, (j + 1) * LANES)
        qs, ks, vs = q[:, sl], kw[:, sl], vw[:, sl]
        outs = []
        for hh in range(2):
            mine = (lane >= B_HEAD_DIM) if hh else (lane < B_HEAD_DIM)
            qm = jnp.where(mine, qs, jnp.zeros_like(qs))
            s = pl.dot(qm, ks, trans_b=True) * scale + bias_ref[2 * j + hh]
            s = jnp.where(kvalid, s, -jnp.inf)
            m = jnp.max(s, axis=-1, keepdims=True)
            p = jnp.exp(s - m)
            l = jnp.sum(p, axis=-1, keepdims=True)
            outs.append(jnp.dot(p.astype(BF16), vs, preferred_element_type=F32) / l)
        yb_ref[0, :, sl] = jnp.where(lane < B_HEAD_DIM, outs[0], outs[1]).astype(BF16)


def _band_bias(rel_bias):
    tq = np.arange(B_QTILE)[:, None]
    j = np.arange(B_WIN)[None, :]
    band = (j // CHUNK >= tq // CHUNK) & (j // CHUNK <= tq // CHUNK + B_LEFT_CHUNKS)
    length = B_QTILE - 1 + B_WIN
    m = np.arange(length)
    line = rel_bias[:, np.clip(B_WIN - 1 - m, -REL_CLIP, REL_CLIP) + REL_CLIP].astype(F32)
    rolled = jnp.tile(line, (1, B_QTILE + 1))[:, :B_QTILE * (length + 1)].reshape(
        line.shape[0], B_QTILE, length + 1)
    table = rolled[:, ::-1, :B_WIN]
    return jnp.where(jnp.asarray(band)[None], table, -jnp.inf)


def _mixer_b(qb, kb_pad, vb_pad, bias, bsz, seq):
    return pl.pallas_call(
        _mixer_b_kernel,
        out_shape=jax.ShapeDtypeStruct((bsz, seq, B_WIDTH), BF16),
        grid=(bsz, seq // B_QTILE),
        in_specs=[
            pl.BlockSpec((1, B_QTILE, B_WIDTH), lambda b, i: (b, i, 0)),
            pl.BlockSpec((1, seq + B_PAD, B_WIDTH), lambda b, i: (b, 0, 0)),
            pl.BlockSpec((1, seq + B_PAD, B_WIDTH), lambda b, i: (b, 0, 0)),
            pl.BlockSpec(bias.shape, lambda b, i: (0, 0, 0), pipeline_mode=pl.Buffered(1)),
        ],
        out_specs=pl.BlockSpec((1, B_QTILE, B_WIDTH), lambda b, i: (b, i, 0)),
        compiler_params=_params(("parallel", "arbitrary")),
        name="mixer_b",
    )(qb, kb_pad, vb_pad, bias)


def _layer_norm(z, g, b):
    mu = jnp.mean(z, axis=-1, keepdims=True)
    var = jnp.mean(jnp.square(z - mu), axis=-1, keepdims=True)
    return ((z - mu) * lax.rsqrt(var + LN_EPS)) * g + b


def _merge_kernel(x_ref, ya_ref, yb_ref, wg_ref, wba_ref, wbb_ref, wo_ref, g_ref, b_ref,
                  wr_ref, br_ref, x1t_ref, topi_ref, gate_ref):
    tm = x_ref.shape[0]
    x = x_ref[...]
    xb = x.astype(BF16)
    ga = jnp.dot(xb, wg_ref[:, :D_MODEL], preferred_element_type=F32)
    a = jnp.dot(ya_ref[...], wba_ref[...], preferred_element_type=F32)
    merged = jax.nn.sigmoid(ga) * a
    gb = jnp.dot(xb, wg_ref[:, D_MODEL:], preferred_element_type=F32)
    b = jnp.dot(yb_ref[...], wbb_ref[...], preferred_element_type=F32)
    merged = merged + jax.nn.sigmoid(gb) * b
    out = jnp.dot(merged.astype(BF16), wo_ref[...], preferred_element_type=F32)
    x1 = _layer_norm(DN_ALPHA * x + out, g_ref[...], b_ref[...])
    for c in range(ROW_TILES):
        x1t_ref[pl.ds(c, tm, stride=ROW_TILES), :] = x1[:, c * LANES:(c + 1) * LANES]

    lg = pl.dot(wr_ref[...], x1.astype(BF16), trans_b=True) + br_ref[...]
    eidx = lax.broadcasted_iota(I32, lg.shape, 0)
    vals, idxs = [], []
    for _ in range(TOP_K):
        m = jnp.max(lg, axis=0, keepdims=True)
        sel = jnp.min(jnp.where(lg == m, eidx, N_EXPERTS), axis=0, keepdims=True)
        vals.append(m)
        idxs.append(sel)
        lg = jnp.where(eidx == sel, -jnp.inf, lg)
    es = [jnp.exp(v - vals[0]) for v in vals]
    tot = es[0] + es[1] + es[2] + es[3]
    topi_ref[...] = jnp.concatenate(idxs, axis=0)
    gate_ref[...] = jnp.concatenate([e / tot for e in es], axis=0)


def _merge(xf, ya, yb, wg, wba, wbb, wo, g1, b1, wr_t, br):
    n, tm = xf.shape[0], min(MATMUL_TILE, xf.shape[0])
    row = lambda w: pl.BlockSpec((tm, w), lambda i: (i, 0))
    col = pl.BlockSpec((TOP_K, tm), lambda i: (0, i))
    return pl.pallas_call(
        _merge_kernel,
        out_shape=[jax.ShapeDtypeStruct((n * ROW_TILES, LANES), F32),
                   jax.ShapeDtypeStruct((TOP_K, n), I32),
                   jax.ShapeDtypeStruct((TOP_K, n), F32)],
        grid=(n // tm,),
        in_specs=[row(D_MODEL), row(A_WIDTH), row(B_WIDTH), _full(wg), _full(wba), _full(wbb),
                  _full(wo), _full(g1), _full(b1), _full(wr_t), _full(br)],
        out_specs=[pl.BlockSpec((tm * ROW_TILES, LANES), lambda i: (i, 0)), col, col],
        compiler_params=_params(("parallel",)),
        name="merge",
    )(xf, ya, yb, wg, wba, wbb, wo, g1, b1, wr_t, br)


def _one_hot_rows(topi, k, width):
    eidx = lax.broadcasted_iota(I32, (N_EXPERTS, width), 0)
    return eidx == topi[k:k + 1, :]


def _route_kernel(topi_ref, tri_ref, rank_ref, cnt_ref, run_ref):
    @pl.when(pl.program_id(0) == 0)
    def _():
        run_ref[...] = jnp.zeros_like(run_ref)

    topi = topi_ref[...]
    tr = topi.shape[1]
    hot = [_one_hot_rows(topi, k, tr) for k in range(TOP_K)]
    oh = jnp.concatenate([jnp.where(h, 1.0, 0.0).astype(BF16) for h in hot], axis=0)
    prefix = jnp.dot(oh, tri_ref[...], preferred_element_type=F32)
    base = run_ref[...]
    ranks = []
    for k in range(TOP_K):
        tbl = prefix[k * N_EXPERTS:(k + 1) * N_EXPERTS, :] + base
        ranks.append(jnp.sum(jnp.where(hot[k], tbl, 0.0), axis=0, keepdims=True))
        base = base + jnp.sum(jnp.where(hot[k], 1.0, 0.0), axis=1, keepdims=True)
    rank_ref[...] = jnp.concatenate(ranks, axis=0).astype(I32)
    run_ref[...] = base
    cnt_ref[...] = base.astype(I32)


def _route(topi):
    n, tr = topi.shape[1], TOKEN_TILE
    tri = (np.arange(tr)[:, None] < np.arange(tr)[None, :]).astype(np.float32)
    tri = jnp.asarray(tri, BF16)
    col = pl.BlockSpec((TOP_K, tr), lambda i: (0, i))
    return pl.pallas_call(
        _route_kernel,
        out_shape=[jax.ShapeDtypeStruct((TOP_K, n), I32), jax.ShapeDtypeStruct((N_EXPERTS, 1), I32)],
        grid=(n // tr,),
        in_specs=[col, _full(tri)],
        out_specs=[col, pl.BlockSpec((N_EXPERTS, 1), lambda i: (0, 0))],
        scratch_shapes=[pltpu.VMEM((N_EXPERTS, 1), F32)],
        compiler_params=_params(("arbitrary",)),
        name="route",
    )(topi, tri)


def _positions_kernel(topi_ref, rank_ref, pstv_ref, pos_ref):
    topi = topi_ref[...]
    rows = []
    for k in range(TOP_K):
        start = jnp.sum(jnp.where(_one_hot_rows(topi, k, topi.shape[1]), pstv_ref[...], 0),
                        axis=0, keepdims=True)
        rows.append(rank_ref[k:k + 1, :] + start)
    pos_ref[...] = jnp.concatenate(rows, axis=0)


def _positions(topi, rank, pstarts):
    n, td = topi.shape[1], TOKEN_TILE
    col = pl.BlockSpec((TOP_K, td), lambda i: (0, i))
    pstv = pstarts.reshape(N_EXPERTS, 1)
    return pl.pallas_call(
        _positions_kernel,
        out_shape=jax.ShapeDtypeStruct((TOP_K, n), I32),
        grid=(n // td,),
        in_specs=[col, col, _full(pstv)],
        out_specs=col,
        compiler_params=_params(("parallel",)),
        name="positions",
    )(topi, rank, pstv)


def _dispatch_kernel(cnt_ref, pst_ref, pos_hbm, x_hbm, xs_hbm,
                     pos_smem, xbuf, zbuf, sem_p, sem_x, sem_d, sem_z):
    i = pl.program_id(0)
    n = pl.num_programs(0)
    ng = pos_smem.shape[1]
    td = ng * DMA_GROUP
    tile_rows = td * ROW_TILES

    def pos_copy(step, slot):
        return pltpu.make_async_copy(pos_hbm.at[pl.ds(step * ng, ng)], pos_smem.at[slot], sem_p.at[slot])

    def x_copy(step, slot):
        return pltpu.make_async_copy(x_hbm.at[pl.ds(step * tile_rows, tile_rows)], xbuf.at[slot],
                                     sem_x.at[slot])

    @pl.when(i == 0)
    def _():
        pos_copy(0, 0).start()
        x_copy(0, 0).start()

    @pl.when(i + 1 < n)
    def _():
        pos_copy(i + 1, (i + 1) % 2).start()
        x_copy(i + 1, (i + 1) % 3).start()

    slot = i % 2
    xslot = i % 3
    pos_copy(i, slot).wait()
    x_copy(i, xslot).wait()

    def step_bytes(s):
        rows_ = TOP_K * tile_rows
        return pltpu.make_async_copy(xs_hbm.at[pl.ds(0, rows_)], xs_hbm.at[pl.ds(0, rows_)], sem_d.at[s])

    def body(g, c):
        dst = [pos_smem[slot, g, j] * ROW_TILES for j in range(DMA_GROUP * TOP_K)]
        for u in range(DMA_GROUP):
            src = xbuf.at[xslot, pl.ds((g * DMA_GROUP + u) * ROW_TILES, ROW_TILES)]
            for k in range(TOP_K):
                pltpu.make_async_copy(src, xs_hbm.at[pl.ds(dst[u * TOP_K + k], ROW_TILES)],
                                      sem_d.at[xslot]).start(priority=k % 2)
        return c
    lax.fori_loop(0, ng, body, 0)

    @pl.when(i > 0)
    def _():
        step_bytes((i + 2) % 3).wait()

    @pl.when(i == n - 1)
    def _():
        step_bytes(xslot).wait()
        zbuf[...] = jnp.zeros_like(zbuf)

        def zero_copy(r):
            return pltpu.make_async_copy(zbuf, xs_hbm.at[pl.ds(r * ROW_TILES, ROW_TILES)], sem_z)

        def zero_rows(lo, hi):
            def start(r, c2):
                zero_copy(r).start()
                return c2

            def wait(r, c2):
                zero_copy(r).wait()
                return c2
            lax.fori_loop(lo, hi, start, 0)
            lax.fori_loop(lo, hi, wait, 0)

        def padded_end(e):
            return pst_ref[e] + (cnt_ref[e] + MOE_BLOCK - 1) // MOE_BLOCK * MOE_BLOCK

        def ebody(e, c):
            zero_rows(pst_ref[e] + cnt_ref[e], padded_end(e))
            return c
        lax.fori_loop(0, N_EXPERTS, ebody, 0)
        zero_rows(padded_end(N_EXPERTS - 1), xs_hbm.shape[0] // ROW_TILES)


def _dispatch(counts, pstarts, pos_g, x1t, cap):
    n, td = pos_g.shape[0] * DMA_GROUP, TOKEN_TILE
    any_spec = pl.BlockSpec(memory_space=pl.ANY)
    return pl.pallas_call(
        _dispatch_kernel,
        out_shape=jax.ShapeDtypeStruct((cap * ROW_TILES, LANES), F32),
        grid_spec=pltpu.PrefetchScalarGridSpec(
            num_scalar_prefetch=2,
            grid=(n // td,),
            in_specs=[any_spec, any_spec],
            out_specs=any_spec,
            scratch_shapes=[
                pltpu.SMEM((2, td // DMA_GROUP, DMA_GROUP * TOP_K), I32),
                pltpu.VMEM((3, td * ROW_TILES, LANES), F32),
                pltpu.VMEM((ROW_TILES, LANES), F32),
                pltpu.SemaphoreType.DMA((2,)),
                pltpu.SemaphoreType.DMA((3,)),
                pltpu.SemaphoreType.DMA((3,)),
                pltpu.SemaphoreType.DMA,
            ],
        ),
        compiler_params=_params(("arbitrary",)),
        name="dispatch",
    )(counts, pstarts, pos_g, x1t)


def _moe_kernel(be_ref, nu_ref, xs_ref, wg_ref, bg_ref, wu_ref, bu_ref, wd_ref, bd_ref, ys_ref,
                wg_bf, wu_bf, wd_bf):
    i = pl.program_id(0)
    blk = MOE_BLOCK
    used = i < nu_ref[0]

    @pl.when(jnp.logical_or(i == 0, be_ref[i] != be_ref[jnp.maximum(i - 1, 0)]))
    def _():
        wg_bf[...] = wg_ref[0].astype(BF16)
        wu_bf[...] = wu_ref[0].astype(BF16)
        wd_bf[...] = wd_ref[0].astype(BF16)

    @pl.when(used)
    def _():
        xb = jnp.concatenate([xs_ref[pl.ds(c, blk, stride=ROW_TILES), :].astype(BF16)
                              for c in range(ROW_TILES)], axis=1)
        hg = jnp.minimum(jnp.dot(xb, wg_bf[...], preferred_element_type=F32) + bg_ref[0], SWIGLU_LIMIT)
        hu = jnp.clip(jnp.dot(xb, wu_bf[...], preferred_element_type=F32) + bu_ref[0],
                      -SWIGLU_LIMIT, SWIGLU_LIMIT)
        h = (hu + 1.0) * (hg * jax.nn.sigmoid(SWIGLU_ALPHA * hg))
        y = jnp.dot(h.astype(BF16), wd_bf[...], preferred_element_type=F32) + bd_ref[0]
        for c in range(ROW_TILES):
            ys_ref[pl.ds(c, blk, stride=ROW_TILES), :] = y[:, c * LANES:(c + 1) * LANES]

    @pl.when(jnp.logical_not(used))
    def _():
        ys_ref[...] = jnp.zeros_like(ys_ref)


def _moe(blk_expert, nused, xs, wg, bg, wu, bu, wd, bd):
    blk = MOE_BLOCK
    nblk = xs.shape[0] // (blk * ROW_TILES)
    wspec = lambda a: pl.BlockSpec((1,) + a.shape[1:], lambda i, be, nu: (be[i], 0, 0))
    xspec = pl.BlockSpec((blk * ROW_TILES, LANES), lambda i, be, nu: (jnp.minimum(i, nu[0] - 1), 0))
    return pl.pallas_call(
        _moe_kernel,
        out_shape=jax.ShapeDtypeStruct(xs.shape, F32),
        grid_spec=pltpu.PrefetchScalarGridSpec(
            num_scalar_prefetch=2,
            grid=(nblk,),
            in_specs=[xspec, wspec(wg), wspec(bg), wspec(wu), wspec(bu), wspec(wd), wspec(bd)],
            out_specs=pl.BlockSpec((blk * ROW_TILES, LANES), lambda i, be, nu: (i, 0)),
            scratch_shapes=[pltpu.VMEM(wg.shape[1:], BF16), pltpu.VMEM(wu.shape[1:], BF16),
                            pltpu.VMEM(wd.shape[1:], BF16)],
        ),
        compiler_params=_params(("arbitrary",)),
        name="moe",
    )(blk_expert, nused, xs, wg, bg, wu, bu, wd, bd)


def _final_kernel(pos_hbm, gt_ref, x1t_ref, ys_hbm, g_ref, b_ref, o_ref,
                  pos_smem, ybuf, sem_p, sem_g):
    i = pl.program_id(0)
    n = pl.num_programs(0)
    ft = FINAL_TILE
    slot_rows = TOP_K * ft * ROW_TILES

    ng = ft // DMA_GROUP

    def pos_copy(tile, slot):
        return pltpu.make_async_copy(pos_hbm.at[pl.ds(tile * ng, ng)], pos_smem.at[slot], sem_p.at[slot])

    def gather_rows(pslot, yslot):
        def body(g, c):
            src = [pos_smem[pslot, g, j] * ROW_TILES for j in range(DMA_GROUP * TOP_K)]
            base = yslot * slot_rows + g * (DMA_GROUP * ROW_TILES)
            for u in range(DMA_GROUP):
                for k in range(TOP_K):
                    dst = base + (k * ft + u) * ROW_TILES
                    pltpu.make_async_copy(ys_hbm.at[pl.ds(src[u * TOP_K + k], ROW_TILES)],
                                          ybuf.at[pl.ds(dst, ROW_TILES)],
                                          sem_g.at[yslot]).start(priority=k % 2)
            return c
        lax.fori_loop(0, ng, body, 0)

    def gather_wait(yslot):
        pltpu.make_async_copy(ys_hbm.at[pl.ds(0, slot_rows)],
                              ybuf.at[pl.ds(yslot * slot_rows, slot_rows)], sem_g.at[yslot]).wait()

    @pl.when(i == 0)
    def _():
        c = pos_copy(0, 0)
        c.start()
        c.wait()
        gather_rows(0, 0)

        @pl.when(n > 1)
        def _():
            pos_copy(1, 1).start()

    @pl.when(i + 1 < n)
    def _():
        pos_copy(i + 1, (i + 1) % 3).wait()
        gather_rows((i + 1) % 3, (i + 1) % 2)

    @pl.when(i + 2 < n)
    def _():
        pos_copy(i + 2, (i + 2) % 3).start()

    slot = i % 2
    gather_wait(slot)
    g = gt_ref[...]
    zs = []
    for c in range(ROW_TILES):
        f = None
        for k in range(TOP_K):
            yk = ybuf[pl.ds(slot * slot_rows + k * ft * ROW_TILES + c, ft, stride=ROW_TILES), :]
            term = g[:, k:k + 1] * yk
            f = term if f is None else f + term
        zs.append(DN_ALPHA * x1t_ref[pl.ds(c, ft, stride=ROW_TILES), :] + f)
    o_ref[...] = _layer_norm(jnp.concatenate(zs, axis=1), g_ref[...], b_ref[...])


def _final(pos, gates_t, x1t, ys, g2, b2):
    ft = FINAL_TILE
    n = gates_t.shape[0]
    any_spec = pl.BlockSpec(memory_space=pl.ANY)
    return pl.pallas_call(
        _final_kernel,
        out_shape=jax.ShapeDtypeStruct((n, D_MODEL), F32),
        grid=(n // ft,),
        in_specs=[any_spec, pl.BlockSpec((ft, TOP_K), lambda i: (i, 0)),
                  pl.BlockSpec((ft * ROW_TILES, LANES), lambda i: (i, 0)), any_spec,
                  _full(g2), _full(b2)],
        out_specs=pl.BlockSpec((ft, D_MODEL), lambda i: (i, 0)),
        scratch_shapes=[
            pltpu.SMEM((3, ft // DMA_GROUP, DMA_GROUP * TOP_K), I32),
            pltpu.VMEM((2 * TOP_K * ft * ROW_TILES, LANES), F32),
            pltpu.SemaphoreType.DMA((3,)),
            pltpu.SemaphoreType.DMA((2,)),
        ],
        compiler_params=_params(("arbitrary",)),
        name="final",
    )(pos, gates_t, x1t, ys, g2, b2)


def _mix_weight(w_in):
    qa, ckv, qi, ki, wi, qb, kb, vb, _, _ = jnp.split(w_in, np.cumsum(SPLITS)[:-1].tolist(), axis=-1)
    pad = lambda a: jnp.pad(a, ((0, 0), (0, LANES - a.shape[1])))
    return jnp.concatenate([qa, qi, ckv, pad(ki), pad(wi), qb, kb, vb], axis=1).astype(BF16)


def _layer(x, w_in, kv_norm_g, idx_k_norm_g, idx_k_norm_b, w_uk, w_uv, rel_bias, w_branch_a,
           w_branch_b, w_out, ln1_g, ln1_b, w_router, b_router, w_gate, b_gate, w_up, b_up,
           w_down, b_down, ln2_g, ln2_b):
    bsz, seq, _ = x.shape
    n = bsz * seq
    assert seq % KEY_TILE == 0 and n % TOKEN_TILE == 0
    xf = x.reshape(n, D_MODEL)
    row = lambda v: v.reshape(1, -1).astype(F32)

    w_mix = _mix_weight(w_in)
    w_gates = w_in[:, sum(SPLITS[:8]):].astype(BF16)
    wuk_bd = jnp.zeros((A_WIDTH, A_HEADS * A_KV_RANK), F32)
    wuv_pad = jnp.zeros((A_HEADS, A_KV_RANK, LANES), F32)
    for h in range(A_HEADS):
        wuk_bd = wuk_bd.at[h * A_HEAD_DIM:(h + 1) * A_HEAD_DIM, h * A_KV_RANK:(h + 1) * A_KV_RANK].set(w_uk[h])
        c0 = (h % 2) * A_HEAD_DIM
        wuv_pad = wuv_pad.at[h, :, c0:c0 + A_HEAD_DIM].set(w_uv[h])
    wuk_bd, wuv_pad = wuk_bd.astype(BF16), wuv_pad.astype(BF16)

    qlat, qi, ckv, ki, wi, qb, kb, vb = _projection(
        xf, w_mix, wuk_bd, row(kv_norm_g), row(idx_k_norm_g), row(idx_k_norm_b))

    nqb, nkt = seq // Q_BLOCK, seq // KEY_TILE
    kidx = ki.reshape(bsz, nkt, KEY_TILE, IDX_DIM)
    ckv4 = ckv.reshape(bsz, nkt, KEY_TILE, A_KV_RANK)
    ckvt = ckv4.transpose(0, 1, 3, 2)
    qi_r = qi.reshape(bsz, nqb, Q_BLOCK, IDX_HEADS, IDX_DIM).transpose(0, 1, 3, 2, 4).reshape(
        bsz, nqb, IDX_HEADS * Q_BLOCK, IDX_DIM)
    wit = wi.reshape(bsz, seq, IDX_HEADS).transpose(0, 2, 1)
    ya = _mixer_a(kidx, ckv4, ckvt, qi_r, wit, qlat.reshape(bsz, seq, -1), wuv_pad, bsz, seq)

    front = lambda a: jnp.pad(a.reshape(bsz, seq, B_WIDTH), ((0, 0), (B_PAD, 0), (0, 0)))
    yb = _mixer_b(qb.reshape(bsz, seq, B_WIDTH), front(kb), front(vb), _band_bias(rel_bias), bsz, seq)

    x1t, topi, gates = _merge(
        xf, ya.reshape(n, A_WIDTH), yb.reshape(n, B_WIDTH), w_gates, w_branch_a.astype(BF16),
        w_branch_b.astype(BF16), w_out.astype(BF16), row(ln1_g), row(ln1_b),
        w_router.T.astype(BF16), b_router.reshape(-1, 1).astype(F32))

    blk = MOE_BLOCK
    cap = n * TOP_K + N_EXPERTS * blk
    nblk = cap // blk
    rank, cnt = _route(topi)
    counts = cnt[:, 0]
    padded = (counts + blk - 1) // blk * blk
    pends = jnp.cumsum(padded).astype(I32)
    pstarts = pends - padded
    nused = (pends[-1:] // blk).astype(I32)
    blk_start = jnp.arange(nblk, dtype=I32) * blk
    blk_expert = jnp.minimum(jnp.sum(blk_start[:, None] >= pends[None, :], axis=1), N_EXPERTS - 1).astype(I32)

    pos = _positions(topi, rank, pstarts)
    pos_g = pos.T.reshape(n // DMA_GROUP, DMA_GROUP * TOP_K)
    xs = _dispatch(counts, pstarts, pos_g, x1t, cap)
    ys = _moe(blk_expert, nused, xs, w_gate, b_gate[:, None, :], w_up, b_up[:, None, :],
              w_down, b_down[:, None, :])
    out = _final(pos_g, gates.T, x1t, ys, row(ln2_g), row(ln2_b))
    return out.reshape(bsz, seq, D_MODEL)


def kernel(x, w_in, kv_norm_g, idx_k_norm_g, idx_k_norm_b, w_uk, w_uv, rel_bias, w_branch_a,
           w_branch_b, w_out, ln1_g, ln1_b, w_router, b_router, w_gate, b_gate, w_up, b_up,
           w_down, b_down, ln2_g, ln2_b):
    for l in range(DEPTH):
        x = _layer(x, w_in[l], kv_norm_g[l], idx_k_norm_g[l], idx_k_norm_b[l], w_uk[l], w_uv[l],
                   rel_bias[l], w_branch_a[l], w_branch_b[l], w_out[l], ln1_g[l], ln1_b[l],
                   w_router[l], b_router[l], w_gate[l], b_gate[l], w_up[l], b_up[l], w_down[l],
                   b_down[l], ln2_g[l], ln2_b[l])
    return x
```

```python
import functools

import jax
import jax.numpy as jnp
import numpy as np
from jax import lax
from jax.experimental import pallas as pl
from jax.experimental.pallas import tpu as pltpu

F32 = jnp.float32
BF16 = jnp.bfloat16
I32 = jnp.int32

D_MODEL = 1024
CHUNK = 64
Q_BLOCK = 128
A_HEADS = 8
A_HEAD_DIM = 64
A_KV_RANK = 128
IDX_HEADS = 8
IDX_DIM = 64
IDX_TOPK = 256
IDX_SCALE = (IDX_HEADS * IDX_DIM) ** -0.5
B_HEADS = 8
B_HEAD_DIM = 64
B_LEFT_CHUNKS = 8
REL_CLIP = 128
N_EXPERTS = 32
TOP_K = 4
D_FF = 1024
SWIGLU_LIMIT = 7.0
SWIGLU_ALPHA = 1.702
DEPTH = 1
DN_ALPHA = (2.0 * DEPTH) ** 0.25
LN_EPS = 1e-5

A_WIDTH = A_HEADS * A_HEAD_DIM
B_WIDTH = B_HEADS * B_HEAD_DIM
SPLITS = [A_WIDTH, A_KV_RANK, IDX_HEADS * IDX_DIM, IDX_DIM, IDX_HEADS,
          B_WIDTH, B_WIDTH, B_WIDTH, D_MODEL, D_MODEL]

LANES = 128
SUBLANES = 8
KEY_TILE = 256
B_QTILE = 256
B_PAD = B_LEFT_CHUNKS * CHUNK
B_WIN = B_PAD + B_QTILE
TOKEN_TILE = 512
MATMUL_TILE = 1024
MOE_BLOCK = 512
FINAL_TILE = 512
ROW_TILES = D_MODEL // LANES
DMA_GROUP = 4
INT_MIN = -2 ** 31
VMEM_LIMIT = 56 * 1024 * 1024

A_SLOPES = [2.0 ** (-8.0 * (h + 1) / A_HEADS) for h in range(A_HEADS)]

_C_QA = 0
_C_QI = _C_QA + A_WIDTH
_C_CKV = _C_QI + IDX_HEADS * IDX_DIM
_C_KI = _C_CKV + A_KV_RANK
_C_WI = _C_KI + LANES
_C_QKVB = _C_WI + LANES
_C_END = _C_QKVB + 3 * B_WIDTH


def _params(sem, vmem=VMEM_LIMIT):
    return pltpu.CompilerParams(dimension_semantics=sem, vmem_limit_bytes=vmem)


def _full(a):
    return pl.BlockSpec(a.shape, lambda *_: (0,) * a.ndim)


def _proj_kernel(x_ref, w_ref, wuk_ref, kvg_ref, kig_ref, kib_ref,
                 qlat_ref, qi_ref, ckv_ref, ki_ref, wi_ref, qb_ref, kb_ref, vb_ref):
    xb = x_ref[...].astype(BF16)

    def mm(c0, c1):
        return jnp.dot(xb, w_ref[:, c0:c1], preferred_element_type=F32)

    qa = mm(_C_QA, _C_QI)
    qlat = jnp.dot(qa.astype(BF16), wuk_ref[...], preferred_element_type=F32)
    qlat_ref[...] = (qlat * (A_HEAD_DIM ** -0.5)).astype(BF16)
    qi_ref[...] = mm(_C_QI, _C_CKV).astype(BF16)

    ckv = mm(_C_CKV, _C_KI)
    ms = jnp.mean(ckv * ckv, axis=-1, keepdims=True)
    ckv_ref[...] = ((ckv * lax.rsqrt(ms + LN_EPS)) * kvg_ref[...]).astype(BF16)

    ki = mm(_C_KI, _C_WI)[:, :IDX_DIM]
    mu = jnp.mean(ki, axis=-1, keepdims=True)
    var = jnp.mean(jnp.square(ki - mu), axis=-1, keepdims=True)
    ki_ref[...] = (((ki - mu) * lax.rsqrt(var + LN_EPS)) * kig_ref[...] + kib_ref[...]).astype(BF16)

    wi_ref[...] = mm(_C_WI, _C_QKVB)[:, :IDX_HEADS]
    qb_ref[...] = mm(_C_QKVB, _C_QKVB + B_WIDTH).astype(BF16)
    kb_ref[...] = mm(_C_QKVB + B_WIDTH, _C_QKVB + 2 * B_WIDTH).astype(BF16)
    vb_ref[...] = mm(_C_QKVB + 2 * B_WIDTH, _C_END).astype(BF16)


def _projection(xf, w_mix, wuk_bd, kvg, kig, kib):
    n, tm = xf.shape[0], min(MATMUL_TILE, xf.shape[0])
    row = lambda w: pl.BlockSpec((tm, w), lambda i: (i, 0))
    outs = [(A_HEADS * A_KV_RANK, BF16), (IDX_HEADS * IDX_DIM, BF16), (A_KV_RANK, BF16),
            (IDX_DIM, BF16), (IDX_HEADS, F32), (B_WIDTH, BF16), (B_WIDTH, BF16), (B_WIDTH, BF16)]
    return pl.pallas_call(
        _proj_kernel,
        out_shape=[jax.ShapeDtypeStruct((n, w), dt) for w, dt in outs],
        grid=(n // tm,),
        in_specs=[row(D_MODEL), _full(w_mix), _full(wuk_bd), _full(kvg), _full(kig), _full(kib)],
        out_specs=[row(w) for w, _ in outs],
        compiler_params=_params(("parallel",)),
        name="proj",
    )(xf, w_mix, wuk_bd, kvg, kig, kib)


def _mixer_a_kernel(kidx_ref, ckv_ref, ckvt_ref, qi_ref, wit_ref, qlat_ref, wuv_ref,
                    ya_ref, keys_ref, s_ref, acc_ref, *, seq, topk):
    i = pl.program_id(1)
    nkt = i // 2 + 1
    nsub = KEY_TILE // SUBLANES
    lane = lax.broadcasted_iota(I32, (1, LANES), 1)
    qpos = i * Q_BLOCK + lane
    qchunk = qpos >> 6
    row_iota = lax.broadcasted_iota(I32, (KEY_TILE, LANES), 0)

    qi_blk = qi_ref[0, 0]
    w_all = wit_ref[0] * IDX_SCALE

    def idx_body(kt, carry):
        lg = pl.dot(kidx_ref[0, kt], qi_blk, trans_b=True)
        isc = jnp.zeros((KEY_TILE, LANES), F32)
        for h in range(IDX_HEADS):
            isc = isc + jnp.maximum(lg[:, h * LANES:(h + 1) * LANES], 0.0) * w_all[h:h + 1, :]
        bits = lax.bitcast_convert_type(isc + 0.0, I32)
        key = bits ^ ((bits >> 31) & 0x7FFFFFFF)
        spos = kt * KEY_TILE + row_iota
        key = jnp.where((spos >> 6) <= qchunk, key, INT_MIN)
        keys_ref[kt] = key
        return carry

    def over_tiles(body, init):
        def pair(j, c):
            return body(2 * j + 1, body(2 * j, c))
        c = lax.fori_loop(0, nkt // 2, pair, init)
        return lax.cond(nkt % 2 == 1, lambda c: body(nkt - 1, c), lambda c: c, c)

    over_tiles(idx_body, 0)

    def count(pred):
        def body(kt, acc):
            spos = kt * KEY_TILE + row_iota
            m = jnp.where(pred(keys_ref[kt], spos), 1, 0).astype(I32)
            return acc + jnp.sum(m.reshape(nsub, SUBLANES, LANES), axis=0)
        acc = over_tiles(body, jnp.zeros((SUBLANES, LANES), I32))
        return jnp.sum(acc, axis=0, keepdims=True)

    def bit_body(it, carry):
        thr, cge = carry
        cand = thr ^ lax.shift_left(jnp.int32(1), 31 - it)
        cnt = count(lambda kv, spos: kv >= cand)
        ok = cnt >= topk
        return jnp.where(ok, cand, thr), jnp.where(ok, cnt, cge)

    start = (jnp.full((1, LANES), INT_MIN, I32), jnp.zeros((1, LANES), I32))
    thr, cge = lax.cond(nkt * KEY_TILE > topk,
                        lambda: lax.fori_loop(0, 32, bit_body, start), lambda: start)

    nbits = int(np.log2(seq)) + 1
    has_thr = thr > INT_MIN
    need_tie = jnp.max(jnp.where(has_thr, cge, 0)) > topk

    def tie_fn():
        room = topk - count(lambda kv, spos: kv > thr)

        def jb(it, cut):
            cand = cut | lax.shift_left(jnp.int32(1), nbits - 1 - it)
            f = count(lambda kv, spos: (kv == thr) & (spos < cand))
            return jnp.where(f <= room, cand, cut)
        return lax.fori_loop(0, nbits, jb, jnp.zeros((1, LANES), I32))

    cut = lax.cond(need_tie, tie_fn, lambda: jnp.full((1, LANES), 2 * seq, I32))
    cut = jnp.where(has_thr, cut, 0)

    qlat = qlat_ref[0]
    qr = jnp.concatenate([qlat[:, h * LANES:(h + 1) * LANES] for h in range(A_HEADS)], axis=0)

    def sc_body(kt, m):
        st = pl.dot(ckv_ref[0, kt], qr, trans_b=True)
        kv = keys_ref[kt]
        spos = kt * KEY_TILE + row_iota
        sel = (kv > thr) | ((kv == thr) & (spos < cut))
        dist = jnp.abs(qpos - spos).astype(F32)
        out = []
        for h in range(A_HEADS):
            sh = st[:, h * LANES:(h + 1) * LANES] - A_SLOPES[h] * dist
            sh = jnp.where(sel, sh, -jnp.inf)
            s_ref[kt, :, h * LANES:(h + 1) * LANES] = sh
            out.append(jnp.maximum(m[h], jnp.max(sh.reshape(nsub, SUBLANES, LANES), axis=0)))
        return tuple(out)

    m8 = over_tiles(sc_body, tuple(jnp.full((SUBLANES, LANES), -jnp.inf, F32) for _ in range(A_HEADS)))
    mx = [jnp.max(m, axis=0, keepdims=True) for m in m8]

    acc_ref[...] = jnp.zeros_like(acc_ref)

    def pv_body(kt, l):
        out, ps = [], []
        for h in range(A_HEADS):
            p = jnp.exp(s_ref[kt, :, h * LANES:(h + 1) * LANES] - mx[h])
            out.append(l[h] + jnp.sum(p.reshape(nsub, SUBLANES, LANES), axis=0))
            ps.append(p.astype(BF16))
        acc_ref[...] += jnp.dot(ckvt_ref[0, kt], jnp.concatenate(ps, axis=1),
                                preferred_element_type=F32)
        return tuple(out)

    l8 = over_tiles(pv_body, tuple(jnp.zeros((SUBLANES, LANES), F32) for _ in range(A_HEADS)))
    inv = [1.0 / jnp.sum(l, axis=0, keepdims=True) for l in l8]

    for j in range(A_HEADS // 2):
        y = None
        for h in (2 * j, 2 * j + 1):
            o = (acc_ref[:, h * LANES:(h + 1) * LANES] * inv[h]).astype(BF16)
            t = pl.dot(o, wuv_ref[h], trans_a=True)
            y = t if y is None else y + t
        ya_ref[0, :, j * LANES:(j + 1) * LANES] = y.astype(BF16)


def _mixer_a(kidx, ckv, ckvt, qi_r, wit, qlat, wuv_pad, bsz, seq):
    nqb = seq // Q_BLOCK
    nkt = seq // KEY_TILE
    topk = min(IDX_TOPK, seq // 4)
    kern = functools.partial(_mixer_a_kernel, seq=seq, topk=topk)
    return pl.pallas_call(
        kern,
        out_shape=jax.ShapeDtypeStruct((bsz, seq, A_WIDTH), BF16),
        grid=(bsz, nqb),
        in_specs=[
            pl.BlockSpec((1, nkt, KEY_TILE, IDX_DIM), lambda b, i: (b, 0, 0, 0)),
            pl.BlockSpec((1, nkt, KEY_TILE, A_KV_RANK), lambda b, i: (b, 0, 0, 0)),
            pl.BlockSpec((1, nkt, A_KV_RANK, KEY_TILE), lambda b, i: (b, 0, 0, 0)),
            pl.BlockSpec((1, 1, IDX_HEADS * Q_BLOCK, IDX_DIM), lambda b, i: (b, i, 0, 0)),
            pl.BlockSpec((1, IDX_HEADS, Q_BLOCK), lambda b, i: (b, 0, i)),
            pl.BlockSpec((1, Q_BLOCK, A_HEADS * A_KV_RANK), lambda b, i: (b, i, 0)),
            pl.BlockSpec(wuv_pad.shape, lambda b, i: (0, 0, 0)),
        ],
        out_specs=pl.BlockSpec((1, Q_BLOCK, A_WIDTH), lambda b, i: (b, i, 0)),
        scratch_shapes=[
            pltpu.VMEM((nkt, KEY_TILE, LANES), I32),
            pltpu.VMEM((nkt, KEY_TILE, A_HEADS * LANES), F32),
            pltpu.VMEM((A_KV_RANK, A_HEADS * LANES), F32),
        ],
        compiler_params=_params(("parallel", "arbitrary")),
        name="mixer_a",
    )(kidx, ckv, ckvt, qi_r, wit, qlat, wuv_pad)


def _mixer_b_kernel(q_ref, k_ref, v_ref, bias_ref, yb_ref):
    t0 = pl.multiple_of(pl.program_id(1) * B_QTILE, B_QTILE)
    kw = k_ref[0, pl.ds(t0, B_WIN), :]
    vw = v_ref[0, pl.ds(t0, B_WIN), :]
    q = q_ref[0]
    kvalid = (t0 - B_PAD + lax.broadcasted_iota(I32, (1, B_WIN), 1)) >= 0
    lane = lax.broadcasted_iota(I32, (1, LANES), 1)
    scale = B_HEAD_DIM ** -0.5
    for j in range(B_HEADS // 2):
        sl = slice(j * LANES, (j + 1) * LANES)
        qs, ks, vs = q[:, sl], kw[:, sl], vw[:, sl]
        outs = []
        for hh in range(2):
            mine = (lane >= B_HEAD_DIM) if hh else (lane < B_HEAD_DIM)
            qm = jnp.where(mine, qs, jnp.zeros_like(qs))
            s = pl.dot(qm, ks, trans_b=True) * scale + bias_ref[2 * j + hh]
            s = jnp.where(kvalid, s, -jnp.inf)
            m = jnp.max(s, axis=-1, keepdims=True)
            p = jnp.exp(s - m)
            l = jnp.sum(p, axis=-1, keepdims=True)
            outs.append(jnp.dot(p.astype(BF16), vs, preferred_element_type=F32) / l)
        yb_ref[0, :, sl] = jnp.where(lane < B_HEAD_DIM, outs[0], outs[1]).astype(BF16)


def _band_bias(rel_bias):
    tq = np.arange(B_QTILE)[:, None]
    j = np.arange(B_WIN)[None, :]
    band = (j // CHUNK >= tq // CHUNK) & (j // CHUNK <= tq // CHUNK + B_LEFT_CHUNKS)
    m = np.arange(B_QTILE - 1 + B_WIN)
    line = rel_bias[:, np.clip(B_WIN - 1 - m, -REL_CLIP, REL_CLIP) + REL_CLIP].astype(F32)
    table = jnp.stack([line[:, B_QTILE - 1 - t:B_QTILE - 1 - t + B_WIN] for t in range(B_QTILE)], axis=1)
    return jnp.where(jnp.asarray(band)[None], table, -jnp.inf)


def _mixer_b(qb, kb_pad, vb_pad, bias, bsz, seq):
    return pl.pallas_call(
        _mixer_b_kernel,
        out_shape=jax.ShapeDtypeStruct((bsz, seq, B_WIDTH), BF16),
        grid=(bsz, seq // B_QTILE),
        in_specs=[
            pl.BlockSpec((1, B_QTILE, B_WIDTH), lambda b, i: (b, i, 0)),
            pl.BlockSpec((1, seq + B_PAD, B_WIDTH), lambda b, i: (b, 0, 0)),
            pl.BlockSpec((1, seq + B_PAD, B_WIDTH), lambda b, i: (b, 0, 0)),
            pl.BlockSpec(bias.shape, lambda b, i: (0, 0, 0)),
        ],
        out_specs=pl.BlockSpec((1, B_QTILE, B_WIDTH), lambda b, i: (b, i, 0)),
        compiler_params=_params(("parallel", "arbitrary")),
        name="mixer_b",
    )(qb, kb_pad, vb_pad, bias)


def _layer_norm(z, g, b):
    mu = jnp.mean(z, axis=-1, keepdims=True)
    var = jnp.mean(jnp.square(z - mu), axis=-1, keepdims=True)
    return ((z - mu) * lax.rsqrt(var + LN_EPS)) * g + b


def _merge_kernel(x_ref, ya_ref, yb_ref, wg_ref, wba_ref, wbb_ref, wo_ref, g_ref, b_ref,
                  wr_ref, br_ref, x1t_ref, topi_ref, gate_ref):
    tm = x_ref.shape[0]
    x = x_ref[...]
    xb = x.astype(BF16)
    ga = jnp.dot(xb, wg_ref[:, :D_MODEL], preferred_element_type=F32)
    a = jnp.dot(ya_ref[...], wba_ref[...], preferred_element_type=F32)
    merged = jax.nn.sigmoid(ga) * a
    gb = jnp.dot(xb, wg_ref[:, D_MODEL:], preferred_element_type=F32)
    b = jnp.dot(yb_ref[...], wbb_ref[...], preferred_element_type=F32)
    merged = merged + jax.nn.sigmoid(gb) * b
    out = jnp.dot(merged.astype(BF16), wo_ref[...], preferred_element_type=F32)
    x1 = _layer_norm(DN_ALPHA * x + out, g_ref[...], b_ref[...])
    for c in range(ROW_TILES):
        x1t_ref[pl.ds(c, tm, stride=ROW_TILES), :] = x1[:, c * LANES:(c + 1) * LANES]

    lg = pl.dot(wr_ref[...], x1.astype(BF16), trans_b=True) + br_ref[...]
    eidx = lax.broadcasted_iota(I32, lg.shape, 0)
    vals, idxs = [], []
    for _ in range(TOP_K):
        m = jnp.max(lg, axis=0, keepdims=True)
        sel = jnp.min(jnp.where(lg == m, eidx, N_EXPERTS), axis=0, keepdims=True)
        vals.append(m)
        idxs.append(sel)
        lg = jnp.where(eidx == sel, -jnp.inf, lg)
    es = [jnp.exp(v - vals[0]) for v in vals]
    tot = es[0] + es[1] + es[2] + es[3]
    topi_ref[...] = jnp.concatenate(idxs, axis=0)
    gate_ref[...] = jnp.concatenate([e / tot for e in es], axis=0)


def _merge(xf, ya, yb, wg, wba, wbb, wo, g1, b1, wr_t, br):
    n, tm = xf.shape[0], min(MATMUL_TILE, xf.shape[0])
    row = lambda w: pl.BlockSpec((tm, w), lambda i: (i, 0))
    col = pl.BlockSpec((TOP_K, tm), lambda i: (0, i))
    return pl.pallas_call(
        _merge_kernel,
        out_shape=[jax.ShapeDtypeStruct((n * ROW_TILES, LANES), F32),
                   jax.ShapeDtypeStruct((TOP_K, n), I32),
                   jax.ShapeDtypeStruct((TOP_K, n), F32)],
        grid=(n // tm,),
        in_specs=[row(D_MODEL), row(A_WIDTH), row(B_WIDTH), _full(wg), _full(wba), _full(wbb),
                  _full(wo), _full(g1), _full(b1), _full(wr_t), _full(br)],
        out_specs=[pl.BlockSpec((tm * ROW_TILES, LANES), lambda i: (i, 0)), col, col],
        compiler_params=_params(("parallel",)),
        name="merge",
    )(xf, ya, yb, wg, wba, wbb, wo, g1, b1, wr_t, br)


def _one_hot_rows(topi, k, width):
    eidx = lax.broadcasted_iota(I32, (N_EXPERTS, width), 0)
    return eidx == topi[k:k + 1, :]


def _route_kernel(topi_ref, tri_ref, rank_ref, cnt_ref, run_ref):
    @pl.when(pl.program_id(0) == 0)
    def _():
        run_ref[...] = jnp.zeros_like(run_ref)

    topi = topi_ref[...]
    tr = topi.shape[1]
    hot = [_one_hot_rows(topi, k, tr) for k in range(TOP_K)]
    oh = jnp.concatenate([jnp.where(h, 1.0, 0.0).astype(BF16) for h in hot], axis=0)
    prefix = jnp.dot(oh, tri_ref[...], preferred_element_type=F32)
    base = run_ref[...]
    ranks = []
    for k in range(TOP_K):
        tbl = prefix[k * N_EXPERTS:(k + 1) * N_EXPERTS, :] + base
        ranks.append(jnp.sum(jnp.where(hot[k], tbl, 0.0), axis=0, keepdims=True))
        base = base + jnp.sum(jnp.where(hot[k], 1.0, 0.0), axis=1, keepdims=True)
    rank_ref[...] = jnp.concatenate(ranks, axis=0).astype(I32)
    run_ref[...] = base
    cnt_ref[...] = base.astype(I32)


def _route(topi):
    n, tr = topi.shape[1], TOKEN_TILE
    tri = (np.arange(tr)[:, None] < np.arange(tr)[None, :]).astype(np.float32)
    tri = jnp.asarray(tri, BF16)
    col = pl.BlockSpec((TOP_K, tr), lambda i: (0, i))
    return pl.pallas_call(
        _route_kernel,
        out_shape=[jax.ShapeDtypeStruct((TOP_K, n), I32), jax.ShapeDtypeStruct((N_EXPERTS, 1), I32)],
        grid=(n // tr,),
        in_specs=[col, _full(tri)],
        out_specs=[col, pl.BlockSpec((N_EXPERTS, 1), lambda i: (0, 0))],
        scratch_shapes=[pltpu.VMEM((N_EXPERTS, 1), F32)],
        compiler_params=_params(("arbitrary",)),
        name="route",
    )(topi, tri)


def _positions_kernel(topi_ref, rank_ref, pstv_ref, pos_ref):
    topi = topi_ref[...]
    rows = []
    for k in range(TOP_K):
        start = jnp.sum(jnp.where(_one_hot_rows(topi, k, topi.shape[1]), pstv_ref[...], 0),
                        axis=0, keepdims=True)
        rows.append(rank_ref[k:k + 1, :] + start)
    pos_ref[...] = jnp.concatenate(rows, axis=0)


def _positions(topi, rank, pstarts):
    n, td = topi.shape[1], TOKEN_TILE
    col = pl.BlockSpec((TOP_K, td), lambda i: (0, i))
    pstv = pstarts.reshape(N_EXPERTS, 1)
    return pl.pallas_call(
        _positions_kernel,
        out_shape=jax.ShapeDtypeStruct((TOP_K, n), I32),
        grid=(n // td,),
        in_specs=[col, col, _full(pstv)],
        out_specs=col,
        compiler_params=_params(("parallel",)),
        name="positions",
    )(topi, rank, pstv)


def _dispatch_kernel(cnt_ref, pst_ref, pos_hbm, x_hbm, xs_hbm,
                     pos_smem, xbuf, zbuf, sem_p, sem_x, sem_d, sem_z):
    i = pl.program_id(0)
    n = pl.num_programs(0)
    ng = pos_smem.shape[1]
    td = ng * DMA_GROUP
    tile_rows = td * ROW_TILES

    def pos_copy(step, slot):
        return pltpu.make_async_copy(pos_hbm.at[pl.ds(step * ng, ng)], pos_smem.at[slot], sem_p.at[slot])

    def x_copy(step, slot):
        return pltpu.make_async_copy(x_hbm.at[pl.ds(step * tile_rows, tile_rows)], xbuf.at[slot],
                                     sem_x.at[slot])

    @pl.when(i == 0)
    def _():
        pos_copy(0, 0).start()
        x_copy(0, 0).start()

    @pl.when(i + 1 < n)
    def _():
        pos_copy(i + 1, (i + 1) % 2).start()
        x_copy(i + 1, (i + 1) % 3).start()

    slot = i % 2
    xslot = i % 3
    pos_copy(i, slot).wait()
    x_copy(i, xslot).wait()

    def step_bytes(s):
        rows_ = TOP_K * tile_rows
        return pltpu.make_async_copy(xs_hbm.at[pl.ds(0, rows_)], xs_hbm.at[pl.ds(0, rows_)], sem_d.at[s])

    def body(g, c):
        dst = [pos_smem[slot, g, j] * ROW_TILES for j in range(DMA_GROUP * TOP_K)]
        for u in range(DMA_GROUP):
            src = xbuf.at[xslot, pl.ds((g * DMA_GROUP + u) * ROW_TILES, ROW_TILES)]
            for k in range(TOP_K):
                pltpu.make_async_copy(src, xs_hbm.at[pl.ds(dst[u * TOP_K + k], ROW_TILES)],
                                      sem_d.at[xslot]).start(priority=k % 2)
        return c
    lax.fori_loop(0, ng, body, 0)

    @pl.when(i > 0)
    def _():
        step_bytes((i + 2) % 3).wait()

    @pl.when(i == n - 1)
    def _():
        step_bytes(xslot).wait()
        zbuf[...] = jnp.zeros_like(zbuf)

        def zero_copy(r):
            return pltpu.make_async_copy(zbuf, xs_hbm.at[pl.ds(r * ROW_TILES, ROW_TILES)], sem_z)

        def zero_rows(lo, hi):
            def start(r, c2):
                zero_copy(r).start()
                return c2

            def wait(r, c2):
                zero_copy(r).wait()
                return c2
            lax.fori_loop(lo, hi, start, 0)
            lax.fori_loop(lo, hi, wait, 0)

        def padded_end(e):
            return pst_ref[e] + (cnt_ref[e] + MOE_BLOCK - 1) // MOE_BLOCK * MOE_BLOCK

        def ebody(e, c):
            zero_rows(pst_ref[e] + cnt_ref[e], padded_end(e))
            return c
        lax.fori_loop(0, N_EXPERTS, ebody, 0)
        zero_rows(padded_end(N_EXPERTS - 1), xs_hbm.shape[0] // ROW_TILES)


def _dispatch(counts, pstarts, pos_g, x1t, cap):
    n, td = pos_g.shape[0] * DMA_GROUP, TOKEN_TILE
    any_spec = pl.BlockSpec(memory_space=pl.ANY)
    return pl.pallas_call(
        _dispatch_kernel,
        out_shape=jax.ShapeDtypeStruct((cap * ROW_TILES, LANES), F32),
        grid_spec=pltpu.PrefetchScalarGridSpec(
            num_scalar_prefetch=2,
            grid=(n // td,),
            in_specs=[any_spec, any_spec],
            out_specs=any_spec,
            scratch_shapes=[
                pltpu.SMEM((2, td // DMA_GROUP, DMA_GROUP * TOP_K), I32),
                pltpu.VMEM((3, td * ROW_TILES, LANES), F32),
                pltpu.VMEM((ROW_TILES, LANES), F32),
                pltpu.SemaphoreType.DMA((2,)),
                pltpu.SemaphoreType.DMA((3,)),
                pltpu.SemaphoreType.DMA((3,)),
                pltpu.SemaphoreType.DMA,
            ],
        ),
        compiler_params=_params(("arbitrary",)),
        name="dispatch",
    )(counts, pstarts, pos_g, x1t)


def _moe_kernel(be_ref, nu_ref, xs_ref, wg_ref, bg_ref, wu_ref, bu_ref, wd_ref, bd_ref, ys_ref,
                wg_bf, wu_bf, wd_bf):
    i = pl.program_id(0)
    blk = MOE_BLOCK
    used = i < nu_ref[0]

    @pl.when(jnp.logical_or(i == 0, be_ref[i] != be_ref[jnp.maximum(i - 1, 0)]))
    def _():
        wg_bf[...] = wg_ref[0].astype(BF16)
        wu_bf[...] = wu_ref[0].astype(BF16)
        wd_bf[...] = wd_ref[0].astype(BF16)

    @pl.when(used)
    def _():
        xb = jnp.concatenate([xs_ref[pl.ds(c, blk, stride=ROW_TILES), :].astype(BF16)
                              for c in range(ROW_TILES)], axis=1)
        hg = jnp.minimum(jnp.dot(xb, wg_bf[...], preferred_element_type=F32) + bg_ref[0], SWIGLU_LIMIT)
        hu = jnp.clip(jnp.dot(xb, wu_bf[...], preferred_element_type=F32) + bu_ref[0],
                      -SWIGLU_LIMIT, SWIGLU_LIMIT)
        h = (hu + 1.0) * (hg * jax.nn.sigmoid(SWIGLU_ALPHA * hg))
        y = jnp.dot(h.astype(BF16), wd_bf[...], preferred_element_type=F32) + bd_ref[0]
        for c in range(ROW_TILES):
            ys_ref[pl.ds(c, blk, stride=ROW_TILES), :] = y[:, c * LANES:(c + 1) * LANES]

    @pl.when(jnp.logical_not(used))
    def _():
        ys_ref[...] = jnp.zeros_like(ys_ref)


def _moe(blk_expert, nused, xs, wg, bg, wu, bu, wd, bd):
    blk = MOE_BLOCK
    nblk = xs.shape[0] // (blk * ROW_TILES)
    wspec = lambda a: pl.BlockSpec((1,) + a.shape[1:], lambda i, be, nu: (be[i], 0, 0))
    xspec = pl.BlockSpec((blk * ROW_TILES, LANES), lambda i, be, nu: (jnp.minimum(i, nu[0] - 1), 0))
    return pl.pallas_call(
        _moe_kernel,
        out_shape=jax.ShapeDtypeStruct(xs.shape, F32),
        grid_spec=pltpu.PrefetchScalarGridSpec(
            num_scalar_prefetch=2,
            grid=(nblk,),
            in_specs=[xspec, wspec(wg), wspec(bg), wspec(wu), wspec(bu), wspec(wd), wspec(bd)],
            out_specs=pl.BlockSpec((blk * ROW_TILES, LANES), lambda i, be, nu: (i, 0)),
            scratch_shapes=[pltpu.VMEM(wg.shape[1:], BF16), pltpu.VMEM(wu.shape[1:], BF16),
                            pltpu.VMEM(wd.shape[1:], BF16)],
        ),
        compiler_params=_params(("arbitrary",)),
        name="moe",
    )(blk_expert, nused, xs, wg, bg, wu, bu, wd, bd)


def _final_kernel(pos_hbm, gt_ref, x1t_ref, ys_hbm, g_ref, b_ref, o_ref,
                  pos_smem, ybuf, sem_p, sem_g):
    i = pl.program_id(0)
    n = pl.num_programs(0)
    ft = FINAL_TILE
    slot_rows = TOP_K * ft * ROW_TILES

    ng = ft // DMA_GROUP

    def pos_copy(tile, slot):
        return pltpu.make_async_copy(pos_hbm.at[pl.ds(tile * ng, ng)], pos_smem.at[slot], sem_p.at[slot])

    def gather_rows(pslot, yslot):
        def body(g, c):
            src = [pos_smem[pslot, g, j] * ROW_TILES for j in range(DMA_GROUP * TOP_K)]
            base = yslot * slot_rows + g * (DMA_GROUP * ROW_TILES)
            for u in range(DMA_GROUP):
                for k in range(TOP_K):
                    dst = base + (k * ft + u) * ROW_TILES
                    pltpu.make_async_copy(ys_hbm.at[pl.ds(src[u * TOP_K + k], ROW_TILES)],
                                          ybuf.at[pl.ds(dst, ROW_TILES)],
                                          sem_g.at[yslot]).start(priority=k % 2)
            return c
        lax.fori_loop(0, ng, body, 0)

    def gather_wait(yslot):
        pltpu.make_async_copy(ys_hbm.at[pl.ds(0, slot_rows)],
                              ybuf.at[pl.ds(yslot * slot_rows, slot_rows)], sem_g.at[yslot]).wait()

    @pl.when(i == 0)
    def _():
        c = pos_copy(0, 0)
        c.start()
        c.wait()
        gather_rows(0, 0)

        @pl.when(n > 1)
        def _():
            pos_copy(1, 1).start()

    @pl.when(i + 1 < n)
    def _():
        pos_copy(i + 1, (i + 1) % 3).wait()
        gather_rows((i + 1) % 3, (i + 1) % 2)

    @pl.when(i + 2 < n)
    def _():
        pos_copy(i + 2, (i + 2) % 3).start()

    slot = i % 2
    gather_wait(slot)
    g = gt_ref[...]
    zs = []
    for c in range(ROW_TILES):
        f = None
        for k in range(TOP_K):
            yk = ybuf[pl.ds(slot * slot_rows + k * ft * ROW_TILES + c, ft, stride=ROW_TILES), :]
            term = g[:, k:k + 1] * yk
            f = term if f is None else f + term
        zs.append(DN_ALPHA * x1t_ref[pl.ds(c, ft, stride=ROW_TILES), :] + f)
    o_ref[...] = _layer_norm(jnp.concatenate(zs, axis=1), g_ref[...], b_ref[...])


def _final(pos, gates_t, x1t, ys, g2, b2):
    ft = FINAL_TILE
    n = gates_t.shape[0]
    any_spec = pl.BlockSpec(memory_space=pl.ANY)
    return pl.pallas_call(
        _final_kernel,
        out_shape=jax.ShapeDtypeStruct((n, D_MODEL), F32),
        grid=(n // ft,),
        in_specs=[any_spec, pl.BlockSpec((ft, TOP_K), lambda i: (i, 0)),
                  pl.BlockSpec((ft * ROW_TILES, LANES), lambda i: (i, 0)), any_spec,
                  _full(g2), _full(b2)],
        out_specs=pl.BlockSpec((ft, D_MODEL), lambda i: (i, 0)),
        scratch_shapes=[
            pltpu.SMEM((3, ft // DMA_GROUP, DMA_GROUP * TOP_K), I32),
            pltpu.VMEM((2 * TOP_K * ft * ROW_TILES, LANES), F32),
            pltpu.SemaphoreType.DMA((3,)),
            pltpu.SemaphoreType.DMA((2,)),
        ],
        compiler_params=_params(("arbitrary",)),
        name="final",
    )(pos, gates_t, x1t, ys, g2, b2)


def _mix_weight(w_in):
    qa, ckv, qi, ki, wi, qb, kb, vb, _, _ = jnp.split(w_in, np.cumsum(SPLITS)[:-1].tolist(), axis=-1)
    pad = lambda a: jnp.pad(a, ((0, 0), (0, LANES - a.shape[1])))
    return jnp.concatenate([qa, qi, ckv, pad(ki), pad(wi), qb, kb, vb], axis=1).astype(BF16)


def _layer(x, w_in, kv_norm_g, idx_k_norm_g, idx_k_norm_b, w_uk, w_uv, rel_bias, w_branch_a,
           w_branch_b, w_out, ln1_g, ln1_b, w_router, b_router, w_gate, b_gate, w_up, b_up,
           w_down, b_down, ln2_g, ln2_b):
    bsz, seq, _ = x.shape
    n = bsz * seq
    assert seq % KEY_TILE == 0 and n % TOKEN_TILE == 0
    xf = x.reshape(n, D_MODEL)
    row = lambda v: v.reshape(1, -1).astype(F32)

    w_mix = _mix_weight(w_in)
    w_gates = w_in[:, sum(SPLITS[:8]):].astype(BF16)
    wuk_bd = jnp.zeros((A_WIDTH, A_HEADS * A_KV_RANK), F32)
    wuv_pad = jnp.zeros((A_HEADS, A_KV_RANK, LANES), F32)
    for h in range(A_HEADS):
        wuk_bd = wuk_bd.at[h * A_HEAD_DIM:(h + 1) * A_HEAD_DIM, h * A_KV_RANK:(h + 1) * A_KV_RANK].set(w_uk[h])
        c0 = (h % 2) * A_HEAD_DIM
        wuv_pad = wuv_pad.at[h, :, c0:c0 + A_HEAD_DIM].set(w_uv[h])
    wuk_bd, wuv_pad = wuk_bd.astype(BF16), wuv_pad.astype(BF16)

    qlat, qi, ckv, ki, wi, qb, kb, vb = _projection(
        xf, w_mix, wuk_bd, row(kv_norm_g), row(idx_k_norm_g), row(idx_k_norm_b))

    nqb, nkt = seq // Q_BLOCK, seq // KEY_TILE
    kidx = ki.reshape(bsz, nkt, KEY_TILE, IDX_DIM)
    ckv4 = ckv.reshape(bsz, nkt, KEY_TILE, A_KV_RANK)
    ckvt = ckv4.transpose(0, 1, 3, 2)
    qi_r = qi.reshape(bsz, nqb, Q_BLOCK, IDX_HEADS, IDX_DIM).transpose(0, 1, 3, 2, 4).reshape(
        bsz, nqb, IDX_HEADS * Q_BLOCK, IDX_DIM)
    wit = wi.reshape(bsz, seq, IDX_HEADS).transpose(0, 2, 1)
    ya = _mixer_a(kidx, ckv4, ckvt, qi_r, wit, qlat.reshape(bsz, seq, -1), wuv_pad, bsz, seq)

    front = lambda a: jnp.pad(a.reshape(bsz, seq, B_WIDTH), ((0, 0), (B_PAD, 0), (0, 0)))
    yb = _mixer_b(qb.reshape(bsz, seq, B_WIDTH), front(kb), front(vb), _band_bias(rel_bias), bsz, seq)

    x1t, topi, gates = _merge(
        xf, ya.reshape(n, A_WIDTH), yb.reshape(n, B_WIDTH), w_gates, w_branch_a.astype(BF16),
        w_branch_b.astype(BF16), w_out.astype(BF16), row(ln1_g), row(ln1_b),
        w_router.T.astype(BF16), b_router.reshape(-1, 1).astype(F32))

    blk = MOE_BLOCK
    cap = n * TOP_K + N_EXPERTS * blk
    nblk = cap // blk
    rank, cnt = _route(topi)
    counts = cnt[:, 0]
    padded = (counts + blk - 1) // blk * blk
    pends = jnp.cumsum(padded).astype(I32)
    pstarts = pends - padded
    nused = (pends[-1:] // blk).astype(I32)
    blk_start = jnp.arange(nblk, dtype=I32) * blk
    blk_expert = jnp.minimum(jnp.sum(blk_start[:, None] >= pends[None, :], axis=1), N_EXPERTS - 1).astype(I32)

    pos = _positions(topi, rank, pstarts)
    pos_g = pos.T.reshape(n // DMA_GROUP, DMA_GROUP * TOP_K)
    xs = _dispatch(counts, pstarts, pos_g, x1t, cap)
    ys = _moe(blk_expert, nused, xs, w_gate, b_gate[:, None, :], w_up, b_up[:, None, :],
              w_down, b_down[:, None, :])
    out = _final(pos_g, gates.T, x1t, ys, row(ln2_g), row(ln2_b))
    return out.reshape(bsz, seq, D_MODEL)


def kernel(x, w_in, kv_norm_g, idx_k_norm_g, idx_k_norm_b, w_uk, w_uv, rel_bias, w_branch_a,
           w_branch_b, w_out, ln1_g, ln1_b, w_router, b_router, w_gate, b_gate, w_up, b_up,
           w_down, b_down, ln2_g, ln2_b):
    for l in range(DEPTH):
        x = _layer(x, w_in[l], kv_norm_g[l], idx_k_norm_g[l], idx_k_norm_b[l], w_uk[l], w_uv[l],
                   rel_bias[l], w_branch_a[l], w_branch_b[l], w_out[l], ln1_g[l], ln1_b[l],
                   w_router[l], b_router[l], w_gate[l], b_gate[l], w_up[l], b_up[l], w_down[l],
                   b_down[l], ln2_g[l], ln2_b[l])
    return x
```

```python
import functools

import jax
import jax.numpy as jnp
import numpy as np
from jax import lax
from jax.experimental import pallas as pl
from jax.experimental.pallas import tpu as pltpu

F32 = jnp.float32
BF16 = jnp.bfloat16
I32 = jnp.int32

D_MODEL = 1024
CHUNK = 64
Q_BLOCK = 128
A_HEADS = 8
A_HEAD_DIM = 64
A_KV_RANK = 128
IDX_HEADS = 8
IDX_DIM = 64
IDX_TOPK = 256
IDX_SCALE = (IDX_HEADS * IDX_DIM) ** -0.5
B_HEADS = 8
B_HEAD_DIM = 64
B_LEFT_CHUNKS = 8
REL_CLIP = 128
N_EXPERTS = 32
TOP_K = 4
D_FF = 1024
SWIGLU_LIMIT = 7.0
SWIGLU_ALPHA = 1.702
DEPTH = 1
DN_ALPHA = (2.0 * DEPTH) ** 0.25
LN_EPS = 1e-5

A_WIDTH = A_HEADS * A_HEAD_DIM
B_WIDTH = B_HEADS * B_HEAD_DIM
SPLITS = [A_WIDTH, A_KV_RANK, IDX_HEADS * IDX_DIM, IDX_DIM, IDX_HEADS,
          B_WIDTH, B_WIDTH, B_WIDTH, D_MODEL, D_MODEL]

LANES = 128
SUBLANES = 8
KEY_TILE = 256
B_QTILE = 512
B_PAD = B_LEFT_CHUNKS * CHUNK
B_WIN = B_PAD + B_QTILE
TOKEN_TILE = 512
MATMUL_TILE = 1024
MOE_BLOCK = 512
FINAL_TILE = 512
ROW_TILES = D_MODEL // LANES
DMA_GROUP = 4
INT_MIN = -2 ** 31
VMEM_LIMIT = 56 * 1024 * 1024

A_SLOPES = [2.0 ** (-8.0 * (h + 1) / A_HEADS) for h in range(A_HEADS)]

_C_QA = 0
_C_QI = _C_QA + A_WIDTH
_C_CKV = _C_QI + IDX_HEADS * IDX_DIM
_C_KI = _C_CKV + A_KV_RANK
_C_WI = _C_KI + LANES
_C_QKVB = _C_WI + LANES
_C_END = _C_QKVB + 3 * B_WIDTH


def _params(sem, vmem=VMEM_LIMIT):
    return pltpu.CompilerParams(dimension_semantics=sem, vmem_limit_bytes=vmem)


def _full(a):
    return pl.BlockSpec(a.shape, lambda *_: (0,) * a.ndim)


def _proj_kernel(x_ref, w_ref, wuk_ref, kvg_ref, kig_ref, kib_ref,
                 qlat_ref, qi_ref, ckv_ref, ki_ref, wi_ref, qb_ref, kb_ref, vb_ref):
    xb = x_ref[...].astype(BF16)

    def mm(c0, c1):
        return jnp.dot(xb, w_ref[:, c0:c1], preferred_element_type=F32)

    qa = mm(_C_QA, _C_QI)
    qlat = jnp.dot(qa.astype(BF16), wuk_ref[...], preferred_element_type=F32)
    qlat_ref[...] = (qlat * (A_HEAD_DIM ** -0.5)).astype(BF16)
    qi_ref[...] = mm(_C_QI, _C_CKV).astype(BF16)

    ckv = mm(_C_CKV, _C_KI)
    ms = jnp.mean(ckv * ckv, axis=-1, keepdims=True)
    ckv_ref[...] = ((ckv * lax.rsqrt(ms + LN_EPS)) * kvg_ref[...]).astype(BF16)

    ki = mm(_C_KI, _C_WI)[:, :IDX_DIM]
    mu = jnp.mean(ki, axis=-1, keepdims=True)
    var = jnp.mean(jnp.square(ki - mu), axis=-1, keepdims=True)
    ki_ref[...] = (((ki - mu) * lax.rsqrt(var + LN_EPS)) * kig_ref[...] + kib_ref[...]).astype(BF16)

    wi_ref[...] = mm(_C_WI, _C_QKVB)[:, :IDX_HEADS]
    qb_ref[...] = mm(_C_QKVB, _C_QKVB + B_WIDTH).astype(BF16)
    kb_ref[...] = mm(_C_QKVB + B_WIDTH, _C_QKVB + 2 * B_WIDTH).astype(BF16)
    vb_ref[...] = mm(_C_QKVB + 2 * B_WIDTH, _C_END).astype(BF16)


def _projection(xf, w_mix, wuk_bd, kvg, kig, kib):
    n, tm = xf.shape[0], min(MATMUL_TILE, xf.shape[0])
    row = lambda w: pl.BlockSpec((tm, w), lambda i: (i, 0))
    outs = [(A_HEADS * A_KV_RANK, BF16), (IDX_HEADS * IDX_DIM, BF16), (A_KV_RANK, BF16),
            (IDX_DIM, BF16), (IDX_HEADS, F32), (B_WIDTH, BF16), (B_WIDTH, BF16), (B_WIDTH, BF16)]
    return pl.pallas_call(
        _proj_kernel,
        out_shape=[jax.ShapeDtypeStruct((n, w), dt) for w, dt in outs],
        grid=(n // tm,),
        in_specs=[row(D_MODEL), _full(w_mix), _full(wuk_bd), _full(kvg), _full(kig), _full(kib)],
        out_specs=[row(w) for w, _ in outs],
        compiler_params=_params(("parallel",)),
        name="proj",
    )(xf, w_mix, wuk_bd, kvg, kig, kib)


def _mixer_a_kernel(kidx_ref, ckv_ref, ckvt_ref, qi_ref, wit_ref, qlat_ref, wuv_ref,
                    ya_ref, keys_ref, s_ref, acc_ref, *, seq, topk):
    i = pl.program_id(1)
    nkt = i // 2 + 1
    nsub = KEY_TILE // SUBLANES
    lane = lax.broadcasted_iota(I32, (1, LANES), 1)
    qpos = i * Q_BLOCK + lane
    qchunk = qpos >> 6
    row_iota = lax.broadcasted_iota(I32, (KEY_TILE, LANES), 0)

    qi_blk = qi_ref[0, 0]
    w_all = wit_ref[0] * IDX_SCALE

    def idx_body(kt, carry):
        lg = pl.dot(kidx_ref[0, kt], qi_blk, trans_b=True)
        isc = jnp.zeros((KEY_TILE, LANES), F32)
        for h in range(IDX_HEADS):
            isc = isc + jnp.maximum(lg[:, h * LANES:(h + 1) * LANES], 0.0) * w_all[h:h + 1, :]
        bits = lax.bitcast_convert_type(isc + 0.0, I32)
        key = bits ^ ((bits >> 31) & 0x7FFFFFFF)
        spos = kt * KEY_TILE + row_iota
        key = jnp.where((spos >> 6) <= qchunk, key, INT_MIN)
        keys_ref[kt] = key
        return carry

    def over_tiles(body, init):
        def pair(j, c):
            return body(2 * j + 1, body(2 * j, c))
        c = lax.fori_loop(0, nkt // 2, pair, init)
        return lax.cond(nkt % 2 == 1, lambda c: body(nkt - 1, c), lambda c: c, c)

    over_tiles(idx_body, 0)

    def count(pred):
        def body(kt, acc):
            spos = kt * KEY_TILE + row_iota
            m = jnp.where(pred(keys_ref[kt], spos), 1, 0).astype(I32)
            return acc + jnp.sum(m.reshape(nsub, SUBLANES, LANES), axis=0)
        acc = over_tiles(body, jnp.zeros((SUBLANES, LANES), I32))
        return jnp.sum(acc, axis=0, keepdims=True)

    def bit_body(it, carry):
        thr, cge = carry
        cand = thr ^ lax.shift_left(jnp.int32(1), 31 - it)
        cnt = count(lambda kv, spos: kv >= cand)
        ok = cnt >= topk
        return jnp.where(ok, cand, thr), jnp.where(ok, cnt, cge)

    start = (jnp.full((1, LANES), INT_MIN, I32), jnp.zeros((1, LANES), I32))
    thr, cge = lax.cond(nkt * KEY_TILE > topk,
                        lambda: lax.fori_loop(0, 32, bit_body, start), lambda: start)

    nbits = int(np.log2(seq)) + 1
    has_thr = thr > INT_MIN
    need_tie = jnp.max(jnp.where(has_thr, cge, 0)) > topk

    def tie_fn():
        room = topk - count(lambda kv, spos: kv > thr)

        def jb(it, cut):
            cand = cut | lax.shift_left(jnp.int32(1), nbits - 1 - it)
            f = count(lambda kv, spos: (kv == thr) & (spos < cand))
            return jnp.where(f <= room, cand, cut)
        return lax.fori_loop(0, nbits, jb, jnp.zeros((1, LANES), I32))

    cut = lax.cond(need_tie, tie_fn, lambda: jnp.full((1, LANES), 2 * seq, I32))
    cut = jnp.where(has_thr, cut, 0)

    qlat = qlat_ref[0]
    qr = jnp.concatenate([qlat[:, h * LANES:(h + 1) * LANES] for h in range(A_HEADS)], axis=0)

    def sc_body(kt, m):
        st = pl.dot(ckv_ref[0, kt], qr, trans_b=True)
        kv = keys_ref[kt]
        spos = kt * KEY_TILE + row_iota
        sel = (kv > thr) | ((kv == thr) & (spos < cut))
        dist = jnp.abs(qpos - spos).astype(F32)
        out = []
        for h in range(A_HEADS):
            sh = st[:, h * LANES:(h + 1) * LANES] - A_SLOPES[h] * dist
            sh = jnp.where(sel, sh, -jnp.inf)
            s_ref[kt, :, h * LANES:(h + 1) * LANES] = sh
            out.append(jnp.maximum(m[h], jnp.max(sh.reshape(nsub, SUBLANES, LANES), axis=0)))
        return tuple(out)

    m8 = over_tiles(sc_body, tuple(jnp.full((SUBLANES, LANES), -jnp.inf, F32) for _ in range(A_HEADS)))
    mx = [jnp.max(m, axis=0, keepdims=True) for m in m8]

    acc_ref[...] = jnp.zeros_like(acc_ref)

    def pv_body(kt, l):
        out, ps = [], []
        for h in range(A_HEADS):
            p = jnp.exp(s_ref[kt, :, h * LANES:(h + 1) * LANES] - mx[h])
            out.append(l[h] + jnp.sum(p.reshape(nsub, SUBLANES, LANES), axis=0))
            ps.append(p.astype(BF16))
        acc_ref[...] += jnp.dot(ckvt_ref[0, kt], jnp.concatenate(ps, axis=1),
                                preferred_element_type=F32)
        return tuple(out)

    l8 = over_tiles(pv_body, tuple(jnp.zeros((SUBLANES, LANES), F32) for _ in range(A_HEADS)))
    inv = [1.0 / jnp.sum(l, axis=0, keepdims=True) for l in l8]

    for j in range(A_HEADS // 2):
        y = None
        for h in (2 * j, 2 * j + 1):
            o = (acc_ref[:, h * LANES:(h + 1) * LANES] * inv[h]).astype(BF16)
            t = pl.dot(o, wuv_ref[h], trans_a=True)
            y = t if y is None else y + t
        ya_ref[0, :, j * LANES:(j + 1) * LANES] = y.astype(BF16)


def _mixer_a(kidx, ckv, ckvt, qi_r, wit, qlat, wuv_pad, bsz, seq):
    nqb = seq // Q_BLOCK
    nkt = seq // KEY_TILE
    topk = min(IDX_TOPK, seq // 4)
    kern = functools.partial(_mixer_a_kernel, seq=seq, topk=topk)
    return pl.pallas_call(
        kern,
        out_shape=jax.ShapeDtypeStruct((bsz, seq, A_WIDTH), BF16),
        grid=(bsz, nqb),
        in_specs=[
            pl.BlockSpec((1, nkt, KEY_TILE, IDX_DIM), lambda b, i: (b, 0, 0, 0)),
            pl.BlockSpec((1, nkt, KEY_TILE, A_KV_RANK), lambda b, i: (b, 0, 0, 0)),
            pl.BlockSpec((1, nkt, A_KV_RANK, KEY_TILE), lambda b, i: (b, 0, 0, 0)),
            pl.BlockSpec((1, 1, IDX_HEADS * Q_BLOCK, IDX_DIM), lambda b, i: (b, i, 0, 0)),
            pl.BlockSpec((1, IDX_HEADS, Q_BLOCK), lambda b, i: (b, 0, i)),
            pl.BlockSpec((1, Q_BLOCK, A_HEADS * A_KV_RANK), lambda b, i: (b, i, 0)),
            pl.BlockSpec(wuv_pad.shape, lambda b, i: (0, 0, 0)),
        ],
        out_specs=pl.BlockSpec((1, Q_BLOCK, A_WIDTH), lambda b, i: (b, i, 0)),
        scratch_shapes=[
            pltpu.VMEM((nkt, KEY_TILE, LANES), I32),
            pltpu.VMEM((nkt, KEY_TILE, A_HEADS * LANES), F32),
            pltpu.VMEM((A_KV_RANK, A_HEADS * LANES), F32),
        ],
        compiler_params=_params(("parallel", "arbitrary")),
        name="mixer_a",
    )(kidx, ckv, ckvt, qi_r, wit, qlat, wuv_pad)


def _mixer_b_kernel(q_ref, k_ref, v_ref, bias_ref, yb_ref):
    t0 = pl.multiple_of(pl.program_id(1) * B_QTILE, B_QTILE)
    kw = k_ref[0, pl.ds(t0, B_WIN), :]
    vw = v_ref[0, pl.ds(t0, B_WIN), :]
    q = q_ref[0]
    kvalid = (t0 - B_PAD + lax.broadcasted_iota(I32, (1, B_WIN), 1)) >= 0
    lane = lax.broadcasted_iota(I32, (1, LANES), 1)
    scale = B_HEAD_DIM ** -0.5
    for j in range(B_HEADS // 2):
        sl = slice(j * LANES, (j + 1) * LANES)
        qs, ks, vs = q[:, sl], kw[:, sl], vw[:, sl]
        outs = []
        for hh in range(2):
            mine = (lane >= B_HEAD_DIM) if hh else (lane < B_HEAD_DIM)
            qm = jnp.where(mine, qs, jnp.zeros_like(qs))
            s = pl.dot(qm, ks, trans_b=True) * scale + bias_ref[2 * j + hh]
            s = jnp.where(kvalid, s, -jnp.inf)
            m = jnp.max(s, axis=-1, keepdims=True)
            p = jnp.exp(s - m)
            l = jnp.sum(p, axis=-1, keepdims=True)
            outs.append(jnp.dot(p.astype(BF16), vs, preferred_element_type=F32) / l)
        yb_ref[0, :, sl] = jnp.where(lane < B_HEAD_DIM, outs[0], outs[1]).astype(BF16)


def _band_bias(rel_bias):
    tq = np.arange(B_QTILE)[:, None]
    j = np.arange(B_WIN)[None, :]
    band = (j // CHUNK >= tq // CHUNK) & (j // CHUNK <= tq // CHUNK + B_LEFT_CHUNKS)
    m = np.arange(B_QTILE - 1 + B_WIN)
    line = rel_bias[:, np.clip(B_WIN - 1 - m, -REL_CLIP, REL_CLIP) + REL_CLIP].astype(F32)
    step = 256
    wide = B_WIN + B_QTILE - step
    base = jnp.stack([line[:, step - 1 - b:step - 1 - b + wide] for b in range(step)], axis=1)
    table = jnp.concatenate([base[:, :, B_QTILE - step * (a + 1):B_QTILE - step * (a + 1) + B_WIN]
                             for a in range(B_QTILE // step)], axis=1)
    return jnp.where(jnp.asarray(band)[None], table, -jnp.inf)


def _mixer_b(qb, kb_pad, vb_pad, bias, bsz, seq):
    return pl.pallas_call(
        _mixer_b_kernel,
        out_shape=jax.ShapeDtypeStruct((bsz, seq, B_WIDTH), BF16),
        grid=(bsz, seq // B_QTILE),
        in_specs=[
            pl.BlockSpec((1, B_QTILE, B_WIDTH), lambda b, i: (b, i, 0)),
            pl.BlockSpec((1, seq + B_PAD, B_WIDTH), lambda b, i: (b, 0, 0)),
            pl.BlockSpec((1, seq + B_PAD, B_WIDTH), lambda b, i: (b, 0, 0)),
            pl.BlockSpec(bias.shape, lambda b, i: (0, 0, 0), pipeline_mode=pl.Buffered(1)),
        ],
        out_specs=pl.BlockSpec((1, B_QTILE, B_WIDTH), lambda b, i: (b, i, 0)),
        compiler_params=_params(("parallel", "arbitrary")),
        name="mixer_b",
    )(qb, kb_pad, vb_pad, bias)


def _layer_norm(z, g, b):
    mu = jnp.mean(z, axis=-1, keepdims=True)
    var = jnp.mean(jnp.square(z - mu), axis=-1, keepdims=True)
    return ((z - mu) * lax.rsqrt(var + LN_EPS)) * g + b


def _merge_kernel(x_ref, ya_ref, yb_ref, wg_ref, wba_ref, wbb_ref, wo_ref, g_ref, b_ref,
                  wr_ref, br_ref, x1t_ref, topi_ref, gate_ref):
    tm = x_ref.shape[0]
    x = x_ref[...]
    xb = x.astype(BF16)
    ga = jnp.dot(xb, wg_ref[:, :D_MODEL], preferred_element_type=F32)
    a = jnp.dot(ya_ref[...], wba_ref[...], preferred_element_type=F32)
    merged = jax.nn.sigmoid(ga) * a
    gb = jnp.dot(xb, wg_ref[:, D_MODEL:], preferred_element_type=F32)
    b = jnp.dot(yb_ref[...], wbb_ref[...], preferred_element_type=F32)
    merged = merged + jax.nn.sigmoid(gb) * b
    out = jnp.dot(merged.astype(BF16), wo_ref[...], preferred_element_type=F32)
    x1 = _layer_norm(DN_ALPHA * x + out, g_ref[...], b_ref[...])
    for c in range(ROW_TILES):
        x1t_ref[pl.ds(c, tm, stride=ROW_TILES), :] = x1[:, c * LANES:(c + 1) * LANES]

    lg = pl.dot(wr_ref[...], x1.astype(BF16), trans_b=True) + br_ref[...]
    eidx = lax.broadcasted_iota(I32, lg.shape, 0)
    vals, idxs = [], []
    for _ in range(TOP_K):
        m = jnp.max(lg, axis=0, keepdims=True)
        sel = jnp.min(jnp.where(lg == m, eidx, N_EXPERTS), axis=0, keepdims=True)
        vals.append(m)
        idxs.append(sel)
        lg = jnp.where(eidx == sel, -jnp.inf, lg)
    es = [jnp.exp(v - vals[0]) for v in vals]
    tot = es[0] + es[1] + es[2] + es[3]
    topi_ref[...] = jnp.concatenate(idxs, axis=0)
    gate_ref[...] = jnp.concatenate([e / tot for e in es], axis=0)


def _merge(xf, ya, yb, wg, wba, wbb, wo, g1, b1, wr_t, br):
    n, tm = xf.shape[0], min(MATMUL_TILE, xf.shape[0])
    row = lambda w: pl.BlockSpec((tm, w), lambda i: (i, 0))
    col = pl.BlockSpec((TOP_K, tm), lambda i: (0, i))
    return pl.pallas_call(
        _merge_kernel,
        out_shape=[jax.ShapeDtypeStruct((n * ROW_TILES, LANES), F32),
                   jax.ShapeDtypeStruct((TOP_K, n), I32),
                   jax.ShapeDtypeStruct((TOP_K, n), F32)],
        grid=(n // tm,),
        in_specs=[row(D_MODEL), row(A_WIDTH), row(B_WIDTH), _full(wg), _full(wba), _full(wbb),
                  _full(wo), _full(g1), _full(b1), _full(wr_t), _full(br)],
        out_specs=[pl.BlockSpec((tm * ROW_TILES, LANES), lambda i: (i, 0)), col, col],
        compiler_params=_params(("parallel",)),
        name="merge",
    )(xf, ya, yb, wg, wba, wbb, wo, g1, b1, wr_t, br)


def _one_hot_rows(topi, k, width):
    eidx = lax.broadcasted_iota(I32, (N_EXPERTS, width), 0)
    return eidx == topi[k:k + 1, :]


def _route_kernel(topi_ref, tri_ref, rank_ref, cnt_ref, run_ref):
    @pl.when(pl.program_id(0) == 0)
    def _():
        run_ref[...] = jnp.zeros_like(run_ref)

    topi = topi_ref[...]
    tr = topi.shape[1]
    hot = [_one_hot_rows(topi, k, tr) for k in range(TOP_K)]
    oh = jnp.concatenate([jnp.where(h, 1.0, 0.0).astype(BF16) for h in hot], axis=0)
    prefix = jnp.dot(oh, tri_ref[...], preferred_element_type=F32)
    base = run_ref[...]
    ranks = []
    for k in range(TOP_K):
        tbl = prefix[k * N_EXPERTS:(k + 1) * N_EXPERTS, :] + base
        ranks.append(jnp.sum(jnp.where(hot[k], tbl, 0.0), axis=0, keepdims=True))
        base = base + jnp.sum(jnp.where(hot[k], 1.0, 0.0), axis=1, keepdims=True)
    rank_ref[...] = jnp.concatenate(ranks, axis=0).astype(I32)
    run_ref[...] = base
    cnt_ref[...] = base.astype(I32)


def _route(topi):
    n, tr = topi.shape[1], TOKEN_TILE
    tri = (np.arange(tr)[:, None] < np.arange(tr)[None, :]).astype(np.float32)
    tri = jnp.asarray(tri, BF16)
    col = pl.BlockSpec((TOP_K, tr), lambda i: (0, i))
    return pl.pallas_call(
        _route_kernel,
        out_shape=[jax.ShapeDtypeStruct((TOP_K, n), I32), jax.ShapeDtypeStruct((N_EXPERTS, 1), I32)],
        grid=(n // tr,),
        in_specs=[col, _full(tri)],
        out_specs=[col, pl.BlockSpec((N_EXPERTS, 1), lambda i: (0, 0))],
        scratch_shapes=[pltpu.VMEM((N_EXPERTS, 1), F32)],
        compiler_params=_params(("arbitrary",)),
        name="route",
    )(topi, tri)


def _positions_kernel(topi_ref, rank_ref, pstv_ref, pos_ref):
    topi = topi_ref[...]
    rows = []
    for k in range(TOP_K):
        start = jnp.sum(jnp.where(_one_hot_rows(topi, k, topi.shape[1]), pstv_ref[...], 0),
                        axis=0, keepdims=True)
        rows.append(rank_ref[k:k + 1, :] + start)
    pos_ref[...] = jnp.concatenate(rows, axis=0)


def _positions(topi, rank, pstarts):
    n, td = topi.shape[1], TOKEN_TILE
    col = pl.BlockSpec((TOP_K, td), lambda i: (0, i))
    pstv = pstarts.reshape(N_EXPERTS, 1)
    return pl.pallas_call(
        _positions_kernel,
        out_shape=jax.ShapeDtypeStruct((TOP_K, n), I32),
        grid=(n // td,),
        in_specs=[col, col, _full(pstv)],
        out_specs=col,
        compiler_params=_params(("parallel",)),
        name="positions",
    )(topi, rank, pstv)


def _dispatch_kernel(cnt_ref, pst_ref, pos_hbm, x_hbm, xs_hbm,
                     pos_smem, xbuf, zbuf, sem_p, sem_x, sem_d, sem_z):
    i = pl.program_id(0)
    n = pl.num_programs(0)
    ng = pos_smem.shape[1]
    td = ng * DMA_GROUP
    tile_rows = td * ROW_TILES

    def pos_copy(step, slot):
        return pltpu.make_async_copy(pos_hbm.at[pl.ds(step * ng, ng)], pos_smem.at[slot], sem_p.at[slot])

    def x_copy(step, slot):
        return pltpu.make_async_copy(x_hbm.at[pl.ds(step * tile_rows, tile_rows)], xbuf.at[slot],
                                     sem_x.at[slot])

    @pl.when(i == 0)
    def _():
        pos_copy(0, 0).start()
        x_copy(0, 0).start()

    @pl.when(i + 1 < n)
    def _():
        pos_copy(i + 1, (i + 1) % 2).start()
        x_copy(i + 1, (i + 1) % 3).start()

    slot = i % 2
    xslot = i % 3
    pos_copy(i, slot).wait()
    x_copy(i, xslot).wait()

    def step_bytes(s):
        rows_ = TOP_K * tile_rows
        return pltpu.make_async_copy(xs_hbm.at[pl.ds(0, rows_)], xs_hbm.at[pl.ds(0, rows_)], sem_d.at[s])

    def body(g, c):
        dst = [pos_smem[slot, g, j] * ROW_TILES for j in range(DMA_GROUP * TOP_K)]
        for u in range(DMA_GROUP):
            src = xbuf.at[xslot, pl.ds((g * DMA_GROUP + u) * ROW_TILES, ROW_TILES)]
            for k in range(TOP_K):
                pltpu.make_async_copy(src, xs_hbm.at[pl.ds(dst[u * TOP_K + k], ROW_TILES)],
                                      sem_d.at[xslot]).start(priority=k % 2)
        return c
    lax.fori_loop(0, ng, body, 0)

    @pl.when(i > 0)
    def _():
        step_bytes((i + 2) % 3).wait()

    @pl.when(i == n - 1)
    def _():
        step_bytes(xslot).wait()
        zbuf[...] = jnp.zeros_like(zbuf)

        def zero_copy(r):
            return pltpu.make_async_copy(zbuf, xs_hbm.at[pl.ds(r * ROW_TILES, ROW_TILES)], sem_z)

        def zero_rows(lo, hi):
            def start(r, c2):
                zero_copy(r).start()
                return c2

            def wait(r, c2):
                zero_copy(r).wait()
                return c2
            lax.fori_loop(lo, hi, start, 0)
            lax.fori_loop(lo, hi, wait, 0)

        def padded_end(e):
            return pst_ref[e] + (cnt_ref[e] + MOE_BLOCK - 1) // MOE_BLOCK * MOE_BLOCK

        def ebody(e, c):
            zero_rows(pst_ref[e] + cnt_ref[e], padded_end(e))
            return c
        lax.fori_loop(0, N_EXPERTS, ebody, 0)
        zero_rows(padded_end(N_EXPERTS - 1), xs_hbm.shape[0] // ROW_TILES)


def _dispatch(counts, pstarts, pos_g, x1t, cap):
    n, td = pos_g.shape[0] * DMA_GROUP, TOKEN_TILE
    any_spec = pl.BlockSpec(memory_space=pl.ANY)
    return pl.pallas_call(
        _dispatch_kernel,
        out_shape=jax.ShapeDtypeStruct((cap * ROW_TILES, LANES), F32),
        grid_spec=pltpu.PrefetchScalarGridSpec(
            num_scalar_prefetch=2,
            grid=(n // td,),
            in_specs=[any_spec, any_spec],
            out_specs=any_spec,
            scratch_shapes=[
                pltpu.SMEM((2, td // DMA_GROUP, DMA_GROUP * TOP_K), I32),
                pltpu.VMEM((3, td * ROW_TILES, LANES), F32),
                pltpu.VMEM((ROW_TILES, LANES), F32),
                pltpu.SemaphoreType.DMA((2,)),
                pltpu.SemaphoreType.DMA((3,)),
                pltpu.SemaphoreType.DMA((3,)),
                pltpu.SemaphoreType.DMA,
            ],
        ),
        compiler_params=_params(("arbitrary",)),
        name="dispatch",
    )(counts, pstarts, pos_g, x1t)


def _moe_kernel(be_ref, nu_ref, xs_ref, wg_ref, bg_ref, wu_ref, bu_ref, wd_ref, bd_ref, ys_ref,
                wg_bf, wu_bf, wd_bf):
    i = pl.program_id(0)
    blk = MOE_BLOCK
    used = i < nu_ref[0]

    @pl.when(jnp.logical_or(i == 0, be_ref[i] != be_ref[jnp.maximum(i - 1, 0)]))
    def _():
        wg_bf[...] = wg_ref[0].astype(BF16)
        wu_bf[...] = wu_ref[0].astype(BF16)
        wd_bf[...] = wd_ref[0].astype(BF16)

    @pl.when(used)
    def _():
        xb = jnp.concatenate([xs_ref[pl.ds(c, blk, stride=ROW_TILES), :].astype(BF16)
                              for c in range(ROW_TILES)], axis=1)
        hg = jnp.minimum(jnp.dot(xb, wg_bf[...], preferred_element_type=F32) + bg_ref[0], SWIGLU_LIMIT)
        hu = jnp.clip(jnp.dot(xb, wu_bf[...], preferred_element_type=F32) + bu_ref[0],
                      -SWIGLU_LIMIT, SWIGLU_LIMIT)
        h = (hu + 1.0) * (hg * jax.nn.sigmoid(SWIGLU_ALPHA * hg))
        y = jnp.dot(h.astype(BF16), wd_bf[...], preferred_element_type=F32) + bd_ref[0]
        for c in range(ROW_TILES):
            ys_ref[pl.ds(c, blk, stride=ROW_TILES), :] = y[:, c * LANES:(c + 1) * LANES]

    @pl.when(jnp.logical_not(used))
    def _():
        ys_ref[...] = jnp.zeros_like(ys_ref)


def _moe(blk_expert, nused, xs, wg, bg, wu, bu, wd, bd):
    blk = MOE_BLOCK
    nblk = xs.shape[0] // (blk * ROW_TILES)
    wspec = lambda a: pl.BlockSpec((1,) + a.shape[1:], lambda i, be, nu: (be[i], 0, 0))
    xspec = pl.BlockSpec((blk * ROW_TILES, LANES), lambda i, be, nu: (jnp.minimum(i, nu[0] - 1), 0))
    return pl.pallas_call(
        _moe_kernel,
        out_shape=jax.ShapeDtypeStruct(xs.shape, F32),
        grid_spec=pltpu.PrefetchScalarGridSpec(
            num_scalar_prefetch=2,
            grid=(nblk,),
            in_specs=[xspec, wspec(wg), wspec(bg), wspec(wu), wspec(bu), wspec(wd), wspec(bd)],
            out_specs=pl.BlockSpec((blk * ROW_TILES, LANES), lambda i, be, nu: (i, 0)),
            scratch_shapes=[pltpu.VMEM(wg.shape[1:], BF16), pltpu.VMEM(wu.shape[1:], BF16),
                            pltpu.VMEM(wd.shape[1:], BF16)],
        ),
        compiler_params=_params(("arbitrary",)),
        name="moe",
    )(blk_expert, nused, xs, wg, bg, wu, bu, wd, bd)


def _final_kernel(pos_hbm, gt_ref, x1t_ref, ys_hbm, g_ref, b_ref, o_ref,
                  pos_smem, ybuf, sem_p, sem_g):
    i = pl.program_id(0)
    n = pl.num_programs(0)
    ft = FINAL_TILE
    slot_rows = TOP_K * ft * ROW_TILES

    ng = ft // DMA_GROUP

    def pos_copy(tile, slot):
        return pltpu.make_async_copy(pos_hbm.at[pl.ds(tile * ng, ng)], pos_smem.at[slot], sem_p.at[slot])

    def gather_rows(pslot, yslot):
        def body(g, c):
            src = [pos_smem[pslot, g, j] * ROW_TILES for j in range(DMA_GROUP * TOP_K)]
            base = yslot * slot_rows + g * (DMA_GROUP * ROW_TILES)
            for u in range(DMA_GROUP):
                for k in range(TOP_K):
                    dst = base + (k * ft + u) * ROW_TILES
                    pltpu.make_async_copy(ys_hbm.at[pl.ds(src[u * TOP_K + k], ROW_TILES)],
                                          ybuf.at[pl.ds(dst, ROW_TILES)],
                                          sem_g.at[yslot]).start(priority=k % 2)
            return c
        lax.fori_loop(0, ng, body, 0)

    def gather_wait(yslot):
        pltpu.make_async_copy(ys_hbm.at[pl.ds(0, slot_rows)],
                              ybuf.at[pl.ds(yslot * slot_rows, slot_rows)], sem_g.at[yslot]).wait()

    @pl.when(i == 0)
    def _():
        c = pos_copy(0, 0)
        c.start()
        c.wait()
        gather_rows(0, 0)

        @pl.when(n > 1)
        def _():
            pos_copy(1, 1).start()

    @pl.when(i + 1 < n)
    def _():
        pos_copy(i + 1, (i + 1) % 3).wait()
        gather_rows((i + 1) % 3, (i + 1) % 2)

    @pl.when(i + 2 < n)
    def _():
        pos_copy(i + 2, (i + 2) % 3).start()

    slot = i % 2
    gather_wait(slot)
    g = gt_ref[...]
    zs = []
    for c in range(ROW_TILES):
        f = None
        for k in range(TOP_K):
            yk = ybuf[pl.ds(slot * slot_rows + k * ft * ROW_TILES + c, ft, stride=ROW_TILES), :]
            term = g[:, k:k + 1] * yk
            f = term if f is None else f + term
        zs.append(DN_ALPHA * x1t_ref[pl.ds(c, ft, stride=ROW_TILES), :] + f)
    o_ref[...] = _layer_norm(jnp.concatenate(zs, axis=1), g_ref[...], b_ref[...])


def _final(pos, gates_t, x1t, ys, g2, b2):
    ft = FINAL_TILE
    n = gates_t.shape[0]
    any_spec = pl.BlockSpec(memory_space=pl.ANY)
    return pl.pallas_call(
        _final_kernel,
        out_shape=jax.ShapeDtypeStruct((n, D_MODEL), F32),
        grid=(n // ft,),
        in_specs=[any_spec, pl.BlockSpec((ft, TOP_K), lambda i: (i, 0)),
                  pl.BlockSpec((ft * ROW_TILES, LANES), lambda i: (i, 0)), any_spec,
                  _full(g2), _full(b2)],
        out_specs=pl.BlockSpec((ft, D_MODEL), lambda i: (i, 0)),
        scratch_shapes=[
            pltpu.SMEM((3, ft // DMA_GROUP, DMA_GROUP * TOP_K), I32),
            pltpu.VMEM((2 * TOP_K * ft * ROW_TILES, LANES), F32),
            pltpu.SemaphoreType.DMA((3,)),
            pltpu.SemaphoreType.DMA((2,)),
        ],
        compiler_params=_params(("arbitrary",)),
        name="final",
    )(pos, gates_t, x1t, ys, g2, b2)


def _mix_weight(w_in):
    qa, ckv, qi, ki, wi, qb, kb, vb, _, _ = jnp.split(w_in, np.cumsum(SPLITS)[:-1].tolist(), axis=-1)
    pad = lambda a: jnp.pad(a, ((0, 0), (0, LANES - a.shape[1])))
    return jnp.concatenate([qa, qi, ckv, pad(ki), pad(wi), qb, kb, vb], axis=1).astype(BF16)


def _layer(x, w_in, kv_norm_g, idx_k_norm_g, idx_k_norm_b, w_uk, w_uv, rel_bias, w_branch_a,
           w_branch_b, w_out, ln1_g, ln1_b, w_router, b_router, w_gate, b_gate, w_up, b_up,
           w_down, b_down, ln2_g, ln2_b):
    bsz, seq, _ = x.shape
    n = bsz * seq
    assert seq % KEY_TILE == 0 and n % TOKEN_TILE == 0
    xf = x.reshape(n, D_MODEL)
    row = lambda v: v.reshape(1, -1).astype(F32)

    w_mix = _mix_weight(w_in)
    w_gates = w_in[:, sum(SPLITS[:8]):].astype(BF16)
    wuk_bd = jnp.zeros((A_WIDTH, A_HEADS * A_KV_RANK), F32)
    wuv_pad = jnp.zeros((A_HEADS, A_KV_RANK, LANES), F32)
    for h in range(A_HEADS):
        wuk_bd = wuk_bd.at[h * A_HEAD_DIM:(h + 1) * A_HEAD_DIM, h * A_KV_RANK:(h + 1) * A_KV_RANK].set(w_uk[h])
        c0 = (h % 2) * A_HEAD_DIM
        wuv_pad = wuv_pad.at[h, :, c0:c0 + A_HEAD_DIM].set(w_uv[h])
    wuk_bd, wuv_pad = wuk_bd.astype(BF16), wuv_pad.astype(BF16)

    qlat, qi, ckv, ki, wi, qb, kb, vb = _projection(
        xf, w_mix, wuk_bd, row(kv_norm_g), row(idx_k_norm_g), row(idx_k_norm_b))

    nqb, nkt = seq // Q_BLOCK, seq // KEY_TILE
    kidx = ki.reshape(bsz, nkt, KEY_TILE, IDX_DIM)
    ckv4 = ckv.reshape(bsz, nkt, KEY_TILE, A_KV_RANK)
    ckvt = ckv4.transpose(0, 1, 3, 2)
    qi_r = qi.reshape(bsz, nqb, Q_BLOCK, IDX_HEADS, IDX_DIM).transpose(0, 1, 3, 2, 4).reshape(
        bsz, nqb, IDX_HEADS * Q_BLOCK, IDX_DIM)
    wit = wi.reshape(bsz, seq, IDX_HEADS).transpose(0, 2, 1)
    ya = _mixer_a(kidx, ckv4, ckvt, qi_r, wit, qlat.reshape(bsz, seq, -1), wuv_pad, bsz, seq)

    front = lambda a: jnp.pad(a.reshape(bsz, seq, B_WIDTH), ((0, 0), (B_PAD, 0), (0, 0)))
    yb = _mixer_b(qb.reshape(bsz, seq, B_WIDTH), front(kb), front(vb), _band_bias(rel_bias), bsz, seq)

    x1t, topi, gates = _merge(
        xf, ya.reshape(n, A_WIDTH), yb.reshape(n, B_WIDTH), w_gates, w_branch_a.astype(BF16),
        w_branch_b.astype(BF16), w_out.astype(BF16), row(ln1_g), row(ln1_b),
        w_router.T.astype(BF16), b_router.reshape(-1, 1).astype(F32))

    blk = MOE_BLOCK
    cap = n * TOP_K + N_EXPERTS * blk
    nblk = cap // blk
    rank, cnt = _route(topi)
    counts = cnt[:, 0]
    padded = (counts + blk - 1) // blk * blk
    pends = jnp.cumsum(padded).astype(I32)
    pstarts = pends - padded
    nused = (pends[-1:] // blk).astype(I32)
    blk_start = jnp.arange(nblk, dtype=I32) * blk
    blk_expert = jnp.minimum(jnp.sum(blk_start[:, None] >= pends[None, :], axis=1), N_EXPERTS - 1).astype(I32)

    pos = _positions(topi, rank, pstarts)
    pos_g = pos.T.reshape(n // DMA_GROUP, DMA_GROUP * TOP_K)
    xs = _dispatch(counts, pstarts, pos_g, x1t, cap)
    ys = _moe(blk_expert, nused, xs, w_gate, b_gate[:, None, :], w_up, b_up[:, None, :],
              w_down, b_down[:, None, :])
    out = _final(pos_g, gates.T, x1t, ys, row(ln2_g), row(ln2_b))
    return out.reshape(bsz, seq, D_MODEL)


def kernel(x, w_in, kv_norm_g, idx_k_norm_g, idx_k_norm_b, w_uk, w_uv, rel_bias, w_branch_a,
           w_branch_b, w_out, ln1_g, ln1_b, w_router, b_router, w_gate, b_gate, w_up, b_up,
           w_down, b_down, ln2_g, ln2_b):
    for l in range(DEPTH):
        x = _layer(x, w_in[l], kv_norm_g[l], idx_k_norm_g[l], idx_k_norm_b[l], w_uk[l], w_uv[l],
                   rel_bias[l], w_branch_a[l], w_branch_b[l], w_out[l], ln1_g[l], ln1_b[l],
                   w_router[l], b_router[l], w_gate[l], b_gate[l], w_up[l], b_up[l], w_down[l],
                   b_down[l], ln2_g[l], ln2_b[l])
    return x
```

```python
import functools

import jax
import jax.numpy as jnp
import numpy as np
from jax import lax
from jax.experimental import pallas as pl
from jax.experimental.pallas import tpu as pltpu

F32 = jnp.float32
BF16 = jnp.bfloat16
I32 = jnp.int32

D_MODEL = 1024
CHUNK = 64
Q_BLOCK = 128
A_HEADS = 8
A_HEAD_DIM = 64
A_KV_RANK = 128
IDX_HEADS = 8
IDX_DIM = 64
IDX_TOPK = 256
IDX_SCALE = (IDX_HEADS * IDX_DIM) ** -0.5
B_HEADS = 8
B_HEAD_DIM = 64
B_LEFT_CHUNKS = 8
REL_CLIP = 128
N_EXPERTS = 32
TOP_K = 4
D_FF = 1024
SWIGLU_LIMIT = 7.0
SWIGLU_ALPHA = 1.702
DEPTH = 1
DN_ALPHA = (2.0 * DEPTH) ** 0.25
LN_EPS = 1e-5

A_WIDTH = A_HEADS * A_HEAD_DIM
B_WIDTH = B_HEADS * B_HEAD_DIM
SPLITS = [A_WIDTH, A_KV_RANK, IDX_HEADS * IDX_DIM, IDX_DIM, IDX_HEADS,
          B_WIDTH, B_WIDTH, B_WIDTH, D_MODEL, D_MODEL]

LANES = 128
SUBLANES = 8
KEY_TILE = 256
B_QTILE = 512
B_PAD = B_LEFT_CHUNKS * CHUNK
B_WIN = B_PAD + B_QTILE
TOKEN_TILE = 512
MATMUL_TILE = 1024
MOE_BLOCK = 512
FINAL_TILE = 512
ROW_TILES = D_MODEL // LANES
DMA_GROUP = 4
INT_MIN = -2 ** 31
VMEM_LIMIT = 56 * 1024 * 1024

A_SLOPES = [2.0 ** (-8.0 * (h + 1) / A_HEADS) for h in range(A_HEADS)]

_C_QA = 0
_C_QI = _C_QA + A_WIDTH
_C_CKV = _C_QI + IDX_HEADS * IDX_DIM
_C_KI = _C_CKV + A_KV_RANK
_C_WI = _C_KI + LANES
_C_QKVB = _C_WI + LANES
_C_END = _C_QKVB + 3 * B_WIDTH


def _params(sem, vmem=VMEM_LIMIT):
    return pltpu.CompilerParams(dimension_semantics=sem, vmem_limit_bytes=vmem)


def _full(a):
    return pl.BlockSpec(a.shape, lambda *_: (0,) * a.ndim)


def _proj_kernel(x_ref, w_ref, wuk_ref, kvg_ref, kig_ref, kib_ref,
                 qlat_ref, qi_ref, ckv_ref, ki_ref, wi_ref, qb_ref, kb_ref, vb_ref):
    xb = x_ref[...].astype(BF16)

    def mm(c0, c1):
        return jnp.dot(xb, w_ref[:, c0:c1], preferred_element_type=F32)

    qa = mm(_C_QA, _C_QI)
    qlat = jnp.dot(qa.astype(BF16), wuk_ref[...], preferred_element_type=F32)
    qlat_ref[...] = (qlat * (A_HEAD_DIM ** -0.5)).astype(BF16)
    qi_ref[...] = mm(_C_QI, _C_CKV).astype(BF16)

    ckv = mm(_C_CKV, _C_KI)
    ms = jnp.mean(ckv * ckv, axis=-1, keepdims=True)
    ckv_ref[...] = ((ckv * lax.rsqrt(ms + LN_EPS)) * kvg_ref[...]).astype(BF16)

    ki = mm(_C_KI, _C_WI)[:, :IDX_DIM]
    mu = jnp.mean(ki, axis=-1, keepdims=True)
    var = jnp.mean(jnp.square(ki - mu), axis=-1, keepdims=True)
    ki_ref[...] = (((ki - mu) * lax.rsqrt(var + LN_EPS)) * kig_ref[...] + kib_ref[...]).astype(BF16)

    wi_ref[...] = mm(_C_WI, _C_QKVB)[:, :IDX_HEADS]
    qb_ref[...] = mm(_C_QKVB, _C_QKVB + B_WIDTH).astype(BF16)
    kb_ref[...] = mm(_C_QKVB + B_WIDTH, _C_QKVB + 2 * B_WIDTH).astype(BF16)
    vb_ref[...] = mm(_C_QKVB + 2 * B_WIDTH, _C_END).astype(BF16)


def _projection(xf, w_mix, wuk_bd, kvg, kig, kib):
    n, tm = xf.shape[0], min(MATMUL_TILE, xf.shape[0])
    row = lambda w: pl.BlockSpec((tm, w), lambda i: (i, 0))
    outs = [(A_HEADS * A_KV_RANK, BF16), (IDX_HEADS * IDX_DIM, BF16), (A_KV_RANK, BF16),
            (IDX_DIM, BF16), (IDX_HEADS, F32), (B_WIDTH, BF16), (B_WIDTH, BF16), (B_WIDTH, BF16)]
    return pl.pallas_call(
        _proj_kernel,
        out_shape=[jax.ShapeDtypeStruct((n, w), dt) for w, dt in outs],
        grid=(n // tm,),
        in_specs=[row(D_MODEL), _full(w_mix), _full(wuk_bd), _full(kvg), _full(kig), _full(kib)],
        out_specs=[row(w) for w, _ in outs],
        compiler_params=_params(("parallel",)),
        name="proj",
    )(xf, w_mix, wuk_bd, kvg, kig, kib)


def _mixer_a_kernel(kidx_ref, ckv_ref, ckvt_ref, qi_ref, wit_ref, qlat_ref, wuv_ref,
                    ya_ref, keys_ref, s_ref, acc_ref, *, seq, topk):
    i = pl.program_id(1)
    nkt = i // 2 + 1
    nsub = KEY_TILE // SUBLANES
    lane = lax.broadcasted_iota(I32, (1, LANES), 1)
    qpos = i * Q_BLOCK + lane
    qchunk = qpos >> 6
    row_iota = lax.broadcasted_iota(I32, (KEY_TILE, LANES), 0)

    qi_blk = qi_ref[0, 0]
    w_all = wit_ref[0] * IDX_SCALE

    def idx_body(kt, carry):
        lg = pl.dot(kidx_ref[0, kt], qi_blk, trans_b=True)
        isc = jnp.zeros((KEY_TILE, LANES), F32)
        for h in range(IDX_HEADS):
            isc = isc + jnp.maximum(lg[:, h * LANES:(h + 1) * LANES], 0.0) * w_all[h:h + 1, :]
        bits = lax.bitcast_convert_type(isc + 0.0, I32)
        key = bits ^ ((bits >> 31) & 0x7FFFFFFF)
        spos = kt * KEY_TILE + row_iota
        key = jnp.where((spos >> 6) <= qchunk, key, INT_MIN)
        keys_ref[kt] = key
        return carry

    def over_tiles(body, init):
        def pair(j, c):
            return body(2 * j + 1, body(2 * j, c))
        c = lax.fori_loop(0, nkt // 2, pair, init)
        return lax.cond(nkt % 2 == 1, lambda c: body(nkt - 1, c), lambda c: c, c)

    over_tiles(idx_body, 0)

    def count(pred):
        def body(kt, acc):
            spos = kt * KEY_TILE + row_iota
            m = jnp.where(pred(keys_ref[kt], spos), 1, 0).astype(I32)
            return acc + jnp.sum(m.reshape(nsub, SUBLANES, LANES), axis=0)
        acc = over_tiles(body, jnp.zeros((SUBLANES, LANES), I32))
        return jnp.sum(acc, axis=0, keepdims=True)

    def bit_body(it, carry):
        thr, cge = carry
        cand = thr ^ lax.shift_left(jnp.int32(1), 31 - it)
        cnt = count(lambda kv, spos: kv >= cand)
        ok = cnt >= topk
        return jnp.where(ok, cand, thr), jnp.where(ok, cnt, cge)

    start = (jnp.full((1, LANES), INT_MIN, I32), jnp.zeros((1, LANES), I32))
    thr, cge = lax.cond(nkt * KEY_TILE > topk,
                        lambda: lax.fori_loop(0, 32, bit_body, start), lambda: start)

    nbits = int(np.log2(seq)) + 1
    has_thr = thr > INT_MIN
    need_tie = jnp.max(jnp.where(has_thr, cge, 0)) > topk

    def tie_fn():
        room = topk - count(lambda kv, spos: kv > thr)

        def jb(it, cut):
            cand = cut | lax.shift_left(jnp.int32(1), nbits - 1 - it)
            f = count(lambda kv, spos: (kv == thr) & (spos < cand))
            return jnp.where(f <= room, cand, cut)
        return lax.fori_loop(0, nbits, jb, jnp.zeros((1, LANES), I32))

    cut = lax.cond(need_tie, tie_fn, lambda: jnp.full((1, LANES), 2 * seq, I32))
    cut = jnp.where(has_thr, cut, 0)

    qlat = qlat_ref[0]
    qr = jnp.concatenate([qlat[:, h * LANES:(h + 1) * LANES] for h in range(A_HEADS)], axis=0)

    def sc_body(kt, m):
        st = pl.dot(ckv_ref[0, kt], qr, trans_b=True)
        kv = keys_ref[kt]
        spos = kt * KEY_TILE + row_iota
        sel = (kv > thr) | ((kv == thr) & (spos < cut))
        dist = jnp.abs(qpos - spos).astype(F32)
        out = []
        for h in range(A_HEADS):
            sh = st[:, h * LANES:(h + 1) * LANES] - A_SLOPES[h] * dist
            sh = jnp.where(sel, sh, -jnp.inf)
            s_ref[kt, :, h * LANES:(h + 1) * LANES] = sh
            out.append(jnp.maximum(m[h], jnp.max(sh.reshape(nsub, SUBLANES, LANES), axis=0)))
        return tuple(out)

    m8 = over_tiles(sc_body, tuple(jnp.full((SUBLANES, LANES), -jnp.inf, F32) for _ in range(A_HEADS)))
    mx = [jnp.max(m, axis=0, keepdims=True) for m in m8]

    acc_ref[...] = jnp.zeros_like(acc_ref)

    def pv_body(kt, l):
        out, ps = [], []
        for h in range(A_HEADS):
            p = jnp.exp(s_ref[kt, :, h * LANES:(h + 1) * LANES] - mx[h])
            out.append(l[h] + jnp.sum(p.reshape(nsub, SUBLANES, LANES), axis=0))
            ps.append(p.astype(BF16))
        acc_ref[...] += jnp.dot(ckvt_ref[0, kt], jnp.concatenate(ps, axis=1),
                                preferred_element_type=F32)
        return tuple(out)

    l8 = over_tiles(pv_body, tuple(jnp.zeros((SUBLANES, LANES), F32) for _ in range(A_HEADS)))
    inv = [1.0 / jnp.sum(l, axis=0, keepdims=True) for l in l8]

    for j in range(A_HEADS // 2):
        y = None
        for h in (2 * j, 2 * j + 1):
            o = (acc_ref[:, h * LANES:(h + 1) * LANES] * inv[h]).astype(BF16)
            t = pl.dot(o, wuv_ref[h], trans_a=True)
            y = t if y is None else y + t
        ya_ref[0, :, j * LANES:(j + 1) * LANES] = y.astype(BF16)


def _mixer_a(kidx, ckv, ckvt, qi_r, wit, qlat, wuv_pad, bsz, seq):
    nqb = seq // Q_BLOCK
    nkt = seq // KEY_TILE
    topk = min(IDX_TOPK, seq // 4)
    kern = functools.partial(_mixer_a_kernel, seq=seq, topk=topk)
    return pl.pallas_call(
        kern,
        out_shape=jax.ShapeDtypeStruct((bsz, seq, A_WIDTH), BF16),
        grid=(bsz, nqb),
        in_specs=[
            pl.BlockSpec((1, nkt, KEY_TILE, IDX_DIM), lambda b, i: (b, 0, 0, 0)),
            pl.BlockSpec((1, nkt, KEY_TILE, A_KV_RANK), lambda b, i: (b, 0, 0, 0)),
            pl.BlockSpec((1, nkt, A_KV_RANK, KEY_TILE), lambda b, i: (b, 0, 0, 0)),
            pl.BlockSpec((1, 1, IDX_HEADS * Q_BLOCK, IDX_DIM), lambda b, i: (b, i, 0, 0)),
            pl.BlockSpec((1, IDX_HEADS, Q_BLOCK), lambda b, i: (b, 0, i)),
            pl.BlockSpec((1, Q_BLOCK, A_HEADS * A_KV_RANK), lambda b, i: (b, i, 0)),
            pl.BlockSpec(wuv_pad.shape, lambda b, i: (0, 0, 0)),
        ],
        out_specs=pl.BlockSpec((1, Q_BLOCK, A_WIDTH), lambda b, i: (b, i, 0)),
        scratch_shapes=[
            pltpu.VMEM((nkt, KEY_TILE, LANES), I32),
            pltpu.VMEM((nkt, KEY_TILE, A_HEADS * LANES), F32),
            pltpu.VMEM((A_KV_RANK, A_HEADS * LANES), F32),
        ],
        compiler_params=_params(("parallel", "arbitrary")),
        name="mixer_a",
    )(kidx, ckv, ckvt, qi_r, wit, qlat, wuv_pad)


def _mixer_b_kernel(q_ref, k_ref, v_ref, line_ref, yb_ref, bias_ref):
    @pl.when((pl.program_id(0) == 0) & (pl.program_id(1) == 0))
    def _():
        tq = lax.broadcasted_iota(I32, (B_QTILE, B_WIN), 0) >> 6
        kj = lax.broadcasted_iota(I32, (B_QTILE, B_WIN), 1) >> 6
        band = (kj >= tq) & (kj <= tq + B_LEFT_CHUNKS)
        for h in range(B_HEADS):
            rows = jnp.broadcast_to(line_ref[h:h + 1, :], (B_QTILE, B_QTILE + B_WIN))
            rolled = pltpu.roll(rows, 1, 1, stride=1, stride_axis=0)
            bias_ref[h] = jnp.where(band, rolled[:, B_QTILE:], -jnp.inf)

    t0 = pl.multiple_of(pl.program_id(1) * B_QTILE, B_QTILE)
    kw = k_ref[0, pl.ds(t0, B_WIN), :]
    vw = v_ref[0, pl.ds(t0, B_WIN), :]
    q = q_ref[0]
    kvalid = (t0 - B_PAD + lax.broadcasted_iota(I32, (1, B_WIN), 1)) >= 0
    lane = lax.broadcasted_iota(I32, (1, LANES), 1)
    scale = B_HEAD_DIM ** -0.5
    for j in range(B_HEADS // 2):
        sl = slice(j * LANES, (j + 1) * LANES)
        qs, ks, vs = q[:, sl], kw[:, sl], vw[:, sl]
        outs = []
        for hh in range(2):
            mine = (lane >= B_HEAD_DIM) if hh else (lane < B_HEAD_DIM)
            qm = jnp.where(mine, qs, jnp.zeros_like(qs))
            s = pl.dot(qm, ks, trans_b=True) * scale + bias_ref[2 * j + hh]
            s = jnp.where(kvalid, s, -jnp.inf)
            m = jnp.max(s, axis=-1, keepdims=True)
            p = jnp.exp(s - m)
            l = jnp.sum(p, axis=-1, keepdims=True)
            outs.append(jnp.dot(p.astype(BF16), vs, preferred_element_type=F32) / l)
        yb_ref[0, :, sl] = jnp.where(lane < B_HEAD_DIM, outs[0], outs[1]).astype(BF16)


def _band_line(rel_bias):
    m = np.arange(B_QTILE + B_WIN)
    return rel_bias[:, np.clip(B_WIN - 1 - m, -REL_CLIP, REL_CLIP) + REL_CLIP].astype(F32)


def _mixer_b(qb, kb_pad, vb_pad, line, bsz, seq):
    return pl.pallas_call(
        _mixer_b_kernel,
        out_shape=jax.ShapeDtypeStruct((bsz, seq, B_WIDTH), BF16),
        grid=(bsz, seq // B_QTILE),
        in_specs=[
            pl.BlockSpec((1, B_QTILE, B_WIDTH), lambda b, i: (b, i, 0)),
            pl.BlockSpec((1, seq + B_PAD, B_WIDTH), lambda b, i: (b, 0, 0)),
            pl.BlockSpec((1, seq + B_PAD, B_WIDTH), lambda b, i: (b, 0, 0)),
            pl.BlockSpec(line.shape, lambda b, i: (0, 0)),
        ],
        out_specs=pl.BlockSpec((1, B_QTILE, B_WIDTH), lambda b, i: (b, i, 0)),
        scratch_shapes=[pltpu.VMEM((B_HEADS, B_QTILE, B_WIN), F32)],
        compiler_params=_params(("arbitrary", "arbitrary")),
        name="mixer_b",
    )(qb, kb_pad, vb_pad, line)


def _layer_norm(z, g, b):
    mu = jnp.mean(z, axis=-1, keepdims=True)
    var = jnp.mean(jnp.square(z - mu), axis=-1, keepdims=True)
    return ((z - mu) * lax.rsqrt(var + LN_EPS)) * g + b


def _merge_kernel(x_ref, ya_ref, yb_ref, wg_ref, wba_ref, wbb_ref, wo_ref, g_ref, b_ref,
                  wr_ref, br_ref, x1t_ref, topi_ref, gate_ref):
    tm = x_ref.shape[0]
    x = x_ref[...]
    xb = x.astype(BF16)
    ga = jnp.dot(xb, wg_ref[:, :D_MODEL], preferred_element_type=F32)
    a = jnp.dot(ya_ref[...], wba_ref[...], preferred_element_type=F32)
    merged = jax.nn.sigmoid(ga) * a
    gb = jnp.dot(xb, wg_ref[:, D_MODEL:], preferred_element_type=F32)
    b = jnp.dot(yb_ref[...], wbb_ref[...], preferred_element_type=F32)
    merged = merged + jax.nn.sigmoid(gb) * b
    out = jnp.dot(merged.astype(BF16), wo_ref[...], preferred_element_type=F32)
    x1 = _layer_norm(DN_ALPHA * x + out, g_ref[...], b_ref[...])
    for c in range(ROW_TILES):
        x1t_ref[pl.ds(c, tm, stride=ROW_TILES), :] = x1[:, c * LANES:(c + 1) * LANES]

    lg = pl.dot(wr_ref[...], x1.astype(BF16), trans_b=True) + br_ref[...]
    eidx = lax.broadcasted_iota(I32, lg.shape, 0)
    vals, idxs = [], []
    for _ in range(TOP_K):
        m = jnp.max(lg, axis=0, keepdims=True)
        sel = jnp.min(jnp.where(lg == m, eidx, N_EXPERTS), axis=0, keepdims=True)
        vals.append(m)
        idxs.append(sel)
        lg = jnp.where(eidx == sel, -jnp.inf, lg)
    es = [jnp.exp(v - vals[0]) for v in vals]
    tot = es[0] + es[1] + es[2] + es[3]
    topi_ref[...] = jnp.concatenate(idxs, axis=0)
    gate_ref[...] = jnp.concatenate([e / tot for e in es], axis=0)


def _merge(xf, ya, yb, wg, wba, wbb, wo, g1, b1, wr_t, br):
    n, tm = xf.shape[0], min(MATMUL_TILE, xf.shape[0])
    row = lambda w: pl.BlockSpec((tm, w), lambda i: (i, 0))
    col = pl.BlockSpec((TOP_K, tm), lambda i: (0, i))
    return pl.pallas_call(
        _merge_kernel,
        out_shape=[jax.ShapeDtypeStruct((n * ROW_TILES, LANES), F32),
                   jax.ShapeDtypeStruct((TOP_K, n), I32),
                   jax.ShapeDtypeStruct((TOP_K, n), F32)],
        grid=(n // tm,),
        in_specs=[row(D_MODEL), row(A_WIDTH), row(B_WIDTH), _full(wg), _full(wba), _full(wbb),
                  _full(wo), _full(g1), _full(b1), _full(wr_t), _full(br)],
        out_specs=[pl.BlockSpec((tm * ROW_TILES, LANES), lambda i: (i, 0)), col, col],
        compiler_params=_params(("parallel",)),
        name="merge",
    )(xf, ya, yb, wg, wba, wbb, wo, g1, b1, wr_t, br)


def _one_hot_rows(topi, k, width):
    eidx = lax.broadcasted_iota(I32, (N_EXPERTS, width), 0)
    return eidx == topi[k:k + 1, :]


def _route_kernel(topi_ref, tri_ref, rank_ref, cnt_ref, run_ref):
    @pl.when(pl.program_id(0) == 0)
    def _():
        run_ref[...] = jnp.zeros_like(run_ref)

    topi = topi_ref[...]
    tr = topi.shape[1]
    hot = [_one_hot_rows(topi, k, tr) for k in range(TOP_K)]
    oh = jnp.concatenate([jnp.where(h, 1.0, 0.0).astype(BF16) for h in hot], axis=0)
    prefix = jnp.dot(oh, tri_ref[...], preferred_element_type=F32)
    base = run_ref[...]
    ranks = []
    for k in range(TOP_K):
        tbl = prefix[k * N_EXPERTS:(k + 1) * N_EXPERTS, :] + base
        ranks.append(jnp.sum(jnp.where(hot[k], tbl, 0.0), axis=0, keepdims=True))
        base = base + jnp.sum(jnp.where(hot[k], 1.0, 0.0), axis=1, keepdims=True)
    rank_ref[...] = jnp.concatenate(ranks, axis=0).astype(I32)
    run_ref[...] = base
    cnt_ref[...] = base.astype(I32)


def _route(topi):
    n, tr = topi.shape[1], TOKEN_TILE
    tri = (np.arange(tr)[:, None] < np.arange(tr)[None, :]).astype(np.float32)
    tri = jnp.asarray(tri, BF16)
    col = pl.BlockSpec((TOP_K, tr), lambda i: (0, i))
    return pl.pallas_call(
        _route_kernel,
        out_shape=[jax.ShapeDtypeStruct((TOP_K, n), I32), jax.ShapeDtypeStruct((N_EXPERTS, 1), I32)],
        grid=(n // tr,),
        in_specs=[col, _full(tri)],
        out_specs=[col, pl.BlockSpec((N_EXPERTS, 1), lambda i: (0, 0))],
        scratch_shapes=[pltpu.VMEM((N_EXPERTS, 1), F32)],
        compiler_params=_params(("arbitrary",)),
        name="route",
    )(topi, tri)


def _positions_kernel(topi_ref, rank_ref, pstv_ref, pos_ref):
    topi = topi_ref[...]
    rows = []
    for k in range(TOP_K):
        start = jnp.sum(jnp.where(_one_hot_rows(topi, k, topi.shape[1]), pstv_ref[...], 0),
                        axis=0, keepdims=True)
        rows.append(rank_ref[k:k + 1, :] + start)
    pos_ref[...] = jnp.concatenate(rows, axis=0)


def _positions(topi, rank, pstarts):
    n, td = topi.shape[1], TOKEN_TILE
    col = pl.BlockSpec((TOP_K, td), lambda i: (0, i))
    pstv = pstarts.reshape(N_EXPERTS, 1)
    return pl.pallas_call(
        _positions_kernel,
        out_shape=jax.ShapeDtypeStruct((TOP_K, n), I32),
        grid=(n // td,),
        in_specs=[col, col, _full(pstv)],
        out_specs=col,
        compiler_params=_params(("parallel",)),
        name="positions",
    )(topi, rank, pstv)


def _dispatch_kernel(cnt_ref, pst_ref, pos_hbm, x_hbm, xs_hbm,
                     pos_smem, xbuf, zbuf, sem_p, sem_x, sem_d, sem_z):
    i = pl.program_id(0)
    n = pl.num_programs(0)
    ng = pos_smem.shape[1]
    td = ng * DMA_GROUP
    tile_rows = td * ROW_TILES

    def pos_copy(step, slot):
        return pltpu.make_async_copy(pos_hbm.at[pl.ds(step * ng, ng)], pos_smem.at[slot], sem_p.at[slot])

    def x_copy(step, slot):
        return pltpu.make_async_copy(x_hbm.at[pl.ds(step * tile_rows, tile_rows)], xbuf.at[slot],
                                     sem_x.at[slot])

    @pl.when(i == 0)
    def _():
        pos_copy(0, 0).start()
        x_copy(0, 0).start()

    @pl.when(i + 1 < n)
    def _():
        pos_copy(i + 1, (i + 1) % 2).start()
        x_copy(i + 1, (i + 1) % 3).start()

    slot = i % 2
    xslot = i % 3
    pos_copy(i, slot).wait()
    x_copy(i, xslot).wait()

    def step_bytes(s):
        rows_ = TOP_K * tile_rows
        return pltpu.make_async_copy(xs_hbm.at[pl.ds(0, rows_)], xs_hbm.at[pl.ds(0, rows_)], sem_d.at[s])

    def body(g, c):
        dst = [pos_smem[slot, g, j] * ROW_TILES for j in range(DMA_GROUP * TOP_K)]
        for u in range(DMA_GROUP):
            src = xbuf.at[xslot, pl.ds((g * DMA_GROUP + u) * ROW_TILES, ROW_TILES)]
            for k in range(TOP_K):
                pltpu.make_async_copy(src, xs_hbm.at[pl.ds(dst[u * TOP_K + k], ROW_TILES)],
                                      sem_d.at[xslot]).start(priority=k % 2)
        return c
    lax.fori_loop(0, ng, body, 0)

    @pl.when(i > 0)
    def _():
        step_bytes((i + 2) % 3).wait()

    @pl.when(i == n - 1)
    def _():
        step_bytes(xslot).wait()
        zbuf[...] = jnp.zeros_like(zbuf)

        def zero_copy(r):
            return pltpu.make_async_copy(zbuf, xs_hbm.at[pl.ds(r * ROW_TILES, ROW_TILES)], sem_z)

        def zero_rows(lo, hi):
            def start(r, c2):
                zero_copy(r).start()
                return c2

            def wait(r, c2):
                zero_copy(r).wait()
                return c2
            lax.fori_loop(lo, hi, start, 0)
            lax.fori_loop(lo, hi, wait, 0)

        def padded_end(e):
            return pst_ref[e] + (cnt_ref[e] + MOE_BLOCK - 1) // MOE_BLOCK * MOE_BLOCK

        def ebody(e, c):
            zero_rows(pst_ref[e] + cnt_ref[e], padded_end(e))
            return c
        lax.fori_loop(0, N_EXPERTS, ebody, 0)
        zero_rows(padded_end(N_EXPERTS - 1), xs_hbm.shape[0] // ROW_TILES)


def _dispatch(counts, pstarts, pos_g, x1t, cap):
    n, td = pos_g.shape[0] * DMA_GROUP, TOKEN_TILE
    any_spec = pl.BlockSpec(memory_space=pl.ANY)
    return pl.pallas_call(
        _dispatch_kernel,
        out_shape=jax.ShapeDtypeStruct((cap * ROW_TILES, LANES), F32),
        grid_spec=pltpu.PrefetchScalarGridSpec(
            num_scalar_prefetch=2,
            grid=(n // td,),
            in_specs=[any_spec, any_spec],
            out_specs=any_spec,
            scratch_shapes=[
                pltpu.SMEM((2, td // DMA_GROUP, DMA_GROUP * TOP_K), I32),
                pltpu.VMEM((3, td * ROW_TILES, LANES), F32),
                pltpu.VMEM((ROW_TILES, LANES), F32),
                pltpu.SemaphoreType.DMA((2,)),
                pltpu.SemaphoreType.DMA((3,)),
                pltpu.SemaphoreType.DMA((3,)),
                pltpu.SemaphoreType.DMA,
            ],
        ),
        compiler_params=_params(("arbitrary",)),
        name="dispatch",
    )(counts, pstarts, pos_g, x1t)


def _moe_kernel(be_ref, nu_ref, xs_ref, wg_ref, bg_ref, wu_ref, bu_ref, wd_ref, bd_ref, ys_ref,
                wg_bf, wu_bf, wd_bf):
    i = pl.program_id(0)
    blk = MOE_BLOCK
    used = i < nu_ref[0]

    @pl.when(jnp.logical_or(i == 0, be_ref[i] != be_ref[jnp.maximum(i - 1, 0)]))
    def _():
        wg_bf[...] = wg_ref[0].astype(BF16)
        wu_bf[...] = wu_ref[0].astype(BF16)
        wd_bf[...] = wd_ref[0].astype(BF16)

    @pl.when(used)
    def _():
        xb = jnp.concatenate([xs_ref[pl.ds(c, blk, stride=ROW_TILES), :].astype(BF16)
                              for c in range(ROW_TILES)], axis=1)
        hg = jnp.minimum(jnp.dot(xb, wg_bf[...], preferred_element_type=F32) + bg_ref[0], SWIGLU_LIMIT)
        hu = jnp.clip(jnp.dot(xb, wu_bf[...], preferred_element_type=F32) + bu_ref[0],
                      -SWIGLU_LIMIT, SWIGLU_LIMIT)
        h = (hu + 1.0) * (hg * jax.nn.sigmoid(SWIGLU_ALPHA * hg))
        y = jnp.dot(h.astype(BF16), wd_bf[...], preferred_element_type=F32) + bd_ref[0]
        for c in range(ROW_TILES):
            ys_ref[pl.ds(c, blk, stride=ROW_TILES), :] = y[:, c * LANES:(c + 1) * LANES]

    @pl.when(jnp.logical_not(used))
    def _():
        ys_ref[...] = jnp.zeros_like(ys_ref)


def _moe(blk_expert, nused, xs, wg, bg, wu, bu, wd, bd):
    blk = MOE_BLOCK
    nblk = xs.shape[0] // (blk * ROW_TILES)
    wspec = lambda a: pl.BlockSpec((1,) + a.shape[1:], lambda i, be, nu: (be[i], 0, 0))
    xspec = pl.BlockSpec((blk * ROW_TILES, LANES), lambda i, be, nu: (jnp.minimum(i, nu[0] - 1), 0))
    return pl.pallas_call(
        _moe_kernel,
        out_shape=jax.ShapeDtypeStruct(xs.shape, F32),
        grid_spec=pltpu.PrefetchScalarGridSpec(
            num_scalar_prefetch=2,
            grid=(nblk,),
            in_specs=[xspec, wspec(wg), wspec(bg), wspec(wu), wspec(bu), wspec(wd), wspec(bd)],
            out_specs=pl.BlockSpec((blk * ROW_TILES, LANES), lambda i, be, nu: (i, 0)),
            scratch_shapes=[pltpu.VMEM(wg.shape[1:], BF16), pltpu.VMEM(wu.shape[1:], BF16),
                            pltpu.VMEM(wd.shape[1:], BF16)],
        ),
        compiler_params=_params(("arbitrary",)),
        name="moe",
    )(blk_expert, nused, xs, wg, bg, wu, bu, wd, bd)


def _final_kernel(pos_hbm, gt_ref, x1t_ref, ys_hbm, g_ref, b_ref, o_ref,
                  pos_smem, ybuf, sem_p, sem_g):
    i = pl.program_id(0)
    n = pl.num_programs(0)
    ft = FINAL_TILE
    slot_rows = TOP_K * ft * ROW_TILES

    ng = ft // DMA_GROUP

    def pos_copy(tile, slot):
        return pltpu.make_async_copy(pos_hbm.at[pl.ds(tile * ng, ng)], pos_smem.at[slot], sem_p.at[slot])

    def gather_rows(pslot, yslot):
        def body(g, c):
            src = [pos_smem[pslot, g, j] * ROW_TILES for j in range(DMA_GROUP * TOP_K)]
            base = yslot * slot_rows + g * (DMA_GROUP * ROW_TILES)
            for u in range(DMA_GROUP):
                for k in range(TOP_K):
                    dst = base + (k * ft + u) * ROW_TILES
                    pltpu.make_async_copy(ys_hbm.at[pl.ds(src[u * TOP_K + k], ROW_TILES)],
                                          ybuf.at[pl.ds(dst, ROW_TILES)],
                                          sem_g.at[yslot]).start(priority=k % 2)
            return c
        lax.fori_loop(0, ng, body, 0)

    def gather_wait(yslot):
        pltpu.make_async_copy(ys_hbm.at[pl.ds(0, slot_rows)],
                              ybuf.at[pl.ds(yslot * slot_rows, slot_rows)], sem_g.at[yslot]).wait()

    @pl.when(i == 0)
    def _():
        c = pos_copy(0, 0)
        c.start()
        c.wait()
        gather_rows(0, 0)

        @pl.when(n > 1)
        def _():
            pos_copy(1, 1).start()

    @pl.when(i + 1 < n)
    def _():
        pos_copy(i + 1, (i + 1) % 3).wait()
        gather_rows((i + 1) % 3, (i + 1) % 2)

    @pl.when(i + 2 < n)
    def _():
        pos_copy(i + 2, (i + 2) % 3).start()

    slot = i % 2
    gather_wait(slot)
    g = gt_ref[...]
    zs = []
    for c in range(ROW_TILES):
        f = None
        for k in range(TOP_K):
            yk = ybuf[pl.ds(slot * slot_rows + k * ft * ROW_TILES + c, ft, stride=ROW_TILES), :]
            term = g[:, k:k + 1] * yk
            f = term if f is None else f + term
        zs.append(DN_ALPHA * x1t_ref[pl.ds(c, ft, stride=ROW_TILES), :] + f)
    o_ref[...] = _layer_norm(jnp.concatenate(zs, axis=1), g_ref[...], b_ref[...])


def _final(pos, gates_t, x1t, ys, g2, b2):
    ft = FINAL_TILE
    n = gates_t.shape[0]
    any_spec = pl.BlockSpec(memory_space=pl.ANY)
    return pl.pallas_call(
        _final_kernel,
        out_shape=jax.ShapeDtypeStruct((n, D_MODEL), F32),
        grid=(n // ft,),
        in_specs=[any_spec, pl.BlockSpec((ft, TOP_K), lambda i: (i, 0)),
                  pl.BlockSpec((ft * ROW_TILES, LANES), lambda i: (i, 0)), any_spec,
                  _full(g2), _full(b2)],
        out_specs=pl.BlockSpec((ft, D_MODEL), lambda i: (i, 0)),
        scratch_shapes=[
            pltpu.SMEM((3, ft // DMA_GROUP, DMA_GROUP * TOP_K), I32),
            pltpu.VMEM((2 * TOP_K * ft * ROW_TILES, LANES), F32),
            pltpu.SemaphoreType.DMA((3,)),
            pltpu.SemaphoreType.DMA((2,)),
        ],
        compiler_params=_params(("arbitrary",)),
        name="final",
    )(pos, gates_t, x1t, ys, g2, b2)


def _mix_weight(w_in):
    qa, ckv, qi, ki, wi, qb, kb, vb, _, _ = jnp.split(w_in, np.cumsum(SPLITS)[:-1].tolist(), axis=-1)
    pad = lambda a: jnp.pad(a, ((0, 0), (0, LANES - a.shape[1])))
    return jnp.concatenate([qa, qi, ckv, pad(ki), pad(wi), qb, kb, vb], axis=1).astype(BF16)


def _layer(x, w_in, kv_norm_g, idx_k_norm_g, idx_k_norm_b, w_uk, w_uv, rel_bias, w_branch_a,
           w_branch_b, w_out, ln1_g, ln1_b, w_router, b_router, w_gate, b_gate, w_up, b_up,
           w_down, b_down, ln2_g, ln2_b):
    bsz, seq, _ = x.shape
    n = bsz * seq
    assert seq % KEY_TILE == 0 and n % TOKEN_TILE == 0
    xf = x.reshape(n, D_MODEL)
    row = lambda v: v.reshape(1, -1).astype(F32)

    w_mix = _mix_weight(w_in)
    w_gates = w_in[:, sum(SPLITS[:8]):].astype(BF16)
    wuk_bd = jnp.zeros((A_WIDTH, A_HEADS * A_KV_RANK), F32)
    wuv_pad = jnp.zeros((A_HEADS, A_KV_RANK, LANES), F32)
    for h in range(A_HEADS):
        wuk_bd = wuk_bd.at[h * A_HEAD_DIM:(h + 1) * A_HEAD_DIM, h * A_KV_RANK:(h + 1) * A_KV_RANK].set(w_uk[h])
        c0 = (h % 2) * A_HEAD_DIM
        wuv_pad = wuv_pad.at[h, :, c0:c0 + A_HEAD_DIM].set(w_uv[h])
    wuk_bd, wuv_pad = wuk_bd.astype(BF16), wuv_pad.astype(BF16)

    qlat, qi, ckv, ki, wi, qb, kb, vb = _projection(
        xf, w_mix, wuk_bd, row(kv_norm_g), row(idx_k_norm_g), row(idx_k_norm_b))

    nqb, nkt = seq // Q_BLOCK, seq // KEY_TILE
    kidx = ki.reshape(bsz, nkt, KEY_TILE, IDX_DIM)
    ckv4 = ckv.reshape(bsz, nkt, KEY_TILE, A_KV_RANK)
    ckvt = ckv4.transpose(0, 1, 3, 2)
    qi_r = qi.reshape(bsz, nqb, Q_BLOCK, IDX_HEADS, IDX_DIM).transpose(0, 1, 3, 2, 4).reshape(
        bsz, nqb, IDX_HEADS * Q_BLOCK, IDX_DIM)
    wit = wi.reshape(bsz, seq, IDX_HEADS).transpose(0, 2, 1)
    ya = _mixer_a(kidx, ckv4, ckvt, qi_r, wit, qlat.reshape(bsz, seq, -1), wuv_pad, bsz, seq)

    front = lambda a: jnp.pad(a.reshape(bsz, seq, B_WIDTH), ((0, 0), (B_PAD, 0), (0, 0)))
    yb = _mixer_b(qb.reshape(bsz, seq, B_WIDTH), front(kb), front(vb), _band_line(rel_bias), bsz, seq)

    x1t, topi, gates = _merge(
        xf, ya.reshape(n, A_WIDTH), yb.reshape(n, B_WIDTH), w_gates, w_branch_a.astype(BF16),
        w_branch_b.astype(BF16), w_out.astype(BF16), row(ln1_g), row(ln1_b),
        w_router.T.astype(BF16), b_router.reshape(-1, 1).astype(F32))

    blk = MOE_BLOCK
    cap = n * TOP_K + N_EXPERTS * blk
    nblk = cap // blk
    rank, cnt = _route(topi)
    counts = cnt[:, 0]
    padded = (counts + blk - 1) // blk * blk
    pends = jnp.cumsum(padded).astype(I32)
    pstarts = pends - padded
    nused = (pends[-1:] // blk).astype(I32)
    blk_start = jnp.arange(nblk, dtype=I32) * blk
    blk_expert = jnp.minimum(jnp.sum(blk_start[:, None] >= pends[None, :], axis=1), N_EXPERTS - 1).astype(I32)

    pos = _positions(topi, rank, pstarts)
    pos_g = pos.T.reshape(n // DMA_GROUP, DMA_GROUP * TOP_K)
    xs = _dispatch(counts, pstarts, pos_g, x1t, cap)
    ys = _moe(blk_expert, nused, xs, w_gate, b_gate[:, None, :], w_up, b_up[:, None, :],
              w_down, b_down[:, None, :])
    out = _final(pos_g, gates.T, x1t, ys, row(ln2_g), row(ln2_b))
    return out.reshape(bsz, seq, D_MODEL)


def kernel(x, w_in, kv_norm_g, idx_k_norm_g, idx_k_norm_b, w_uk, w_uv, rel_bias, w_branch_a,
           w_branch_b, w_out, ln1_g, ln1_b, w_router, b_router, w_gate, b_gate, w_up, b_up,
           w_down, b_down, ln2_g, ln2_b):
    for l in range(DEPTH):
        x = _layer(x, w_in[l], kv_norm_g[l], idx_k_norm_g[l], idx_k_norm_b[l], w_uk[l], w_uv[l],
                   rel_bias[l], w_branch_a[l], w_branch_b[l], w_out[l], ln1_g[l], ln1_b[l],
                   w_router[l], b_router[l], w_gate[l], b_gate[l], w_up[l], b_up[l], w_down[l],
                   b_down[l], ln2_g[l], ln2_b[l])
    return x
```

```python
import functools

import jax
import jax.numpy as jnp
import numpy as np
from jax import lax
from jax.experimental import pallas as pl
from jax.experimental.pallas import tpu as pltpu

F32 = jnp.float32
BF16 = jnp.bfloat16
I32 = jnp.int32

D_MODEL = 1024
CHUNK = 64
Q_BLOCK = 128
A_HEADS = 8
A_HEAD_DIM = 64
A_KV_RANK = 128
IDX_HEADS = 8
IDX_DIM = 64
IDX_TOPK = 256
IDX_SCALE = (IDX_HEADS * IDX_DIM) ** -0.5
B_HEADS = 8
B_HEAD_DIM = 64
B_LEFT_CHUNKS = 8
REL_CLIP = 128
N_EXPERTS = 32
TOP_K = 4
D_FF = 1024
SWIGLU_LIMIT = 7.0
SWIGLU_ALPHA = 1.702
DEPTH = 1
DN_ALPHA = (2.0 * DEPTH) ** 0.25
LN_EPS = 1e-5

A_WIDTH = A_HEADS * A_HEAD_DIM
B_WIDTH = B_HEADS * B_HEAD_DIM
SPLITS = [A_WIDTH, A_KV_RANK, IDX_HEADS * IDX_DIM, IDX_DIM, IDX_HEADS,
          B_WIDTH, B_WIDTH, B_WIDTH, D_MODEL, D_MODEL]

LANES = 128
SUBLANES = 8
KEY_TILE = 256
B_QTILE = 512
B_PAD = B_LEFT_CHUNKS * CHUNK
B_WIN = B_PAD + B_QTILE
TOKEN_TILE = 1024
MATMUL_TILE = 1024
MOE_BLOCK = 512
FINAL_TILE = 512
ROW_TILES = D_MODEL // LANES
DMA_GROUP = 4
INT_MIN = -2 ** 31
VMEM_LIMIT = 56 * 1024 * 1024

A_SLOPES = [2.0 ** (-8.0 * (h + 1) / A_HEADS) for h in range(A_HEADS)]

_C_QA = 0
_C_QI = _C_QA + A_WIDTH
_C_CKV = _C_QI + IDX_HEADS * IDX_DIM
_C_KI = _C_CKV + A_KV_RANK
_C_WI = _C_KI + LANES
_C_QKVB = _C_WI + LANES
_C_END = _C_QKVB + 3 * B_WIDTH


def _params(sem, vmem=VMEM_LIMIT):
    return pltpu.CompilerParams(dimension_semantics=sem, vmem_limit_bytes=vmem)


def _full(a):
    return pl.BlockSpec(a.shape, lambda *_: (0,) * a.ndim)


def _proj_kernel(x_ref, w_ref, wuk_ref, kvg_ref, kig_ref, kib_ref,
                 qlat_ref, qi_ref, ckv_ref, ki_ref, wi_ref, qb_ref, kb_ref, vb_ref):
    xb = x_ref[...].astype(BF16)

    def mm(c0, c1):
        return jnp.dot(xb, w_ref[:, c0:c1], preferred_element_type=F32)

    qa = mm(_C_QA, _C_QI)
    qlat = jnp.dot(qa.astype(BF16), wuk_ref[...], preferred_element_type=F32)
    qlat_ref[...] = (qlat * (A_HEAD_DIM ** -0.5)).astype(BF16)
    qi_ref[...] = mm(_C_QI, _C_CKV).astype(BF16)

    ckv = mm(_C_CKV, _C_KI)
    ms = jnp.mean(ckv * ckv, axis=-1, keepdims=True)
    ckv_ref[...] = ((ckv * lax.rsqrt(ms + LN_EPS)) * kvg_ref[...]).astype(BF16)

    ki = mm(_C_KI, _C_WI)[:, :IDX_DIM]
    mu = jnp.mean(ki, axis=-1, keepdims=True)
    var = jnp.mean(jnp.square(ki - mu), axis=-1, keepdims=True)
    ki_ref[...] = (((ki - mu) * lax.rsqrt(var + LN_EPS)) * kig_ref[...] + kib_ref[...]).astype(BF16)

    wi_ref[...] = mm(_C_WI, _C_QKVB)[:, :IDX_HEADS]
    qb_ref[...] = mm(_C_QKVB, _C_QKVB + B_WIDTH).astype(BF16)
    kb_ref[...] = mm(_C_QKVB + B_WIDTH, _C_QKVB + 2 * B_WIDTH).astype(BF16)
    vb_ref[...] = mm(_C_QKVB + 2 * B_WIDTH, _C_END).astype(BF16)


def _projection(xf, w_mix, wuk_bd, kvg, kig, kib):
    n, tm = xf.shape[0], min(MATMUL_TILE, xf.shape[0])
    row = lambda w: pl.BlockSpec((tm, w), lambda i: (i, 0))
    outs = [(A_HEADS * A_KV_RANK, BF16), (IDX_HEADS * IDX_DIM, BF16), (A_KV_RANK, BF16),
            (IDX_DIM, BF16), (IDX_HEADS, F32), (B_WIDTH, BF16), (B_WIDTH, BF16), (B_WIDTH, BF16)]
    return pl.pallas_call(
        _proj_kernel,
        out_shape=[jax.ShapeDtypeStruct((n, w), dt) for w, dt in outs],
        grid=(n // tm,),
        in_specs=[row(D_MODEL), _full(w_mix), _full(wuk_bd), _full(kvg), _full(kig), _full(kib)],
        out_specs=[row(w) for w, _ in outs],
        compiler_params=_params(("parallel",)),
        name="proj",
    )(xf, w_mix, wuk_bd, kvg, kig, kib)


def _mixer_a_kernel(kidx_ref, ckv_ref, ckvt_ref, qi_ref, wit_ref, qlat_ref, wuv_ref,
                    ya_ref, keys_ref, s_ref, acc_ref, *, seq, topk):
    i = pl.program_id(1)
    nkt = i // 2 + 1
    nsub = KEY_TILE // SUBLANES
    lane = lax.broadcasted_iota(I32, (1, LANES), 1)
    qpos = i * Q_BLOCK + lane
    qchunk = qpos >> 6
    row_iota = lax.broadcasted_iota(I32, (KEY_TILE, LANES), 0)

    qi_blk = qi_ref[0, 0]
    w_all = wit_ref[0] * IDX_SCALE

    def idx_body(kt, carry):
        lg = pl.dot(kidx_ref[0, kt], qi_blk, trans_b=True)
        isc = jnp.zeros((KEY_TILE, LANES), F32)
        for h in range(IDX_HEADS):
            isc = isc + jnp.maximum(lg[:, h * LANES:(h + 1) * LANES], 0.0) * w_all[h:h + 1, :]
        bits = lax.bitcast_convert_type(isc + 0.0, I32)
        key = bits ^ ((bits >> 31) & 0x7FFFFFFF)
        spos = kt * KEY_TILE + row_iota
        key = jnp.where((spos >> 6) <= qchunk, key, INT_MIN)
        keys_ref[kt] = key
        return carry

    def over_tiles(body, init):
        def pair(j, c):
            return body(2 * j + 1, body(2 * j, c))
        c = lax.fori_loop(0, nkt // 2, pair, init)
        return lax.cond(nkt % 2 == 1, lambda c: body(nkt - 1, c), lambda c: c, c)

    over_tiles(idx_body, 0)

    def count(pred):
        def body(kt, acc):
            spos = kt * KEY_TILE + row_iota
            m = jnp.where(pred(keys_ref[kt], spos), 1, 0).astype(I32)
            return acc + jnp.sum(m.reshape(nsub, SUBLANES, LANES), axis=0)
        acc = over_tiles(body, jnp.zeros((SUBLANES, LANES), I32))
        return jnp.sum(acc, axis=0, keepdims=True)

    def bit_body(it, carry):
        thr, cge = carry
        cand = thr ^ lax.shift_left(jnp.int32(1), 31 - it)
        cnt = count(lambda kv, spos: kv >= cand)
        ok = cnt >= topk
        return jnp.where(ok, cand, thr), jnp.where(ok, cnt, cge)

    start = (jnp.full((1, LANES), INT_MIN, I32), jnp.zeros((1, LANES), I32))
    thr, cge = lax.cond(nkt * KEY_TILE > topk,
                        lambda: lax.fori_loop(0, 32, bit_body, start), lambda: start)

    nbits = int(np.log2(seq)) + 1
    has_thr = thr > INT_MIN
    need_tie = jnp.max(jnp.where(has_thr, cge, 0)) > topk

    def tie_fn():
        room = topk - count(lambda kv, spos: kv > thr)

        def jb(it, cut):
            cand = cut | lax.shift_left(jnp.int32(1), nbits - 1 - it)
            f = count(lambda kv, spos: (kv == thr) & (spos < cand))
            return jnp.where(f <= room, cand, cut)
        return lax.fori_loop(0, nbits, jb, jnp.zeros((1, LANES), I32))

    cut = lax.cond(need_tie, tie_fn, lambda: jnp.full((1, LANES), 2 * seq, I32))
    cut = jnp.where(has_thr, cut, 0)

    qlat = qlat_ref[0]
    qr = jnp.concatenate([qlat[:, h * LANES:(h + 1) * LANES] for h in range(A_HEADS)], axis=0)

    def sc_body(kt, m):
        st = pl.dot(ckv_ref[0, kt], qr, trans_b=True)
        kv = keys_ref[kt]
        spos = kt * KEY_TILE + row_iota
        sel = (kv > thr) | ((kv == thr) & (spos < cut))
        dist = jnp.abs(qpos - spos).astype(F32)
        out = []
        for h in range(A_HEADS):
            sh = st[:, h * LANES:(h + 1) * LANES] - A_SLOPES[h] * dist
            sh = jnp.where(sel, sh, -jnp.inf)
            s_ref[kt, :, h * LANES:(h + 1) * LANES] = sh
            out.append(jnp.maximum(m[h], jnp.max(sh.reshape(nsub, SUBLANES, LANES), axis=0)))
        return tuple(out)

    m8 = over_tiles(sc_body, tuple(jnp.full((SUBLANES, LANES), -jnp.inf, F32) for _ in range(A_HEADS)))
    mx = [jnp.max(m, axis=0, keepdims=True) for m in m8]

    acc_ref[...] = jnp.zeros_like(acc_ref)

    def pv_body(kt, l):
        out, ps = [], []
        for h in range(A_HEADS):
            p = jnp.exp(s_ref[kt, :, h * LANES:(h + 1) * LANES] - mx[h])
            out.append(l[h] + jnp.sum(p.reshape(nsub, SUBLANES, LANES), axis=0))
            ps.append(p.astype(BF16))
        acc_ref[...] += jnp.dot(ckvt_ref[0, kt], jnp.concatenate(ps, axis=1),
                                preferred_element_type=F32)
        return tuple(out)

    l8 = over_tiles(pv_body, tuple(jnp.zeros((SUBLANES, LANES), F32) for _ in range(A_HEADS)))
    inv = [1.0 / jnp.sum(l, axis=0, keepdims=True) for l in l8]

    for j in range(A_HEADS // 2):
        y = None
        for h in (2 * j, 2 * j + 1):
            o = (acc_ref[:, h * LANES:(h + 1) * LANES] * inv[h]).astype(BF16)
            t = pl.dot(o, wuv_ref[h], trans_a=True)
            y = t if y is None else y + t
        ya_ref[0, :, j * LANES:(j + 1) * LANES] = y.astype(BF16)


def _mixer_a(kidx, ckv, ckvt, qi_r, wit, qlat, wuv_pad, bsz, seq):
    nqb = seq // Q_BLOCK
    nkt = seq // KEY_TILE
    topk = min(IDX_TOPK, seq // 4)
    kern = functools.partial(_mixer_a_kernel, seq=seq, topk=topk)
    return pl.pallas_call(
        kern,
        out_shape=jax.ShapeDtypeStruct((bsz, seq, A_WIDTH), BF16),
        grid=(bsz, nqb),
        in_specs=[
            pl.BlockSpec((1, nkt, KEY_TILE, IDX_DIM), lambda b, i: (b, 0, 0, 0)),
            pl.BlockSpec((1, nkt, KEY_TILE, A_KV_RANK), lambda b, i: (b, 0, 0, 0)),
            pl.BlockSpec((1, nkt, A_KV_RANK, KEY_TILE), lambda b, i: (b, 0, 0, 0)),
            pl.BlockSpec((1, 1, IDX_HEADS * Q_BLOCK, IDX_DIM), lambda b, i: (b, i, 0, 0)),
            pl.BlockSpec((1, IDX_HEADS, Q_BLOCK), lambda b, i: (b, 0, i)),
            pl.BlockSpec((1, Q_BLOCK, A_HEADS * A_KV_RANK), lambda b, i: (b, i, 0)),
            pl.BlockSpec(wuv_pad.shape, lambda b, i: (0, 0, 0)),
        ],
        out_specs=pl.BlockSpec((1, Q_BLOCK, A_WIDTH), lambda b, i: (b, i, 0)),
        scratch_shapes=[
            pltpu.VMEM((nkt, KEY_TILE, LANES), I32),
            pltpu.VMEM((nkt, KEY_TILE, A_HEADS * LANES), F32),
            pltpu.VMEM((A_KV_RANK, A_HEADS * LANES), F32),
        ],
        compiler_params=_params(("parallel", "arbitrary")),
        name="mixer_a",
    )(kidx, ckv, ckvt, qi_r, wit, qlat, wuv_pad)


def _mixer_b_kernel(q_ref, k_ref, v_ref, line_ref, yb_ref, bias_ref):
    @pl.when((pl.program_id(0) == 0) & (pl.program_id(1) == 0))
    def _():
        tq = lax.broadcasted_iota(I32, (B_QTILE, B_WIN), 0) >> 6
        kj = lax.broadcasted_iota(I32, (B_QTILE, B_WIN), 1) >> 6
        band = (kj >= tq) & (kj <= tq + B_LEFT_CHUNKS)
        for h in range(B_HEADS):
            rows = jnp.broadcast_to(line_ref[h:h + 1, :], (B_QTILE, B_QTILE + B_WIN))
            rolled = pltpu.roll(rows, 1, 1, stride=1, stride_axis=0)
            bias_ref[h] = jnp.where(band, rolled[:, B_QTILE:], -jnp.inf)

    t0 = pl.multiple_of(pl.program_id(1) * B_QTILE, B_QTILE)
    kw = k_ref[0, pl.ds(t0, B_WIN), :]
    vw = v_ref[0, pl.ds(t0, B_WIN), :]
    q = q_ref[0]
    kvalid = (t0 - B_PAD + lax.broadcasted_iota(I32, (1, B_WIN), 1)) >= 0
    lane = lax.broadcasted_iota(I32, (1, LANES), 1)
    scale = B_HEAD_DIM ** -0.5
    for j in range(B_HEADS // 2):
        sl = slice(j * LANES, (j + 1) * LANES)
        qs, ks, vs = q[:, sl], kw[:, sl], vw[:, sl]
        outs = []
        for hh in range(2):
            mine = (lane >= B_HEAD_DIM) if hh else (lane < B_HEAD_DIM)
            qm = jnp.where(mine, qs, jnp.zeros_like(qs))
            s = pl.dot(qm, ks, trans_b=True) * scale + bias_ref[2 * j + hh]
            s = jnp.where(kvalid, s, -jnp.inf)
            m = jnp.max(s, axis=-1, keepdims=True)
            p = jnp.exp(s - m)
            l = jnp.sum(p, axis=-1, keepdims=True)
            outs.append(jnp.dot(p.astype(BF16), vs, preferred_element_type=F32) / l)
        yb_ref[0, :, sl] = jnp.where(lane < B_HEAD_DIM, outs[0], outs[1]).astype(BF16)


def _band_line(rel_bias):
    m = np.arange(B_QTILE + B_WIN)
    return rel_bias[:, np.clip(B_WIN - 1 - m, -REL_CLIP, REL_CLIP) + REL_CLIP].astype(F32)


def _mixer_b(qb, kb_pad, vb_pad, line, bsz, seq):
    return pl.pallas_call(
        _mixer_b_kernel,
        out_shape=jax.ShapeDtypeStruct((bsz, seq, B_WIDTH), BF16),
        grid=(bsz, seq // B_QTILE),
        in_specs=[
            pl.BlockSpec((1, B_QTILE, B_WIDTH), lambda b, i: (b, i, 0)),
            pl.BlockSpec((1, seq + B_PAD, B_WIDTH), lambda b, i: (b, 0, 0)),
            pl.BlockSpec((1, seq + B_PAD, B_WIDTH), lambda b, i: (b, 0, 0)),
            pl.BlockSpec(line.shape, lambda b, i: (0, 0)),
        ],
        out_specs=pl.BlockSpec((1, B_QTILE, B_WIDTH), lambda b, i: (b, i, 0)),
        scratch_shapes=[pltpu.VMEM((B_HEADS, B_QTILE, B_WIN), F32)],
        compiler_params=_params(("arbitrary", "arbitrary")),
        name="mixer_b",
    )(qb, kb_pad, vb_pad, line)


def _layer_norm(z, g, b):
    mu = jnp.mean(z, axis=-1, keepdims=True)
    var = jnp.mean(jnp.square(z - mu), axis=-1, keepdims=True)
    return ((z - mu) * lax.rsqrt(var + LN_EPS)) * g + b


def _merge_kernel(x_ref, ya_ref, yb_ref, wg_ref, wba_ref, wbb_ref, wo_ref, g_ref, b_ref,
                  wr_ref, br_ref, x1t_ref, topi_ref, gate_ref):
    tm = x_ref.shape[0]
    x = x_ref[...]
    xb = x.astype(BF16)
    ga = jnp.dot(xb, wg_ref[:, :D_MODEL], preferred_element_type=F32)
    a = jnp.dot(ya_ref[...], wba_ref[...], preferred_element_type=F32)
    merged = jax.nn.sigmoid(ga) * a
    gb = jnp.dot(xb, wg_ref[:, D_MODEL:], preferred_element_type=F32)
    b = jnp.dot(yb_ref[...], wbb_ref[...], preferred_element_type=F32)
    merged = merged + jax.nn.sigmoid(gb) * b
    out = jnp.dot(merged.astype(BF16), wo_ref[...], preferred_element_type=F32)
    x1 = _layer_norm(DN_ALPHA * x + out, g_ref[...], b_ref[...])
    for c in range(ROW_TILES):
        x1t_ref[pl.ds(c, tm, stride=ROW_TILES), :] = x1[:, c * LANES:(c + 1) * LANES]

    lg = pl.dot(wr_ref[...], x1.astype(BF16), trans_b=True) + br_ref[...]
    eidx = lax.broadcasted_iota(I32, lg.shape, 0)
    vals, idxs = [], []
    for _ in range(TOP_K):
        m = jnp.max(lg, axis=0, keepdims=True)
        sel = jnp.min(jnp.where(lg == m, eidx, N_EXPERTS), axis=0, keepdims=True)
        vals.append(m)
        idxs.append(sel)
        lg = jnp.where(eidx == sel, -jnp.inf, lg)
    es = [jnp.exp(v - vals[0]) for v in vals]
    tot = es[0] + es[1] + es[2] + es[3]
    topi_ref[...] = jnp.concatenate(idxs, axis=0)
    gate_ref[...] = jnp.concatenate([e / tot for e in es], axis=0)


def _merge(xf, ya, yb, wg, wba, wbb, wo, g1, b1, wr_t, br):
    n, tm = xf.shape[0], min(MATMUL_TILE, xf.shape[0])
    row = lambda w: pl.BlockSpec((tm, w), lambda i: (i, 0))
    col = pl.BlockSpec((TOP_K, tm), lambda i: (0, i))
    return pl.pallas_call(
        _merge_kernel,
        out_shape=[jax.ShapeDtypeStruct((n * ROW_TILES, LANES), F32),
                   jax.ShapeDtypeStruct((TOP_K, n), I32),
                   jax.ShapeDtypeStruct((TOP_K, n), F32)],
        grid=(n // tm,),
        in_specs=[row(D_MODEL), row(A_WIDTH), row(B_WIDTH), _full(wg), _full(wba), _full(wbb),
                  _full(wo), _full(g1), _full(b1), _full(wr_t), _full(br)],
        out_specs=[pl.BlockSpec((tm * ROW_TILES, LANES), lambda i: (i, 0)), col, col],
        compiler_params=_params(("parallel",)),
        name="merge",
    )(xf, ya, yb, wg, wba, wbb, wo, g1, b1, wr_t, br)


def _one_hot_rows(topi, k, width):
    eidx = lax.broadcasted_iota(I32, (N_EXPERTS, width), 0)
    return eidx == topi[k:k + 1, :]


def _route_kernel(topi_ref, tri_ref, rank_ref, cnt_ref, run_ref):
    @pl.when(pl.program_id(0) == 0)
    def _():
        run_ref[...] = jnp.zeros_like(run_ref)

    topi = topi_ref[...]
    tr = topi.shape[1]
    hot = [_one_hot_rows(topi, k, tr) for k in range(TOP_K)]
    oh = jnp.concatenate([jnp.where(h, 1.0, 0.0).astype(BF16) for h in hot], axis=0)
    prefix = jnp.dot(oh, tri_ref[...], preferred_element_type=F32)
    base = run_ref[...]
    ranks = []
    for k in range(TOP_K):
        tbl = prefix[k * N_EXPERTS:(k + 1) * N_EXPERTS, :] + base
        ranks.append(jnp.sum(jnp.where(hot[k], tbl, 0.0), axis=0, keepdims=True))
        base = base + jnp.sum(jnp.where(hot[k], 1.0, 0.0), axis=1, keepdims=True)
    rank_ref[...] = jnp.concatenate(ranks, axis=0).astype(I32)
    run_ref[...] = base
    cnt_ref[...] = base.astype(I32)


def _route(topi):
    n, tr = topi.shape[1], TOKEN_TILE
    tri = (np.arange(tr)[:, None] < np.arange(tr)[None, :]).astype(np.float32)
    tri = jnp.asarray(tri, BF16)
    col = pl.BlockSpec((TOP_K, tr), lambda i: (0, i))
    return pl.pallas_call(
        _route_kernel,
        out_shape=[jax.ShapeDtypeStruct((TOP_K, n), I32), jax.ShapeDtypeStruct((N_EXPERTS, 1), I32)],
        grid=(n // tr,),
        in_specs=[col, _full(tri)],
        out_specs=[col, pl.BlockSpec((N_EXPERTS, 1), lambda i: (0, 0))],
        scratch_shapes=[pltpu.VMEM((N_EXPERTS, 1), F32)],
        compiler_params=_params(("arbitrary",)),
        name="route",
    )(topi, tri)


def _positions_kernel(topi_ref, rank_ref, pstv_ref, pos_ref):
    topi = topi_ref[...]
    rows = []
    for k in range(TOP_K):
        start = jnp.sum(jnp.where(_one_hot_rows(topi, k, topi.shape[1]), pstv_ref[...], 0),
                        axis=0, keepdims=True)
        rows.append(rank_ref[k:k + 1, :] + start)
    pos_ref[...] = jnp.concatenate(rows, axis=0)


def _positions(topi, rank, pstarts):
    n, td = topi.shape[1], TOKEN_TILE
    col = pl.BlockSpec((TOP_K, td), lambda i: (0, i))
    pstv = pstarts.reshape(N_EXPERTS, 1)
    return pl.pallas_call(
        _positions_kernel,
        out_shape=jax.ShapeDtypeStruct((TOP_K, n), I32),
        grid=(n // td,),
        in_specs=[col, col, _full(pstv)],
        out_specs=col,
        compiler_params=_params(("parallel",)),
        name="positions",
    )(topi, rank, pstv)


def _dispatch_kernel(cnt_ref, pst_ref, pos_hbm, x_hbm, xs_hbm,
                     pos_smem, xbuf, zbuf, sem_p, sem_x, sem_d, sem_z):
    i = pl.program_id(0)
    n = pl.num_programs(0)
    ng = pos_smem.shape[1]
    td = ng * DMA_GROUP
    tile_rows = td * ROW_TILES

    def pos_copy(step, slot):
        return pltpu.make_async_copy(pos_hbm.at[pl.ds(step * ng, ng)], pos_smem.at[slot], sem_p.at[slot])

    def x_copy(step, slot):
        return pltpu.make_async_copy(x_hbm.at[pl.ds(step * tile_rows, tile_rows)], xbuf.at[slot],
                                     sem_x.at[slot])

    @pl.when(i == 0)
    def _():
        pos_copy(0, 0).start()
        x_copy(0, 0).start()

    @pl.when(i + 1 < n)
    def _():
        pos_copy(i + 1, (i + 1) % 2).start()
        x_copy(i + 1, (i + 1) % 3).start()

    slot = i % 2
    xslot = i % 3
    pos_copy(i, slot).wait()
    x_copy(i, xslot).wait()

    def step_bytes(s):
        rows_ = TOP_K * tile_rows
        return pltpu.make_async_copy(xs_hbm.at[pl.ds(0, rows_)], xs_hbm.at[pl.ds(0, rows_)], sem_d.at[s])

    def body(g, c):
        dst = [pos_smem[slot, g, j] * ROW_TILES for j in range(DMA_GROUP * TOP_K)]
        for u in range(DMA_GROUP):
            src = xbuf.at[xslot, pl.ds((g * DMA_GROUP + u) * ROW_TILES, ROW_TILES)]
            for k in range(TOP_K):
                pltpu.make_async_copy(src, xs_hbm.at[pl.ds(dst[u * TOP_K + k], ROW_TILES)],
                                      sem_d.at[xslot]).start(priority=k % 2)
        return c
    lax.fori_loop(0, ng, body, 0)

    @pl.when(i > 0)
    def _():
        step_bytes((i + 2) % 3).wait()

    @pl.when(i == n - 1)
    def _():
        step_bytes(xslot).wait()
        zbuf[...] = jnp.zeros_like(zbuf)

        def zero_copy(r):
            return pltpu.make_async_copy(zbuf, xs_hbm.at[pl.ds(r * ROW_TILES, ROW_TILES)], sem_z)

        def zero_rows(lo, hi):
            def start(r, c2):
                zero_copy(r).start()
                return c2

            def wait(r, c2):
                zero_copy(r).wait()
                return c2
            lax.fori_loop(lo, hi, start, 0)
            lax.fori_loop(lo, hi, wait, 0)

        def padded_end(e):
            return pst_ref[e] + (cnt_ref[e] + MOE_BLOCK - 1) // MOE_BLOCK * MOE_BLOCK

        def ebody(e, c):
            zero_rows(pst_ref[e] + cnt_ref[e], padded_end(e))
            return c
        lax.fori_loop(0, N_EXPERTS, ebody, 0)
        zero_rows(padded_end(N_EXPERTS - 1), xs_hbm.shape[0] // ROW_TILES)


def _dispatch(counts, pstarts, pos_g, x1t, cap):
    n, td = pos_g.shape[0] * DMA_GROUP, TOKEN_TILE
    any_spec = pl.BlockSpec(memory_space=pl.ANY)
    return pl.pallas_call(
        _dispatch_kernel,
        out_shape=jax.ShapeDtypeStruct((cap * ROW_TILES, LANES), F32),
        grid_spec=pltpu.PrefetchScalarGridSpec(
            num_scalar_prefetch=2,
            grid=(n // td,),
            in_specs=[any_spec, any_spec],
            out_specs=any_spec,
            scratch_shapes=[
                pltpu.SMEM((2, td // DMA_GROUP, DMA_GROUP * TOP_K), I32),
                pltpu.VMEM((3, td * ROW_TILES, LANES), F32),
                pltpu.VMEM((ROW_TILES, LANES), F32),
                pltpu.SemaphoreType.DMA((2,)),
                pltpu.SemaphoreType.DMA((3,)),
                pltpu.SemaphoreType.DMA((3,)),
                pltpu.SemaphoreType.DMA,
            ],
        ),
        compiler_params=_params(("arbitrary",)),
        name="dispatch",
    )(counts, pstarts, pos_g, x1t)


def _moe_kernel(be_ref, nu_ref, xs_ref, wg_ref, bg_ref, wu_ref, bu_ref, wd_ref, bd_ref, ys_ref,
                wg_bf, wu_bf, wd_bf):
    i = pl.program_id(0)
    blk = MOE_BLOCK
    used = i < nu_ref[0]

    @pl.when(jnp.logical_or(i == 0, be_ref[i] != be_ref[jnp.maximum(i - 1, 0)]))
    def _():
        wg_bf[...] = wg_ref[0].astype(BF16)
        wu_bf[...] = wu_ref[0].astype(BF16)
        wd_bf[...] = wd_ref[0].astype(BF16)

    @pl.when(used)
    def _():
        xb = jnp.concatenate([xs_ref[pl.ds(c, blk, stride=ROW_TILES), :].astype(BF16)
                              for c in range(ROW_TILES)], axis=1)
        hg = jnp.minimum(jnp.dot(xb, wg_bf[...], preferred_element_type=F32) + bg_ref[0], SWIGLU_LIMIT)
        hu = jnp.clip(jnp.dot(xb, wu_bf[...], preferred_element_type=F32) + bu_ref[0],
                      -SWIGLU_LIMIT, SWIGLU_LIMIT)
        h = (hu + 1.0) * (hg * jax.nn.sigmoid(SWIGLU_ALPHA * hg))
        y = jnp.dot(h.astype(BF16), wd_bf[...], preferred_element_type=F32) + bd_ref[0]
        for c in range(ROW_TILES):
            ys_ref[pl.ds(c, blk, stride=ROW_TILES), :] = y[:, c * LANES:(c + 1) * LANES]

    @pl.when(jnp.logical_not(used))
    def _():
        ys_ref[...] = jnp.zeros_like(ys_ref)


def _moe(blk_expert, nused, xs, wg, bg, wu, bu, wd, bd):
    blk = MOE_BLOCK
    nblk = xs.shape[0] // (blk * ROW_TILES)
    wspec = lambda a: pl.BlockSpec((1,) + a.shape[1:], lambda i, be, nu: (be[i], 0, 0))
    xspec = pl.BlockSpec((blk * ROW_TILES, LANES), lambda i, be, nu: (jnp.minimum(i, nu[0] - 1), 0))
    return pl.pallas_call(
        _moe_kernel,
        out_shape=jax.ShapeDtypeStruct(xs.shape, F32),
        grid_spec=pltpu.PrefetchScalarGridSpec(
            num_scalar_prefetch=2,
            grid=(nblk,),
            in_specs=[xspec, wspec(wg), wspec(bg), wspec(wu), wspec(bu), wspec(wd), wspec(bd)],
            out_specs=pl.BlockSpec((blk * ROW_TILES, LANES), lambda i, be, nu: (i, 0)),
            scratch_shapes=[pltpu.VMEM(wg.shape[1:], BF16), pltpu.VMEM(wu.shape[1:], BF16),
                            pltpu.VMEM(wd.shape[1:], BF16)],
        ),
        compiler_params=_params(("arbitrary",)),
        name="moe",
    )(blk_expert, nused, xs, wg, bg, wu, bu, wd, bd)


def _final_kernel(pos_hbm, gt_ref, x1t_ref, ys_hbm, g_ref, b_ref, o_ref,
                  pos_smem, ybuf, sem_p, sem_g):
    i = pl.program_id(0)
    n = pl.num_programs(0)
    ft = FINAL_TILE
    slot_rows = TOP_K * ft * ROW_TILES

    ng = ft // DMA_GROUP

    def pos_copy(tile, slot):
        return pltpu.make_async_copy(pos_hbm.at[pl.ds(tile * ng, ng)], pos_smem.at[slot], sem_p.at[slot])

    def gather_rows(pslot, yslot):
        def body(g, c):
            src = [pos_smem[pslot, g, j] * ROW_TILES for j in range(DMA_GROUP * TOP_K)]
            base = yslot * slot_rows + g * (DMA_GROUP * ROW_TILES)
            for u in range(DMA_GROUP):
                for k in range(TOP_K):
                    dst = base + (k * ft + u) * ROW_TILES
                    pltpu.make_async_copy(ys_hbm.at[pl.ds(src[u * TOP_K + k], ROW_TILES)],
                                          ybuf.at[pl.ds(dst, ROW_TILES)],
                                          sem_g.at[yslot]).start(priority=k % 2)
            return c
        lax.fori_loop(0, ng, body, 0)

    def gather_wait(yslot):
        pltpu.make_async_copy(ys_hbm.at[pl.ds(0, slot_rows)],
                              ybuf.at[pl.ds(yslot * slot_rows, slot_rows)], sem_g.at[yslot]).wait()

    @pl.when(i == 0)
    def _():
        c = pos_copy(0, 0)
        c.start()
        c.wait()
        gather_rows(0, 0)

        @pl.when(n > 1)
        def _():
            pos_copy(1, 1).start()

    @pl.when(i + 1 < n)
    def _():
        pos_copy(i + 1, (i + 1) % 3).wait()
        gather_rows((i + 1) % 3, (i + 1) % 2)

    @pl.when(i + 2 < n)
    def _():
        pos_copy(i + 2, (i + 2) % 3).start()

    slot = i % 2
    gather_wait(slot)
    g = gt_ref[...]
    zs = []
    for c in range(ROW_TILES):
        f = None
        for k in range(TOP_K):
            yk = ybuf[pl.ds(slot * slot_rows + k * ft * ROW_TILES + c, ft, stride=ROW_TILES), :]
            term = g[:, k:k + 1] * yk
            f = term if f is None else f + term
        zs.append(DN_ALPHA * x1t_ref[pl.ds(c, ft, stride=ROW_TILES), :] + f)
    o_ref[...] = _layer_norm(jnp.concatenate(zs, axis=1), g_ref[...], b_ref[...])


def _final(pos, gates_t, x1t, ys, g2, b2):
    ft = FINAL_TILE
    n = gates_t.shape[0]
    any_spec = pl.BlockSpec(memory_space=pl.ANY)
    return pl.pallas_call(
        _final_kernel,
        out_shape=jax.ShapeDtypeStruct((n, D_MODEL), F32),
        grid=(n // ft,),
        in_specs=[any_spec, pl.BlockSpec((ft, TOP_K), lambda i: (i, 0)),
                  pl.BlockSpec((ft * ROW_TILES, LANES), lambda i: (i, 0)), any_spec,
                  _full(g2), _full(b2)],
        out_specs=pl.BlockSpec((ft, D_MODEL), lambda i: (i, 0)),
        scratch_shapes=[
            pltpu.SMEM((3, ft // DMA_GROUP, DMA_GROUP * TOP_K), I32),
            pltpu.VMEM((2 * TOP_K * ft * ROW_TILES, LANES), F32),
            pltpu.SemaphoreType.DMA((3,)),
            pltpu.SemaphoreType.DMA((2,)),
        ],
        compiler_params=_params(("arbitrary",)),
        name="final",
    )(pos, gates_t, x1t, ys, g2, b2)


def _mix_weight(w_in):
    qa, ckv, qi, ki, wi, qb, kb, vb, _, _ = jnp.split(w_in, np.cumsum(SPLITS)[:-1].tolist(), axis=-1)
    pad = lambda a: jnp.pad(a, ((0, 0), (0, LANES - a.shape[1])))
    return jnp.concatenate([qa, qi, ckv, pad(ki), pad(wi), qb, kb, vb], axis=1).astype(BF16)


def _layer(x, w_in, kv_norm_g, idx_k_norm_g, idx_k_norm_b, w_uk, w_uv, rel_bias, w_branch_a,
           w_branch_b, w_out, ln1_g, ln1_b, w_router, b_router, w_gate, b_gate, w_up, b_up,
           w_down, b_down, ln2_g, ln2_b):
    bsz, seq, _ = x.shape
    n = bsz * seq
    assert seq % KEY_TILE == 0 and n % TOKEN_TILE == 0
    xf = x.reshape(n, D_MODEL)
    row = lambda v: v.reshape(1, -1).astype(F32)

    w_mix = _mix_weight(w_in)
    w_gates = w_in[:, sum(SPLITS[:8]):].astype(BF16)
    wuk_bd = jnp.zeros((A_WIDTH, A_HEADS * A_KV_RANK), F32)
    wuv_pad = jnp.zeros((A_HEADS, A_KV_RANK, LANES), F32)
    for h in range(A_HEADS):
        wuk_bd = wuk_bd.at[h * A_HEAD_DIM:(h + 1) * A_HEAD_DIM, h * A_KV_RANK:(h + 1) * A_KV_RANK].set(w_uk[h])
        c0 = (h % 2) * A_HEAD_DIM
        wuv_pad = wuv_pad.at[h, :, c0:c0 + A_HEAD_DIM].set(w_uv[h])
    wuk_bd, wuv_pad = wuk_bd.astype(BF16), wuv_pad.astype(BF16)

    qlat, qi, ckv, ki, wi, qb, kb, vb = _projection(
        xf, w_mix, wuk_bd, row(kv_norm_g), row(idx_k_norm_g), row(idx_k_norm_b))

    nqb, nkt = seq // Q_BLOCK, seq // KEY_TILE
    kidx = ki.reshape(bsz, nkt, KEY_TILE, IDX_DIM)
    ckv4 = ckv.reshape(bsz, nkt, KEY_TILE, A_KV_RANK)
    ckvt = ckv4.transpose(0, 1, 3, 2)
    qi_r = qi.reshape(bsz, nqb, Q_BLOCK, IDX_HEADS, IDX_DIM).transpose(0, 1, 3, 2, 4).reshape(
        bsz, nqb, IDX_HEADS * Q_BLOCK, IDX_DIM)
    wit = wi.reshape(bsz, seq, IDX_HEADS).transpose(0, 2, 1)
    ya = _mixer_a(kidx, ckv4, ckvt, qi_r, wit, qlat.reshape(bsz, seq, -1), wuv_pad, bsz, seq)

    front = lambda a: jnp.pad(a.reshape(bsz, seq, B_WIDTH), ((0, 0), (B_PAD, 0), (0, 0)))
    yb = _mixer_b(qb.reshape(bsz, seq, B_WIDTH), front(kb), front(vb), _band_line(rel_bias), bsz, seq)

    x1t, topi, gates = _merge(
        xf, ya.reshape(n, A_WIDTH), yb.reshape(n, B_WIDTH), w_gates, w_branch_a.astype(BF16),
        w_branch_b.astype(BF16), w_out.astype(BF16), row(ln1_g), row(ln1_b),
        w_router.T.astype(BF16), b_router.reshape(-1, 1).astype(F32))

    blk = MOE_BLOCK
    cap = n * TOP_K + N_EXPERTS * blk
    nblk = cap // blk
    rank, cnt = _route(topi)
    counts = cnt[:, 0]
    padded = (counts + blk - 1) // blk * blk
    pends = jnp.cumsum(padded).astype(I32)
    pstarts = pends - padded
    nused = (pends[-1:] // blk).astype(I32)
    blk_start = jnp.arange(nblk, dtype=I32) * blk
    blk_expert = jnp.minimum(jnp.sum(blk_start[:, None] >= pends[None, :], axis=1), N_EXPERTS - 1).astype(I32)

    pos = _positions(topi, rank, pstarts)
    pos_g = pos.T.reshape(n // DMA_GROUP, DMA_GROUP * TOP_K)
    xs = _dispatch(counts, pstarts, pos_g, x1t, cap)
    ys = _moe(blk_expert, nused, xs, w_gate, b_gate[:, None, :], w_up, b_up[:, None, :],
              w_down, b_down[:, None, :])
    out = _final(pos_g, gates.T, x1t, ys, row(ln2_g), row(ln2_b))
    return out.reshape(bsz, seq, D_MODEL)


def kernel(x, w_in, kv_norm_g, idx_k_norm_g, idx_k_norm_b, w_uk, w_uv, rel_bias, w_branch_a,
           w_branch_b, w_out, ln1_g, ln1_b, w_router, b_router, w_gate, b_gate, w_up, b_up,
           w_down, b_down, ln2_g, ln2_b):
    for l in range(DEPTH):
        x = _layer(x, w_in[l], kv_norm_g[l], idx_k_norm_g[l], idx_k_norm_b[l], w_uk[l], w_uv[l],
                   rel_bias[l], w_branch_a[l], w_branch_b[l], w_out[l], ln1_g[l], ln1_b[l],
                   w_router[l], b_router[l], w_gate[l], b_gate[l], w_up[l], b_up[l], w_down[l],
                   b_down[l], ln2_g[l], ln2_b[l])
    return x
```

```python
import functools

import jax
import jax.numpy as jnp
import numpy as np
from jax import lax
from jax.experimental import pallas as pl
from jax.experimental.pallas import tpu as pltpu

F32 = jnp.float32
BF16 = jnp.bfloat16
I32 = jnp.int32

D_MODEL = 1024
CHUNK = 64
Q_BLOCK = 128
A_HEADS = 8
A_HEAD_DIM = 64
A_KV_RANK = 128
IDX_HEADS = 8
IDX_DIM = 64
IDX_TOPK = 256
IDX_SCALE = (IDX_HEADS * IDX_DIM) ** -0.5
B_HEADS = 8
B_HEAD_DIM = 64
B_LEFT_CHUNKS = 8
REL_CLIP = 128
N_EXPERTS = 32
TOP_K = 4
D_FF = 1024
SWIGLU_LIMIT = 7.0
SWIGLU_ALPHA = 1.702
DEPTH = 1
DN_ALPHA = (2.0 * DEPTH) ** 0.25
LN_EPS = 1e-5

A_WIDTH = A_HEADS * A_HEAD_DIM
B_WIDTH = B_HEADS * B_HEAD_DIM
SPLITS = [A_WIDTH, A_KV_RANK, IDX_HEADS * IDX_DIM, IDX_DIM, IDX_HEADS,
          B_WIDTH, B_WIDTH, B_WIDTH, D_MODEL, D_MODEL]

LANES = 128
SUBLANES = 8
KEY_TILE = 256
B_QTILE = 512
B_PAD = B_LEFT_CHUNKS * CHUNK
B_WIN = B_PAD + B_QTILE
TOKEN_TILE = 1024
MATMUL_TILE = 1024
MOE_BLOCK = 512
FINAL_TILE = 512
ROW_TILES = D_MODEL // LANES
DMA_GROUP = 4
INT_MIN = -2 ** 31
VMEM_LIMIT = 56 * 1024 * 1024

A_SLOPES = [2.0 ** (-8.0 * (h + 1) / A_HEADS) for h in range(A_HEADS)]

_C_QA = 0
_C_QI = _C_QA + A_WIDTH
_C_CKV = _C_QI + IDX_HEADS * IDX_DIM
_C_KI = _C_CKV + A_KV_RANK
_C_WI = _C_KI + LANES
_C_QKVB = _C_WI + LANES
_C_END = _C_QKVB + 3 * B_WIDTH


def _params(sem, vmem=VMEM_LIMIT):
    return pltpu.CompilerParams(dimension_semantics=sem, vmem_limit_bytes=vmem)


def _full(a):
    return pl.BlockSpec(a.shape, lambda *_: (0,) * a.ndim)


def _proj_kernel(x_ref, w_ref, wuk_ref, kvg_ref, kig_ref, kib_ref,
                 qlat_ref, qi_ref, ckv_ref, ki_ref, wi_ref, qb_ref, kb_ref, vb_ref):
    xb = x_ref[...].astype(BF16)

    def mm(c0, c1):
        return jnp.dot(xb, w_ref[:, c0:c1], preferred_element_type=F32)

    qa = mm(_C_QA, _C_QI)
    qlat = jnp.dot(qa.astype(BF16), wuk_ref[...], preferred_element_type=F32)
    qlat_ref[...] = (qlat * (A_HEAD_DIM ** -0.5)).astype(BF16)
    qi_ref[...] = mm(_C_QI, _C_CKV).astype(BF16)

    ckv = mm(_C_CKV, _C_KI)
    ms = jnp.mean(ckv * ckv, axis=-1, keepdims=True)
    ckv_ref[...] = ((ckv * lax.rsqrt(ms + LN_EPS)) * kvg_ref[...]).astype(BF16)

    ki = mm(_C_KI, _C_WI)[:, :IDX_DIM]
    mu = jnp.mean(ki, axis=-1, keepdims=True)
    var = jnp.mean(jnp.square(ki - mu), axis=-1, keepdims=True)
    ki_ref[...] = (((ki - mu) * lax.rsqrt(var + LN_EPS)) * kig_ref[...] + kib_ref[...]).astype(BF16)

    wi_ref[...] = mm(_C_WI, _C_QKVB)[:, :IDX_HEADS]
    qb_ref[...] = mm(_C_QKVB, _C_QKVB + B_WIDTH).astype(BF16)
    kb_ref[...] = mm(_C_QKVB + B_WIDTH, _C_QKVB + 2 * B_WIDTH).astype(BF16)
    vb_ref[...] = mm(_C_QKVB + 2 * B_WIDTH, _C_END).astype(BF16)


def _projection(xf, w_mix, wuk_bd, kvg, kig, kib):
    n, tm = xf.shape[0], min(MATMUL_TILE, xf.shape[0])
    row = lambda w: pl.BlockSpec((tm, w), lambda i: (i, 0))
    outs = [(A_HEADS * A_KV_RANK, BF16), (IDX_HEADS * IDX_DIM, BF16), (A_KV_RANK, BF16),
            (IDX_DIM, BF16), (IDX_HEADS, F32), (B_WIDTH, BF16), (B_WIDTH, BF16), (B_WIDTH, BF16)]
    return pl.pallas_call(
        _proj_kernel,
        out_shape=[jax.ShapeDtypeStruct((n, w), dt) for w, dt in outs],
        grid=(n // tm,),
        in_specs=[row(D_MODEL), _full(w_mix), _full(wuk_bd), _full(kvg), _full(kig), _full(kib)],
        out_specs=[row(w) for w, _ in outs],
        compiler_params=_params(("parallel",)),
        name="proj",
    )(xf, w_mix, wuk_bd, kvg, kig, kib)


def _mixer_a_kernel(kidx_ref, ckv_ref, ckvt_ref, qi_ref, wit_ref, qlat_ref, wuv_ref,
                    ya_ref, keys_ref, s_ref, acc_ref, *, seq, topk):
    i = pl.program_id(1)
    nkt = i // 2 + 1
    nsub = KEY_TILE // SUBLANES
    lane = lax.broadcasted_iota(I32, (1, LANES), 1)
    qpos = i * Q_BLOCK + lane
    qchunk = qpos >> 6
    row_iota = lax.broadcasted_iota(I32, (KEY_TILE, LANES), 0)

    qi_blk = qi_ref[0, 0]
    w_all = wit_ref[0] * IDX_SCALE

    def idx_body(kt, carry):
        lg = pl.dot(kidx_ref[0, kt], qi_blk, trans_b=True)
        isc = jnp.zeros((KEY_TILE, LANES), F32)
        for h in range(IDX_HEADS):
            isc = isc + jnp.maximum(lg[:, h * LANES:(h + 1) * LANES], 0.0) * w_all[h:h + 1, :]
        bits = lax.bitcast_convert_type(isc + 0.0, I32)
        key = bits ^ ((bits >> 31) & 0x7FFFFFFF)
        spos = kt * KEY_TILE + row_iota
        key = jnp.where((spos >> 6) <= qchunk, key, INT_MIN)
        keys_ref[kt] = key
        return carry

    def over_tiles(body, init):
        def pair(j, c):
            return body(2 * j + 1, body(2 * j, c))
        c = lax.fori_loop(0, nkt // 2, pair, init)
        return lax.cond(nkt % 2 == 1, lambda c: body(nkt - 1, c), lambda c: c, c)

    over_tiles(idx_body, 0)

    def count(pred):
        def body(kt, acc):
            spos = kt * KEY_TILE + row_iota
            m = jnp.where(pred(keys_ref[kt], spos), 1, 0).astype(I32)
            return acc + jnp.sum(m.reshape(nsub, SUBLANES, LANES), axis=0)
        acc = over_tiles(body, jnp.zeros((SUBLANES, LANES), I32))
        return jnp.sum(acc, axis=0, keepdims=True)

    def bit_body(it, carry):
        thr, cge = carry
        cand = thr ^ lax.shift_left(jnp.int32(1), 31 - it)
        cnt = count(lambda kv, spos: kv >= cand)
        ok = cnt >= topk
        return jnp.where(ok, cand, thr), jnp.where(ok, cnt, cge)

    start = (jnp.full((1, LANES), INT_MIN, I32), jnp.zeros((1, LANES), I32))
    thr, cge = lax.cond(nkt * KEY_TILE > topk,
                        lambda: lax.fori_loop(0, 32, bit_body, start), lambda: start)

    nbits = int(np.log2(seq)) + 1
    has_thr = thr > INT_MIN
    need_tie = jnp.max(jnp.where(has_thr, cge, 0)) > topk

    def tie_fn():
        room = topk - count(lambda kv, spos: kv > thr)

        def jb(it, cut):
            cand = cut | lax.shift_left(jnp.int32(1), nbits - 1 - it)
            f = count(lambda kv, spos: (kv == thr) & (spos < cand))
            return jnp.where(f <= room, cand, cut)
        return lax.fori_loop(0, nbits, jb, jnp.zeros((1, LANES), I32))

    cut = lax.cond(need_tie, tie_fn, lambda: jnp.full((1, LANES), 2 * seq, I32))
    cut = jnp.where(has_thr, cut, 0)

    qlat = qlat_ref[0]
    qr = jnp.concatenate([qlat[:, h * LANES:(h + 1) * LANES] for h in range(A_HEADS)], axis=0)

    def sc_body(kt, m):
        st = pl.dot(ckv_ref[0, kt], qr, trans_b=True)
        kv = keys_ref[kt]
        spos = kt * KEY_TILE + row_iota
        sel = (kv > thr) | ((kv == thr) & (spos < cut))
        dist = jnp.abs(qpos - spos).astype(F32)
        out = []
        for h in range(A_HEADS):
            sh = st[:, h * LANES:(h + 1) * LANES] - A_SLOPES[h] * dist
            sh = jnp.where(sel, sh, -jnp.inf)
            s_ref[kt, :, h * LANES:(h + 1) * LANES] = sh
            out.append(jnp.maximum(m[h], jnp.max(sh.reshape(nsub, SUBLANES, LANES), axis=0)))
        return tuple(out)

    m8 = over_tiles(sc_body, tuple(jnp.full((SUBLANES, LANES), -jnp.inf, F32) for _ in range(A_HEADS)))
    mx = [jnp.max(m, axis=0, keepdims=True) for m in m8]

    acc_ref[...] = jnp.zeros_like(acc_ref)

    def pv_body(kt, l):
        out, ps = [], []
        for h in range(A_HEADS):
            p = jnp.exp(s_ref[kt, :, h * LANES:(h + 1) * LANES] - mx[h])
            out.append(l[h] + jnp.sum(p.reshape(nsub, SUBLANES, LANES), axis=0))
            ps.append(p.astype(BF16))
        acc_ref[...] += jnp.dot(ckvt_ref[0, kt], jnp.concatenate(ps, axis=1),
                                preferred_element_type=F32)
        return tuple(out)

    l8 = over_tiles(pv_body, tuple(jnp.zeros((SUBLANES, LANES), F32) for _ in range(A_HEADS)))
    inv = [1.0 / jnp.sum(l, axis=0, keepdims=True) for l in l8]

    for j in range(A_HEADS // 2):
        y = None
        for h in (2 * j, 2 * j + 1):
            o = (acc_ref[:, h * LANES:(h + 1) * LANES] * inv[h]).astype(BF16)
            t = pl.dot(o, wuv_ref[h], trans_a=True)
            y = t if y is None else y + t
        ya_ref[0, :, j * LANES:(j + 1) * LANES] = y.astype(BF16)


def _mixer_a(kidx, ckv, ckvt, qi_r, wit, qlat, wuv_pad, bsz, seq):
    nqb = seq // Q_BLOCK
    nkt = seq // KEY_TILE
    topk = min(IDX_TOPK, seq // 4)
    kern = functools.partial(_mixer_a_kernel, seq=seq, topk=topk)
    return pl.pallas_call(
        kern,
        out_shape=jax.ShapeDtypeStruct((bsz, seq, A_WIDTH), BF16),
        grid=(bsz, nqb),
        in_specs=[
            pl.BlockSpec((1, nkt, KEY_TILE, IDX_DIM), lambda b, i: (b, 0, 0, 0)),
            pl.BlockSpec((1, nkt, KEY_TILE, A_KV_RANK), lambda b, i: (b, 0, 0, 0)),
            pl.BlockSpec((1, nkt, A_KV_RANK, KEY_TILE), lambda b, i: (b, 0, 0, 0)),
            pl.BlockSpec((1, 1, IDX_HEADS * Q_BLOCK, IDX_DIM), lambda b, i: (b, i, 0, 0)),
            pl.BlockSpec((1, IDX_HEADS, Q_BLOCK), lambda b, i: (b, 0, i)),
            pl.BlockSpec((1, Q_BLOCK, A_HEADS * A_KV_RANK), lambda b, i: (b, i, 0)),
            pl.BlockSpec(wuv_pad.shape, lambda b, i: (0, 0, 0)),
        ],
        out_specs=pl.BlockSpec((1, Q_BLOCK, A_WIDTH), lambda b, i: (b, i, 0)),
        scratch_shapes=[
            pltpu.VMEM((nkt, KEY_TILE, LANES), I32),
            pltpu.VMEM((nkt, KEY_TILE, A_HEADS * LANES), F32),
            pltpu.VMEM((A_KV_RANK, A_HEADS * LANES), F32),
        ],
        compiler_params=_params(("parallel", "arbitrary")),
        name="mixer_a",
    )(kidx, ckv, ckvt, qi_r, wit, qlat, wuv_pad)


def _mixer_b_kernel(q_ref, kp_ref, kc_ref, vp_ref, vc_ref, line_ref, yb_ref, bias_ref):
    @pl.when((pl.program_id(0) == 0) & (pl.program_id(1) == 0))
    def _():
        tq = lax.broadcasted_iota(I32, (B_QTILE, B_WIN), 0) >> 6
        kj = lax.broadcasted_iota(I32, (B_QTILE, B_WIN), 1) >> 6
        band = (kj >= tq) & (kj <= tq + B_LEFT_CHUNKS)
        for h in range(B_HEADS):
            rows = jnp.broadcast_to(line_ref[h:h + 1, :], (B_QTILE, B_QTILE + B_WIN))
            rolled = pltpu.roll(rows, 1, 1, stride=1, stride_axis=0)
            bias_ref[h] = jnp.where(band, rolled[:, B_QTILE:], -jnp.inf)

    t0 = pl.multiple_of(pl.program_id(1) * B_QTILE, B_QTILE)
    kw = jnp.concatenate([kp_ref[0], kc_ref[0]], axis=0)
    vw = jnp.concatenate([vp_ref[0], vc_ref[0]], axis=0)
    q = q_ref[0]
    kvalid = (t0 - B_PAD + lax.broadcasted_iota(I32, (1, B_WIN), 1)) >= 0
    lane = lax.broadcasted_iota(I32, (1, LANES), 1)
    scale = B_HEAD_DIM ** -0.5
    for j in range(B_HEADS // 2):
        sl = slice(j * LANES, (j + 1) * LANES)
        qs, ks, vs = q[:, sl], kw[:, sl], vw[:, sl]
        outs = []
        for hh in range(2):
            mine = (lane >= B_HEAD_DIM) if hh else (lane < B_HEAD_DIM)
            qm = jnp.where(mine, qs, jnp.zeros_like(qs))
            s = pl.dot(qm, ks, trans_b=True) * scale + bias_ref[2 * j + hh]
            s = jnp.where(kvalid, s, -jnp.inf)
            m = jnp.max(s, axis=-1, keepdims=True)
            p = jnp.exp(s - m)
            l = jnp.sum(p, axis=-1, keepdims=True)
            outs.append(jnp.dot(p.astype(BF16), vs, preferred_element_type=F32) / l)
        yb_ref[0, :, sl] = jnp.where(lane < B_HEAD_DIM, outs[0], outs[1]).astype(BF16)


def _band_line(rel_bias):
    m = np.arange(B_QTILE + B_WIN)
    return rel_bias[:, np.clip(B_WIN - 1 - m, -REL_CLIP, REL_CLIP) + REL_CLIP].astype(F32)


def _mixer_b(qb, kb, vb, line, bsz, seq):
    assert B_PAD == B_QTILE
    cur = pl.BlockSpec((1, B_QTILE, B_WIDTH), lambda b, i: (b, i, 0))
    prev = pl.BlockSpec((1, B_QTILE, B_WIDTH), lambda b, i: (b, jnp.maximum(i - 1, 0), 0))
    return pl.pallas_call(
        _mixer_b_kernel,
        out_shape=jax.ShapeDtypeStruct((bsz, seq, B_WIDTH), BF16),
        grid=(bsz, seq // B_QTILE),
        in_specs=[cur, prev, cur, prev, cur, pl.BlockSpec(line.shape, lambda b, i: (0, 0))],
        out_specs=pl.BlockSpec((1, B_QTILE, B_WIDTH), lambda b, i: (b, i, 0)),
        scratch_shapes=[pltpu.VMEM((B_HEADS, B_QTILE, B_WIN), F32)],
        compiler_params=_params(("arbitrary", "arbitrary")),
        name="mixer_b",
    )(qb, kb, kb, vb, vb, line)


def _layer_norm(z, g, b):
    mu = jnp.mean(z, axis=-1, keepdims=True)
    var = jnp.mean(jnp.square(z - mu), axis=-1, keepdims=True)
    return ((z - mu) * lax.rsqrt(var + LN_EPS)) * g + b


def _merge_kernel(x_ref, ya_ref, yb_ref, wg_ref, wba_ref, wbb_ref, wo_ref, g_ref, b_ref,
                  wr_ref, br_ref, x1t_ref, topi_ref, gate_ref):
    tm = x_ref.shape[0]
    x = x_ref[...]
    xb = x.astype(BF16)
    ga = jnp.dot(xb, wg_ref[:, :D_MODEL], preferred_element_type=F32)
    a = jnp.dot(ya_ref[...], wba_ref[...], preferred_element_type=F32)
    merged = jax.nn.sigmoid(ga) * a
    gb = jnp.dot(xb, wg_ref[:, D_MODEL:], preferred_element_type=F32)
    b = jnp.dot(yb_ref[...], wbb_ref[...], preferred_element_type=F32)
    merged = merged + jax.nn.sigmoid(gb) * b
    out = jnp.dot(merged.astype(BF16), wo_ref[...], preferred_element_type=F32)
    x1 = _layer_norm(DN_ALPHA * x + out, g_ref[...], b_ref[...])
    for c in range(ROW_TILES):
        x1t_ref[pl.ds(c, tm, stride=ROW_TILES), :] = x1[:, c * LANES:(c + 1) * LANES]

    lg = pl.dot(wr_ref[...], x1.astype(BF16), trans_b=True) + br_ref[...]
    eidx = lax.broadcasted_iota(I32, lg.shape, 0)
    vals, idxs = [], []
    for _ in range(TOP_K):
        m = jnp.max(lg, axis=0, keepdims=True)
        sel = jnp.min(jnp.where(lg == m, eidx, N_EXPERTS), axis=0, keepdims=True)
        vals.append(m)
        idxs.append(sel)
        lg = jnp.where(eidx == sel, -jnp.inf, lg)
    es = [jnp.exp(v - vals[0]) for v in vals]
    tot = es[0] + es[1] + es[2] + es[3]
    topi_ref[...] = jnp.concatenate(idxs, axis=0)
    gate_ref[...] = jnp.concatenate([e / tot for e in es], axis=0)


def _merge(xf, ya, yb, wg, wba, wbb, wo, g1, b1, wr_t, br):
    n, tm = xf.shape[0], min(MATMUL_TILE, xf.shape[0])
    row = lambda w: pl.BlockSpec((tm, w), lambda i: (i, 0))
    col = pl.BlockSpec((TOP_K, tm), lambda i: (0, i))
    return pl.pallas_call(
        _merge_kernel,
        out_shape=[jax.ShapeDtypeStruct((n * ROW_TILES, LANES), F32),
                   jax.ShapeDtypeStruct((TOP_K, n), I32),
                   jax.ShapeDtypeStruct((TOP_K, n), F32)],
        grid=(n // tm,),
        in_specs=[row(D_MODEL), row(A_WIDTH), row(B_WIDTH), _full(wg), _full(wba), _full(wbb),
                  _full(wo), _full(g1), _full(b1), _full(wr_t), _full(br)],
        out_specs=[pl.BlockSpec((tm * ROW_TILES, LANES), lambda i: (i, 0)), col, col],
        compiler_params=_params(("parallel",)),
        name="merge",
    )(xf, ya, yb, wg, wba, wbb, wo, g1, b1, wr_t, br)


def _one_hot_rows(topi, k, width):
    eidx = lax.broadcasted_iota(I32, (N_EXPERTS, width), 0)
    return eidx == topi[k:k + 1, :]


def _route_kernel(topi_ref, tri_ref, rank_ref, cnt_ref, run_ref):
    @pl.when(pl.program_id(0) == 0)
    def _():
        run_ref[...] = jnp.zeros_like(run_ref)

    topi = topi_ref[...]
    tr = topi.shape[1]
    hot = [_one_hot_rows(topi, k, tr) for k in range(TOP_K)]
    oh = jnp.concatenate([jnp.where(h, 1.0, 0.0).astype(BF16) for h in hot], axis=0)
    prefix = jnp.dot(oh, tri_ref[...], preferred_element_type=F32)
    base = run_ref[...]
    ranks = []
    for k in range(TOP_K):
        tbl = prefix[k * N_EXPERTS:(k + 1) * N_EXPERTS, :] + base
        ranks.append(jnp.sum(jnp.where(hot[k], tbl, 0.0), axis=0, keepdims=True))
        base = base + jnp.sum(jnp.where(hot[k], 1.0, 0.0), axis=1, keepdims=True)
    rank_ref[...] = jnp.concatenate(ranks, axis=0).astype(I32)
    run_ref[...] = base
    cnt_ref[...] = base.astype(I32)


def _route(topi):
    n, tr = topi.shape[1], TOKEN_TILE
    tri = (np.arange(tr)[:, None] < np.arange(tr)[None, :]).astype(np.float32)
    tri = jnp.asarray(tri, BF16)
    col = pl.BlockSpec((TOP_K, tr), lambda i: (0, i))
    return pl.pallas_call(
        _route_kernel,
        out_shape=[jax.ShapeDtypeStruct((TOP_K, n), I32), jax.ShapeDtypeStruct((N_EXPERTS, 1), I32)],
        grid=(n // tr,),
        in_specs=[col, _full(tri)],
        out_specs=[col, pl.BlockSpec((N_EXPERTS, 1), lambda i: (0, 0))],
        scratch_shapes=[pltpu.VMEM((N_EXPERTS, 1), F32)],
        compiler_params=_params(("arbitrary",)),
        name="route",
    )(topi, tri)


def _positions_kernel(topi_ref, rank_ref, pstv_ref, pos_ref):
    topi = topi_ref[...]
    rows = []
    for k in range(TOP_K):
        start = jnp.sum(jnp.where(_one_hot_rows(topi, k, topi.shape[1]), pstv_ref[...], 0),
                        axis=0, keepdims=True)
        rows.append(rank_ref[k:k + 1, :] + start)
    pos_ref[...] = jnp.concatenate(rows, axis=0)


def _positions(topi, rank, pstarts):
    n, td = topi.shape[1], TOKEN_TILE
    col = pl.BlockSpec((TOP_K, td), lambda i: (0, i))
    pstv = pstarts.reshape(N_EXPERTS, 1)
    return pl.pallas_call(
        _positions_kernel,
        out_shape=jax.ShapeDtypeStruct((TOP_K, n), I32),
        grid=(n // td,),
        in_specs=[col, col, _full(pstv)],
        out_specs=col,
        compiler_params=_params(("parallel",)),
        name="positions",
    )(topi, rank, pstv)


def _dispatch_kernel(cnt_ref, pst_ref, pos_hbm, x_hbm, xs_hbm,
                     pos_smem, xbuf, zbuf, sem_p, sem_x, sem_d, sem_z):
    i = pl.program_id(0)
    n = pl.num_programs(0)
    ng = pos_smem.shape[1]
    td = ng * DMA_GROUP
    tile_rows = td * ROW_TILES

    def pos_copy(step, slot):
        return pltpu.make_async_copy(pos_hbm.at[pl.ds(step * ng, ng)], pos_smem.at[slot], sem_p.at[slot])

    def x_copy(step, slot):
        return pltpu.make_async_copy(x_hbm.at[pl.ds(step * tile_rows, tile_rows)], xbuf.at[slot],
                                     sem_x.at[slot])

    @pl.when(i == 0)
    def _():
        pos_copy(0, 0).start()
        x_copy(0, 0).start()

    @pl.when(i + 1 < n)
    def _():
        pos_copy(i + 1, (i + 1) % 2).start()
        x_copy(i + 1, (i + 1) % 3).start()

    slot = i % 2
    xslot = i % 3
    pos_copy(i, slot).wait()
    x_copy(i, xslot).wait()

    def step_bytes(s):
        rows_ = TOP_K * tile_rows
        return pltpu.make_async_copy(xs_hbm.at[pl.ds(0, rows_)], xs_hbm.at[pl.ds(0, rows_)], sem_d.at[s])

    def body(g, c):
        dst = [pos_smem[slot, g, j] * ROW_TILES for j in range(DMA_GROUP * TOP_K)]
        for u in range(DMA_GROUP):
            src = xbuf.at[xslot, pl.ds((g * DMA_GROUP + u) * ROW_TILES, ROW_TILES)]
            for k in range(TOP_K):
                pltpu.make_async_copy(src, xs_hbm.at[pl.ds(dst[u * TOP_K + k], ROW_TILES)],
                                      sem_d.at[xslot]).start(priority=k % 2)
        return c
    lax.fori_loop(0, ng, body, 0)

    @pl.when(i > 0)
    def _():
        step_bytes((i + 2) % 3).wait()

    @pl.when(i == n - 1)
    def _():
        step_bytes(xslot).wait()
        zbuf[...] = jnp.zeros_like(zbuf)

        def zero_copy(r):
            return pltpu.make_async_copy(zbuf, xs_hbm.at[pl.ds(r * ROW_TILES, ROW_TILES)], sem_z)

        def zero_rows(lo, hi):
            def start(r, c2):
                zero_copy(r).start()
                return c2

            def wait(r, c2):
                zero_copy(r).wait()
                return c2
            lax.fori_loop(lo, hi, start, 0)
            lax.fori_loop(lo, hi, wait, 0)

        def padded_end(e):
            return pst_ref[e] + (cnt_ref[e] + MOE_BLOCK - 1) // MOE_BLOCK * MOE_BLOCK

        def ebody(e, c):
            zero_rows(pst_ref[e] + cnt_ref[e], padded_end(e))
            return c
        lax.fori_loop(0, N_EXPERTS, ebody, 0)
        zero_rows(padded_end(N_EXPERTS - 1), xs_hbm.shape[0] // ROW_TILES)


def _dispatch(counts, pstarts, pos_g, x1t, cap):
    n, td = pos_g.shape[0] * DMA_GROUP, TOKEN_TILE
    any_spec = pl.BlockSpec(memory_space=pl.ANY)
    return pl.pallas_call(
        _dispatch_kernel,
        out_shape=jax.ShapeDtypeStruct((cap * ROW_TILES, LANES), F32),
        grid_spec=pltpu.PrefetchScalarGridSpec(
            num_scalar_prefetch=2,
            grid=(n // td,),
            in_specs=[any_spec, any_spec],
            out_specs=any_spec,
            scratch_shapes=[
                pltpu.SMEM((2, td // DMA_GROUP, DMA_GROUP * TOP_K), I32),
                pltpu.VMEM((3, td * ROW_TILES, LANES), F32),
                pltpu.VMEM((ROW_TILES, LANES), F32),
                pltpu.SemaphoreType.DMA((2,)),
                pltpu.SemaphoreType.DMA((3,)),
                pltpu.SemaphoreType.DMA((3,)),
                pltpu.SemaphoreType.DMA,
            ],
        ),
        compiler_params=_params(("arbitrary",)),
        name="dispatch",
    )(counts, pstarts, pos_g, x1t)


def _moe_kernel(be_ref, nu_ref, xs_ref, wg_ref, bg_ref, wu_ref, bu_ref, wd_ref, bd_ref, ys_ref,
                wg_bf, wu_bf, wd_bf):
    i = pl.program_id(0)
    blk = MOE_BLOCK
    used = i < nu_ref[0]

    @pl.when(jnp.logical_or(i == 0, be_ref[i] != be_ref[jnp.maximum(i - 1, 0)]))
    def _():
        wg_bf[...] = wg_ref[0].astype(BF16)
        wu_bf[...] = wu_ref[0].astype(BF16)
        wd_bf[...] = wd_ref[0].astype(BF16)

    @pl.when(used)
    def _():
        xb = jnp.concatenate([xs_ref[pl.ds(c, blk, stride=ROW_TILES), :].astype(BF16)
                              for c in range(ROW_TILES)], axis=1)
        hg = jnp.minimum(jnp.dot(xb, wg_bf[...], preferred_element_type=F32) + bg_ref[0], SWIGLU_LIMIT)
        hu = jnp.clip(jnp.dot(xb, wu_bf[...], preferred_element_type=F32) + bu_ref[0],
                      -SWIGLU_LIMIT, SWIGLU_LIMIT)
        h = (hu + 1.0) * (hg * jax.nn.sigmoid(SWIGLU_ALPHA * hg))
        y = jnp.dot(h.astype(BF16), wd_bf[...], preferred_element_type=F32) + bd_ref[0]
        for c in range(ROW_TILES):
            ys_ref[pl.ds(c, blk, stride=ROW_TILES), :] = y[:, c * LANES:(c + 1) * LANES]

    @pl.when(jnp.logical_not(used))
    def _():
        ys_ref[...] = jnp.zeros_like(ys_ref)


def _moe(blk_expert, nused, xs, wg, bg, wu, bu, wd, bd):
    blk = MOE_BLOCK
    nblk = xs.shape[0] // (blk * ROW_TILES)
    wspec = lambda a: pl.BlockSpec((1,) + a.shape[1:], lambda i, be, nu: (be[i], 0, 0))
    xspec = pl.BlockSpec((blk * ROW_TILES, LANES), lambda i, be, nu: (jnp.minimum(i, nu[0] - 1), 0))
    return pl.pallas_call(
        _moe_kernel,
        out_shape=jax.ShapeDtypeStruct(xs.shape, F32),
        grid_spec=pltpu.PrefetchScalarGridSpec(
            num_scalar_prefetch=2,
            grid=(nblk,),
            in_specs=[xspec, wspec(wg), wspec(bg), wspec(wu), wspec(bu), wspec(wd), wspec(bd)],
            out_specs=pl.BlockSpec((blk * ROW_TILES, LANES), lambda i, be, nu: (i, 0)),
            scratch_shapes=[pltpu.VMEM(wg.shape[1:], BF16), pltpu.VMEM(wu.shape[1:], BF16),
                            pltpu.VMEM(wd.shape[1:], BF16)],
        ),
        compiler_params=_params(("arbitrary",)),
        name="moe",
    )(blk_expert, nused, xs, wg, bg, wu, bu, wd, bd)


def _final_kernel(pos_hbm, gt_ref, x1t_ref, ys_hbm, g_ref, b_ref, o_ref,
                  pos_smem, ybuf, sem_p, sem_g):
    i = pl.program_id(0)
    n = pl.num_programs(0)
    ft = FINAL_TILE
    slot_rows = TOP_K * ft * ROW_TILES

    ng = ft // DMA_GROUP

    def pos_copy(tile, slot):
        return pltpu.make_async_copy(pos_hbm.at[pl.ds(tile * ng, ng)], pos_smem.at[slot], sem_p.at[slot])

    def gather_rows(pslot, yslot):
        def body(g, c):
            src = [pos_smem[pslot, g, j] * ROW_TILES for j in range(DMA_GROUP * TOP_K)]
            base = yslot * slot_rows + g * (DMA_GROUP * ROW_TILES)
            for u in range(DMA_GROUP):
                for k in range(TOP_K):
                    dst = base + (k * ft + u) * ROW_TILES
                    pltpu.make_async_copy(ys_hbm.at[pl.ds(src[u * TOP_K + k], ROW_TILES)],
                                          ybuf.at[pl.ds(dst, ROW_TILES)],
                                          sem_g.at[yslot]).start(priority=k % 2)
            return c
        lax.fori_loop(0, ng, body, 0)

    def gather_wait(yslot):
        pltpu.make_async_copy(ys_hbm.at[pl.ds(0, slot_rows)],
                              ybuf.at[pl.ds(yslot * slot_rows, slot_rows)], sem_g.at[yslot]).wait()

    @pl.when(i == 0)
    def _():
        c = pos_copy(0, 0)
        c.start()
        c.wait()
        gather_rows(0, 0)

        @pl.when(n > 1)
        def _():
            pos_copy(1, 1).start()

    @pl.when(i + 1 < n)
    def _():
        pos_copy(i + 1, (i + 1) % 3).wait()
        gather_rows((i + 1) % 3, (i + 1) % 2)

    @pl.when(i + 2 < n)
    def _():
        pos_copy(i + 2, (i + 2) % 3).start()

    slot = i % 2
    gather_wait(slot)
    g = gt_ref[...]
    zs = []
    for c in range(ROW_TILES):
        f = None
        for k in range(TOP_K):
            yk = ybuf[pl.ds(slot * slot_rows + k * ft * ROW_TILES + c, ft, stride=ROW_TILES), :]
            term = g[:, k:k + 1] * yk
            f = term if f is None else f + term
        zs.append(DN_ALPHA * x1t_ref[pl.ds(c, ft, stride=ROW_TILES), :] + f)
    o_ref[...] = _layer_norm(jnp.concatenate(zs, axis=1), g_ref[...], b_ref[...])


def _final(pos, gates_t, x1t, ys, g2, b2):
    ft = FINAL_TILE
    n = gates_t.shape[0]
    any_spec = pl.BlockSpec(memory_space=pl.ANY)
    return pl.pallas_call(
        _final_kernel,
        out_shape=jax.ShapeDtypeStruct((n, D_MODEL), F32),
        grid=(n // ft,),
        in_specs=[any_spec, pl.BlockSpec((ft, TOP_K), lambda i: (i, 0)),
                  pl.BlockSpec((ft * ROW_TILES, LANES), lambda i: (i, 0)), any_spec,
                  _full(g2), _full(b2)],
        out_specs=pl.BlockSpec((ft, D_MODEL), lambda i: (i, 0)),
        scratch_shapes=[
            pltpu.SMEM((3, ft // DMA_GROUP, DMA_GROUP * TOP_K), I32),
            pltpu.VMEM((2 * TOP_K * ft * ROW_TILES, LANES), F32),
            pltpu.SemaphoreType.DMA((3,)),
            pltpu.SemaphoreType.DMA((2,)),
        ],
        compiler_params=_params(("arbitrary",)),
        name="final",
    )(pos, gates_t, x1t, ys, g2, b2)


def _mix_weight(w_in):
    qa, ckv, qi, ki, wi, qb, kb, vb, _, _ = jnp.split(w_in, np.cumsum(SPLITS)[:-1].tolist(), axis=-1)
    pad = lambda a: jnp.pad(a, ((0, 0), (0, LANES - a.shape[1])))
    return jnp.concatenate([qa, qi, ckv, pad(ki), pad(wi), qb, kb, vb], axis=1).astype(BF16)


def _layer(x, w_in, kv_norm_g, idx_k_norm_g, idx_k_norm_b, w_uk, w_uv, rel_bias, w_branch_a,
           w_branch_b, w_out, ln1_g, ln1_b, w_router, b_router, w_gate, b_gate, w_up, b_up,
           w_down, b_down, ln2_g, ln2_b):
    bsz, seq, _ = x.shape
    n = bsz * seq
    assert seq % KEY_TILE == 0 and n % TOKEN_TILE == 0
    xf = x.reshape(n, D_MODEL)
    row = lambda v: v.reshape(1, -1).astype(F32)

    w_mix = _mix_weight(w_in)
    w_gates = w_in[:, sum(SPLITS[:8]):].astype(BF16)
    wuk_bd = jnp.zeros((A_WIDTH, A_HEADS * A_KV_RANK), F32)
    wuv_pad = jnp.zeros((A_HEADS, A_KV_RANK, LANES), F32)
    for h in range(A_HEADS):
        wuk_bd = wuk_bd.at[h * A_HEAD_DIM:(h + 1) * A_HEAD_DIM, h * A_KV_RANK:(h + 1) * A_KV_RANK].set(w_uk[h])
        c0 = (h % 2) * A_HEAD_DIM
        wuv_pad = wuv_pad.at[h, :, c0:c0 + A_HEAD_DIM].set(w_uv[h])
    wuk_bd, wuv_pad = wuk_bd.astype(BF16), wuv_pad.astype(BF16)

    qlat, qi, ckv, ki, wi, qb, kb, vb = _projection(
        xf, w_mix, wuk_bd, row(kv_norm_g), row(idx_k_norm_g), row(idx_k_norm_b))

    nqb, nkt = seq // Q_BLOCK, seq // KEY_TILE
    kidx = ki.reshape(bsz, nkt, KEY_TILE, IDX_DIM)
    ckv4 = ckv.reshape(bsz, nkt, KEY_TILE, A_KV_RANK)
    ckvt = ckv4.transpose(0, 1, 3, 2)
    qi_r = qi.reshape(bsz, nqb, Q_BLOCK, IDX_HEADS, IDX_DIM).transpose(0, 1, 3, 2, 4).reshape(
        bsz, nqb, IDX_HEADS * Q_BLOCK, IDX_DIM)
    wit = wi.reshape(bsz, seq, IDX_HEADS).transpose(0, 2, 1)
    ya = _mixer_a(kidx, ckv4, ckvt, qi_r, wit, qlat.reshape(bsz, seq, -1), wuv_pad, bsz, seq)

    per_batch = lambda a: a.reshape(bsz, seq, B_WIDTH)
    yb = _mixer_b(per_batch(qb), per_batch(kb), per_batch(vb), _band_line(rel_bias), bsz, seq)

    x1t, topi, gates = _merge(
        xf, ya.reshape(n, A_WIDTH), yb.reshape(n, B_WIDTH), w_gates, w_branch_a.astype(BF16),
        w_branch_b.astype(BF16), w_out.astype(BF16), row(ln1_g), row(ln1_b),
        w_router.T.astype(BF16), b_router.reshape(-1, 1).astype(F32))

    blk = MOE_BLOCK
    cap = n * TOP_K + N_EXPERTS * blk
    nblk = cap // blk
    rank, cnt = _route(topi)
    counts = cnt[:, 0]
    padded = (counts + blk - 1) // blk * blk
    pends = jnp.cumsum(padded).astype(I32)
    pstarts = pends - padded
    nused = (pends[-1:] // blk).astype(I32)
    blk_start = jnp.arange(nblk, dtype=I32) * blk
    blk_expert = jnp.minimum(jnp.sum(blk_start[:, None] >= pends[None, :], axis=1), N_EXPERTS - 1).astype(I32)

    pos = _positions(topi, rank, pstarts)
    pos_g = pos.T.reshape(n // DMA_GROUP, DMA_GROUP * TOP_K)
    xs = _dispatch(counts, pstarts, pos_g, x1t, cap)
    ys = _moe(blk_expert, nused, xs, w_gate, b_gate[:, None, :], w_up, b_up[:, None, :],
              w_down, b_down[:, None, :])
    out = _final(pos_g, gates.T, x1t, ys, row(ln2_g), row(ln2_b))
    return out.reshape(bsz, seq, D_MODEL)


def kernel(x, w_in, kv_norm_g, idx_k_norm_g, idx_k_norm_b, w_uk, w_uv, rel_bias, w_branch_a,
           w_branch_b, w_out, ln1_g, ln1_b, w_router, b_router, w_gate, b_gate, w_up, b_up,
           w_down, b_down, ln2_g, ln2_b):
    for l in range(DEPTH):
        x = _layer(x, w_in[l], kv_norm_g[l], idx_k_norm_g[l], idx_k_norm_b[l], w_uk[l], w_uv[l],
                   rel_bias[l], w_branch_a[l], w_branch_b[l], w_out[l], ln1_g[l], ln1_b[l],
                   w_router[l], b_router[l], w_gate[l], b_gate[l], w_up[l], b_up[l], w_down[l],
                   b_down[l], ln2_g[l], ln2_b[l])
    return x
```
